```python
import math
import jax, jax.numpy as jnp
from jax import lax
import numpy as np

D_MODEL = 1024
BATCH = 4
SEQ = 4096
DEPTH = 2
DEC_BATCH = 128
DEC_SEQ = 1
PAST_LEN = 2048
PAGE_SIZE = 128

BRANCH_W = 256
N_BRANCH = 4
A_HEADS = 4
A_DH = 32
A_DV = 2 * A_DH
A_QK_W = A_HEADS * 2 * A_DH
A_V_W = A_HEADS * A_DV
Q_BLOCK = 128
S5_WIDTH = BRANCH_W
S5_GROUP = 16
S5_GROUPS = S5_WIDTH // S5_GROUP
S5_P = 64
C_HEADS = 4
C_DK = 32
C_DV = 64
C_RANK = 16
GLA_TAU = 16.0
C_QK_W = C_HEADS * C_DK
C_V_W = C_HEADS * C_DV
D_HEADS = 4
D_DK = 64
D_DV = 64
CONV_W = 4
D_QK_W = D_HEADS * D_DK
D_V_W = D_HEADS * D_DV
DN_QKV_W = 2 * D_QK_W + D_V_W
CHUNK = 64
D_FF = 4 * D_MODEL
PLE_DIM = 256
EPS = 1e-6
IN_SPLITS = (A_QK_W, A_QK_W, A_V_W, S5_WIDTH, C_QK_W, C_QK_W, C_V_W, C_V_W, C_RANK, DN_QKV_W, D_HEADS, D_HEADS, D_V_W)
IN_WIDTH = sum(IN_SPLITS)

kernel_name = 'hybrid_gated_parallel_decoder_step'


def rmsnorm(x, g):
    xf = x.astype(jnp.float32)
    y = xf * lax.rsqrt(jnp.mean(xf * xf, axis=-1, keepdims=True) + EPS)
    return (y * g.astype(jnp.float32)).astype(x.dtype)


def l2norm(x):
    xf = x.astype(jnp.float32)
    return xf * lax.rsqrt(jnp.sum(xf * xf, axis=-1, keepdims=True) + EPS)


def alibi_slopes(n):
    return 2.0 ** (-8.0 * np.arange(1, n + 1) / n)


def diff_attention(q, k, v, q_pos, k_pos, lam):
    B, Sq = q.shape[:2]
    qb = math.gcd(Q_BLOCK, Sq)
    nb = Sq // qb
    slopes = jnp.asarray(alibi_slopes(A_HEADS), jnp.float32)
    kf = k.astype(jnp.float32)
    vf = v.astype(jnp.float32)
    k1, k2 = kf[..., :A_DH], kf[..., A_DH:]
    scale = A_DH ** -0.5
    qf = jnp.moveaxis(q.astype(jnp.float32).reshape(B, nb, qb, A_HEADS, 2 * A_DH), 1, 0)
    qp = q_pos.reshape(nb, qb)

    def one_block(args):
        qblk, pos = args
        dist = (pos[:, None] - k_pos[None, :]).astype(jnp.float32)
        bias = -slopes[:, None, None] * jnp.abs(dist)[None]
        causal = k_pos[None, :] <= pos[:, None]

        def probs(qh, kh):
            s = jnp.einsum('bqhd,bkhd->bhqk', qh, kh) * scale + bias
            return jax.nn.softmax(jnp.where(causal, s, -jnp.inf), axis=-1)

        a = probs(qblk[..., :A_DH], k1) - lam * probs(qblk[..., A_DH:], k2)
        return jnp.einsum('bhqk,bkhd->bqhd', a, vf)

    o = lax.map(one_block, (qf, qp))
    return jnp.moveaxis(o, 0, 1).reshape(B, Sq, A_HEADS, A_DV)


def s5_mixer(u, h0_re, h0_im, lam_re, lam_im, log_dt, b_re, b_im, c_re, c_im, d, w_glu, b_glu):
    f32 = jnp.float32
    B, S, _ = u.shape
    uf = u.astype(f32).reshape(B, S, S5_GROUPS, S5_GROUP)
    dt = jnp.exp(log_dt.astype(f32))[:, None]
    lr, li = lam_re.astype(f32), lam_im.astype(f32)
    mag = jnp.exp(lr * dt)
    ab_re, ab_im = mag * jnp.cos(li * dt), mag * jnp.sin(li * dt)
    den = lr * lr + li * li
    nr, ni = ab_re - 1.0, ab_im
    f_re = (nr * lr + ni * li) / den
    f_im = (ni * lr - nr * li) / den
    br, bi = b_re.astype(f32), b_im.astype(f32)
    bb_re = f_re[..., None] * br - f_im[..., None] * bi
    bb_im = f_re[..., None] * bi + f_im[..., None] * br
    bu_re = jnp.einsum('bsgn,gpn->bsgp', uf, bb_re)
    bu_im = jnp.einsum('bsgn,gpn->bsgp', uf, bb_im)
    a_re = jnp.broadcast_to(ab_re, bu_re.shape)
    a_im = jnp.broadcast_to(ab_im, bu_re.shape)

    def combine(e1, e2):
        a1r, a1i, b1r, b1i = e1
        a2r, a2i, b2r, b2i = e2
        return (a1r * a2r - a1i * a2i, a1r * a2i + a1i * a2r,
                a2r * b1r - a2i * b1i + b2r, a2r * b1i + a2i * b1r + b2i)

    cr, ci, sr, si = lax.associative_scan(combine, (a_re, a_im, bu_re, bu_im), axis=1)
    h0r = h0_re.astype(f32)[:, None]
    h0i = h0_im.astype(f32)[:, None]
    h_re = cr * h0r - ci * h0i + sr
    h_im = cr * h0i + ci * h0r + si
    y = jnp.einsum('gnp,bsgp->bsgn', c_re.astype(f32), h_re) - jnp.einsum('gnp,bsgp->bsgn', c_im.astype(f32), h_im)
    y = y.reshape(B, S, S5_WIDTH) + d * uf.reshape(B, S, S5_WIDTH)
    y = jax.nn.gelu(y)
    out = y * jax.nn.sigmoid(y @ w_glu + b_glu)
    return out, h_re[:, -1], h_im[:, -1]


def gla_chunked(q, k, v, log_a, s0):
    f32 = jnp.float32
    B, S, H, _ = q.shape
    dv = v.shape[-1]
    c = math.gcd(CHUNK, S)
    n = S // c

    def to_chunks(t):
        return jnp.moveaxis(t.astype(f32).reshape((B, n, c) + t.shape[2:]), 1, 0)

    tri = jnp.tril(jnp.ones((c, c), bool))

    def step(state, xs):
        qc, kc, vc, gc = xs
        b = jnp.cumsum(gc, axis=1)
        rel = jnp.where(tri[None, :, :, None, None], b[:, :, None] - b[:, None, :], -jnp.inf)
        att = jnp.einsum('bthk,bshk,btshk->bhts', qc, kc, jnp.exp(rel))
        o = jnp.einsum('bthk,bhkv->bthv', qc * jnp.exp(b), state) + jnp.einsum('bhts,bshv->bthv', att, vc)
        b_last = b[:, -1]
        state = state * jnp.exp(b_last)[..., None] + jnp.einsum('bshk,bshv->bhkv', kc * jnp.exp(b_last[:, None] - b), vc)
        return state, o

    s_fin, o = lax.scan(step, s0.astype(f32), tuple(to_chunks(t) for t in (q, k, v, log_a)))
    return jnp.moveaxis(o, 0, 1).reshape(B, S, H, dv), s_fin


def gated_delta_chunked(q, k, v, g, beta, s0):
    f32 = jnp.float32
    B, S, H, _ = q.shape
    dv = v.shape[-1]
    c = math.gcd(CHUNK, S)
    n = S // c

    def to_chunks(t):
        return jnp.moveaxis(t.astype(f32).reshape((B, n, c) + t.shape[2:]), 1, 0)

    tri = jnp.tril(jnp.ones((c, c), bool))
    tri_strict = jnp.tril(jnp.ones((c, c), bool), -1)

    def step(state, xs):
        qc, kc, vc, gc, bc = xs
        qh, kh, vh = (jnp.swapaxes(t, 1, 2) for t in (qc, kc, vc))
        G = jnp.cumsum(jnp.swapaxes(gc, 1, 2), axis=-1)
        bh = jnp.swapaxes(bc, 1, 2)
        decay = jnp.exp(jnp.where(tri, G[..., :, None] - G[..., None, :], -jnp.inf))
        lower = jnp.where(tri_strict, jnp.einsum('bhtk,bhsk->bhts', kh, kh) * decay * bh[..., :, None], 0.0)
        u = lax.linalg.triangular_solve(lower, vh * bh[..., None], left_side=True, lower=True, unit_diagonal=True)
        w = lax.linalg.triangular_solve(lower, kh * (bh * jnp.exp(G))[..., None], left_side=True, lower=True, unit_diagonal=True)
        v_new = u - jnp.einsum('bhtk,bhkv->bhtv', w, state)
        att = jnp.einsum('bhtk,bhsk->bhts', qh, kh) * decay
        o = jnp.einsum('bhtk,bhkv->bhtv', qh * jnp.exp(G)[..., None], state) + jnp.einsum('bhts,bhsv->bhtv', att, v_new)
        g_last = G[..., -1]
        k_dec = kh * jnp.exp(g_last[..., None] - G)[..., None]
        state = state * jnp.exp(g_last)[..., None, None] + jnp.einsum('bhsk,bhsv->bhkv', k_dec, v_new)
        return state, jnp.swapaxes(o, 1, 2)

    s_fin, o = lax.scan(step, s0.astype(f32), tuple(to_chunks(t) for t in (q, k, v, g, beta)))
    return jnp.moveaxis(o, 0, 1).reshape(B, S, H, dv), s_fin


def causal_conv(x, buf, w):
    xe = jnp.concatenate([buf.astype(jnp.float32), x.astype(jnp.float32)], axis=1)
    y = lax.conv_general_dilated(xe, w.astype(jnp.float32)[:, None, :], window_strides=(1,), padding='VALID',
                                 dimension_numbers=('NWC', 'WIO', 'NWC'), feature_group_count=x.shape[-1])
    return y, xe[:, xe.shape[1] - (CONV_W - 1):]


def block(l, h, p_l, past_k, past_v, q_pos, k_pos, ssm_re0, ssm_im0, gla0, dn0, conv0, W):
    f32 = jnp.float32
    B, S, _ = h.shape
    hn = rmsnorm(h, W['g_mix'][l])
    z = hn @ W['w_in'][l]
    offs = [int(o) for o in np.cumsum(IN_SPLITS)[:-1]]
    (za_q, za_k, za_v, zb_u, zc_q, zc_k, zc_v, zc_g, zc_lr, zd_qkv, zd_b, zd_a, zd_z) = jnp.split(z, offs, axis=-1)

    lam_init = 0.8 - 0.6 * math.exp(-0.3 * l)
    lam = (jnp.exp(jnp.sum(W['lam_q1'][l].astype(f32) * W['lam_k1'][l].astype(f32)))
           - jnp.exp(jnp.sum(W['lam_q2'][l].astype(f32) * W['lam_k2'][l].astype(f32))) + lam_init)
    ka = za_k.reshape(B, S, A_HEADS, 2 * A_DH)
    va = za_v.reshape(B, S, A_HEADS, A_DV)
    k_all = ka if past_k is None else jnp.concatenate([past_k.astype(ka.dtype), ka], axis=1)
    v_all = va if past_v is None else jnp.concatenate([past_v.astype(va.dtype), va], axis=1)
    oa = diff_attention(za_q.reshape(B, S, A_HEADS, 2 * A_DH), k_all, v_all, q_pos, k_pos, lam)
    oa = (rmsnorm(oa, W['attn_norm'][l]) * (1.0 - lam_init)).reshape(B, S, A_V_W)

    ob, ssm_re, ssm_im = s5_mixer(zb_u, ssm_re0, ssm_im0, W['s5_lam_re'][l], W['s5_lam_im'][l], W['s5_log_dt'][l],
                                  W['s5_b_re'][l], W['s5_b_im'][l], W['s5_c_re'][l], W['s5_c_im'][l],
                                  W['s5_d'][l], W['s5_w_glu'][l], W['s5_b_glu'][l])

    qc = zc_q.reshape(B, S, C_HEADS, C_DK) * (C_DK ** -0.5)
    kc = zc_k.reshape(B, S, C_HEADS, C_DK)
    vc = zc_v.reshape(B, S, C_HEADS, C_DV)
    log_a = jax.nn.log_sigmoid((zc_lr @ W['gla_w_a2'][l] + W['gla_b_a'][l]).astype(f32)) / GLA_TAU
    oc, gla_s = gla_chunked(qc, kc, vc, log_a.reshape(B, S, C_HEADS, C_DK), gla0)
    oc = (rmsnorm(oc, W['gla_norm'][l]) * jax.nn.silu(zc_g.reshape(B, S, C_HEADS, C_DV))).reshape(B, S, C_V_W)

    qkv, conv_s = causal_conv(zd_qkv, conv0, W['dn_conv_w'][l])
    qkv = jax.nn.silu(qkv)
    qd = l2norm(qkv[..., :D_QK_W].reshape(B, S, D_HEADS, D_DK)) * (D_DK ** -0.5)
    kd = l2norm(qkv[..., D_QK_W:2 * D_QK_W].reshape(B, S, D_HEADS, D_DK))
    vd = qkv[..., 2 * D_QK_W:].reshape(B, S, D_HEADS, D_DV)
    beta = jax.nn.sigmoid(zd_b.astype(f32))
    g = -jnp.exp(W['dn_a_log'][l].astype(f32)) * jax.nn.softplus(zd_a.astype(f32) + W['dn_dt_bias'][l])
    od, dn_s = gated_delta_chunked(qd, kd, vd, g, beta, dn0)
    od = (rmsnorm(od, W['dn_norm'][l]) * jax.nn.silu(zd_z.reshape(B, S, D_HEADS, D_DV))).reshape(B, S, D_V_W)

    branches = (oa, ob, oc, od)
    merged = None
    for n_ in range(N_BRANCH):
        gate = jax.nn.sigmoid(hn @ W['w_branch_gate'][l, n_] + W['b_branch_gate'][l, n_])
        term = gate * (branches[n_] @ W['w_branch'][l, n_])
        merged = term if merged is None else merged + term
    h = h + merged @ W['w_out'][l]

    hf = rmsnorm(h, W['g_ffn'][l])
    h = h + jnp.square(jax.nn.relu(hf @ W['w_ff1'][l])) @ W['w_ff2'][l]

    pg = jax.nn.sigmoid(rmsnorm(h, W['g_ple'][l]) @ W['w_ple_gate'][l])
    h = h + pg * (p_l @ W['w_ple_proj'][l])
    return h, (ka, va, ssm_re, ssm_im, gla_s, dn_s, conv_s)


def setup_inputs(seed: int = 0) -> dict:
    key = jax.random.key(seed)
    keys = jax.random.split(key, 80)
    cnt = [0]
    f32 = jnp.float32

    def nk():
        cnt[0] += 1
        return keys[cnt[0] - 1]

    def nrm(shape, scale):
        return jax.random.normal(nk(), shape, f32) * scale

    def gain(shape):
        return 1.0 + nrm(shape, 0.05)

    n_pages = PAST_LEN // PAGE_SIZE
    n_used = DEC_BATCH * n_pages
    n_pool = n_used + n_used // 4
    page_table = jax.random.permutation(nk(), n_pool)[:n_used].reshape(DEC_BATCH, n_pages).astype(jnp.int32)
    n_idx = jnp.arange(S5_P, dtype=f32)
    dn_dt = jnp.exp(jax.random.uniform(nk(), (DEPTH, D_HEADS), f32, math.log(1e-3), math.log(1e-1)))
    return {
        'x_prompt': nrm((BATCH, SEQ, D_MODEL), 1.0),
        'x_sample': nrm((DEC_BATCH, DEC_SEQ, D_MODEL), 1.0),
        'cache_k': nrm((DEPTH, n_pool, PAGE_SIZE, A_HEADS, 2 * A_DH), 1.0),
        'cache_v': nrm((DEPTH, n_pool, PAGE_SIZE, A_HEADS, A_DV), 1.0),
        'state_ssm_re': nrm((DEPTH, DEC_BATCH, S5_GROUPS, S5_P), 0.3),
        'state_ssm_im': nrm((DEPTH, DEC_BATCH, S5_GROUPS, S5_P), 0.3),
        'state_gla': nrm((DEPTH, DEC_BATCH, C_HEADS, C_DK, C_DV), 0.3),
        'state_delta': nrm((DEPTH, DEC_BATCH, D_HEADS, D_DK, D_DV), 0.3),
        'state_conv': nrm((DEPTH, DEC_BATCH, CONV_W - 1, DN_QKV_W), 1.0),
        'page_table': page_table,
        'p_prompt': nrm((DEPTH, BATCH, SEQ, PLE_DIM), 1.0),
        'p_sample': nrm((DEPTH, DEC_BATCH, DEC_SEQ, PLE_DIM), 1.0),
        'g_mix': gain((DEPTH, D_MODEL)),
        'w_in': nrm((DEPTH, D_MODEL, IN_WIDTH), D_MODEL ** -0.5),
        'lam_q1': nrm((DEPTH, A_DH), 0.1),
        'lam_k1': nrm((DEPTH, A_DH), 0.1),
        'lam_q2': nrm((DEPTH, A_DH), 0.1),
        'lam_k2': nrm((DEPTH, A_DH), 0.1),
        'attn_norm': gain((DEPTH, A_DV)),
        's5_lam_re': -0.5 + nrm((DEPTH, S5_GROUPS, S5_P), 0.01),
        's5_lam_im': math.pi * n_idx + nrm((DEPTH, S5_GROUPS, S5_P), 0.01),
        's5_log_dt': jax.random.uniform(nk(), (DEPTH, S5_GROUPS), f32, math.log(1e-3), math.log(1e-1)),
        's5_b_re': nrm((DEPTH, S5_GROUPS, S5_P, S5_GROUP), (2 * S5_GROUP) ** -0.5),
        's5_b_im': nrm((DEPTH, S5_GROUPS, S5_P, S5_GROUP), (2 * S5_GROUP) ** -0.5),
        's5_c_re': nrm((DEPTH, S5_GROUPS, S5_GROUP, S5_P), S5_P ** -0.5),
        's5_c_im': nrm((DEPTH, S5_GROUPS, S5_GROUP, S5_P), S5_P ** -0.5),
        's5_d': nrm((DEPTH, S5_WIDTH), 1.0),
        's5_w_glu': nrm((DEPTH, S5_WIDTH, S5_WIDTH), S5_WIDTH ** -0.5),
        's5_b_glu': nrm((DEPTH, S5_WIDTH), 0.02),
        'gla_w_a2': nrm((DEPTH, C_RANK, C_QK_W), C_RANK ** -0.5),
        'gla_b_a': nrm((DEPTH, C_QK_W), 0.1),
        'gla_norm': gain((DEPTH, C_DV)),
        'dn_conv_w': nrm((DEPTH, CONV_W, DN_QKV_W), 0.5),
        'dn_a_log': jnp.log(jax.random.uniform(nk(), (DEPTH, D_HEADS), f32, 1.0, 16.0)),
        'dn_dt_bias': dn_dt + jnp.log(-jnp.expm1(-dn_dt)),
        'dn_norm': gain((DEPTH, D_DV)),
        'w_branch_gate': nrm((DEPTH, N_BRANCH, D_MODEL, D_MODEL), D_MODEL ** -0.5),
        'b_branch_gate': nrm((DEPTH, N_BRANCH, D_MODEL), 0.02),
        'w_branch': nrm((DEPTH, N_BRANCH, BRANCH_W, D_MODEL), BRANCH_W ** -0.5),
        'w_out': nrm((DEPTH, D_MODEL, D_MODEL), D_MODEL ** -0.5),
        'g_ffn': gain((DEPTH, D_MODEL)),
        'w_ff1': nrm((DEPTH, D_MODEL, D_FF), D_MODEL ** -0.5),
        'w_ff2': nrm((DEPTH, D_FF, D_MODEL), D_FF ** -0.5),
        'g_ple': gain((DEPTH, D_MODEL)),
        'w_ple_gate': nrm((DEPTH, D_MODEL, D_MODEL), D_MODEL ** -0.5),
        'w_ple_proj': nrm((DEPTH, PLE_DIM, D_MODEL), PLE_DIM ** -0.5),
        'g_final': gain((D_MODEL,)),
    }


def reference(x_prompt, x_sample, cache_k, cache_v, state_ssm_re, state_ssm_im, state_gla, state_delta, state_conv,
              page_table, p_prompt, p_sample, g_mix, w_in, lam_q1, lam_k1, lam_q2, lam_k2, attn_norm,
              s5_lam_re, s5_lam_im, s5_log_dt, s5_b_re, s5_b_im, s5_c_re, s5_c_im, s5_d, s5_w_glu, s5_b_glu,
              gla_w_a2, gla_b_a, gla_norm, dn_conv_w, dn_a_log, dn_dt_bias, dn_norm,
              w_branch_gate, b_branch_gate, w_branch, w_out, g_ffn, w_ff1, w_ff2,
              g_ple, w_ple_gate, w_ple_proj, g_final):
    W = dict(g_mix=g_mix, w_in=w_in, lam_q1=lam_q1, lam_k1=lam_k1, lam_q2=lam_q2, lam_k2=lam_k2,
             attn_norm=attn_norm, s5_lam_re=s5_lam_re, s5_lam_im=s5_lam_im, s5_log_dt=s5_log_dt,
             s5_b_re=s5_b_re, s5_b_im=s5_b_im, s5_c_re=s5_c_re, s5_c_im=s5_c_im, s5_d=s5_d,
             s5_w_glu=s5_w_glu, s5_b_glu=s5_b_glu, gla_w_a2=gla_w_a2, gla_b_a=gla_b_a, gla_norm=gla_norm,
             dn_conv_w=dn_conv_w, dn_a_log=dn_a_log, dn_dt_bias=dn_dt_bias, dn_norm=dn_norm,
             w_branch_gate=w_branch_gate, b_branch_gate=b_branch_gate, w_branch=w_branch, w_out=w_out,
             g_ffn=g_ffn, w_ff1=w_ff1, w_ff2=w_ff2, g_ple=g_ple, w_ple_gate=w_ple_gate, w_ple_proj=w_ple_proj)
    f32 = jnp.float32
    B, S, _ = x_prompt.shape
    DB, DS, _ = x_sample.shape
    past_len = page_table.shape[1] * cache_k.shape[2]
    pos_p = jnp.arange(S, dtype=jnp.int32)
    q_pos_s = past_len + jnp.arange(DS, dtype=jnp.int32)
    k_pos_s = jnp.arange(past_len + DS, dtype=jnp.int32)
    zeros_ssm = jnp.zeros((B, S5_GROUPS, S5_P), f32)
    zeros_gla = jnp.zeros((B, C_HEADS, C_DK, C_DV), f32)
    zeros_dn = jnp.zeros((B, D_HEADS, D_DK, D_DV), f32)
    zeros_conv = jnp.zeros((B, CONV_W - 1, DN_QKV_W), f32)
    hp, hs = x_prompt, x_sample
    st_p, st_s = [], []
    for l in range(DEPTH):
        hp, new_p = block(l, hp, p_prompt[l], None, None, pos_p, pos_p,
                          zeros_ssm, zeros_ssm, zeros_gla, zeros_dn, zeros_conv, W)
        st_p.append(new_p)
        past_k = cache_k[l][page_table].reshape(DB, past_len, A_HEADS, 2 * A_DH)
        past_v = cache_v[l][page_table].reshape(DB, past_len, A_HEADS, A_DV)
        hs, new_s = block(l, hs, p_sample[l], past_k, past_v, q_pos_s, k_pos_s,
                          state_ssm_re[l], state_ssm_im[l], state_gla[l], state_delta[l], state_conv[l], W)
        st_s.append(new_s)

    def stk(states, i):
        return jnp.stack([s[i] for s in states])

    y_prompt = rmsnorm(hp, g_final)
    y_sample = rmsnorm(hs, g_final)
    return (y_prompt, y_sample,
            stk(st_p, 0), stk(st_p, 1), stk(st_p, 2), stk(st_p, 3), stk(st_p, 4), stk(st_p, 5), stk(st_p, 6),
            stk(st_s, 0), stk(st_s, 1), stk(st_s, 2), stk(st_s, 3), stk(st_s, 4), stk(st_s, 5), stk(st_s, 6))
```

```python
import functools
import math

import numpy as np
import jax
import jax.numpy as jnp
from jax import lax
from jax.experimental import pallas as pl
from jax.experimental.pallas import tpu as pltpu

F32 = jnp.float32
BF16 = jnp.bfloat16
HI = lax.Precision.HIGHEST

D_MODEL = 1024
DEPTH = 2
A_HEADS, A_DH, A_DV = 4, 32, 64
S5_GROUPS, S5_GROUP, S5_P = 16, 16, 64
S5_STATE = S5_GROUPS * S5_P
C_HEADS, C_DK, C_DV, C_RANK = 4, 32, 64, 16
GLA_TAU = 16.0
D_HEADS, D_DK, D_DV = 4, 64, 64
CONV_W = 4
DN_QKV_W = 768
D_FF = 4096
PLE_DIM = 256
EPS = 1e-6
BRANCH_W = 256

Z_W = 2944
COL_Q, COL_K, COL_V, COL_U = 0, 256, 512, 768
COL_CQK, COL_CV, COL_DQKV, COL_CG, COL_DZ, COL_SMALL = 1024, 1280, 1536, 2304, 2560, 2816
SMALL_W = 128
SM_LR, SM_B, SM_A = 0, 16, 20

S5_CHUNK = 64
GLA_SUB = 16
DN_CHUNK = 64
VMEM_LIMIT_BYTES = 56 * 1024 * 1024
NEG_INF = float("-inf")


def _cparams(*sem):
    return pltpu.CompilerParams(dimension_semantics=sem, vmem_limit_bytes=VMEM_LIMIT_BYTES)


def _sigmoid(x):
    return 1.0 / (1.0 + jnp.exp(-x))


def _silu(x):
    return x * _sigmoid(x)


def _softplus(x):
    return jnp.maximum(x, 0.0) + jnp.log(1.0 + jnp.exp(-jnp.abs(x)))


def _gelu_tanh(x):
    return 0.5 * x * (1.0 + jnp.tanh(math.sqrt(2.0 / math.pi) * (x + 0.044715 * (x * x * x))))


def _rms(x, g):
    return x * lax.rsqrt(jnp.mean(x * x, axis=-1, keepdims=True) + EPS) * g


def _dot(a, b, **kw):
    return jnp.dot(a, b, preferred_element_type=F32, **kw)


def _dot_nt(a, b, **kw):
    return lax.dot_general(a, b, (((1,), (1,)), ((), ())), preferred_element_type=F32, **kw)


def _dot_tn(a, b, **kw):
    return lax.dot_general(a, b, (((0,), (0,)), ((), ())), preferred_element_type=F32, **kw)


def _full_spec(shape):
    nd = len(shape)
    return pl.BlockSpec(shape, lambda *_: (0,) * nd)


def _seg_matrix(width, seg):
    i = np.arange(width) // seg
    return jnp.asarray((i[:, None] == i[None, :]).astype(np.float32))


def _head_rms(o, seg_ref, gain):
    ms = _dot(o * o, seg_ref[...], precision=HI) * (1.0 / 64.0)
    return o * lax.rsqrt(ms + EPS) * gain


def _inproj_kernel(x_ref, g_ref, w_ref, z_ref):
    hn = _rms(x_ref[...], g_ref[...]).astype(BF16)
    z_ref[...] = _dot(hn, w_ref[...])


def _inproj(x, g, w, tm):
    T = x.shape[0]
    return pl.pallas_call(
        _inproj_kernel,
        out_shape=jax.ShapeDtypeStruct((T, Z_W), F32),
        grid=(T // tm,),
        in_specs=[pl.BlockSpec((tm, D_MODEL), lambda i: (i, 0)), _full_spec((1, D_MODEL)), _full_spec((D_MODEL, Z_W))],
        out_specs=pl.BlockSpec((tm, Z_W), lambda i: (i, 0)),
        compiler_params=_cparams("parallel"),
        name="inproj",
    )(x, g, w)


def _merge_kernel(h_ref, oa_ref, ob_ref, oc_ref, od_ref, g_ref, wg_ref, bg_ref, wb_ref, wo_ref, out_ref):
    h = h_ref[...]
    hn = _rms(h, g_ref[...]).astype(BF16)
    merged = None
    for n, br in enumerate((oa_ref, ob_ref, oc_ref, od_ref)):
        gate = _sigmoid(_dot(hn, wg_ref[n]) + bg_ref[n:n + 1, :])
        term = gate * _dot(br[...].astype(BF16), wb_ref[n])
        merged = term if merged is None else merged + term
    out_ref[...] = h + _dot(merged.astype(BF16), wo_ref[...])


def _merge(h, oa, ob, oc, od, g, wg, bg, wb, wo, tm):
    T = h.shape[0]
    row = lambda w: pl.BlockSpec((tm, w), lambda i: (i, 0))
    return pl.pallas_call(
        _merge_kernel,
        out_shape=jax.ShapeDtypeStruct((T, D_MODEL), F32),
        grid=(T // tm,),
        in_specs=[row(D_MODEL), row(BRANCH_W), row(BRANCH_W), row(BRANCH_W), row(BRANCH_W),
                  _full_spec((1, D_MODEL)), _full_spec((4, D_MODEL, D_MODEL)), _full_spec((4, D_MODEL)),
                  _full_spec((4, BRANCH_W, D_MODEL)), _full_spec((D_MODEL, D_MODEL))],
        out_specs=row(D_MODEL),
        compiler_params=_cparams("parallel"),
        name="merge",
    )(h, oa, ob, oc, od, g, wg, bg, wb, wo)


def _ffn_kernel(h_ref, p_ref, gf_ref, w1_ref, w2_ref, gp_ref, wpg_ref, wpp_ref, gfin_ref, out_ref, *, final):
    h = h_ref[...]
    hf = _rms(h, gf_ref[...]).astype(BF16)
    acc = h
    step = 1024
    for j in range(D_FF // step):
        a = jnp.maximum(_dot(hf, w1_ref[:, j * step:(j + 1) * step]), 0.0)
        acc = acc + _dot((a * a).astype(BF16), w2_ref[j * step:(j + 1) * step, :])
    pg = _sigmoid(_dot(_rms(acc, gp_ref[...]).astype(BF16), wpg_ref[...]))
    out = acc + pg * _dot(p_ref[...].astype(BF16), wpp_ref[...])
    if final:
        out = _rms(out, gfin_ref[...])
    out_ref[...] = out


def _ffn(h, p, gf, w1, w2, gp, wpg, wpp, gfin, tm, final):
    T = h.shape[0]
    row = lambda w: pl.BlockSpec((tm, w), lambda i: (i, 0))
    return pl.pallas_call(
        functools.partial(_ffn_kernel, final=final),
        out_shape=jax.ShapeDtypeStruct((T, D_MODEL), F32),
        grid=(T // tm,),
        in_specs=[row(D_MODEL), row(PLE_DIM), _full_spec((1, D_MODEL)), _full_spec((D_MODEL, D_FF)),
                  _full_spec((D_FF, D_MODEL)), _full_spec((1, D_MODEL)), _full_spec((D_MODEL, D_MODEL)),
                  _full_spec((PLE_DIM, D_MODEL)), _full_spec((1, D_MODEL))],
        out_specs=row(D_MODEL),
        compiler_params=_cparams("parallel"),
        name="ffn_ple",
    )(h, p, gf, w1, w2, gp, wpg, wpp, gfin)


_SLOPES = [float(s) for s in 2.0 ** (-8.0 * np.arange(1, A_HEADS + 1) / A_HEADS)]


def _lam_value(lq1, lk1, lq2, lk2, lam_init):
    return (jnp.exp(jnp.sum(lq1[...] * lk1[...], axis=-1, keepdims=True))
            - jnp.exp(jnp.sum(lq2[...] * lk2[...], axis=-1, keepdims=True)) + lam_init)


def _attn_prompt_kernel(q_ref, k_ref, v_ref, lq1, lk1, lq2, lk2, gain_ref, o_ref,
                        qm_scr, m_scr, l_scr, acc_scr, *, tq, lam_init):
    qi = pl.program_id(1)
    ki = pl.program_id(2)

    @pl.when(ki == 0)
    def _init():
        q = q_ref[...] * (A_DH ** -0.5)
        lane = lax.broadcasted_iota(jnp.int32, q.shape, 1)
        for i in range(2 * A_HEADS):
            lo = (i // 2) * 2 * A_DH + (i % 2) * A_DH
            qm_scr[i] = jnp.where((lane >= lo) & (lane < lo + A_DH), q, 0.0).astype(BF16)
        m_scr[...] = jnp.full(m_scr.shape, NEG_INF, F32)
        l_scr[...] = jnp.zeros(l_scr.shape, F32)
        acc_scr[...] = jnp.zeros(acc_scr.shape, F32)

    def _step(diag):
        k = k_ref[...].astype(BF16)
        v = v_ref[...].astype(BF16)
        row = lax.broadcasted_iota(jnp.int32, (tq, tq), 0)
        col = lax.broadcasted_iota(jnp.int32, (tq, tq), 1)
        dist = (col - row + (ki - qi) * tq).astype(F32)
        for h in range(A_HEADS):
            vh = v[:, h * A_DV:(h + 1) * A_DV]
            bias = dist * _SLOPES[h]
            if diag:
                bias = jnp.where(dist <= 0.0, bias, NEG_INF)
            for half in range(2):
                i = 2 * h + half
                s = _dot_nt(qm_scr[i], k) + bias
                m_prev = m_scr[i]
                m_new = jnp.maximum(m_prev, jnp.max(s, axis=-1, keepdims=True))
                alpha = jnp.exp(m_prev - m_new)
                p = jnp.exp(s - m_new)
                l_scr[i] = alpha * l_scr[i] + jnp.sum(p, axis=-1, keepdims=True)
                acc_scr[i] = alpha * acc_scr[i] + _dot(p.astype(BF16), vh)
                m_scr[i] = m_new

    @pl.when(ki < qi)
    def _off_diagonal():
        _step(False)

    @pl.when(ki == qi)
    def _diagonal():
        _step(True)
        lam = _lam_value(lq1, lk1, lq2, lk2, lam_init)
        outs = []
        for h in range(A_HEADS):
            o = acc_scr[2 * h] / l_scr[2 * h] - lam * (acc_scr[2 * h + 1] / l_scr[2 * h + 1])
            outs.append(_rms(o, gain_ref[...]) * (1.0 - lam_init))
        o_ref[...] = jnp.concatenate(outs, axis=1)


def _attn_prompt(z, lam_params, gain, B, S, tq, lam_init):
    nq = S // tq
    vec = _full_spec((1, A_DH))
    return pl.pallas_call(
        functools.partial(_attn_prompt_kernel, tq=tq, lam_init=lam_init),
        out_shape=jax.ShapeDtypeStruct((B * S, A_HEADS * A_DV), F32),
        grid=(B, nq, nq),
        in_specs=[pl.BlockSpec((tq, 256), lambda b, qi, ki: (b * nq + qi, COL_Q // 256)),
                  pl.BlockSpec((tq, 256), lambda b, qi, ki: (b * nq + jnp.minimum(ki, qi), COL_K // 256)),
                  pl.BlockSpec((tq, 256), lambda b, qi, ki: (b * nq + jnp.minimum(ki, qi), COL_V // 256)),
                  vec, vec, vec, vec, _full_spec((1, A_DV))],
        out_specs=pl.BlockSpec((tq, 256), lambda b, qi, ki: (b * nq + qi, 0)),
        scratch_shapes=[pltpu.VMEM((2 * A_HEADS, tq, 256), BF16), pltpu.VMEM((2 * A_HEADS, tq, 1), F32),
                        pltpu.VMEM((2 * A_HEADS, tq, 1), F32), pltpu.VMEM((2 * A_HEADS, tq, A_DV), F32)],
        compiler_params=_cparams("parallel", "parallel", "arbitrary"),
        name="attn_prompt",
    )(z, z, z, *lam_params, gain)


def _attn_decode_kernel(pt_ref, *refs, n_pages, page, lam_init):
    k_refs = refs[:n_pages]
    v_refs = refs[n_pages:2 * n_pages]
    z_ref, lq1, lk1, lq2, lk2, gain_ref, o_ref = refs[2 * n_pages:]
    del pt_ref
    past = n_pages * page
    zrow = z_ref[0]
    q = zrow[:, COL_Q:COL_Q + 256] * (A_DH ** -0.5)
    k_own = zrow[:, COL_K:COL_K + 256]
    v_own = zrow[:, COL_V:COL_V + 256]
    lane = lax.broadcasted_iota(jnp.int32, (2 * A_HEADS, 256), 1)
    r = lax.broadcasted_iota(jnp.int32, (2 * A_HEADS, 256), 0)
    lo = (r % A_HEADS) * 2 * A_DH + (r // A_HEADS) * A_DH
    qm = jnp.where((lane >= lo) & (lane < lo + A_DH), jnp.broadcast_to(q, (2 * A_HEADS, 256)), 0.0)
    qmb = qm.astype(BF16)
    s = jnp.concatenate([_dot_nt(qmb, kr[0].astype(BF16)) for kr in k_refs], axis=1)
    rr = lax.broadcasted_iota(jnp.int32, (2 * A_HEADS, 1), 0) % A_HEADS
    slope = jnp.zeros((2 * A_HEADS, 1), F32)
    for h in range(A_HEADS):
        slope = jnp.where(rr == h, _SLOPES[h], slope)
    kpos = lax.broadcasted_iota(jnp.int32, (2 * A_HEADS, past), 1).astype(F32)
    s = s - slope * (float(past) - kpos)
    s_own = jnp.sum(qm * k_own, axis=-1, keepdims=True)
    m = jnp.maximum(jnp.max(s, axis=-1, keepdims=True), s_own)
    p = jnp.exp(s - m)
    p_own = jnp.exp(s_own - m)
    denom = jnp.sum(p, axis=-1, keepdims=True) + p_own
    pb = p.astype(BF16)
    o = p_own * v_own
    for j, vr in enumerate(v_refs):
        o = o + _dot(pb[:, j * page:(j + 1) * page], vr[0].astype(BF16))
    o = o / denom
    lam = _lam_value(lq1, lk1, lq2, lk2, lam_init)
    d = o[:A_HEADS] - lam * o[A_HEADS:]
    hl = lax.broadcasted_iota(jnp.int32, (A_HEADS, 256), 1) // A_DV
    hr = lax.broadcasted_iota(jnp.int32, (A_HEADS, 256), 0)
    own = hl == hr
    d = jnp.where(own, d, 0.0)
    ms = jnp.sum(d * d, axis=-1, keepdims=True) * (1.0 / A_DV)
    d = d * lax.rsqrt(ms + EPS)
    gain = jnp.concatenate([gain_ref[...]] * A_HEADS, axis=1)
    o_ref[0] = jnp.sum(d, axis=0, keepdims=True) * gain * (1.0 - lam_init)


def _attn_decode(page_table, ck, cv, z3, lam_params, gain, lam_init):
    DB, n_pages = page_table.shape
    page = ck.shape[1]
    kv_specs = [pl.BlockSpec((1, page, 256), functools.partial(lambda b, pt, j: (pt[b, j], 0, 0), j=j))
                for j in range(n_pages)]
    vec = pl.BlockSpec((1, A_DH), lambda b, pt: (0, 0))
    grid_spec = pltpu.PrefetchScalarGridSpec(
        num_scalar_prefetch=1,
        grid=(DB,),
        in_specs=kv_specs + kv_specs + [pl.BlockSpec((1, 1, Z_W), lambda b, pt: (b, 0, 0)), vec, vec, vec, vec,
                                        pl.BlockSpec((1, A_DV), lambda b, pt: (0, 0))],
        out_specs=pl.BlockSpec((1, 1, 256), lambda b, pt: (b, 0, 0)),
    )
    return pl.pallas_call(
        functools.partial(_attn_decode_kernel, n_pages=n_pages, page=page, lam_init=lam_init),
        out_shape=jax.ShapeDtypeStruct((DB, 1, 256), F32),
        grid_spec=grid_spec,
        compiler_params=_cparams("arbitrary"),
        name="attn_decode",
    )(page_table, *([ck] * n_pages), *([cv] * n_pages), z3, *lam_params, gain)


def _s5_params(lam_re, lam_im, log_dt, b_re, b_im, c_re, c_im):
    dt = jnp.exp(log_dt)[:, None]
    lr, li = lam_re, lam_im
    a, th = lr * dt, li * dt
    mag = jnp.exp(a)
    ab_re, ab_im = mag * jnp.cos(th), mag * jnp.sin(th)
    den = lr * lr + li * li
    nr, ni = ab_re - 1.0, ab_im
    f_re = (nr * lr + ni * li) / den
    f_im = (ni * lr - nr * li) / den
    bb_re = f_re[..., None] * b_re - f_im[..., None] * b_im
    bb_im = f_re[..., None] * b_im + f_im[..., None] * b_re
    eye = jnp.eye(S5_GROUPS, dtype=F32)
    expand_b = lambda m: jnp.einsum("gpn,gh->gnhp", m, eye).reshape(S5_GROUPS * S5_GROUP, S5_STATE)
    wb = jnp.concatenate([expand_b(bb_re), expand_b(bb_im)], axis=1)
    expand_c = lambda m: jnp.einsum("gnp,gh->gphn", m, eye).reshape(S5_STATE, S5_GROUPS * S5_GROUP)
    cm = jnp.concatenate([expand_c(c_re), -expand_c(c_im)], axis=0)
    s = jnp.arange(S5_CHUNK, dtype=F32)[:, None]
    af, tf = a.reshape(1, S5_STATE), th.reshape(1, S5_STATE)
    en_re, en_im = jnp.exp(-s * af) * jnp.cos(s * tf), -jnp.exp(-s * af) * jnp.sin(s * tf)
    ep_re, ep_im = jnp.exp(s * af) * jnp.cos(s * tf), jnp.exp(s * af) * jnp.sin(s * tf)
    return dict(wb=wb, cm=cm, ab_re=ab_re.reshape(1, S5_STATE), ab_im=ab_im.reshape(1, S5_STATE),
                en_re=en_re, en_im=en_im, ep_re=ep_re, ep_im=ep_im)


def _s5_out(y, u, d_ref, wg_ref, bg_ref):
    y = _gelu_tanh(y + d_ref[...] * u)
    return y * _sigmoid(_dot(y.astype(BF16), wg_ref[...]) + bg_ref[...])


def _s5_prompt_kernel(u_ref, wb_ref, cm_ref, abr_ref, abi_ref, enr_ref, eni_ref, epr_ref, epi_ref,
                      d_ref, wg_ref, bg_ref, o_ref, hre_ref, him_ref, hr_scr, hi_scr, *, ts):
    t = pl.program_id(1)
    L = S5_CHUNK
    CB = 256

    @pl.when(t == 0)
    def _init():
        hr_scr[...] = jnp.zeros(hr_scr.shape, F32)
        hi_scr[...] = jnp.zeros(hi_scr.shape, F32)

    tri = (lax.broadcasted_iota(jnp.int32, (L, L), 0) >= lax.broadcasted_iota(jnp.int32, (L, L), 1))
    tri = jnp.where(tri, 1.0, 0.0).astype(BF16)
    for c in range(ts // L):
        u = u_ref[c * L:(c + 1) * L, :]
        ub = u.astype(BF16)
        y = None
        for cb in range(S5_STATE // CB):
            sl = slice(cb * CB, (cb + 1) * CB)
            bur = _dot(ub, wb_ref[:, cb * CB:(cb + 1) * CB])
            bui = _dot(ub, wb_ref[:, S5_STATE + cb * CB:S5_STATE + (cb + 1) * CB])
            enr, eni = enr_ref[:, sl], eni_ref[:, sl]
            xr = bur * enr - bui * eni
            xi = bur * eni + bui * enr
            cs = _dot(tri, jnp.concatenate([xr, xi], axis=1).astype(BF16))
            h0r, h0i = hr_scr[:, sl], hi_scr[:, sl]
            abr, abi = abr_ref[:, sl], abi_ref[:, sl]
            gr = cs[:, :CB] + (abr * h0r - abi * h0i)
            gi = cs[:, CB:] + (abr * h0i + abi * h0r)
            epr, epi = epr_ref[:, sl], epi_ref[:, sl]
            hr = epr * gr - epi * gi
            hi = epr * gi + epi * gr
            hr_scr[:, sl] = hr[L - 1:L, :]
            hi_scr[:, sl] = hi[L - 1:L, :]
            yc = (_dot(hr.astype(BF16), cm_ref[cb * CB:(cb + 1) * CB, :])
                  + _dot(hi.astype(BF16), cm_ref[S5_STATE + cb * CB:S5_STATE + (cb + 1) * CB, :]))
            y = yc if y is None else y + yc
        o_ref[c * L:(c + 1) * L, :] = _s5_out(y, u, d_ref, wg_ref, bg_ref)
    hre_ref[0] = hr_scr[...]
    him_ref[0] = hi_scr[...]


def _s5_prompt(z, sp, d, wg, bg, B, S, ts):
    nt = S // ts
    tab = _full_spec((S5_CHUNK, S5_STATE))
    vec = _full_spec((1, S5_STATE))
    st = pl.BlockSpec((1, 1, S5_STATE), lambda b, t: (b, 0, 0))
    return pl.pallas_call(
        functools.partial(_s5_prompt_kernel, ts=ts),
        out_shape=(jax.ShapeDtypeStruct((B * S, 256), F32), jax.ShapeDtypeStruct((B, 1, S5_STATE), F32),
                   jax.ShapeDtypeStruct((B, 1, S5_STATE), F32)),
        grid=(B, nt),
        in_specs=[pl.BlockSpec((ts, 256), lambda b, t: (b * nt + t, COL_U // 256)),
                  _full_spec((256, 2 * S5_STATE)), _full_spec((2 * S5_STATE, 256)), vec, vec, tab, tab, tab, tab,
                  _full_spec((1, 256)), _full_spec((256, 256)), _full_spec((1, 256))],
        out_specs=(pl.BlockSpec((ts, 256), lambda b, t: (b * nt + t, 0)), st, st),
        scratch_shapes=[pltpu.VMEM((1, S5_STATE), F32), pltpu.VMEM((1, S5_STATE), F32)],
        compiler_params=_cparams("parallel", "arbitrary"),
        name="s5_prompt",
    )(z, sp["wb"].astype(BF16), sp["cm"].astype(BF16), sp["ab_re"], sp["ab_im"], sp["en_re"], sp["en_im"],
      sp["ep_re"], sp["ep_im"], d, wg, bg)


def _gla_log_a(small, wa_ref, ba_ref):
    x = _dot(small, wa_ref[...], precision=HI) + ba_ref[...]
    return (jnp.minimum(x, 0.0) - jnp.log(1.0 + jnp.exp(-jnp.abs(x)))) * (1.0 / GLA_TAU)


def _gla_prompt_kernel(qk_ref, v_ref, cg_ref, sm_ref, wa_ref, ba_ref, gain_ref, segx_ref, seg_ref, bmask_ref,
                       o_ref, st_ref, b_scr, o_scr, st_scr, *, ts):
    t = pl.program_id(1)
    n = GLA_SUB

    @pl.when(t == 0)
    def _init():
        st_scr[...] = jnp.zeros(st_scr.shape, F32)

    log_a = _gla_log_a(sm_ref[...], wa_ref, ba_ref)
    ri = lax.broadcasted_iota(jnp.int32, (ts, ts), 0)
    ci = lax.broadcasted_iota(jnp.int32, (ts, ts), 1)
    tri = jnp.where((ri >= ci) & (ri // n == ci // n), 1.0, 0.0)
    b_scr[...] = _dot(tri, log_a, precision=HI)
    rows = lax.broadcasted_iota(jnp.int32, (n, 128), 0)

    def body(c, carry):
        r0 = pl.multiple_of(c * n, n)
        qk = qk_ref[pl.ds(r0, n), :]
        q = qk[:, :128] * (C_DK ** -0.5)
        k = qk[:, 128:]
        v = v_ref[pl.ds(r0, n), :]
        b = b_scr[pl.ds(r0, n), :]
        parts = []
        for s in range(n):
            e = jnp.exp(jnp.where(rows >= s, b - b[s:s + 1, :], NEG_INF))
            parts.append(q * k[s:s + 1, :] * e)
        pall = jnp.concatenate(parts, axis=0).astype(BF16)
        aexp = _dot(pall, segx_ref[...])
        acc = None
        for s in range(n):
            term = aexp[s * n:(s + 1) * n, :] * v[s:s + 1, :]
            acc = term if acc is None else acc + term
        st = st_scr[...]
        o = acc + _dot_nt((q * jnp.exp(b)).astype(BF16), st.astype(BF16))
        b_last = b[n - 1:n, :]
        kt = k * jnp.exp(b_last - b)
        upd = _dot_tn(v.astype(BF16), kt.astype(BF16))
        st_scr[...] = st * jnp.exp(b_last) + upd * bmask_ref[...]
        o_scr[pl.ds(r0, n), :] = o
        return carry

    lax.fori_loop(0, ts // n, body, 0)
    o = o_scr[...]
    cg = cg_ref[...]
    o_ref[...] = _head_rms(o, seg_ref, gain_ref[...]) * _silu(cg)
    st_ref[0] = st_scr[...]


def _gla_consts():
    hk = np.arange(128) // C_DK
    hv = np.arange(256) // C_DV
    segx = jnp.asarray((hk[:, None] == hv[None, :]).astype(np.float32)).astype(BF16)
    bmask = jnp.asarray((hv[:, None] == hk[None, :]).astype(np.float32))
    return segx, bmask


def _gla_prompt(z, wa, ba, gain, B, S, ts):
    nt = S // ts
    segx, bmask = _gla_consts()
    return pl.pallas_call(
        functools.partial(_gla_prompt_kernel, ts=ts),
        out_shape=(jax.ShapeDtypeStruct((B * S, 256), F32), jax.ShapeDtypeStruct((B, 256, 128), F32)),
        grid=(B, nt),
        in_specs=[pl.BlockSpec((ts, 256), lambda b, t: (b * nt + t, COL_CQK // 256)),
                  pl.BlockSpec((ts, 256), lambda b, t: (b * nt + t, COL_CV // 256)),
                  pl.BlockSpec((ts, 256), lambda b, t: (b * nt + t, COL_CG // 256)),
                  pl.BlockSpec((ts, SMALL_W), lambda b, t: (b * nt + t, COL_SMALL // SMALL_W)),
                  _full_spec((SMALL_W, 128)), _full_spec((1, 128)), _full_spec((1, 256)),
                  _full_spec((128, 256)), _full_spec((256, 256)), _full_spec((256, 128))],
        out_specs=(pl.BlockSpec((ts, 256), lambda b, t: (b * nt + t, 0)),
                   pl.BlockSpec((1, 256, 128), lambda b, t: (b, 0, 0))),
        scratch_shapes=[pltpu.VMEM((ts, 128), F32), pltpu.VMEM((ts, 256), F32), pltpu.VMEM((256, 128), F32)],
        compiler_params=_cparams("parallel", "arbitrary"),
        name="gla_prompt",
    )(z, z, z, z, wa, ba, gain, segx, _seg_matrix(256, 64), bmask)


def _dn_gates(small, acoef_ref, dtb_ref):
    beta = _sigmoid(small)
    g = acoef_ref[...] * _softplus(small + dtb_ref[...])
    return beta, g


def _dn_qkv(y, seg_ref):
    y = _silu(y)
    q, k, v = y[:, :256], y[:, 256:512], y[:, 512:768]
    nq = _dot(q * q, seg_ref[...], precision=HI)
    nk = _dot(k * k, seg_ref[...], precision=HI)
    q = q * lax.rsqrt(nq + EPS) * (D_DK ** -0.5)
    k = k * lax.rsqrt(nk + EPS)
    return q, k, v


def _dn_prompt_kernel(x_ref, dz_ref, sm_ref, cw_ref, acoef_ref, dtb_ref, gain_ref, seg_ref,
                      o_ref, st_ref, cs_ref, xp_scr, q_scr, k_scr, v_scr, gb_scr, o_scr, st_scr, *, ts):
    t = pl.program_id(1)
    C = DN_CHUNK
    PADR = 8

    @pl.when(t == 0)
    def _init():
        st_scr[...] = jnp.zeros(st_scr.shape, F32)
        xp_scr[0:PADR, :] = jnp.zeros((PADR, DN_QKV_W), F32)

    x = x_ref[...]
    xp_scr[PADR:PADR + ts, :] = x
    y = cw_ref[3:4, :] * x
    for i in range(CONV_W - 1):
        y = y + cw_ref[i:i + 1, :] * xp_scr[PADR - 3 + i:PADR - 3 + i + ts, :]
    xp_scr[PADR - 3:PADR, :] = x[ts - 3:ts, :]
    cs_ref[0] = x[ts - 3:ts, :]
    q, k, v = _dn_qkv(y, seg_ref)
    q_scr[...] = q
    k_scr[...] = k
    v_scr[...] = v
    beta, g = _dn_gates(sm_ref[...], acoef_ref, dtb_ref)
    ri = lax.broadcasted_iota(jnp.int32, (ts, ts), 0)
    ci = lax.broadcasted_iota(jnp.int32, (ts, ts), 1)
    tri = jnp.where((ri >= ci) & (ri // C == ci // C), 1.0, 0.0)
    gcum = _dot(tri, g, precision=HI)
    lane = lax.broadcasted_iota(jnp.int32, (ts, 128), 1)
    gb = jnp.where(lane >= SM_A, gcum, beta)
    gb_scr[...] = gb
    rr = lax.broadcasted_iota(jnp.int32, (C, C), 0)
    cc = lax.broadcasted_iota(jnp.int32, (C, C), 1)
    lower_incl = rr >= cc
    lower_strict = rr > cc
    eye = jnp.where(rr == cc, 1.0, 0.0)
    blocks = [(rr // w) == (cc // w) for w in (8, 16, 32, C)]

    def body(c, carry):
        r0 = pl.multiple_of(c * C, C)
        qc = q_scr[pl.ds(r0, C), :]
        kc = k_scr[pl.ds(r0, C), :]
        vc = v_scr[pl.ds(r0, C), :]
        gbc = gb_scr[pl.ds(r0, C), :]
        gbt = gbc.T
        outs = []
        for h in range(D_HEADS):
            sl = slice(h * 64, (h + 1) * 64)
            qh, kh, vh = qc[:, sl], kc[:, sl], vc[:, sl]
            bcol = gbc[:, SM_B + h:SM_B + h + 1]
            gcol = gbc[:, SM_A + h:SM_A + h + 1]
            grow = gbt[SM_A + h:SM_A + h + 1, :]
            glast = gbc[C - 1:C, SM_A + h:SM_A + h + 1]
            decay = jnp.exp(jnp.where(lower_incl, gcol - grow, NEG_INF))
            khb = kh.astype(BF16)
            kk = _dot_nt(khb, khb)
            m = jnp.where(lower_strict, -(kk * decay * bcol), 0.0)
            pw = jnp.where(blocks[0], m, 0.0)
            tm = eye + pw
            for _ in range(2):
                pwb = pw.astype(BF16)
                pw = _dot(pwb, pwb)
                tm = tm + _dot(tm.astype(BF16), pw.astype(BF16))
            for inner, outer in zip(blocks[:-1], blocks[1:]):
                tmb = tm.astype(BF16)
                off = jnp.where(outer & jnp.logical_not(inner), m, 0.0).astype(BF16)
                tm = tm + _dot(_dot(tmb, off).astype(BF16), tmb)
            eg = jnp.exp(gcol)
            rhs = jnp.concatenate([vh * bcol, kh * (bcol * eg)], axis=1)
            uw = _dot(tm.astype(BF16), rhs.astype(BF16))
            u, w = uw[:, :64], uw[:, 64:]
            sh = st_scr[h]
            shb = sh.astype(BF16)
            v_new = u - _dot(w.astype(BF16), shb)
            att = _dot_nt(qh.astype(BF16), khb) * decay
            vnb = v_new.astype(BF16)
            outs.append(_dot((qh * eg).astype(BF16), shb) + _dot(att.astype(BF16), vnb))
            k_dec = kh * jnp.exp(glast - gcol)
            st_scr[h] = sh * jnp.exp(glast) + _dot_tn(k_dec.astype(BF16), vnb)
        o_scr[pl.ds(r0, C), :] = jnp.concatenate(outs, axis=1)
        return carry

    lax.fori_loop(0, ts // C, body, 0)
    o_ref[...] = _head_rms(o_scr[...], seg_ref, gain_ref[...]) * _silu(dz_ref[...])
    st_ref[0] = st_scr[...]


def _dn_prompt(z, cw, acoef, dtb, gain, B, S, ts):
    nt = S // ts
    return pl.pallas_call(
        functools.partial(_dn_prompt_kernel, ts=ts),
        out_shape=(jax.ShapeDtypeStruct((B * S, 256), F32), jax.ShapeDtypeStruct((B, D_HEADS, D_DK, D_DV), F32),
                   jax.ShapeDtypeStruct((B, CONV_W - 1, DN_QKV_W), F32)),
        grid=(B, nt),
        in_specs=[pl.BlockSpec((ts, DN_QKV_W), lambda b, t: (b * nt + t, COL_DQKV // DN_QKV_W)),
                  pl.BlockSpec((ts, 256), lambda b, t: (b * nt + t, COL_DZ // 256)),
                  pl.BlockSpec((ts, SMALL_W), lambda b, t: (b * nt + t, COL_SMALL // SMALL_W)),
                  _full_spec((CONV_W, DN_QKV_W)), _full_spec((1, 128)), _full_spec((1, 128)), _full_spec((1, 256)),
                  _full_spec((256, 256))],
        out_specs=(pl.BlockSpec((ts, 256), lambda b, t: (b * nt + t, 0)),
                   pl.BlockSpec((1, D_HEADS, D_DK, D_DV), lambda b, t: (b, 0, 0, 0)),
                   pl.BlockSpec((1, CONV_W - 1, DN_QKV_W), lambda b, t: (b, 0, 0))),
        scratch_shapes=[pltpu.VMEM((8 + ts, DN_QKV_W), F32), pltpu.VMEM((ts, 256), F32), pltpu.VMEM((ts, 256), F32),
                        pltpu.VMEM((ts, 256), F32), pltpu.VMEM((ts, 128), F32),
                        pltpu.VMEM((ts, 256), F32), pltpu.VMEM((D_HEADS, D_DK, D_DV), F32)],
        compiler_params=_cparams("parallel", "arbitrary"),
        name="dn_prompt",
    )(z, z, z, cw, acoef, dtb, gain, _seg_matrix(256, 64))


def _dec_prep_kernel(z_ref, h0r_ref, h0i_ref, conv_ref, wb_ref, cm_ref, abr_ref, abi_ref, d_ref, wg_ref, bg_ref,
                     wa_ref, ba_ref, cw_ref, acoef_ref, dtb_ref, seg_ref,
                     ob_ref, hr_ref, hi_ref, gq_ref, gk_ref, ga_ref, dq_ref, dk_ref, dv_ref, dgb_ref, cs_ref):
    z = z_ref[...]
    u = z[:, COL_U:COL_U + 256]
    bu = _dot(u, wb_ref[...], precision=HI)
    bur, bui = bu[:, :S5_STATE], bu[:, S5_STATE:]
    h0r, h0i = h0r_ref[...], h0i_ref[...]
    abr, abi = abr_ref[...], abi_ref[...]
    hr = abr * h0r - abi * h0i + bur
    hi = abr * h0i + abi * h0r + bui
    hr_ref[...] = hr
    hi_ref[...] = hi
    y = _dot(jnp.concatenate([hr, hi], axis=1).astype(BF16), cm_ref[...])
    ob_ref[...] = _s5_out(y, u, d_ref, wg_ref, bg_ref)
    small = z[:, COL_SMALL:COL_SMALL + SMALL_W]
    gq_ref[...] = z[:, COL_CQK:COL_CQK + 128] * (C_DK ** -0.5)
    gk_ref[...] = z[:, COL_CQK + 128:COL_CQK + 256]
    ga_ref[...] = _gla_log_a(small, wa_ref, ba_ref)
    x = z[:, COL_DQKV:COL_DQKV + DN_QKV_W]
    y = cw_ref[3:4, :] * x
    for i in range(CONV_W - 1):
        y = y + cw_ref[i:i + 1, :] * conv_ref[i]
    cs_ref[0] = conv_ref[1]
    cs_ref[1] = conv_ref[2]
    cs_ref[2] = x
    q, k, v = _dn_qkv(y, seg_ref)
    dq_ref[...] = q
    dk_ref[...] = k
    dv_ref[...] = v
    beta, g = _dn_gates(small, acoef_ref, dtb_ref)
    lane = lax.broadcasted_iota(jnp.int32, small.shape, 1)
    dgb_ref[...] = jnp.where(lane >= SM_A, jnp.exp(g), beta)


def _dec_prep(z, h0r, h0i, conv3, sp, d, wg, bg, wa, ba, cw, acoef, dtb):
    DB = z.shape[0]
    sds = lambda *s: jax.ShapeDtypeStruct(s, F32)
    out_shape = (sds(DB, 256), sds(DB, S5_STATE), sds(DB, S5_STATE), sds(DB, 128), sds(DB, 128), sds(DB, 128),
                 sds(DB, 256), sds(DB, 256), sds(DB, 256), sds(DB, 128), sds(CONV_W - 1, DB, DN_QKV_W))
    return pl.pallas_call(
        _dec_prep_kernel,
        out_shape=out_shape,
        compiler_params=pltpu.CompilerParams(vmem_limit_bytes=VMEM_LIMIT_BYTES),
        name="dec_prep",
    )(z, h0r, h0i, conv3, sp["wb"], sp["cm"].astype(BF16), sp["ab_re"], sp["ab_im"], d, wg, bg, wa, ba, cw, acoef, dtb,
      _seg_matrix(256, 64))


def _gla_step_kernel(s_ref, q_ref, k_ref, a_ref, v_ref, cg_ref, gain_ref, sn_ref, o_ref):
    v = v_ref[...]
    sn = s_ref[...] * jnp.exp(a_ref[...]) + k_ref[...] * v[:, None, :]
    sn_ref[...] = sn
    o = jnp.sum(q_ref[...] * sn, axis=1)
    o_ref[...] = _rms(o, gain_ref[...]) * _silu(cg_ref[...])


def _gla_step(s, qcol, kcol, acol, v, cg, gain, tb):
    n = s.shape[0]
    big = pl.BlockSpec((tb, C_DK, C_DV), lambda i: (i, 0, 0))
    row = pl.BlockSpec((tb, C_DV), lambda i: (i, 0))
    return pl.pallas_call(
        _gla_step_kernel,
        out_shape=(jax.ShapeDtypeStruct(s.shape, F32), jax.ShapeDtypeStruct((n, C_DV), F32)),
        grid=(n // tb,),
        in_specs=[big, big, big, big, row, row, _full_spec((1, C_DV))],
        out_specs=(big, row),
        compiler_params=_cparams("parallel"),
        name="gla_step",
    )(s, qcol, kcol, acol, v, cg, gain)


def _dn_step_kernel(s_ref, q_ref, k_ref, v_ref, eg_ref, beta_ref, dz_ref, gain_ref, sn_ref, o_ref):
    s = s_ref[...]
    kcol = k_ref[...]
    eg = eg_ref[...]
    ks = jnp.sum(kcol * s, axis=1)
    v_new = beta_ref[...] * (v_ref[...] - eg * ks)
    sn = s * eg[:, None, :] + kcol * v_new[:, None, :]
    sn_ref[...] = sn
    o = jnp.sum(q_ref[...] * sn, axis=1)
    o_ref[...] = _rms(o, gain_ref[...]) * _silu(dz_ref[...])


def _dn_step(s, qcol, kcol, v, eg, beta, dz, gain, tb):
    n = s.shape[0]
    big = pl.BlockSpec((tb, D_DK, D_DV), lambda i: (i, 0, 0))
    row = pl.BlockSpec((tb, D_DV), lambda i: (i, 0))
    return pl.pallas_call(
        _dn_step_kernel,
        out_shape=(jax.ShapeDtypeStruct(s.shape, F32), jax.ShapeDtypeStruct((n, D_DV), F32)),
        grid=(n // tb,),
        in_specs=[big, big, big, row, row, row, row, _full_spec((1, D_DV))],
        out_specs=(big, row),
        compiler_params=_cparams("parallel"),
        name="dn_step",
    )(s, qcol, kcol, v, eg, beta, dz, gain)


def _prep_w_in(w):
    o = np.cumsum([0, 256, 256, 256, 256, 128, 128, 256, 256, 16, 768, 4, 4, 256])
    q, k, v, u, cq, ck, cv, cg, lr, dqkv, db, da, dz = [w[:, o[i]:o[i + 1]] for i in range(13)]
    pad = jnp.zeros((w.shape[0], SMALL_W - 24), w.dtype)
    return jnp.concatenate([q, k, v, u, cq, ck, cv, dqkv, cg, dz, lr, db, da, pad], axis=1).astype(BF16)


def _lane_row(vals, offset):
    return jnp.zeros((1, 128), F32).at[0, offset:offset + vals.shape[0]].set(vals)


def kernel(x_prompt, x_sample, cache_k, cache_v, state_ssm_re, state_ssm_im, state_gla, state_delta, state_conv, page_table, p_prompt, p_sample, g_mix, w_in, lam_q1, lam_k1, lam_q2, lam_k2, attn_norm, s5_lam_re, s5_lam_im, s5_log_dt, s5_b_re, s5_b_im, s5_c_re, s5_c_im, s5_d, s5_w_glu, s5_b_glu, gla_w_a2, gla_b_a, gla_norm, dn_conv_w, dn_a_log, dn_dt_bias, dn_norm, w_branch_gate, b_branch_gate, w_branch, w_out, g_ffn, w_ff1, w_ff2, g_ple, w_ple_gate, w_ple_proj, g_final):
    B, S, _ = x_prompt.shape
    DB = x_sample.shape[0]
    T = B * S
    n_pool, page = cache_k.shape[1], cache_k.shape[2]
    hp = x_prompt.reshape(T, D_MODEL)
    hs = x_sample.reshape(DB, D_MODEL)
    row = lambda a: a.reshape(1, -1)
    st_p, st_s = [], []
    for l in range(DEPTH):
        lam_init = 0.8 - 0.6 * math.exp(-0.3 * l)
        final = l == DEPTH - 1
        w_in_l = _prep_w_in(w_in[l])
        wg, wb, wo = w_branch_gate[l].astype(BF16), w_branch[l].astype(BF16), w_out[l].astype(BF16)
        w1, w2 = w_ff1[l].astype(BF16), w_ff2[l].astype(BF16)
        wpg, wpp = w_ple_gate[l].astype(BF16), w_ple_proj[l].astype(BF16)
        lam_params = (row(lam_q1[l]), row(lam_k1[l]), row(lam_q2[l]), row(lam_k2[l]))
        sp = _s5_params(s5_lam_re[l], s5_lam_im[l], s5_log_dt[l], s5_b_re[l], s5_b_im[l], s5_c_re[l], s5_c_im[l])
        s5d, s5wg, s5bg = row(s5_d[l]), s5_w_glu[l].astype(BF16), row(s5_b_glu[l])
        wa = jnp.zeros((SMALL_W, 128), F32).at[SM_LR:SM_LR + C_RANK].set(gla_w_a2[l])
        ba = row(gla_b_a[l])
        gla_gain4 = row(jnp.tile(gla_norm[l], C_HEADS))
        dn_gain4 = row(jnp.tile(dn_norm[l], D_HEADS))
        acoef = _lane_row(-jnp.exp(dn_a_log[l]), SM_A)
        dtb = _lane_row(dn_dt_bias[l], SM_A)
        cw = dn_conv_w[l]

        zp = _inproj(hp, row(g_mix[l]), w_in_l, 512)
        oa = _attn_prompt(zp, lam_params, row(attn_norm[l]), B, S, 512, lam_init)
        ob, p_hre, p_him = _s5_prompt(zp, sp, s5d, s5wg, s5bg, B, S, 512)
        oc, p_gla_t = _gla_prompt(zp, wa, ba, gla_gain4, B, S, 512)
        od, p_dn, p_conv = _dn_prompt(zp, cw, acoef, dtb, dn_gain4, B, S, 512)
        hp = _merge(hp, oa, ob, oc, od, row(g_mix[l]), wg, b_branch_gate[l], wb, wo, 512)
        hp = _ffn(hp, p_prompt[l].reshape(T, PLE_DIM), row(g_ffn[l]), w1, w2, row(g_ple[l]), wpg, wpp, row(g_final),
                  256, final)
        p_gla = jnp.stack([p_gla_t[:, h * C_DV:(h + 1) * C_DV, h * C_DK:(h + 1) * C_DK] for h in range(C_HEADS)],
                          axis=1).swapaxes(-1, -2)
        st_p.append((zp[:, COL_K:COL_K + 256].reshape(B, S, A_HEADS, 2 * A_DH),
                     zp[:, COL_V:COL_V + 256].reshape(B, S, A_HEADS, A_DV),
                     p_hre.reshape(B, S5_GROUPS, S5_P), p_him.reshape(B, S5_GROUPS, S5_P), p_gla, p_dn, p_conv))

        zs = _inproj(hs, row(g_mix[l]), w_in_l, DB)
        oa_s = _attn_decode(page_table, cache_k[l].reshape(n_pool, page, 256), cache_v[l].reshape(n_pool, page, 256),
                            zs.reshape(DB, 1, Z_W), lam_params, row(attn_norm[l]), lam_init).reshape(DB, 256)
        (ob_s, s_hre, s_him, gq, gk, ga, dq, dk, dv, dgb, s_conv3) = _dec_prep(
            zs, state_ssm_re[l].reshape(DB, S5_STATE), state_ssm_im[l].reshape(DB, S5_STATE),
            state_conv[l].swapaxes(0, 1), sp, s5d, s5wg, s5bg, wa, ba, cw, acoef, dtb)
        col = lambda a, h, dk, dv: jnp.broadcast_to(a.reshape(DB * h, dk, 1), (DB * h, dk, dv))
        s_gla, oc_s = _gla_step(state_gla[l].reshape(DB * C_HEADS, C_DK, C_DV), col(gq, C_HEADS, C_DK, C_DV),
                                col(gk, C_HEADS, C_DK, C_DV), col(ga, C_HEADS, C_DK, C_DV),
                                zs[:, COL_CV:COL_CV + 256].reshape(DB * C_HEADS, C_DV),
                                zs[:, COL_CG:COL_CG + 256].reshape(DB * C_HEADS, C_DV), row(gla_norm[l]), 128)
        lanes = lambda a, off: jnp.broadcast_to(a[:, off:off + D_HEADS].reshape(DB * D_HEADS, 1), (DB * D_HEADS, D_DV))
        s_dn, od_s = _dn_step(state_delta[l].reshape(DB * D_HEADS, D_DK, D_DV), col(dq, D_HEADS, D_DK, D_DV),
                              col(dk, D_HEADS, D_DK, D_DV), dv.reshape(DB * D_HEADS, D_DV), lanes(dgb, SM_A),
                              lanes(dgb, SM_B), zs[:, COL_DZ:COL_DZ + 256].reshape(DB * D_HEADS, D_DV),
                              row(dn_norm[l]), 64)
        hs = _merge(hs, oa_s, ob_s, oc_s.reshape(DB, 256), od_s.reshape(DB, 256), row(g_mix[l]), wg, b_branch_gate[l],
                    wb, wo, DB)
        hs = _ffn(hs, p_sample[l].reshape(DB, PLE_DIM), row(g_ffn[l]), w1, w2, row(g_ple[l]), wpg, wpp, row(g_final),
                  DB, final)
        st_s.append((zs[:, COL_K:COL_K + 256].reshape(DB, 1, A_HEADS, 2 * A_DH),
                     zs[:, COL_V:COL_V + 256].reshape(DB, 1, A_HEADS, A_DV),
                     s_hre.reshape(DB, S5_GROUPS, S5_P), s_him.reshape(DB, S5_GROUPS, S5_P),
                     s_gla.reshape(DB, C_HEADS, C_DK, C_DV), s_dn.reshape(DB, D_HEADS, D_DK, D_DV),
                     s_conv3.swapaxes(0, 1)))

    stk = lambda states, i: jnp.stack([s[i] for s in states])
    return (hp.reshape(B, S, D_MODEL), hs.reshape(DB, 1, D_MODEL),
            stk(st_p, 0), stk(st_p, 1), stk(st_p, 2), stk(st_p, 3), stk(st_p, 4), stk(st_p, 5), stk(st_p, 6),
            stk(st_s, 0), stk(st_s, 1), stk(st_s, 2), stk(st_s, 3), stk(st_s, 4), stk(st_s, 5), stk(st_s, 6))
```

```python
import functools
import math

import numpy as np
import jax
import jax.numpy as jnp
from jax import lax
from jax.experimental import pallas as pl
from jax.experimental.pallas import tpu as pltpu

F32 = jnp.float32
BF16 = jnp.bfloat16
HI = lax.Precision.HIGHEST

D_MODEL = 1024
DEPTH = 2
A_HEADS, A_DH, A_DV = 4, 32, 64
S5_GROUPS, S5_GROUP, S5_P = 16, 16, 64
S5_STATE = S5_GROUPS * S5_P
C_HEADS, C_DK, C_DV, C_RANK = 4, 32, 64, 16
GLA_TAU = 16.0
D_HEADS, D_DK, D_DV = 4, 64, 64
CONV_W = 4
DN_QKV_W = 768
D_FF = 4096
PLE_DIM = 256
EPS = 1e-6
BRANCH_W = 256

Z_W = 2944
COL_Q, COL_K, COL_V, COL_U = 0, 256, 512, 768
COL_CQK, COL_CV, COL_DQKV, COL_CG, COL_DZ, COL_SMALL = 1024, 1280, 1536, 2304, 2560, 2816
SMALL_W = 128
SM_LR, SM_B, SM_A = 0, 16, 20

S5_CHUNK = 64
GLA_SUB = 16
DN_CHUNK = 64
VMEM_LIMIT_BYTES = 56 * 1024 * 1024
NEG_INF = float("-inf")


def _cparams(*sem):
    return pltpu.CompilerParams(dimension_semantics=sem, vmem_limit_bytes=VMEM_LIMIT_BYTES)


def _sigmoid(x):
    return 1.0 / (1.0 + jnp.exp(-x))


def _silu(x):
    return x * _sigmoid(x)


def _softplus(x):
    return jnp.maximum(x, 0.0) + jnp.log(1.0 + jnp.exp(-jnp.abs(x)))


def _gelu_tanh(x):
    return 0.5 * x * (1.0 + jnp.tanh(math.sqrt(2.0 / math.pi) * (x + 0.044715 * (x * x * x))))


def _rms(x, g):
    return x * lax.rsqrt(jnp.mean(x * x, axis=-1, keepdims=True) + EPS) * g


def _dot(a, b, **kw):
    return jnp.dot(a, b, preferred_element_type=F32, **kw)


def _dot_nt(a, b, **kw):
    return lax.dot_general(a, b, (((1,), (1,)), ((), ())), preferred_element_type=F32, **kw)


def _dot_tn(a, b, **kw):
    return lax.dot_general(a, b, (((0,), (0,)), ((), ())), preferred_element_type=F32, **kw)


def _full_spec(shape):
    nd = len(shape)
    return pl.BlockSpec(shape, lambda *_: (0,) * nd)


def _seg_matrix(width, seg):
    i = np.arange(width) // seg
    return jnp.asarray((i[:, None] == i[None, :]).astype(np.float32))


def _head_rms(o, seg_ref, gain):
    ms = _dot(o * o, seg_ref[...], precision=HI) * (1.0 / 64.0)
    return o * lax.rsqrt(ms + EPS) * gain


def _inproj_kernel(x_ref, g_ref, w_ref, z_ref):
    hn = _rms(x_ref[...], g_ref[...]).astype(BF16)
    z_ref[...] = _dot(hn, w_ref[...])


def _inproj(x, g, w, tm):
    T = x.shape[0]
    return pl.pallas_call(
        _inproj_kernel,
        out_shape=jax.ShapeDtypeStruct((T, Z_W), F32),
        grid=(T // tm,),
        in_specs=[pl.BlockSpec((tm, D_MODEL), lambda i: (i, 0)), _full_spec((1, D_MODEL)), _full_spec((D_MODEL, Z_W))],
        out_specs=pl.BlockSpec((tm, Z_W), lambda i: (i, 0)),
        compiler_params=_cparams("parallel"),
        name="inproj",
    )(x, g, w)


def _merge_kernel(h_ref, oa_ref, ob_ref, oc_ref, od_ref, g_ref, wg_ref, bg_ref, wb_ref, wo_ref, out_ref):
    h = h_ref[...]
    hn = _rms(h, g_ref[...]).astype(BF16)
    merged = None
    for n, br in enumerate((oa_ref, ob_ref, oc_ref, od_ref)):
        gate = _sigmoid(_dot(hn, wg_ref[n]) + bg_ref[n:n + 1, :])
        term = gate * _dot(br[...].astype(BF16), wb_ref[n])
        merged = term if merged is None else merged + term
    out_ref[...] = h + _dot(merged.astype(BF16), wo_ref[...])


def _merge(h, oa, ob, oc, od, g, wg, bg, wb, wo, tm):
    T = h.shape[0]
    row = lambda w: pl.BlockSpec((tm, w), lambda i: (i, 0))
    return pl.pallas_call(
        _merge_kernel,
        out_shape=jax.ShapeDtypeStruct((T, D_MODEL), F32),
        grid=(T // tm,),
        in_specs=[row(D_MODEL), row(BRANCH_W), row(BRANCH_W), row(BRANCH_W), row(BRANCH_W),
                  _full_spec((1, D_MODEL)), _full_spec((4, D_MODEL, D_MODEL)), _full_spec((4, D_MODEL)),
                  _full_spec((4, BRANCH_W, D_MODEL)), _full_spec((D_MODEL, D_MODEL))],
        out_specs=row(D_MODEL),
        compiler_params=_cparams("parallel"),
        name="merge",
    )(h, oa, ob, oc, od, g, wg, bg, wb, wo)


def _ffn_kernel(h_ref, p_ref, gf_ref, w1_ref, w2_ref, gp_ref, wpg_ref, wpp_ref, gfin_ref, out_ref, *, final):
    h = h_ref[...]
    hf = _rms(h, gf_ref[...]).astype(BF16)
    acc = h
    step = 1024
    for j in range(D_FF // step):
        a = jnp.maximum(_dot(hf, w1_ref[:, j * step:(j + 1) * step]), 0.0)
        acc = acc + _dot((a * a).astype(BF16), w2_ref[j * step:(j + 1) * step, :])
    pg = _sigmoid(_dot(_rms(acc, gp_ref[...]).astype(BF16), wpg_ref[...]))
    out = acc + pg * _dot(p_ref[...].astype(BF16), wpp_ref[...])
    if final:
        out = _rms(out, gfin_ref[...])
    out_ref[...] = out


def _ffn(h, p, gf, w1, w2, gp, wpg, wpp, gfin, tm, final):
    T = h.shape[0]
    row = lambda w: pl.BlockSpec((tm, w), lambda i: (i, 0))
    return pl.pallas_call(
        functools.partial(_ffn_kernel, final=final),
        out_shape=jax.ShapeDtypeStruct((T, D_MODEL), F32),
        grid=(T // tm,),
        in_specs=[row(D_MODEL), row(PLE_DIM), _full_spec((1, D_MODEL)), _full_spec((D_MODEL, D_FF)),
                  _full_spec((D_FF, D_MODEL)), _full_spec((1, D_MODEL)), _full_spec((D_MODEL, D_MODEL)),
                  _full_spec((PLE_DIM, D_MODEL)), _full_spec((1, D_MODEL))],
        out_specs=row(D_MODEL),
        compiler_params=_cparams("parallel"),
        name="ffn_ple",
    )(h, p, gf, w1, w2, gp, wpg, wpp, gfin)


_SLOPES = [float(s) for s in 2.0 ** (-8.0 * np.arange(1, A_HEADS + 1) / A_HEADS)]
LOG2E = 1.0 / math.log(2.0)


def _lam_value(lq1, lk1, lq2, lk2, lam_init):
    return (jnp.exp(jnp.sum(lq1[...] * lk1[...], axis=-1, keepdims=True))
            - jnp.exp(jnp.sum(lq2[...] * lk2[...], axis=-1, keepdims=True)) + lam_init)


def _attn_prompt_kernel(q_ref, k_ref, v_ref, lq1, lk1, lq2, lk2, gain_ref, o_ref, qm_scr, m_scr, acc_scr,
                        *, tq, lam_init):
    qi = pl.program_id(1)
    ki = pl.program_id(2)

    @pl.when(ki == 0)
    def _init():
        q = q_ref[...] * (A_DH ** -0.5 * LOG2E)
        lane = lax.broadcasted_iota(jnp.int32, q.shape, 1)
        for i in range(2 * A_HEADS):
            lo = (i // 2) * 2 * A_DH + (i % 2) * A_DH
            qm_scr[i] = jnp.where((lane >= lo) & (lane < lo + A_DH), q, 0.0).astype(BF16)
        m_scr[...] = jnp.full(m_scr.shape, NEG_INF, F32)
        acc_scr[...] = jnp.zeros(acc_scr.shape, F32)

    def _step(diag):
        k = k_ref[...].astype(BF16)
        v = v_ref[...].astype(BF16)
        kpos = (lax.broadcasted_iota(jnp.int32, (1, tq), 1) + (ki - qi) * tq).astype(F32)
        ones_col = jnp.where(lax.broadcasted_iota(jnp.int32, (tq, A_DV), 1) == 0, 1.0, 0.0).astype(BF16)
        if diag:
            causal = (lax.broadcasted_iota(jnp.int32, (tq, tq), 1) <= lax.broadcasted_iota(jnp.int32, (tq, tq), 0))
        for h in range(A_HEADS):
            vext = jnp.concatenate([v[:, h * A_DV:(h + 1) * A_DV], ones_col], axis=1)
            bias = kpos * (_SLOPES[h] * LOG2E)
            if diag:
                bias = jnp.where(causal, bias, NEG_INF)
            for half in range(2):
                i = 2 * h + half
                s = _dot_nt(qm_scr[i], k) + bias
                m_prev = m_scr[i]
                m_new = jnp.maximum(m_prev, jnp.max(s, axis=-1, keepdims=True))
                alpha = jnp.exp2(m_prev - m_new)
                p = jnp.exp2(s - jnp.concatenate([m_new] * (tq // 128), axis=1))
                acc_scr[i] = alpha * acc_scr[i] + _dot(p.astype(BF16), vext)
                m_scr[i] = m_new

    @pl.when(ki < qi)
    def _off_diagonal():
        _step(False)

    @pl.when(ki == qi)
    def _diagonal():
        _step(True)
        lam = _lam_value(lq1, lk1, lq2, lk2, lam_init)
        outs = []
        for h in range(A_HEADS):
            a1, a2 = acc_scr[2 * h], acc_scr[2 * h + 1]
            o = a1[:, :A_DV] / a1[:, A_DV:A_DV + 1] - lam * (a2[:, :A_DV] / a2[:, A_DV:A_DV + 1])
            outs.append(_rms(o, gain_ref[...]) * (1.0 - lam_init))
        o_ref[...] = jnp.concatenate(outs, axis=1)


def _attn_prompt(z, lam_params, gain, B, S, tq, lam_init):
    nq = S // tq
    vec = _full_spec((1, A_DH))
    return pl.pallas_call(
        functools.partial(_attn_prompt_kernel, tq=tq, lam_init=lam_init),
        out_shape=jax.ShapeDtypeStruct((B * S, A_HEADS * A_DV), F32),
        grid=(B, nq, nq),
        in_specs=[pl.BlockSpec((tq, 256), lambda b, qi, ki: (b * nq + qi, COL_Q // 256)),
                  pl.BlockSpec((tq, 256), lambda b, qi, ki: (b * nq + jnp.minimum(ki, qi), COL_K // 256)),
                  pl.BlockSpec((tq, 256), lambda b, qi, ki: (b * nq + jnp.minimum(ki, qi), COL_V // 256)),
                  vec, vec, vec, vec, _full_spec((1, A_DV))],
        out_specs=pl.BlockSpec((tq, 256), lambda b, qi, ki: (b * nq + qi, 0)),
        scratch_shapes=[pltpu.VMEM((2 * A_HEADS, tq, 256), BF16), pltpu.VMEM((2 * A_HEADS, tq, 128), F32),
                        pltpu.VMEM((2 * A_HEADS, tq, 128), F32)],
        compiler_params=_cparams("parallel", "parallel", "arbitrary"),
        name="attn_prompt",
    )(z, z, z, *lam_params, gain)


def _attn_decode_kernel(pt_ref, *refs, n_pages, page, lam_init):
    k_refs = refs[:n_pages]
    v_refs = refs[n_pages:2 * n_pages]
    z_ref, lq1, lk1, lq2, lk2, gain_ref, o_ref = refs[2 * n_pages:]
    del pt_ref
    past = n_pages * page
    zrow = z_ref[0]
    q = zrow[:, COL_Q:COL_Q + 256] * (A_DH ** -0.5)
    k_own = zrow[:, COL_K:COL_K + 256]
    v_own = zrow[:, COL_V:COL_V + 256]
    lane = lax.broadcasted_iota(jnp.int32, (2 * A_HEADS, 256), 1)
    r = lax.broadcasted_iota(jnp.int32, (2 * A_HEADS, 256), 0)
    lo = (r % A_HEADS) * 2 * A_DH + (r // A_HEADS) * A_DH
    qm = jnp.where((lane >= lo) & (lane < lo + A_DH), jnp.broadcast_to(q, (2 * A_HEADS, 256)), 0.0)
    qmb = qm.astype(BF16)
    s = jnp.concatenate([_dot(qmb, kr[0, 0].astype(BF16)) for kr in k_refs], axis=1)
    rr = lax.broadcasted_iota(jnp.int32, (2 * A_HEADS, 1), 0) % A_HEADS
    slope = jnp.zeros((2 * A_HEADS, 1), F32)
    for h in range(A_HEADS):
        slope = jnp.where(rr == h, _SLOPES[h], slope)
    kpos = lax.broadcasted_iota(jnp.int32, (2 * A_HEADS, past), 1).astype(F32)
    s = s - slope * (float(past) - kpos)
    s_own = jnp.sum(qm * k_own, axis=-1, keepdims=True)
    m = jnp.maximum(jnp.max(s, axis=-1, keepdims=True), s_own)
    p = jnp.exp(s - m)
    p_own = jnp.exp(s_own - m)
    denom = jnp.sum(p, axis=-1, keepdims=True) + p_own
    pb = p.astype(BF16)
    o = p_own * v_own
    for j, vr in enumerate(v_refs):
        o = o + _dot_nt(pb[:, j * page:(j + 1) * page], vr[0, 0].astype(BF16))
    o = o / denom
    lam = _lam_value(lq1, lk1, lq2, lk2, lam_init)
    d = o[:A_HEADS] - lam * o[A_HEADS:]
    hl = lax.broadcasted_iota(jnp.int32, (A_HEADS, 256), 1) // A_DV
    hr = lax.broadcasted_iota(jnp.int32, (A_HEADS, 256), 0)
    own = hl == hr
    d = jnp.where(own, d, 0.0)
    ms = jnp.sum(d * d, axis=-1, keepdims=True) * (1.0 / A_DV)
    d = d * lax.rsqrt(ms + EPS)
    gain = jnp.concatenate([gain_ref[...]] * A_HEADS, axis=1)
    o_ref[0] = jnp.sum(d, axis=0, keepdims=True) * gain * (1.0 - lam_init)


def _attn_decode(page_table, ck, cv, layer, z3, lam_params, gain, lam_init):
    DB, n_pages = page_table.shape
    page = ck.shape[-1]
    kv_specs = [pl.BlockSpec((1, 1, 256, page), functools.partial(lambda b, pt, j: (layer, pt[b, j], 0, 0), j=j))
                for j in range(n_pages)]
    vec = pl.BlockSpec((1, A_DH), lambda b, pt: (0, 0))
    grid_spec = pltpu.PrefetchScalarGridSpec(
        num_scalar_prefetch=1,
        grid=(DB,),
        in_specs=kv_specs + kv_specs + [pl.BlockSpec((1, 1, Z_W), lambda b, pt: (b, 0, 0)), vec, vec, vec, vec,
                                        pl.BlockSpec((1, A_DV), lambda b, pt: (0, 0))],
        out_specs=pl.BlockSpec((1, 1, 256), lambda b, pt: (b, 0, 0)),
    )
    return pl.pallas_call(
        functools.partial(_attn_decode_kernel, n_pages=n_pages, page=page, lam_init=lam_init),
        out_shape=jax.ShapeDtypeStruct((DB, 1, 256), F32),
        grid_spec=grid_spec,
        compiler_params=_cparams("arbitrary"),
        name="attn_decode",
    )(page_table, *([ck] * n_pages), *([cv] * n_pages), z3, *lam_params, gain)


def _s5_params(lam_re, lam_im, log_dt, b_re, b_im, c_re, c_im):
    dt = jnp.exp(log_dt)[:, None]
    lr, li = lam_re, lam_im
    a, th = lr * dt, li * dt
    mag = jnp.exp(a)
    ab_re, ab_im = mag * jnp.cos(th), mag * jnp.sin(th)
    den = lr * lr + li * li
    nr, ni = ab_re - 1.0, ab_im
    f_re = (nr * lr + ni * li) / den
    f_im = (ni * lr - nr * li) / den
    bb_re = f_re[..., None] * b_re - f_im[..., None] * b_im
    bb_im = f_re[..., None] * b_im + f_im[..., None] * b_re
    eye = jnp.eye(S5_GROUPS, dtype=F32)
    expand_b = lambda m: jnp.einsum("gpn,gh->gnhp", m, eye).reshape(S5_GROUPS * S5_GROUP, S5_STATE)
    wb = jnp.concatenate([expand_b(bb_re), expand_b(bb_im)], axis=1)
    expand_c = lambda m: jnp.einsum("gnp,gh->gphn", m, eye).reshape(S5_STATE, S5_GROUPS * S5_GROUP)
    cm = jnp.concatenate([expand_c(c_re), -expand_c(c_im)], axis=0)
    s = jnp.arange(S5_CHUNK, dtype=F32)[:, None]
    af, tf = a.reshape(1, S5_STATE), th.reshape(1, S5_STATE)
    en_re, en_im = jnp.exp(-s * af) * jnp.cos(s * tf), -jnp.exp(-s * af) * jnp.sin(s * tf)
    ep_re, ep_im = jnp.exp(s * af) * jnp.cos(s * tf), jnp.exp(s * af) * jnp.sin(s * tf)
    return dict(wb=wb, cm=cm, ab_re=ab_re.reshape(1, S5_STATE), ab_im=ab_im.reshape(1, S5_STATE),
                en_re=en_re, en_im=en_im, ep_re=ep_re, ep_im=ep_im)


def _s5_out(y, u, d_ref, wg_ref, bg_ref):
    y = _gelu_tanh(y + d_ref[...] * u)
    return y * _sigmoid(_dot(y.astype(BF16), wg_ref[...]) + bg_ref[...])


def _s5_prompt_kernel(u_ref, wb_ref, cm_ref, abr_ref, abi_ref, enr_ref, eni_ref, epr_ref, epi_ref,
                      d_ref, wg_ref, bg_ref, o_ref, hre_ref, him_ref, hr_scr, hi_scr, *, ts):
    t = pl.program_id(1)
    L = S5_CHUNK
    CB = 256

    @pl.when(t == 0)
    def _init():
        hr_scr[...] = jnp.zeros(hr_scr.shape, F32)
        hi_scr[...] = jnp.zeros(hi_scr.shape, F32)

    tri = (lax.broadcasted_iota(jnp.int32, (L, L), 0) >= lax.broadcasted_iota(jnp.int32, (L, L), 1))
    tri = jnp.where(tri, 1.0, 0.0).astype(BF16)
    for c in range(ts // L):
        u = u_ref[c * L:(c + 1) * L, :]
        ub = u.astype(BF16)
        y = None
        for cb in range(S5_STATE // CB):
            sl = slice(cb * CB, (cb + 1) * CB)
            bur = _dot(ub, wb_ref[:, cb * CB:(cb + 1) * CB])
            bui = _dot(ub, wb_ref[:, S5_STATE + cb * CB:S5_STATE + (cb + 1) * CB])
            enr, eni = enr_ref[:, sl], eni_ref[:, sl]
            xr = bur * enr - bui * eni
            xi = bur * eni + bui * enr
            cs = _dot(tri, jnp.concatenate([xr, xi], axis=1).astype(BF16))
            h0r, h0i = hr_scr[:, sl], hi_scr[:, sl]
            abr, abi = abr_ref[:, sl], abi_ref[:, sl]
            gr = cs[:, :CB] + (abr * h0r - abi * h0i)
            gi = cs[:, CB:] + (abr * h0i + abi * h0r)
            epr, epi = epr_ref[:, sl], epi_ref[:, sl]
            hr = epr * gr - epi * gi
            hi = epr * gi + epi * gr
            hr_scr[:, sl] = hr[L - 1:L, :]
            hi_scr[:, sl] = hi[L - 1:L, :]
            yc = (_dot(hr.astype(BF16), cm_ref[cb * CB:(cb + 1) * CB, :])
                  + _dot(hi.astype(BF16), cm_ref[S5_STATE + cb * CB:S5_STATE + (cb + 1) * CB, :]))
            y = yc if y is None else y + yc
        o_ref[c * L:(c + 1) * L, :] = _s5_out(y, u, d_ref, wg_ref, bg_ref)
    hre_ref[0] = hr_scr[...]
    him_ref[0] = hi_scr[...]


def _s5_prompt(z, sp, d, wg, bg, B, S, ts):
    nt = S // ts
    tab = _full_spec((S5_CHUNK, S5_STATE))
    vec = _full_spec((1, S5_STATE))
    st = pl.BlockSpec((1, 1, S5_STATE), lambda b, t: (b, 0, 0))
    return pl.pallas_call(
        functools.partial(_s5_prompt_kernel, ts=ts),
        out_shape=(jax.ShapeDtypeStruct((B * S, 256), F32), jax.ShapeDtypeStruct((B, 1, S5_STATE), F32),
                   jax.ShapeDtypeStruct((B, 1, S5_STATE), F32)),
        grid=(B, nt),
        in_specs=[pl.BlockSpec((ts, 256), lambda b, t: (b * nt + t, COL_U // 256)),
                  _full_spec((256, 2 * S5_STATE)), _full_spec((2 * S5_STATE, 256)), vec, vec, tab, tab, tab, tab,
                  _full_spec((1, 256)), _full_spec((256, 256)), _full_spec((1, 256))],
        out_specs=(pl.BlockSpec((ts, 256), lambda b, t: (b * nt + t, 0)), st, st),
        scratch_shapes=[pltpu.VMEM((1, S5_STATE), F32), pltpu.VMEM((1, S5_STATE), F32)],
        compiler_params=_cparams("parallel", "arbitrary"),
        name="s5_prompt",
    )(z, sp["wb"].astype(BF16), sp["cm"].astype(BF16), sp["ab_re"], sp["ab_im"], sp["en_re"], sp["en_im"],
      sp["ep_re"], sp["ep_im"], d, wg, bg)


def _gla_log_a(small, wa_ref, ba_ref):
    x = _dot(small, wa_ref[...], precision=HI) + ba_ref[...]
    return (jnp.minimum(x, 0.0) - jnp.log(1.0 + jnp.exp(-jnp.abs(x)))) * (1.0 / GLA_TAU)


def _gla_prompt_kernel(qk_ref, v_ref, cg_ref, sm_ref, wa_ref, ba_ref, gain_ref, segx_ref, seg_ref, bmask_ref,
                       o_ref, st_ref, b_scr, o_scr, st_scr, *, ts):
    t = pl.program_id(1)
    n = GLA_SUB

    @pl.when(t == 0)
    def _init():
        st_scr[...] = jnp.zeros(st_scr.shape, F32)

    log_a = _gla_log_a(sm_ref[...], wa_ref, ba_ref)
    ri = lax.broadcasted_iota(jnp.int32, (ts, ts), 0)
    ci = lax.broadcasted_iota(jnp.int32, (ts, ts), 1)
    tri = jnp.where((ri >= ci) & (ri // n == ci // n), 1.0, 0.0)
    b_scr[...] = _dot(tri, log_a, precision=HI)
    rows = lax.broadcasted_iota(jnp.int32, (n, 128), 0)

    def body(c, carry):
        r0 = pl.multiple_of(c * n, n)
        qk = qk_ref[pl.ds(r0, n), :]
        q = qk[:, :128] * (C_DK ** -0.5)
        k = qk[:, 128:]
        v = v_ref[pl.ds(r0, n), :]
        b = b_scr[pl.ds(r0, n), :]
        parts = []
        for s in range(n):
            e = jnp.exp(jnp.where(rows >= s, b - b[s:s + 1, :], NEG_INF))
            parts.append(q * k[s:s + 1, :] * e)
        pall = jnp.concatenate(parts, axis=0).astype(BF16)
        aexp = _dot(pall, segx_ref[...])
        acc = None
        for s in range(n):
            term = aexp[s * n:(s + 1) * n, :] * v[s:s + 1, :]
            acc = term if acc is None else acc + term
        st = st_scr[...]
        o = acc + _dot_nt((q * jnp.exp(b)).astype(BF16), st.astype(BF16))
        b_last = b[n - 1:n, :]
        kt = k * jnp.exp(b_last - b)
        upd = _dot_tn(v.astype(BF16), kt.astype(BF16))
        st_scr[...] = st * jnp.exp(b_last) + upd * bmask_ref[...]
        o_scr[pl.ds(r0, n), :] = o
        return carry

    lax.fori_loop(0, ts // n, body, 0)
    o = o_scr[...]
    cg = cg_ref[...]
    o_ref[...] = _head_rms(o, seg_ref, gain_ref[...]) * _silu(cg)
    st_ref[0] = st_scr[...]


def _gla_consts():
    hk = np.arange(128) // C_DK
    hv = np.arange(256) // C_DV
    segx = jnp.asarray((hk[:, None] == hv[None, :]).astype(np.float32)).astype(BF16)
    bmask = jnp.asarray((hv[:, None] == hk[None, :]).astype(np.float32))
    return segx, bmask


def _gla_prompt(z, wa, ba, gain, B, S, ts):
    nt = S // ts
    segx, bmask = _gla_consts()
    return pl.pallas_call(
        functools.partial(_gla_prompt_kernel, ts=ts),
        out_shape=(jax.ShapeDtypeStruct((B * S, 256), F32), jax.ShapeDtypeStruct((B, 256, 128), F32)),
        grid=(B, nt),
        in_specs=[pl.BlockSpec((ts, 256), lambda b, t: (b * nt + t, COL_CQK // 256)),
                  pl.BlockSpec((ts, 256), lambda b, t: (b * nt + t, COL_CV // 256)),
                  pl.BlockSpec((ts, 256), lambda b, t: (b * nt + t, COL_CG // 256)),
                  pl.BlockSpec((ts, SMALL_W), lambda b, t: (b * nt + t, COL_SMALL // SMALL_W)),
                  _full_spec((SMALL_W, 128)), _full_spec((1, 128)), _full_spec((1, 256)),
                  _full_spec((128, 256)), _full_spec((256, 256)), _full_spec((256, 128))],
        out_specs=(pl.BlockSpec((ts, 256), lambda b, t: (b * nt + t, 0)),
                   pl.BlockSpec((1, 256, 128), lambda b, t: (b, 0, 0))),
        scratch_shapes=[pltpu.VMEM((ts, 128), F32), pltpu.VMEM((ts, 256), F32), pltpu.VMEM((256, 128), F32)],
        compiler_params=_cparams("parallel", "arbitrary"),
        name="gla_prompt",
    )(z, z, z, z, wa, ba, gain, segx, _seg_matrix(256, 64), bmask)


def _dn_gates(small, acoef_ref, dtb_ref):
    beta = _sigmoid(small)
    g = acoef_ref[...] * _softplus(small + dtb_ref[...])
    return beta, g


def _dn_qkv(y, seg_ref):
    y = _silu(y)
    q, k, v = y[:, :256], y[:, 256:512], y[:, 512:768]
    nq = _dot(q * q, seg_ref[...], precision=HI)
    nk = _dot(k * k, seg_ref[...], precision=HI)
    q = q * lax.rsqrt(nq + EPS) * (D_DK ** -0.5)
    k = k * lax.rsqrt(nk + EPS)
    return q, k, v


def _dn_prompt_kernel(x_ref, dz_ref, sm_ref, cw_ref, acoef_ref, dtb_ref, gain_ref, seg_ref,
                      o_ref, st_ref, cs_ref, xp_scr, q_scr, k_scr, v_scr, gb_scr, o_scr, st_scr, *, ts):
    t = pl.program_id(1)
    C = DN_CHUNK
    PADR = 8

    @pl.when(t == 0)
    def _init():
        st_scr[...] = jnp.zeros(st_scr.shape, F32)
        xp_scr[0:PADR, :] = jnp.zeros((PADR, DN_QKV_W), F32)

    x = x_ref[...]
    xp_scr[PADR:PADR + ts, :] = x
    y = cw_ref[3:4, :] * x
    for i in range(CONV_W - 1):
        y = y + cw_ref[i:i + 1, :] * xp_scr[PADR - 3 + i:PADR - 3 + i + ts, :]
    xp_scr[PADR - 3:PADR, :] = x[ts - 3:ts, :]
    cs_ref[0] = x[ts - 3:ts, :]
    q, k, v = _dn_qkv(y, seg_ref)
    q_scr[...] = q
    k_scr[...] = k
    v_scr[...] = v
    beta, g = _dn_gates(sm_ref[...], acoef_ref, dtb_ref)
    ri = lax.broadcasted_iota(jnp.int32, (ts, ts), 0)
    ci = lax.broadcasted_iota(jnp.int32, (ts, ts), 1)
    tri = jnp.where((ri >= ci) & (ri // C == ci // C), 1.0, 0.0)
    gcum = _dot(tri, g, precision=HI)
    lane = lax.broadcasted_iota(jnp.int32, (ts, 128), 1)
    gb = jnp.where(lane >= SM_A, gcum, beta)
    gb_scr[...] = gb
    rr = lax.broadcasted_iota(jnp.int32, (C, C), 0)
    cc = lax.broadcasted_iota(jnp.int32, (C, C), 1)
    lower_incl = rr >= cc
    lower_strict = rr > cc
    eye = jnp.where(rr == cc, 1.0, 0.0)
    blocks = [(rr // w) == (cc // w) for w in (8, 16, 32, C)]

    def body(c, carry):
        r0 = pl.multiple_of(c * C, C)
        qc = q_scr[pl.ds(r0, C), :]
        kc = k_scr[pl.ds(r0, C), :]
        vc = v_scr[pl.ds(r0, C), :]
        gbc = gb_scr[pl.ds(r0, C), :]
        gbt = gbc.T
        outs = []
        for h in range(D_HEADS):
            sl = slice(h * 64, (h + 1) * 64)
            qh, kh, vh = qc[:, sl], kc[:, sl], vc[:, sl]
            bcol = gbc[:, SM_B + h:SM_B + h + 1]
            gcol = gbc[:, SM_A + h:SM_A + h + 1]
            grow = gbt[SM_A + h:SM_A + h + 1, :]
            glast = gbc[C - 1:C, SM_A + h:SM_A + h + 1]
            decay = jnp.exp(jnp.where(lower_incl, gcol - grow, NEG_INF))
            khb = kh.astype(BF16)
            kk = _dot_nt(khb, khb)
            m = jnp.where(lower_strict, -(kk * decay * bcol), 0.0)
            pw = jnp.where(blocks[0], m, 0.0)
            tm = eye + pw
            for _ in range(2):
                pwb = pw.astype(BF16)
                pw = _dot(pwb, pwb)
                tm = tm + _dot(tm.astype(BF16), pw.astype(BF16))
            for inner, outer in zip(blocks[:-1], blocks[1:]):
                tmb = tm.astype(BF16)
                off = jnp.where(outer & jnp.logical_not(inner), m, 0.0).astype(BF16)
                tm = tm + _dot(_dot(tmb, off).astype(BF16), tmb)
            eg = jnp.exp(gcol)
            rhs = jnp.concatenate([vh * bcol, kh * (bcol * eg)], axis=1)
            uw = _dot(tm.astype(BF16), rhs.astype(BF16))
            u, w = uw[:, :64], uw[:, 64:]
            sh = st_scr[h]
            shb = sh.astype(BF16)
            v_new = u - _dot(w.astype(BF16), shb)
            att = _dot_nt(qh.astype(BF16), khb) * decay
            vnb = v_new.astype(BF16)
            outs.append(_dot((qh * eg).astype(BF16), shb) + _dot(att.astype(BF16), vnb))
            k_dec = kh * jnp.exp(glast - gcol)
            st_scr[h] = sh * jnp.exp(glast) + _dot_tn(k_dec.astype(BF16), vnb)
        o_scr[pl.ds(r0, C), :] = jnp.concatenate(outs, axis=1)
        return carry

    lax.fori_loop(0, ts // C, body, 0)
    o_ref[...] = _head_rms(o_scr[...], seg_ref, gain_ref[...]) * _silu(dz_ref[...])
    st_ref[0] = st_scr[...]


def _dn_prompt(z, cw, acoef, dtb, gain, B, S, ts):
    nt = S // ts
    return pl.pallas_call(
        functools.partial(_dn_prompt_kernel, ts=ts),
        out_shape=(jax.ShapeDtypeStruct((B * S, 256), F32), jax.ShapeDtypeStruct((B, D_HEADS, D_DK, D_DV), F32),
                   jax.ShapeDtypeStruct((B, CONV_W - 1, DN_QKV_W), F32)),
        grid=(B, nt),
        in_specs=[pl.BlockSpec((ts, DN_QKV_W), lambda b, t: (b * nt + t, COL_DQKV // DN_QKV_W)),
                  pl.BlockSpec((ts, 256), lambda b, t: (b * nt + t, COL_DZ // 256)),
                  pl.BlockSpec((ts, SMALL_W), lambda b, t: (b * nt + t, COL_SMALL // SMALL_W)),
                  _full_spec((CONV_W, DN_QKV_W)), _full_spec((1, 128)), _full_spec((1, 128)), _full_spec((1, 256)),
                  _full_spec((256, 256))],
        out_specs=(pl.BlockSpec((ts, 256), lambda b, t: (b * nt + t, 0)),
                   pl.BlockSpec((1, D_HEADS, D_DK, D_DV), lambda b, t: (b, 0, 0, 0)),
                   pl.BlockSpec((1, CONV_W - 1, DN_QKV_W), lambda b, t: (b, 0, 0))),
        scratch_shapes=[pltpu.VMEM((8 + ts, DN_QKV_W), F32), pltpu.VMEM((ts, 256), F32), pltpu.VMEM((ts, 256), F32),
                        pltpu.VMEM((ts, 256), F32), pltpu.VMEM((ts, 128), F32),
                        pltpu.VMEM((ts, 256), F32), pltpu.VMEM((D_HEADS, D_DK, D_DV), F32)],
        compiler_params=_cparams("parallel", "arbitrary"),
        name="dn_prompt",
    )(z, z, z, cw, acoef, dtb, gain, _seg_matrix(256, 64))


def _dec_prep_kernel(z_ref, h0r_ref, h0i_ref, conv_ref, wb_ref, cm_ref, abr_ref, abi_ref, d_ref, wg_ref, bg_ref,
                     wa_ref, ba_ref, cw_ref, acoef_ref, dtb_ref, seg_ref,
                     ob_ref, hr_ref, hi_ref, gq_ref, gk_ref, ga_ref, dq_ref, dk_ref, dv_ref, dgb_ref, cs_ref):
    z = z_ref[...]
    u = z[:, COL_U:COL_U + 256]
    bu = _dot(u, wb_ref[...], precision=HI)
    bur, bui = bu[:, :S5_STATE], bu[:, S5_STATE:]
    h0r, h0i = h0r_ref[...], h0i_ref[...]
    abr, abi = abr_ref[...], abi_ref[...]
    hr = abr * h0r - abi * h0i + bur
    hi = abr * h0i + abi * h0r + bui
    hr_ref[...] = hr
    hi_ref[...] = hi
    y = _dot(jnp.concatenate([hr, hi], axis=1).astype(BF16), cm_ref[...])
    ob_ref[...] = _s5_out(y, u, d_ref, wg_ref, bg_ref)
    small = z[:, COL_SMALL:COL_SMALL + SMALL_W]
    gq_ref[...] = z[:, COL_CQK:COL_CQK + 128] * (C_DK ** -0.5)
    gk_ref[...] = z[:, COL_CQK + 128:COL_CQK + 256]
    ga_ref[...] = _gla_log_a(small, wa_ref, ba_ref)
    x = z[:, COL_DQKV:COL_DQKV + DN_QKV_W]
    y = cw_ref[3:4, :] * x
    for i in range(CONV_W - 1):
        y = y + cw_ref[i:i + 1, :] * conv_ref[i]
    cs_ref[0] = conv_ref[1]
    cs_ref[1] = conv_ref[2]
    cs_ref[2] = x
    q, k, v = _dn_qkv(y, seg_ref)
    dq_ref[...] = q
    dk_ref[...] = k
    dv_ref[...] = v
    beta, g = _dn_gates(small, acoef_ref, dtb_ref)
    lane = lax.broadcasted_iota(jnp.int32, small.shape, 1)
    dgb_ref[...] = jnp.where(lane >= SM_A, jnp.exp(g), beta)


def _dec_prep(z, h0r, h0i, conv3, sp, d, wg, bg, wa, ba, cw, acoef, dtb):
    DB = z.shape[0]
    sds = lambda *s: jax.ShapeDtypeStruct(s, F32)
    out_shape = (sds(DB, 256), sds(DB, S5_STATE), sds(DB, S5_STATE), sds(DB, 128), sds(DB, 128), sds(DB, 128),
                 sds(DB, 256), sds(DB, 256), sds(DB, 256), sds(DB, 128), sds(CONV_W - 1, DB, DN_QKV_W))
    return pl.pallas_call(
        _dec_prep_kernel,
        out_shape=out_shape,
        compiler_params=pltpu.CompilerParams(vmem_limit_bytes=VMEM_LIMIT_BYTES),
        name="dec_prep",
    )(z, h0r, h0i, conv3, sp["wb"], sp["cm"].astype(BF16), sp["ab_re"], sp["ab_im"], d, wg, bg, wa, ba, cw, acoef, dtb,
      _seg_matrix(256, 64))


def _gla_step_kernel(s_ref, q_ref, k_ref, a_ref, v_ref, cg_ref, gain_ref, sn_ref, o_ref):
    v = v_ref[...]
    sn = s_ref[...] * jnp.exp(a_ref[...]) + k_ref[...] * v[:, None, :]
    sn_ref[...] = sn
    o = jnp.sum(q_ref[...] * sn, axis=1)
    o_ref[...] = _rms(o, gain_ref[...]) * _silu(cg_ref[...])


def _gla_step(s, qcol, kcol, acol, v, cg, gain, tb):
    n = s.shape[0]
    big = pl.BlockSpec((tb, C_DK, C_DV), lambda i: (i, 0, 0))
    row = pl.BlockSpec((tb, C_DV), lambda i: (i, 0))
    return pl.pallas_call(
        _gla_step_kernel,
        out_shape=(jax.ShapeDtypeStruct(s.shape, F32), jax.ShapeDtypeStruct((n, C_DV), F32)),
        grid=(n // tb,),
        in_specs=[big, big, big, big, row, row, _full_spec((1, C_DV))],
        out_specs=(big, row),
        compiler_params=_cparams("parallel"),
        name="gla_step",
    )(s, qcol, kcol, acol, v, cg, gain)


def _dn_step_kernel(s_ref, q_ref, k_ref, v_ref, eg_ref, beta_ref, dz_ref, gain_ref, sn_ref, o_ref):
    s = s_ref[...]
    kcol = k_ref[...]
    eg = eg_ref[...]
    ks = jnp.sum(kcol * s, axis=1)
    v_new = beta_ref[...] * (v_ref[...] - eg * ks)
    sn = s * eg[:, None, :] + kcol * v_new[:, None, :]
    sn_ref[...] = sn
    o = jnp.sum(q_ref[...] * sn, axis=1)
    o_ref[...] = _rms(o, gain_ref[...]) * _silu(dz_ref[...])


def _dn_step(s, qcol, kcol, v, eg, beta, dz, gain, tb):
    n = s.shape[0]
    big = pl.BlockSpec((tb, D_DK, D_DV), lambda i: (i, 0, 0))
    row = pl.BlockSpec((tb, D_DV), lambda i: (i, 0))
    return pl.pallas_call(
        _dn_step_kernel,
        out_shape=(jax.ShapeDtypeStruct(s.shape, F32), jax.ShapeDtypeStruct((n, D_DV), F32)),
        grid=(n // tb,),
        in_specs=[big, big, big, row, row, row, row, _full_spec((1, D_DV))],
        out_specs=(big, row),
        compiler_params=_cparams("parallel"),
        name="dn_step",
    )(s, qcol, kcol, v, eg, beta, dz, gain)


def _prep_w_in(w):
    o = np.cumsum([0, 256, 256, 256, 256, 128, 128, 256, 256, 16, 768, 4, 4, 256])
    q, k, v, u, cq, ck, cv, cg, lr, dqkv, db, da, dz = [w[:, o[i]:o[i + 1]] for i in range(13)]
    pad = jnp.zeros((w.shape[0], SMALL_W - 24), w.dtype)
    return jnp.concatenate([q, k, v, u, cq, ck, cv, dqkv, cg, dz, lr, db, da, pad], axis=1).astype(BF16)


def _lane_row(vals, offset):
    return jnp.zeros((1, 128), F32).at[0, offset:offset + vals.shape[0]].set(vals)


def kernel(x_prompt, x_sample, cache_k, cache_v, state_ssm_re, state_ssm_im, state_gla, state_delta, state_conv, page_table, p_prompt, p_sample, g_mix, w_in, lam_q1, lam_k1, lam_q2, lam_k2, attn_norm, s5_lam_re, s5_lam_im, s5_log_dt, s5_b_re, s5_b_im, s5_c_re, s5_c_im, s5_d, s5_w_glu, s5_b_glu, gla_w_a2, gla_b_a, gla_norm, dn_conv_w, dn_a_log, dn_dt_bias, dn_norm, w_branch_gate, b_branch_gate, w_branch, w_out, g_ffn, w_ff1, w_ff2, g_ple, w_ple_gate, w_ple_proj, g_final):
    B, S, _ = x_prompt.shape
    DB = x_sample.shape[0]
    T = B * S
    n_pool, page = cache_k.shape[1], cache_k.shape[2]
    hp = x_prompt.reshape(T, D_MODEL)
    hs = x_sample.reshape(DB, D_MODEL)
    row = lambda a: a.reshape(1, -1)
    ck_t = cache_k.transpose(0, 1, 3, 4, 2).reshape(DEPTH, n_pool, A_HEADS * 2 * A_DH, page)
    cv_t = cache_v.transpose(0, 1, 3, 4, 2).reshape(DEPTH, n_pool, A_HEADS * A_DV, page)
    st_p, st_s = [], []
    for l in range(DEPTH):
        lam_init = 0.8 - 0.6 * math.exp(-0.3 * l)
        final = l == DEPTH - 1
        w_in_l = _prep_w_in(w_in[l])
        wg, wb, wo = w_branch_gate[l].astype(BF16), w_branch[l].astype(BF16), w_out[l].astype(BF16)
        w1, w2 = w_ff1[l].astype(BF16), w_ff2[l].astype(BF16)
        wpg, wpp = w_ple_gate[l].astype(BF16), w_ple_proj[l].astype(BF16)
        lam_params = (row(lam_q1[l]), row(lam_k1[l]), row(lam_q2[l]), row(lam_k2[l]))
        sp = _s5_params(s5_lam_re[l], s5_lam_im[l], s5_log_dt[l], s5_b_re[l], s5_b_im[l], s5_c_re[l], s5_c_im[l])
        s5d, s5wg, s5bg = row(s5_d[l]), s5_w_glu[l].astype(BF16), row(s5_b_glu[l])
        wa = jnp.zeros((SMALL_W, 128), F32).at[SM_LR:SM_LR + C_RANK].set(gla_w_a2[l])
        ba = row(gla_b_a[l])
        gla_gain4 = row(jnp.tile(gla_norm[l], C_HEADS))
        dn_gain4 = row(jnp.tile(dn_norm[l], D_HEADS))
        acoef = _lane_row(-jnp.exp(dn_a_log[l]), SM_A)
        dtb = _lane_row(dn_dt_bias[l], SM_A)
        cw = dn_conv_w[l]

        zp = _inproj(hp, row(g_mix[l]), w_in_l, 512)
        oa = _attn_prompt(zp, lam_params, row(attn_norm[l]), B, S, 512, lam_init)
        ob, p_hre, p_him = _s5_prompt(zp, sp, s5d, s5wg, s5bg, B, S, 512)
        oc, p_gla_t = _gla_prompt(zp, wa, ba, gla_gain4, B, S, 512)
        od, p_dn, p_conv = _dn_prompt(zp, cw, acoef, dtb, dn_gain4, B, S, 512)
        hp = _merge(hp, oa, ob, oc, od, row(g_mix[l]), wg, b_branch_gate[l], wb, wo, 512)
        hp = _ffn(hp, p_prompt[l].reshape(T, PLE_DIM), row(g_ffn[l]), w1, w2, row(g_ple[l]), wpg, wpp, row(g_final),
                  256, final)
        p_gla = jnp.stack([p_gla_t[:, h * C_DV:(h + 1) * C_DV, h * C_DK:(h + 1) * C_DK] for h in range(C_HEADS)],
                          axis=1).swapaxes(-1, -2)
        st_p.append((zp[:, COL_K:COL_K + 256].reshape(B, S, A_HEADS, 2 * A_DH),
                     zp[:, COL_V:COL_V + 256].reshape(B, S, A_HEADS, A_DV),
                     p_hre.reshape(B, S5_GROUPS, S5_P), p_him.reshape(B, S5_GROUPS, S5_P), p_gla, p_dn, p_conv))

        zs = _inproj(hs, row(g_mix[l]), w_in_l, DB)
        oa_s = _attn_decode(page_table, ck_t, cv_t, l, zs.reshape(DB, 1, Z_W), lam_params, row(attn_norm[l]),
                            lam_init).reshape(DB, 256)
        (ob_s, s_hre, s_him, gq, gk, ga, dq, dk, dv, dgb, s_conv3) = _dec_prep(
            zs, state_ssm_re[l].reshape(DB, S5_STATE), state_ssm_im[l].reshape(DB, S5_STATE),
            state_conv[l].swapaxes(0, 1), sp, s5d, s5wg, s5bg, wa, ba, cw, acoef, dtb)
        col = lambda a, h, dk, dv: jnp.broadcast_to(a.reshape(DB * h, dk, 1), (DB * h, dk, dv))
        s_gla, oc_s = _gla_step(state_gla[l].reshape(DB * C_HEADS, C_DK, C_DV), col(gq, C_HEADS, C_DK, C_DV),
                                col(gk, C_HEADS, C_DK, C_DV), col(ga, C_HEADS, C_DK, C_DV),
                                zs[:, COL_CV:COL_CV + 256].reshape(DB * C_HEADS, C_DV),
                                zs[:, COL_CG:COL_CG + 256].reshape(DB * C_HEADS, C_DV), row(gla_norm[l]), 128)
        lanes = lambda a, off: jnp.broadcast_to(a[:, off:off + D_HEADS].reshape(DB * D_HEADS, 1), (DB * D_HEADS, D_DV))
        s_dn, od_s = _dn_step(state_delta[l].reshape(DB * D_HEADS, D_DK, D_DV), col(dq, D_HEADS, D_DK, D_DV),
                              col(dk, D_HEADS, D_DK, D_DV), dv.reshape(DB * D_HEADS, D_DV), lanes(dgb, SM_A),
                              lanes(dgb, SM_B), zs[:, COL_DZ:COL_DZ + 256].reshape(DB * D_HEADS, D_DV),
                              row(dn_norm[l]), 64)
        hs = _merge(hs, oa_s, ob_s, oc_s.reshape(DB, 256), od_s.reshape(DB, 256), row(g_mix[l]), wg, b_branch_gate[l],
                    wb, wo, DB)
        hs = _ffn(hs, p_sample[l].reshape(DB, PLE_DIM), row(g_ffn[l]), w1, w2, row(g_ple[l]), wpg, wpp, row(g_final),
                  DB, final)
        st_s.append((zs[:, COL_K:COL_K + 256].reshape(DB, 1, A_HEADS, 2 * A_DH),
                     zs[:, COL_V:COL_V + 256].reshape(DB, 1, A_HEADS, A_DV),
                     s_hre.reshape(DB, S5_GROUPS, S5_P), s_him.reshape(DB, S5_GROUPS, S5_P),
                     s_gla.reshape(DB, C_HEADS, C_DK, C_DV), s_dn.reshape(DB, D_HEADS, D_DK, D_DV),
                     s_conv3.swapaxes(0, 1)))

    stk = lambda states, i: jnp.stack([s[i] for s in states])
    return (hp.reshape(B, S, D_MODEL), hs.reshape(DB, 1, D_MODEL),
            stk(st_p, 0), stk(st_p, 1), stk(st_p, 2), stk(st_p, 3), stk(st_p, 4), stk(st_p, 5), stk(st_p, 6),
            stk(st_s, 0), stk(st_s, 1), stk(st_s, 2), stk(st_s, 3), stk(st_s, 4), stk(st_s, 5), stk(st_s, 6))
```

```python
import functools
import math

import numpy as np
import jax
import jax.numpy as jnp
from jax import lax
from jax.experimental import pallas as pl
from jax.experimental.pallas import tpu as pltpu

F32 = jnp.float32
BF16 = jnp.bfloat16
HI = lax.Precision.HIGHEST

D_MODEL = 1024
DEPTH = 2
A_HEADS, A_DH, A_DV = 4, 32, 64
S5_GROUPS, S5_GROUP, S5_P = 16, 16, 64
S5_STATE = S5_GROUPS * S5_P
C_HEADS, C_DK, C_DV, C_RANK = 4, 32, 64, 16
GLA_TAU = 16.0
D_HEADS, D_DK, D_DV = 4, 64, 64
CONV_W = 4
DN_QKV_W = 768
D_FF = 4096
PLE_DIM = 256
EPS = 1e-6
BRANCH_W = 256

Z_W = 2944
COL_Q, COL_K, COL_V, COL_U = 0, 256, 512, 768
COL_CQK, COL_CV, COL_DQKV, COL_CG, COL_DZ, COL_SMALL = 1024, 1280, 1536, 2304, 2560, 2816
SMALL_W = 128
SM_LR, SM_B, SM_A = 0, 16, 20

S5_CHUNK = 64
GLA_SUB = 16
DN_CHUNK = 64
VMEM_LIMIT_BYTES = 56 * 1024 * 1024
NEG_INF = float("-inf")


def _cparams(*sem):
    return pltpu.CompilerParams(dimension_semantics=sem, vmem_limit_bytes=VMEM_LIMIT_BYTES)


def _sigmoid(x):
    return 1.0 / (1.0 + jnp.exp(-x))


def _silu(x):
    return x * _sigmoid(x)


def _softplus(x):
    return jnp.maximum(x, 0.0) + jnp.log(1.0 + jnp.exp(-jnp.abs(x)))


def _gelu_tanh(x):
    return 0.5 * x * (1.0 + jnp.tanh(math.sqrt(2.0 / math.pi) * (x + 0.044715 * (x * x * x))))


def _rms(x, g):
    return x * lax.rsqrt(jnp.mean(x * x, axis=-1, keepdims=True) + EPS) * g


def _dot(a, b, **kw):
    return jnp.dot(a, b, preferred_element_type=F32, **kw)


def _dot_nt(a, b, **kw):
    return lax.dot_general(a, b, (((1,), (1,)), ((), ())), preferred_element_type=F32, **kw)


def _dot_tn(a, b, **kw):
    return lax.dot_general(a, b, (((0,), (0,)), ((), ())), preferred_element_type=F32, **kw)


def _full_spec(shape):
    nd = len(shape)
    return pl.BlockSpec(shape, lambda *_: (0,) * nd)


def _seg_matrix(width, seg):
    i = np.arange(width) // seg
    return jnp.asarray((i[:, None] == i[None, :]).astype(np.float32)).astype(BF16)


def _bf16_parts(x, parts):
    out = []
    for i in range(parts):
        p = x.astype(BF16)
        out.append(p)
        if i + 1 < parts:
            x = x - p.astype(F32)
    return out


def _dot01(x, e, parts):
    acc = None
    for p in _bf16_parts(x, parts):
        t = _dot(p, e)
        acc = t if acc is None else acc + t
    return acc


def _dot01_left(e, x, parts):
    acc = None
    for p in _bf16_parts(x, parts):
        t = _dot(e, p)
        acc = t if acc is None else acc + t
    return acc


def _head_rms(o, seg_ref, gain):
    ms = _dot01(o * o, seg_ref[...], 2) * (1.0 / 64.0)
    return o * lax.rsqrt(ms + EPS) * gain


def _inproj_kernel(x_ref, g_ref, w_ref, z_ref):
    hn = _rms(x_ref[...], g_ref[...]).astype(BF16)
    z_ref[...] = _dot(hn, w_ref[...])


def _inproj(x, g, w, tm):
    T = x.shape[0]
    return pl.pallas_call(
        _inproj_kernel,
        out_shape=jax.ShapeDtypeStruct((T, Z_W), F32),
        grid=(T // tm,),
        in_specs=[pl.BlockSpec((tm, D_MODEL), lambda i: (i, 0)), _full_spec((1, D_MODEL)), _full_spec((D_MODEL, Z_W))],
        out_specs=pl.BlockSpec((tm, Z_W), lambda i: (i, 0)),
        compiler_params=_cparams("parallel"),
        name="inproj",
    )(x, g, w)


def _merge_kernel(h_ref, oa_ref, ob_ref, oc_ref, od_ref, g_ref, wg_ref, bg_ref, wb_ref, wo_ref, out_ref):
    h = h_ref[...]
    hn = _rms(h, g_ref[...]).astype(BF16)
    merged = None
    for n, br in enumerate((oa_ref, ob_ref, oc_ref, od_ref)):
        gate = _sigmoid(_dot(hn, wg_ref[n]) + bg_ref[n:n + 1, :])
        term = gate * _dot(br[...].astype(BF16), wb_ref[n])
        merged = term if merged is None else merged + term
    out_ref[...] = h + _dot(merged.astype(BF16), wo_ref[...])


def _merge(h, oa, ob, oc, od, g, wg, bg, wb, wo, tm):
    T = h.shape[0]
    row = lambda w: pl.BlockSpec((tm, w), lambda i: (i, 0))
    return pl.pallas_call(
        _merge_kernel,
        out_shape=jax.ShapeDtypeStruct((T, D_MODEL), F32),
        grid=(T // tm,),
        in_specs=[row(D_MODEL), row(BRANCH_W), row(BRANCH_W), row(BRANCH_W), row(BRANCH_W),
                  _full_spec((1, D_MODEL)), _full_spec((4, D_MODEL, D_MODEL)), _full_spec((4, D_MODEL)),
                  _full_spec((4, BRANCH_W, D_MODEL)), _full_spec((D_MODEL, D_MODEL))],
        out_specs=row(D_MODEL),
        compiler_params=_cparams("parallel"),
        name="merge",
    )(h, oa, ob, oc, od, g, wg, bg, wb, wo)


def _ffn_kernel(h_ref, p_ref, gf_ref, w1_ref, w2_ref, gp_ref, wpg_ref, wpp_ref, gfin_ref, out_ref, *, final):
    h = h_ref[...]
    hf = _rms(h, gf_ref[...]).astype(BF16)
    acc = h
    step = 1024
    for j in range(D_FF // step):
        a = jnp.maximum(_dot(hf, w1_ref[:, j * step:(j + 1) * step]), 0.0)
        acc = acc + _dot((a * a).astype(BF16), w2_ref[j * step:(j + 1) * step, :])
    pg = _sigmoid(_dot(_rms(acc, gp_ref[...]).astype(BF16), wpg_ref[...]))
    out = acc + pg * _dot(p_ref[...].astype(BF16), wpp_ref[...])
    if final:
        out = _rms(out, gfin_ref[...])
    out_ref[...] = out


def _ffn(h, p, gf, w1, w2, gp, wpg, wpp, gfin, tm, final):
    T = h.shape[0]
    row = lambda w: pl.BlockSpec((tm, w), lambda i: (i, 0))
    return pl.pallas_call(
        functools.partial(_ffn_kernel, final=final),
        out_shape=jax.ShapeDtypeStruct((T, D_MODEL), F32),
        grid=(T // tm,),
        in_specs=[row(D_MODEL), row(PLE_DIM), _full_spec((1, D_MODEL)), _full_spec((D_MODEL, D_FF)),
                  _full_spec((D_FF, D_MODEL)), _full_spec((1, D_MODEL)), _full_spec((D_MODEL, D_MODEL)),
                  _full_spec((PLE_DIM, D_MODEL)), _full_spec((1, D_MODEL))],
        out_specs=row(D_MODEL),
        compiler_params=_cparams("parallel"),
        name="ffn_ple",
    )(h, p, gf, w1, w2, gp, wpg, wpp, gfin)


_SLOPES = [float(s) for s in 2.0 ** (-8.0 * np.arange(1, A_HEADS + 1) / A_HEADS)]
LOG2E = 1.0 / math.log(2.0)


def _lam_value(lq1, lk1, lq2, lk2, lam_init):
    return (jnp.exp(jnp.sum(lq1[...] * lk1[...], axis=-1, keepdims=True))
            - jnp.exp(jnp.sum(lq2[...] * lk2[...], axis=-1, keepdims=True)) + lam_init)


def _attn_prompt_kernel(q_ref, k_ref, v_ref, lq1, lk1, lq2, lk2, gain_ref, o_ref, qm_scr, m_scr, acc_scr,
                        *, tq, lam_init):
    qi = pl.program_id(1)
    ki = pl.program_id(2)

    @pl.when(ki == 0)
    def _init():
        q = q_ref[...] * (A_DH ** -0.5 * LOG2E)
        lane = lax.broadcasted_iota(jnp.int32, q.shape, 1)
        for i in range(2 * A_HEADS):
            lo = (i // 2) * 2 * A_DH + (i % 2) * A_DH
            qm_scr[i] = jnp.where((lane >= lo) & (lane < lo + A_DH), q, 0.0).astype(BF16)
        m_scr[...] = jnp.full(m_scr.shape, NEG_INF, F32)
        acc_scr[...] = jnp.zeros(acc_scr.shape, F32)

    def _step(diag):
        k = k_ref[...].astype(BF16)
        v = v_ref[...].astype(BF16)
        kpos = (lax.broadcasted_iota(jnp.int32, (1, tq), 1) + (ki - qi) * tq).astype(F32)
        ones_col = jnp.where(lax.broadcasted_iota(jnp.int32, (tq, A_DV), 1) == 0, 1.0, 0.0).astype(BF16)
        if diag:
            causal = (lax.broadcasted_iota(jnp.int32, (tq, tq), 1) <= lax.broadcasted_iota(jnp.int32, (tq, tq), 0))
        for h in range(A_HEADS):
            vext = jnp.concatenate([v[:, h * A_DV:(h + 1) * A_DV], ones_col], axis=1)
            bias = kpos * (_SLOPES[h] * LOG2E)
            if diag:
                bias = jnp.where(causal, bias, NEG_INF)
            for half in range(2):
                i = 2 * h + half
                s = _dot_nt(qm_scr[i], k) + bias
                m_prev = m_scr[i]
                m_new = jnp.maximum(m_prev, jnp.max(s, axis=-1, keepdims=True))
                alpha = jnp.exp2(m_prev - m_new)
                p = jnp.exp2(s - jnp.concatenate([m_new] * (tq // 128), axis=1))
                acc_scr[i] = alpha * acc_scr[i] + _dot(p.astype(BF16), vext)
                m_scr[i] = m_new

    @pl.when(ki < qi)
    def _off_diagonal():
        _step(False)

    @pl.when(ki == qi)
    def _diagonal():
        _step(True)
        lam = _lam_value(lq1, lk1, lq2, lk2, lam_init)
        outs = []
        for h in range(A_HEADS):
            a1, a2 = acc_scr[2 * h], acc_scr[2 * h + 1]
            o = a1[:, :A_DV] / a1[:, A_DV:A_DV + 1] - lam * (a2[:, :A_DV] / a2[:, A_DV:A_DV + 1])
            outs.append(_rms(o, gain_ref[...]) * (1.0 - lam_init))
        o_ref[...] = jnp.concatenate(outs, axis=1)


def _attn_prompt(z, lam_params, gain, B, S, tq, lam_init):
    nq = S // tq
    vec = _full_spec((1, A_DH))
    return pl.pallas_call(
        functools.partial(_attn_prompt_kernel, tq=tq, lam_init=lam_init),
        out_shape=jax.ShapeDtypeStruct((B * S, A_HEADS * A_DV), F32),
        grid=(B, nq, nq),
        in_specs=[pl.BlockSpec((tq, 256), lambda b, qi, ki: (b * nq + qi, COL_Q // 256)),
                  pl.BlockSpec((tq, 256), lambda b, qi, ki: (b * nq + jnp.minimum(ki, qi), COL_K // 256)),
                  pl.BlockSpec((tq, 256), lambda b, qi, ki: (b * nq + jnp.minimum(ki, qi), COL_V // 256)),
                  vec, vec, vec, vec, _full_spec((1, A_DV))],
        out_specs=pl.BlockSpec((tq, 256), lambda b, qi, ki: (b * nq + qi, 0)),
        scratch_shapes=[pltpu.VMEM((2 * A_HEADS, tq, 256), BF16), pltpu.VMEM((2 * A_HEADS, tq, 128), F32),
                        pltpu.VMEM((2 * A_HEADS, tq, 128), F32)],
        compiler_params=_cparams("parallel", "parallel", "arbitrary"),
        name="attn_prompt",
    )(z, z, z, *lam_params, gain)


def _attn_decode_kernel(pt_ref, *refs, n_pages, page, lam_init):
    k_refs = refs[:n_pages]
    v_refs = refs[n_pages:2 * n_pages]
    z_ref, lq1, lk1, lq2, lk2, gain_ref, o_ref = refs[2 * n_pages:]
    del pt_ref
    past = n_pages * page
    zrow = z_ref[0]
    q = zrow[:, COL_Q:COL_Q + 256] * (A_DH ** -0.5)
    k_own = zrow[:, COL_K:COL_K + 256]
    v_own = zrow[:, COL_V:COL_V + 256]
    lane = lax.broadcasted_iota(jnp.int32, (2 * A_HEADS, 256), 1)
    r = lax.broadcasted_iota(jnp.int32, (2 * A_HEADS, 256), 0)
    lo = (r % A_HEADS) * 2 * A_DH + (r // A_HEADS) * A_DH
    qm = jnp.where((lane >= lo) & (lane < lo + A_DH), jnp.broadcast_to(q, (2 * A_HEADS, 256)), 0.0)
    qmb = qm.astype(BF16)
    s = jnp.concatenate([_dot(qmb, kr[0, 0].astype(BF16)) for kr in k_refs], axis=1)
    rr = lax.broadcasted_iota(jnp.int32, (2 * A_HEADS, 1), 0) % A_HEADS
    slope = jnp.zeros((2 * A_HEADS, 1), F32)
    for h in range(A_HEADS):
        slope = jnp.where(rr == h, _SLOPES[h], slope)
    kpos = lax.broadcasted_iota(jnp.int32, (2 * A_HEADS, past), 1).astype(F32)
    s = s - slope * (float(past) - kpos)
    s_own = jnp.sum(qm * k_own, axis=-1, keepdims=True)
    m = jnp.maximum(jnp.max(s, axis=-1, keepdims=True), s_own)
    p = jnp.exp(s - m)
    p_own = jnp.exp(s_own - m)
    denom = jnp.sum(p, axis=-1, keepdims=True) + p_own
    pb = p.astype(BF16)
    o = p_own * v_own
    for j, vr in enumerate(v_refs):
        o = o + _dot_nt(pb[:, j * page:(j + 1) * page], vr[0, 0].astype(BF16))
    o = o / denom
    lam = _lam_value(lq1, lk1, lq2, lk2, lam_init)
    d = o[:A_HEADS] - lam * o[A_HEADS:]
    hl = lax.broadcasted_iota(jnp.int32, (A_HEADS, 256), 1) // A_DV
    hr = lax.broadcasted_iota(jnp.int32, (A_HEADS, 256), 0)
    own = hl == hr
    d = jnp.where(own, d, 0.0)
    ms = jnp.sum(d * d, axis=-1, keepdims=True) * (1.0 / A_DV)
    d = d * lax.rsqrt(ms + EPS)
    gain = jnp.concatenate([gain_ref[...]] * A_HEADS, axis=1)
    o_ref[0] = jnp.sum(d, axis=0, keepdims=True) * gain * (1.0 - lam_init)


def _attn_decode(page_table, ck, cv, layer, z3, lam_params, gain, lam_init):
    DB, n_pages = page_table.shape
    page = ck.shape[-1]
    kv_specs = [pl.BlockSpec((1, 1, 256, page), functools.partial(lambda b, pt, j: (layer, pt[b, j], 0, 0), j=j))
                for j in range(n_pages)]
    vec = pl.BlockSpec((1, A_DH), lambda b, pt: (0, 0))
    grid_spec = pltpu.PrefetchScalarGridSpec(
        num_scalar_prefetch=1,
        grid=(DB,),
        in_specs=kv_specs + kv_specs + [pl.BlockSpec((1, 1, Z_W), lambda b, pt: (b, 0, 0)), vec, vec, vec, vec,
                                        pl.BlockSpec((1, A_DV), lambda b, pt: (0, 0))],
        out_specs=pl.BlockSpec((1, 1, 256), lambda b, pt: (b, 0, 0)),
    )
    return pl.pallas_call(
        functools.partial(_attn_decode_kernel, n_pages=n_pages, page=page, lam_init=lam_init),
        out_shape=jax.ShapeDtypeStruct((DB, 1, 256), F32),
        grid_spec=grid_spec,
        compiler_params=_cparams("arbitrary"),
        name="attn_decode",
    )(page_table, *([ck] * n_pages), *([cv] * n_pages), z3, *lam_params, gain)


def _s5_params(lam_re, lam_im, log_dt, b_re, b_im, c_re, c_im):
    dt = jnp.exp(log_dt)[:, None]
    lr, li = lam_re, lam_im
    a, th = lr * dt, li * dt
    mag = jnp.exp(a)
    ab_re, ab_im = mag * jnp.cos(th), mag * jnp.sin(th)
    den = lr * lr + li * li
    nr, ni = ab_re - 1.0, ab_im
    f_re = (nr * lr + ni * li) / den
    f_im = (ni * lr - nr * li) / den
    bb_re = f_re[..., None] * b_re - f_im[..., None] * b_im
    bb_im = f_re[..., None] * b_im + f_im[..., None] * b_re
    eye = jnp.eye(S5_GROUPS, dtype=F32)
    expand_b = lambda m: jnp.einsum("gpn,gh->gnhp", m, eye).reshape(S5_GROUPS * S5_GROUP, S5_STATE)
    wb = jnp.concatenate([expand_b(bb_re), expand_b(bb_im)], axis=1)
    expand_c = lambda m: jnp.einsum("gnp,gh->gphn", m, eye).reshape(S5_STATE, S5_GROUPS * S5_GROUP)
    cm = jnp.concatenate([expand_c(c_re), -expand_c(c_im)], axis=0)
    s = jnp.arange(S5_CHUNK, dtype=F32)[:, None]
    af, tf = a.reshape(1, S5_STATE), th.reshape(1, S5_STATE)
    en_re, en_im = jnp.exp(-s * af) * jnp.cos(s * tf), -jnp.exp(-s * af) * jnp.sin(s * tf)
    ep_re, ep_im = jnp.exp(s * af) * jnp.cos(s * tf), jnp.exp(s * af) * jnp.sin(s * tf)
    return dict(wb=wb, cm=cm, ab_re=ab_re.reshape(1, S5_STATE), ab_im=ab_im.reshape(1, S5_STATE),
                en_re=en_re, en_im=en_im, ep_re=ep_re, ep_im=ep_im)


def _s5_out(y, u, d_ref, wg_ref, bg_ref):
    y = _gelu_tanh(y + d_ref[...] * u)
    return y * _sigmoid(_dot(y.astype(BF16), wg_ref[...]) + bg_ref[...])


def _s5_prompt_kernel(u_ref, wb_ref, cm_ref, abr_ref, abi_ref, enr_ref, eni_ref, epr_ref, epi_ref,
                      d_ref, wg_ref, bg_ref, o_ref, hre_ref, him_ref, hr_scr, hi_scr, *, ts):
    t = pl.program_id(1)
    L = S5_CHUNK
    CB = 256

    @pl.when(t == 0)
    def _init():
        hr_scr[...] = jnp.zeros(hr_scr.shape, F32)
        hi_scr[...] = jnp.zeros(hi_scr.shape, F32)

    tri = (lax.broadcasted_iota(jnp.int32, (L, L), 0) >= lax.broadcasted_iota(jnp.int32, (L, L), 1))
    tri = jnp.where(tri, 1.0, 0.0).astype(BF16)
    for c in range(ts // L):
        u = u_ref[c * L:(c + 1) * L, :]
        ub = u.astype(BF16)
        y = None
        for cb in range(S5_STATE // CB):
            sl = slice(cb * CB, (cb + 1) * CB)
            bur = _dot(ub, wb_ref[:, cb * CB:(cb + 1) * CB])
            bui = _dot(ub, wb_ref[:, S5_STATE + cb * CB:S5_STATE + (cb + 1) * CB])
            enr, eni = enr_ref[:, sl], eni_ref[:, sl]
            xr = bur * enr - bui * eni
            xi = bur * eni + bui * enr
            cs = _dot(tri, jnp.concatenate([xr, xi], axis=1).astype(BF16))
            h0r, h0i = hr_scr[:, sl], hi_scr[:, sl]
            abr, abi = abr_ref[:, sl], abi_ref[:, sl]
            gr = cs[:, :CB] + (abr * h0r - abi * h0i)
            gi = cs[:, CB:] + (abr * h0i + abi * h0r)
            epr, epi = epr_ref[:, sl], epi_ref[:, sl]
            hr = epr * gr - epi * gi
            hi = epr * gi + epi * gr
            hr_scr[:, sl] = hr[L - 1:L, :]
            hi_scr[:, sl] = hi[L - 1:L, :]
            yc = (_dot(hr.astype(BF16), cm_ref[cb * CB:(cb + 1) * CB, :])
                  + _dot(hi.astype(BF16), cm_ref[S5_STATE + cb * CB:S5_STATE + (cb + 1) * CB, :]))
            y = yc if y is None else y + yc
        o_ref[c * L:(c + 1) * L, :] = _s5_out(y, u, d_ref, wg_ref, bg_ref)
    hre_ref[0] = hr_scr[...]
    him_ref[0] = hi_scr[...]


def _s5_prompt(z, sp, d, wg, bg, B, S, ts):
    nt = S // ts
    tab = _full_spec((S5_CHUNK, S5_STATE))
    vec = _full_spec((1, S5_STATE))
    st = pl.BlockSpec((1, 1, S5_STATE), lambda b, t: (b, 0, 0))
    return pl.pallas_call(
        functools.partial(_s5_prompt_kernel, ts=ts),
        out_shape=(jax.ShapeDtypeStruct((B * S, 256), F32), jax.ShapeDtypeStruct((B, 1, S5_STATE), F32),
                   jax.ShapeDtypeStruct((B, 1, S5_STATE), F32)),
        grid=(B, nt),
        in_specs=[pl.BlockSpec((ts, 256), lambda b, t: (b * nt + t, COL_U // 256)),
                  _full_spec((256, 2 * S5_STATE)), _full_spec((2 * S5_STATE, 256)), vec, vec, tab, tab, tab, tab,
                  _full_spec((1, 256)), _full_spec((256, 256)), _full_spec((1, 256))],
        out_specs=(pl.BlockSpec((ts, 256), lambda b, t: (b * nt + t, 0)), st, st),
        scratch_shapes=[pltpu.VMEM((1, S5_STATE), F32), pltpu.VMEM((1, S5_STATE), F32)],
        compiler_params=_cparams("parallel", "arbitrary"),
        name="s5_prompt",
    )(z, sp["wb"].astype(BF16), sp["cm"].astype(BF16), sp["ab_re"], sp["ab_im"], sp["en_re"], sp["en_im"],
      sp["ep_re"], sp["ep_im"], d, wg, bg)


def _gla_log_a(small, wa_ref, ba_ref):
    x = _dot(small, wa_ref[...], precision=HI) + ba_ref[...]
    return (jnp.minimum(x, 0.0) - jnp.log(1.0 + jnp.exp(-jnp.abs(x)))) * (1.0 / GLA_TAU)


def _gla_prompt_kernel(qk_ref, v_ref, cg_ref, sm_ref, wa_ref, ba_ref, gain_ref, segx_ref, seg_ref, bmask_ref,
                       o_ref, st_ref, b_scr, o_scr, st_scr, *, ts):
    t = pl.program_id(1)
    n = GLA_SUB

    @pl.when(t == 0)
    def _init():
        st_scr[...] = jnp.zeros(st_scr.shape, F32)

    log_a = _gla_log_a(sm_ref[...], wa_ref, ba_ref)
    ri = lax.broadcasted_iota(jnp.int32, (ts, ts), 0)
    ci = lax.broadcasted_iota(jnp.int32, (ts, ts), 1)
    tri = jnp.where((ri >= ci) & (ri // n == ci // n), 1.0, 0.0).astype(BF16)
    b_scr[...] = _dot01_left(tri, log_a, 3)
    rows = lax.broadcasted_iota(jnp.int32, (n, 128), 0)

    def body(c, carry):
        r0 = pl.multiple_of(c * n, n)
        qk = qk_ref[pl.ds(r0, n), :]
        q = qk[:, :128] * (C_DK ** -0.5)
        k = qk[:, 128:]
        v = v_ref[pl.ds(r0, n), :]
        b = b_scr[pl.ds(r0, n), :]
        parts = []
        for s in range(n):
            e = jnp.exp(jnp.where(rows >= s, b - b[s:s + 1, :], NEG_INF))
            parts.append(q * k[s:s + 1, :] * e)
        pall = jnp.concatenate(parts, axis=0).astype(BF16)
        aexp = _dot(pall, segx_ref[...])
        acc = None
        for s in range(n):
            term = aexp[s * n:(s + 1) * n, :] * v[s:s + 1, :]
            acc = term if acc is None else acc + term
        st = st_scr[...]
        o = acc + _dot_nt((q * jnp.exp(b)).astype(BF16), st.astype(BF16))
        b_last = b[n - 1:n, :]
        kt = k * jnp.exp(b_last - b)
        upd = _dot_tn(v.astype(BF16), kt.astype(BF16))
        st_scr[...] = st * jnp.exp(b_last) + upd * bmask_ref[...]
        o_scr[pl.ds(r0, n), :] = o
        return carry

    lax.fori_loop(0, ts // n, body, 0)
    o = o_scr[...]
    cg = cg_ref[...]
    o_ref[...] = _head_rms(o, seg_ref, gain_ref[...]) * _silu(cg)
    st_ref[0] = st_scr[...]


def _gla_consts():
    hk = np.arange(128) // C_DK
    hv = np.arange(256) // C_DV
    segx = jnp.asarray((hk[:, None] == hv[None, :]).astype(np.float32)).astype(BF16)
    bmask = jnp.asarray((hv[:, None] == hk[None, :]).astype(np.float32))
    return segx, bmask


def _gla_prompt(z, wa, ba, gain, B, S, ts):
    nt = S // ts
    segx, bmask = _gla_consts()
    return pl.pallas_call(
        functools.partial(_gla_prompt_kernel, ts=ts),
        out_shape=(jax.ShapeDtypeStruct((B * S, 256), F32), jax.ShapeDtypeStruct((B, 256, 128), F32)),
        grid=(B, nt),
        in_specs=[pl.BlockSpec((ts, 256), lambda b, t: (b * nt + t, COL_CQK // 256)),
                  pl.BlockSpec((ts, 256), lambda b, t: (b * nt + t, COL_CV // 256)),
                  pl.BlockSpec((ts, 256), lambda b, t: (b * nt + t, COL_CG // 256)),
                  pl.BlockSpec((ts, SMALL_W), lambda b, t: (b * nt + t, COL_SMALL // SMALL_W)),
                  _full_spec((SMALL_W, 128)), _full_spec((1, 128)), _full_spec((1, 256)),
                  _full_spec((128, 256)), _full_spec((256, 256)), _full_spec((256, 128))],
        out_specs=(pl.BlockSpec((ts, 256), lambda b, t: (b * nt + t, 0)),
                   pl.BlockSpec((1, 256, 128), lambda b, t: (b, 0, 0))),
        scratch_shapes=[pltpu.VMEM((ts, 128), F32), pltpu.VMEM((ts, 256), F32), pltpu.VMEM((256, 128), F32)],
        compiler_params=_cparams("parallel", "arbitrary"),
        name="gla_prompt",
    )(z, z, z, z, wa, ba, gain, segx, _seg_matrix(256, 64), bmask)


def _dn_gates(small, acoef_ref, dtb_ref):
    beta = _sigmoid(small)
    g = acoef_ref[...] * _softplus(small + dtb_ref[...])
    return beta, g


def _dn_qkv(y, seg_ref):
    y = _silu(y)
    q, k, v = y[:, :256], y[:, 256:512], y[:, 512:768]
    nq = _dot01(q * q, seg_ref[...], 2)
    nk = _dot01(k * k, seg_ref[...], 2)
    q = q * lax.rsqrt(nq + EPS) * (D_DK ** -0.5)
    k = k * lax.rsqrt(nk + EPS)
    return q, k, v


def _dn_prompt_kernel(x_ref, dz_ref, sm_ref, cw_ref, acoef_ref, dtb_ref, gain_ref, seg_ref, esel_ref,
                      o_ref, st_ref, cs_ref, xp_scr, q_scr, k_scr, v_scr, gb_scr, ge_scr, be_scr,
                      u_scr, w_scr, att_scr, o_scr, st_scr, *, ts):
    t = pl.program_id(1)
    C = DN_CHUNK
    PADR = 8
    W = D_HEADS * D_DK

    @pl.when(t == 0)
    def _init():
        st_scr[...] = jnp.zeros(st_scr.shape, F32)
        xp_scr[0:PADR, :] = jnp.zeros((PADR, DN_QKV_W), F32)

    x = x_ref[...]
    xp_scr[PADR:PADR + ts, :] = x
    y = cw_ref[3:4, :] * x
    for i in range(CONV_W - 1):
        y = y + cw_ref[i:i + 1, :] * xp_scr[PADR - 3 + i:PADR - 3 + i + ts, :]
    xp_scr[PADR - 3:PADR, :] = x[ts - 3:ts, :]
    cs_ref[0] = x[ts - 3:ts, :]
    q, k, v = _dn_qkv(y, seg_ref)
    q_scr[...] = q
    k_scr[...] = k
    v_scr[...] = v
    beta, g = _dn_gates(sm_ref[...], acoef_ref, dtb_ref)
    ri = lax.broadcasted_iota(jnp.int32, (ts, ts), 0)
    ci = lax.broadcasted_iota(jnp.int32, (ts, ts), 1)
    tri = jnp.where((ri >= ci) & (ri // C == ci // C), 1.0, 0.0).astype(BF16)
    gcum = _dot01_left(tri, g, 3)
    lane = lax.broadcasted_iota(jnp.int32, (ts, 128), 1)
    gb = jnp.where(lane >= SM_A, gcum, beta)
    gb_scr[...] = gb
    ex = _dot01(gb, esel_ref[...], 3)
    ge_scr[...] = ex[:, :W]
    be_scr[...] = ex[:, W:]

    rr = lax.broadcasted_iota(jnp.int32, (W, W), 0)
    cc = lax.broadcasted_iota(jnp.int32, (W, W), 1)
    same_head = (rr // C) == (cc // C)
    lower_incl = same_head & (rr >= cc)
    lower_strict = same_head & (rr > cc)
    eye = jnp.where(rr == cc, 1.0, 0.0)
    blocks = [(rr // w) == (cc // w) for w in (8, 16, 32, C)]

    def stack_heads(a):
        return jnp.where(same_head, jnp.concatenate([a] * D_HEADS, axis=0), 0.0)

    def collapse(a):
        return a[0:C] + a[C:2 * C] + a[2 * C:3 * C] + a[3 * C:4 * C]

    def solve_body(c, carry):
        r0 = pl.multiple_of(c * C, C)
        qc = q_scr[pl.ds(r0, C), :]
        kc = k_scr[pl.ds(r0, C), :]
        vc = v_scr[pl.ds(r0, C), :]
        gbc = gb_scr[pl.ds(r0, C), :]
        ge = ge_scr[pl.ds(r0, C), :]
        be = be_scr[pl.ds(r0, C), :]
        gbt = gbc.T
        gcol = jnp.concatenate([gbc[:, SM_A + h:SM_A + h + 1] for h in range(D_HEADS)], axis=0)
        grow = jnp.concatenate([gbt[SM_A + h:SM_A + h + 1, :] for h in range(D_HEADS)], axis=1)
        decay = jnp.exp(jnp.where(lower_incl, gcol - grow, NEG_INF))
        kst = stack_heads(kc).astype(BF16)
        kq = _dot_nt(jnp.concatenate([stack_heads(kc * be), stack_heads(qc)], axis=0).astype(BF16), kst)
        m = jnp.where(lower_strict, -(kq[:W] * decay), 0.0)
        m1 = jnp.where(blocks[0], m, 0.0)
        m1b = m1.astype(BF16)
        m2 = _dot(m1b, m1b)
        s1 = eye + m1
        r2 = _dot(m2.astype(BF16), jnp.concatenate([m2, s1], axis=1).astype(BF16))
        s3 = s1 + r2[:, W:]
        tm = s3 + _dot(r2[:, :W].astype(BF16), s3.astype(BF16))
        for inner, outer in zip(blocks[:-1], blocks[1:]):
            tmb = tm.astype(BF16)
            off = jnp.where(outer & jnp.logical_not(inner), m, 0.0).astype(BF16)
            tm = tm + _dot(_dot(tmb, off).astype(BF16), tmb)
        rhs = jnp.concatenate([stack_heads(vc * be), stack_heads(kc * (be * jnp.exp(ge)))], axis=1)
        uw = _dot(tm.astype(BF16), rhs.astype(BF16))
        u_scr[c] = collapse(uw[:, :W])
        w_scr[c] = collapse(uw[:, W:])
        att_scr[c] = (kq[W:] * decay).astype(BF16)
        return carry

    lax.fori_loop(0, ts // C, solve_body, 0, unroll=2)

    def state_body(c, carry):
        r0 = pl.multiple_of(c * C, C)
        qc = q_scr[pl.ds(r0, C), :]
        kc = k_scr[pl.ds(r0, C), :]
        ge = ge_scr[pl.ds(r0, C), :]
        glast = ge[C - 1:C, :]
        s = st_scr[...]
        wq = jnp.concatenate([w_scr[c], qc * jnp.exp(ge)], axis=0)
        sw = _dot(wq.astype(BF16), s.astype(BF16))
        v_new = u_scr[c] - sw[:C]
        ov = _dot(att_scr[c], stack_heads(v_new).astype(BF16))
        o_scr[pl.ds(r0, C), :] = sw[C:] + collapse(ov)
        k_dec = kc * jnp.exp(glast - ge)
        upd = _dot_tn(k_dec.astype(BF16), v_new.astype(BF16))
        st_scr[...] = s * jnp.exp(glast) + jnp.where(same_head, upd, 0.0)
        return carry

    lax.fori_loop(0, ts // C, state_body, 0)
    o_ref[...] = _head_rms(o_scr[...], seg_ref, gain_ref[...]) * _silu(dz_ref[...])
    st_ref[0] = st_scr[...]


def _dn_esel():
    e = np.zeros((128, 512), np.float32)
    for h in range(D_HEADS):
        e[SM_A + h, h * 64:(h + 1) * 64] = 1.0
        e[SM_B + h, 256 + h * 64:256 + (h + 1) * 64] = 1.0
    return jnp.asarray(e).astype(BF16)


def _dn_prompt(z, cw, acoef, dtb, gain, B, S, ts):
    nt = S // ts
    nc = ts // DN_CHUNK
    W = D_HEADS * D_DK
    return pl.pallas_call(
        functools.partial(_dn_prompt_kernel, ts=ts),
        out_shape=(jax.ShapeDtypeStruct((B * S, 256), F32), jax.ShapeDtypeStruct((B, W, W), F32),
                   jax.ShapeDtypeStruct((B, CONV_W - 1, DN_QKV_W), F32)),
        grid=(B, nt),
        in_specs=[pl.BlockSpec((ts, DN_QKV_W), lambda b, t: (b * nt + t, COL_DQKV // DN_QKV_W)),
                  pl.BlockSpec((ts, 256), lambda b, t: (b * nt + t, COL_DZ // 256)),
                  pl.BlockSpec((ts, SMALL_W), lambda b, t: (b * nt + t, COL_SMALL // SMALL_W)),
                  _full_spec((CONV_W, DN_QKV_W)), _full_spec((1, 128)), _full_spec((1, 128)), _full_spec((1, 256)),
                  _full_spec((256, 256)), _full_spec((128, 512))],
        out_specs=(pl.BlockSpec((ts, 256), lambda b, t: (b * nt + t, 0)),
                   pl.BlockSpec((1, W, W), lambda b, t: (b, 0, 0)),
                   pl.BlockSpec((1, CONV_W - 1, DN_QKV_W), lambda b, t: (b, 0, 0))),
        scratch_shapes=[pltpu.VMEM((8 + ts, DN_QKV_W), F32), pltpu.VMEM((ts, 256), F32), pltpu.VMEM((ts, 256), F32),
                        pltpu.VMEM((ts, 256), F32), pltpu.VMEM((ts, 128), F32), pltpu.VMEM((ts, W), F32),
                        pltpu.VMEM((ts, W), F32), pltpu.VMEM((nc, DN_CHUNK, W), F32), pltpu.VMEM((nc, DN_CHUNK, W), F32),
                        pltpu.VMEM((nc, W, W), BF16), pltpu.VMEM((ts, 256), F32), pltpu.VMEM((W, W), F32)],
        compiler_params=_cparams("parallel", "arbitrary"),
        name="dn_prompt",
    )(z, z, z, cw, acoef, dtb, gain, _seg_matrix(256, 64), _dn_esel())


def _dec_prep_kernel(z_ref, h0r_ref, h0i_ref, conv_ref, wb_ref, cm_ref, abr_ref, abi_ref, d_ref, wg_ref, bg_ref,
                     wa_ref, ba_ref, cw_ref, acoef_ref, dtb_ref, seg_ref,
                     ob_ref, hr_ref, hi_ref, gq_ref, gk_ref, ga_ref, dq_ref, dk_ref, dv_ref, dgb_ref, cs_ref):
    z = z_ref[...]
    u = z[:, COL_U:COL_U + 256]
    bu = _dot(u, wb_ref[...], precision=HI)
    bur, bui = bu[:, :S5_STATE], bu[:, S5_STATE:]
    h0r, h0i = h0r_ref[...], h0i_ref[...]
    abr, abi = abr_ref[...], abi_ref[...]
    hr = abr * h0r - abi * h0i + bur
    hi = abr * h0i + abi * h0r + bui
    hr_ref[...] = hr
    hi_ref[...] = hi
    y = _dot(jnp.concatenate([hr, hi], axis=1).astype(BF16), cm_ref[...])
    ob_ref[...] = _s5_out(y, u, d_ref, wg_ref, bg_ref)
    small = z[:, COL_SMALL:COL_SMALL + SMALL_W]
    gq_ref[...] = z[:, COL_CQK:COL_CQK + 128] * (C_DK ** -0.5)
    gk_ref[...] = z[:, COL_CQK + 128:COL_CQK + 256]
    ga_ref[...] = _gla_log_a(small, wa_ref, ba_ref)
    x = z[:, COL_DQKV:COL_DQKV + DN_QKV_W]
    y = cw_ref[3:4, :] * x
    for i in range(CONV_W - 1):
        y = y + cw_ref[i:i + 1, :] * conv_ref[i]
    cs_ref[0] = conv_ref[1]
    cs_ref[1] = conv_ref[2]
    cs_ref[2] = x
    q, k, v = _dn_qkv(y, seg_ref)
    dq_ref[...] = q
    dk_ref[...] = k
    dv_ref[...] = v
    beta, g = _dn_gates(small, acoef_ref, dtb_ref)
    lane = lax.broadcasted_iota(jnp.int32, small.shape, 1)
    dgb_ref[...] = jnp.where(lane >= SM_A, jnp.exp(g), beta)


def _dec_prep(z, h0r, h0i, conv3, sp, d, wg, bg, wa, ba, cw, acoef, dtb):
    DB = z.shape[0]
    sds = lambda *s: jax.ShapeDtypeStruct(s, F32)
    out_shape = (sds(DB, 256), sds(DB, S5_STATE), sds(DB, S5_STATE), sds(DB, 128), sds(DB, 128), sds(DB, 128),
                 sds(DB, 256), sds(DB, 256), sds(DB, 256), sds(DB, 128), sds(CONV_W - 1, DB, DN_QKV_W))
    return pl.pallas_call(
        _dec_prep_kernel,
        out_shape=out_shape,
        compiler_params=pltpu.CompilerParams(vmem_limit_bytes=VMEM_LIMIT_BYTES),
        name="dec_prep",
    )(z, h0r, h0i, conv3, sp["wb"], sp["cm"].astype(BF16), sp["ab_re"], sp["ab_im"], d, wg, bg, wa, ba, cw, acoef, dtb,
      _seg_matrix(256, 64))


def _gla_step_kernel(s_ref, q_ref, k_ref, a_ref, v_ref, cg_ref, gain_ref, sn_ref, o_ref):
    v = v_ref[...]
    sn = s_ref[...] * jnp.exp(a_ref[...]) + k_ref[...] * v[:, None, :]
    sn_ref[...] = sn
    o = jnp.sum(q_ref[...] * sn, axis=1)
    o_ref[...] = _rms(o, gain_ref[...]) * _silu(cg_ref[...])


def _gla_step(s, qcol, kcol, acol, v, cg, gain, tb):
    n = s.shape[0]
    big = pl.BlockSpec((tb, C_DK, C_DV), lambda i: (i, 0, 0))
    row = pl.BlockSpec((tb, C_DV), lambda i: (i, 0))
    return pl.pallas_call(
        _gla_step_kernel,
        out_shape=(jax.ShapeDtypeStruct(s.shape, F32), jax.ShapeDtypeStruct((n, C_DV), F32)),
        grid=(n // tb,),
        in_specs=[big, big, big, big, row, row, _full_spec((1, C_DV))],
        out_specs=(big, row),
        compiler_params=_cparams("parallel"),
        name="gla_step",
    )(s, qcol, kcol, acol, v, cg, gain)


def _dn_step_kernel(s_ref, q_ref, k_ref, v_ref, eg_ref, beta_ref, dz_ref, gain_ref, sn_ref, o_ref):
    s = s_ref[...]
    kcol = k_ref[...]
    eg = eg_ref[...]
    ks = jnp.sum(kcol * s, axis=1)
    v_new = beta_ref[...] * (v_ref[...] - eg * ks)
    sn = s * eg[:, None, :] + kcol * v_new[:, None, :]
    sn_ref[...] = sn
    o = jnp.sum(q_ref[...] * sn, axis=1)
    o_ref[...] = _rms(o, gain_ref[...]) * _silu(dz_ref[...])


def _dn_step(s, qcol, kcol, v, eg, beta, dz, gain, tb):
    n = s.shape[0]
    big = pl.BlockSpec((tb, D_DK, D_DV), lambda i: (i, 0, 0))
    row = pl.BlockSpec((tb, D_DV), lambda i: (i, 0))
    return pl.pallas_call(
        _dn_step_kernel,
        out_shape=(jax.ShapeDtypeStruct(s.shape, F32), jax.ShapeDtypeStruct((n, D_DV), F32)),
        grid=(n // tb,),
        in_specs=[big, big, big, row, row, row, row, _full_spec((1, D_DV))],
        out_specs=(big, row),
        compiler_params=_cparams("parallel"),
        name="dn_step",
    )(s, qcol, kcol, v, eg, beta, dz, gain)


def _prep_w_in(w):
    o = np.cumsum([0, 256, 256, 256, 256, 128, 128, 256, 256, 16, 768, 4, 4, 256])
    q, k, v, u, cq, ck, cv, cg, lr, dqkv, db, da, dz = [w[:, o[i]:o[i + 1]] for i in range(13)]
    pad = jnp.zeros((w.shape[0], SMALL_W - 24), w.dtype)
    return jnp.concatenate([q, k, v, u, cq, ck, cv, dqkv, cg, dz, lr, db, da, pad], axis=1).astype(BF16)


def _lane_row(vals, offset):
    return jnp.zeros((1, 128), F32).at[0, offset:offset + vals.shape[0]].set(vals)


def kernel(x_prompt, x_sample, cache_k, cache_v, state_ssm_re, state_ssm_im, state_gla, state_delta, state_conv, page_table, p_prompt, p_sample, g_mix, w_in, lam_q1, lam_k1, lam_q2, lam_k2, attn_norm, s5_lam_re, s5_lam_im, s5_log_dt, s5_b_re, s5_b_im, s5_c_re, s5_c_im, s5_d, s5_w_glu, s5_b_glu, gla_w_a2, gla_b_a, gla_norm, dn_conv_w, dn_a_log, dn_dt_bias, dn_norm, w_branch_gate, b_branch_gate, w_branch, w_out, g_ffn, w_ff1, w_ff2, g_ple, w_ple_gate, w_ple_proj, g_final):
    B, S, _ = x_prompt.shape
    DB = x_sample.shape[0]
    T = B * S
    n_pool, page = cache_k.shape[1], cache_k.shape[2]
    hp = x_prompt.reshape(T, D_MODEL)
    hs = x_sample.reshape(DB, D_MODEL)
    row = lambda a: a.reshape(1, -1)
    ck_t = cache_k.transpose(0, 1, 3, 4, 2).reshape(DEPTH, n_pool, A_HEADS * 2 * A_DH, page)
    cv_t = cache_v.transpose(0, 1, 3, 4, 2).reshape(DEPTH, n_pool, A_HEADS * A_DV, page)
    st_p, st_s = [], []
    for l in range(DEPTH):
        lam_init = 0.8 - 0.6 * math.exp(-0.3 * l)
        final = l == DEPTH - 1
        w_in_l = _prep_w_in(w_in[l])
        wg, wb, wo = w_branch_gate[l].astype(BF16), w_branch[l].astype(BF16), w_out[l].astype(BF16)
        w1, w2 = w_ff1[l].astype(BF16), w_ff2[l].astype(BF16)
        wpg, wpp = w_ple_gate[l].astype(BF16), w_ple_proj[l].astype(BF16)
        lam_params = (row(lam_q1[l]), row(lam_k1[l]), row(lam_q2[l]), row(lam_k2[l]))
        sp = _s5_params(s5_lam_re[l], s5_lam_im[l], s5_log_dt[l], s5_b_re[l], s5_b_im[l], s5_c_re[l], s5_c_im[l])
        s5d, s5wg, s5bg = row(s5_d[l]), s5_w_glu[l].astype(BF16), row(s5_b_glu[l])
        wa = jnp.zeros((SMALL_W, 128), F32).at[SM_LR:SM_LR + C_RANK].set(gla_w_a2[l])
        ba = row(gla_b_a[l])
        gla_gain4 = row(jnp.tile(gla_norm[l], C_HEADS))
        dn_gain4 = row(jnp.tile(dn_norm[l], D_HEADS))
        acoef = _lane_row(-jnp.exp(dn_a_log[l]), SM_A)
        dtb = _lane_row(dn_dt_bias[l], SM_A)
        cw = dn_conv_w[l]

        zp = _inproj(hp, row(g_mix[l]), w_in_l, 512)
        oa = _attn_prompt(zp, lam_params, row(attn_norm[l]), B, S, 512, lam_init)
        ob, p_hre, p_him = _s5_prompt(zp, sp, s5d, s5wg, s5bg, B, S, 512)
        oc, p_gla_t = _gla_prompt(zp, wa, ba, gla_gain4, B, S, 512)
        od, p_dn_bd, p_conv = _dn_prompt(zp, cw, acoef, dtb, dn_gain4, B, S, 512)
        p_dn = jnp.stack([p_dn_bd[:, h * D_DK:(h + 1) * D_DK, h * D_DV:(h + 1) * D_DV] for h in range(D_HEADS)], axis=1)
        hp = _merge(hp, oa, ob, oc, od, row(g_mix[l]), wg, b_branch_gate[l], wb, wo, 512)
        hp = _ffn(hp, p_prompt[l].reshape(T, PLE_DIM), row(g_ffn[l]), w1, w2, row(g_ple[l]), wpg, wpp, row(g_final),
                  256, final)
        p_gla = jnp.stack([p_gla_t[:, h * C_DV:(h + 1) * C_DV, h * C_DK:(h + 1) * C_DK] for h in range(C_HEADS)],
                          axis=1).swapaxes(-1, -2)
        st_p.append((zp[:, COL_K:COL_K + 256].reshape(B, S, A_HEADS, 2 * A_DH),
                     zp[:, COL_V:COL_V + 256].reshape(B, S, A_HEADS, A_DV),
                     p_hre.reshape(B, S5_GROUPS, S5_P), p_him.reshape(B, S5_GROUPS, S5_P), p_gla, p_dn, p_conv))

        zs = _inproj(hs, row(g_mix[l]), w_in_l, DB)
        oa_s = _attn_decode(page_table, ck_t, cv_t, l, zs.reshape(DB, 1, Z_W), lam_params, row(attn_norm[l]),
                            lam_init).reshape(DB, 256)
        (ob_s, s_hre, s_him, gq, gk, ga, dq, dk, dv, dgb, s_conv3) = _dec_prep(
            zs, state_ssm_re[l].reshape(DB, S5_STATE), state_ssm_im[l].reshape(DB, S5_STATE),
            state_conv[l].swapaxes(0, 1), sp, s5d, s5wg, s5bg, wa, ba, cw, acoef, dtb)
        col = lambda a, h, dk, dv: jnp.broadcast_to(a.reshape(DB * h, dk, 1), (DB * h, dk, dv))
        s_gla, oc_s = _gla_step(state_gla[l].reshape(DB * C_HEADS, C_DK, C_DV), col(gq, C_HEADS, C_DK, C_DV),
                                col(gk, C_HEADS, C_DK, C_DV), col(ga, C_HEADS, C_DK, C_DV),
                                zs[:, COL_CV:COL_CV + 256].reshape(DB * C_HEADS, C_DV),
                                zs[:, COL_CG:COL_CG + 256].reshape(DB * C_HEADS, C_DV), row(gla_norm[l]), 128)
        lanes = lambda a, off: jnp.broadcast_to(a[:, off:off + D_HEADS].reshape(DB * D_HEADS, 1), (DB * D_HEADS, D_DV))
        s_dn, od_s = _dn_step(state_delta[l].reshape(DB * D_HEADS, D_DK, D_DV), col(dq, D_HEADS, D_DK, D_DV),
                              col(dk, D_HEADS, D_DK, D_DV), dv.reshape(DB * D_HEADS, D_DV), lanes(dgb, SM_A),
                              lanes(dgb, SM_B), zs[:, COL_DZ:COL_DZ + 256].reshape(DB * D_HEADS, D_DV),
                              row(dn_norm[l]), 64)
        hs = _merge(hs, oa_s, ob_s, oc_s.reshape(DB, 256), od_s.reshape(DB, 256), row(g_mix[l]), wg, b_branch_gate[l],
                    wb, wo, DB)
        hs = _ffn(hs, p_sample[l].reshape(DB, PLE_DIM), row(g_ffn[l]), w1, w2, row(g_ple[l]), wpg, wpp, row(g_final),
                  DB, final)
        st_s.append((zs[:, COL_K:COL_K + 256].reshape(DB, 1, A_HEADS, 2 * A_DH),
                     zs[:, COL_V:COL_V + 256].reshape(DB, 1, A_HEADS, A_DV),
                     s_hre.reshape(DB, S5_GROUPS, S5_P), s_him.reshape(DB, S5_GROUPS, S5_P),
                     s_gla.reshape(DB, C_HEADS, C_DK, C_DV), s_dn.reshape(DB, D_HEADS, D_DK, D_DV),
                     s_conv3.swapaxes(0, 1)))

    stk = lambda states, i: jnp.stack([s[i] for s in states])
    return (hp.reshape(B, S, D_MODEL), hs.reshape(DB, 1, D_MODEL),
            stk(st_p, 0), stk(st_p, 1), stk(st_p, 2), stk(st_p, 3), stk(st_p, 4), stk(st_p, 5), stk(st_p, 6),
            stk(st_s, 0), stk(st_s, 1), stk(st_s, 2), stk(st_s, 3), stk(st_s, 4), stk(st_s, 5), stk(st_s, 6))
```

```python
import functools
import math

import numpy as np
import jax
import jax.numpy as jnp
from jax import lax
from jax.experimental import pallas as pl
from jax.experimental.pallas import tpu as pltpu

F32 = jnp.float32
BF16 = jnp.bfloat16
HI = lax.Precision.HIGHEST

D_MODEL = 1024
DEPTH = 2
A_HEADS, A_DH, A_DV = 4, 32, 64
S5_GROUPS, S5_GROUP, S5_P = 16, 16, 64
S5_STATE = S5_GROUPS * S5_P
C_HEADS, C_DK, C_DV, C_RANK = 4, 32, 64, 16
GLA_TAU = 16.0
D_HEADS, D_DK, D_DV = 4, 64, 64
CONV_W = 4
DN_QKV_W = 768
D_FF = 4096
PLE_DIM = 256
EPS = 1e-6
BRANCH_W = 256

Z_W = 2944
COL_Q, COL_K, COL_V, COL_U = 0, 256, 512, 768
COL_CQK, COL_CV, COL_DQKV, COL_CG, COL_DZ, COL_SMALL = 1024, 1280, 1536, 2304, 2560, 2816
SMALL_W = 128
SM_LR, SM_B, SM_A = 0, 16, 20

S5_SCAN = 64
S5_LEVELS = 6
S5_PAD = S5_SCAN // 2
GLA_SUB = 16
DN_CHUNK = 64
VMEM_LIMIT_BYTES = 56 * 1024 * 1024
NEG_INF = float("-inf")


def _cparams(*sem):
    return pltpu.CompilerParams(dimension_semantics=sem, vmem_limit_bytes=VMEM_LIMIT_BYTES)


def _sigmoid(x):
    return 1.0 / (1.0 + jnp.exp(-x))


def _silu(x):
    return x * _sigmoid(x)


def _softplus(x):
    return jnp.maximum(x, 0.0) + jnp.log(1.0 + jnp.exp(-jnp.abs(x)))


def _gelu_tanh(x):
    return 0.5 * x * (1.0 + jnp.tanh(math.sqrt(2.0 / math.pi) * (x + 0.044715 * (x * x * x))))


def _rms(x, g):
    return x * lax.rsqrt(jnp.mean(x * x, axis=-1, keepdims=True) + EPS) * g


def _dot(a, b, **kw):
    return jnp.dot(a, b, preferred_element_type=F32, **kw)


def _dot_nt(a, b, **kw):
    return lax.dot_general(a, b, (((1,), (1,)), ((), ())), preferred_element_type=F32, **kw)


def _dot_tn(a, b, **kw):
    return lax.dot_general(a, b, (((0,), (0,)), ((), ())), preferred_element_type=F32, **kw)


def _full_spec(shape):
    nd = len(shape)
    return pl.BlockSpec(shape, lambda *_: (0,) * nd)


def _seg_matrix(width, seg):
    i = np.arange(width) // seg
    return jnp.asarray((i[:, None] == i[None, :]).astype(np.float32)).astype(BF16)


def _bf16_parts(x, parts):
    out = []
    for i in range(parts):
        p = x.astype(BF16)
        out.append(p)
        if i + 1 < parts:
            x = x - p.astype(F32)
    return out


def _dot01(x, e, parts):
    acc = None
    for p in _bf16_parts(x, parts):
        t = _dot(p, e)
        acc = t if acc is None else acc + t
    return acc


def _dot01_left(e, x, parts):
    acc = None
    for p in _bf16_parts(x, parts):
        t = _dot(e, p)
        acc = t if acc is None else acc + t
    return acc


def _head_rms(o, seg_ref, gain):
    ms = _dot01(o * o, seg_ref[...], 2) * (1.0 / 64.0)
    return o * lax.rsqrt(ms + EPS) * gain


def _inproj_kernel(x_ref, g_ref, w_ref, z_ref):
    hn = _rms(x_ref[...], g_ref[...]).astype(BF16)
    z_ref[...] = _dot(hn, w_ref[...])


def _inproj(x, g, w, tm):
    T = x.shape[0]
    return pl.pallas_call(
        _inproj_kernel,
        out_shape=jax.ShapeDtypeStruct((T, Z_W), F32),
        grid=(T // tm,),
        in_specs=[pl.BlockSpec((tm, D_MODEL), lambda i: (i, 0)), _full_spec((1, D_MODEL)), _full_spec((D_MODEL, Z_W))],
        out_specs=pl.BlockSpec((tm, Z_W), lambda i: (i, 0)),
        compiler_params=_cparams("parallel"),
        name="inproj",
    )(x, g, w)


def _merge_kernel(h_ref, oa_ref, ob_ref, oc_ref, od_ref, g_ref, wg_ref, bg_ref, wb_ref, wo_ref, out_ref):
    h = h_ref[...]
    hn = _rms(h, g_ref[...]).astype(BF16)
    merged = None
    for n, br in enumerate((oa_ref, ob_ref, oc_ref, od_ref)):
        gate = _sigmoid(_dot(hn, wg_ref[n]) + bg_ref[n:n + 1, :])
        term = gate * _dot(br[...].astype(BF16), wb_ref[n])
        merged = term if merged is None else merged + term
    out_ref[...] = h + _dot(merged.astype(BF16), wo_ref[...])


def _merge(h, oa, ob, oc, od, g, wg, bg, wb, wo, tm):
    T = h.shape[0]
    row = lambda w: pl.BlockSpec((tm, w), lambda i: (i, 0))
    return pl.pallas_call(
        _merge_kernel,
        out_shape=jax.ShapeDtypeStruct((T, D_MODEL), F32),
        grid=(T // tm,),
        in_specs=[row(D_MODEL), row(BRANCH_W), row(BRANCH_W), row(BRANCH_W), row(BRANCH_W),
                  _full_spec((1, D_MODEL)), _full_spec((4, D_MODEL, D_MODEL)), _full_spec((4, D_MODEL)),
                  _full_spec((4, BRANCH_W, D_MODEL)), _full_spec((D_MODEL, D_MODEL))],
        out_specs=row(D_MODEL),
        compiler_params=_cparams("parallel"),
        name="merge",
    )(h, oa, ob, oc, od, g, wg, bg, wb, wo)


def _ffn_kernel(h_ref, p_ref, gf_ref, w1_ref, w2_ref, gp_ref, wpg_ref, wpp_ref, gfin_ref, out_ref, *, final):
    h = h_ref[...]
    hf = _rms(h, gf_ref[...]).astype(BF16)
    acc = h
    step = 1024
    for j in range(D_FF // step):
        a = jnp.maximum(_dot(hf, w1_ref[:, j * step:(j + 1) * step]), 0.0)
        acc = acc + _dot((a * a).astype(BF16), w2_ref[j * step:(j + 1) * step, :])
    pg = _sigmoid(_dot(_rms(acc, gp_ref[...]).astype(BF16), wpg_ref[...]))
    out = acc + pg * _dot(p_ref[...].astype(BF16), wpp_ref[...])
    if final:
        out = _rms(out, gfin_ref[...])
    out_ref[...] = out


def _ffn(h, p, gf, w1, w2, gp, wpg, wpp, gfin, tm, final):
    T = h.shape[0]
    row = lambda w: pl.BlockSpec((tm, w), lambda i: (i, 0))
    return pl.pallas_call(
        functools.partial(_ffn_kernel, final=final),
        out_shape=jax.ShapeDtypeStruct((T, D_MODEL), F32),
        grid=(T // tm,),
        in_specs=[row(D_MODEL), row(PLE_DIM), _full_spec((1, D_MODEL)), _full_spec((D_MODEL, D_FF)),
                  _full_spec((D_FF, D_MODEL)), _full_spec((1, D_MODEL)), _full_spec((D_MODEL, D_MODEL)),
                  _full_spec((PLE_DIM, D_MODEL)), _full_spec((1, D_MODEL))],
        out_specs=row(D_MODEL),
        compiler_params=_cparams("parallel"),
        name="ffn_ple",
    )(h, p, gf, w1, w2, gp, wpg, wpp, gfin)


_SLOPES = [float(s) for s in 2.0 ** (-8.0 * np.arange(1, A_HEADS + 1) / A_HEADS)]
LOG2E = 1.0 / math.log(2.0)


def _lam_value(lq1, lk1, lq2, lk2, lam_init):
    return (jnp.exp(jnp.sum(lq1[...] * lk1[...], axis=-1, keepdims=True))
            - jnp.exp(jnp.sum(lq2[...] * lk2[...], axis=-1, keepdims=True)) + lam_init)


def _attn_prompt_kernel(qi_ref, ki_ref, q_ref, k_ref, v_ref, lq1, lk1, lq2, lk2, gain_ref, o_ref,
                        qm_scr, m_scr, acc_scr, *, tq, lam_init):
    pair = pl.program_id(1)
    qi = qi_ref[pair]
    ki = ki_ref[pair]

    @pl.when(ki == 0)
    def _init():
        q = q_ref[...] * (A_DH ** -0.5 * LOG2E)
        lane = lax.broadcasted_iota(jnp.int32, q.shape, 1)
        for i in range(2 * A_HEADS):
            lo = (i // 2) * 2 * A_DH + (i % 2) * A_DH
            qm_scr[i] = jnp.where((lane >= lo) & (lane < lo + A_DH), q, 0.0).astype(BF16)
        m_scr[...] = jnp.full(m_scr.shape, NEG_INF, F32)
        acc_scr[...] = jnp.zeros(acc_scr.shape, F32)

    def _step(diag):
        k = k_ref[...].astype(BF16)
        vt = v_ref[...].T
        kpos = (lax.broadcasted_iota(jnp.int32, (tq, 1), 0) + (ki - qi) * tq).astype(F32)
        ones_row = jnp.where(lax.broadcasted_iota(jnp.int32, (A_DV, tq), 0) == 0, 1.0, 0.0)
        if diag:
            causal = (lax.broadcasted_iota(jnp.int32, (tq, tq), 0) <= lax.broadcasted_iota(jnp.int32, (tq, tq), 1))
        for h in range(A_HEADS):
            vext = jnp.concatenate([vt[h * A_DV:(h + 1) * A_DV, :], ones_row], axis=0).astype(BF16)
            bias = kpos * (_SLOPES[h] * LOG2E)
            if diag:
                bias = jnp.where(causal, bias, NEG_INF)
            for half in range(2):
                i = 2 * h + half
                s = _dot_nt(k, qm_scr[i]) + bias
                m_prev = m_scr[i]
                m_new = jnp.maximum(m_prev, jnp.max(s, axis=0, keepdims=True))
                alpha = jnp.exp2(m_prev - m_new)
                p = jnp.exp2(s - m_new)
                acc_scr[i] = alpha * acc_scr[i] + _dot(vext, p.astype(BF16))
                m_scr[i] = m_new

    @pl.when(ki < qi)
    def _off_diagonal():
        _step(False)

    @pl.when(ki == qi)
    def _diagonal():
        _step(True)
        lam = _lam_value(lq1, lk1, lq2, lk2, lam_init)
        outs = []
        for h in range(A_HEADS):
            a1, a2 = acc_scr[2 * h], acc_scr[2 * h + 1]
            o = a1[:A_DV, :] / a1[A_DV:A_DV + 1, :] - lam * (a2[:A_DV, :] / a2[A_DV:A_DV + 1, :])
            outs.append(o * lax.rsqrt(jnp.mean(o * o, axis=0, keepdims=True) + EPS))
        gain = jnp.concatenate([gain_ref[...]] * A_HEADS, axis=1)
        o_ref[...] = jnp.concatenate(outs, axis=0).T * gain * (1.0 - lam_init)


def _attn_prompt(z, lam_params, gain, B, S, tq, lam_init):
    nq = S // tq
    pairs = [(qi, ki) for qi in range(nq) for ki in range(qi + 1)]
    qi_arr = jnp.asarray([p[0] for p in pairs], jnp.int32)
    ki_arr = jnp.asarray([p[1] for p in pairs], jnp.int32)
    vec = pl.BlockSpec((1, A_DH), lambda b, p, qa, ka: (0, 0))
    grid_spec = pltpu.PrefetchScalarGridSpec(
        num_scalar_prefetch=2,
        grid=(B, len(pairs)),
        in_specs=[pl.BlockSpec((tq, 256), lambda b, p, qa, ka: (b * nq + qa[p], COL_Q // 256)),
                  pl.BlockSpec((tq, 256), lambda b, p, qa, ka: (b * nq + ka[p], COL_K // 256)),
                  pl.BlockSpec((tq, 256), lambda b, p, qa, ka: (b * nq + ka[p], COL_V // 256)),
                  vec, vec, vec, vec, pl.BlockSpec((1, A_DV), lambda b, p, qa, ka: (0, 0))],
        out_specs=pl.BlockSpec((tq, 256), lambda b, p, qa, ka: (b * nq + qa[p], 0)),
        scratch_shapes=[pltpu.VMEM((2 * A_HEADS, tq, 256), BF16), pltpu.VMEM((2 * A_HEADS, 1, tq), F32),
                        pltpu.VMEM((2 * A_HEADS, 128, tq), F32)],
    )
    return pl.pallas_call(
        functools.partial(_attn_prompt_kernel, tq=tq, lam_init=lam_init),
        out_shape=jax.ShapeDtypeStruct((B * S, A_HEADS * A_DV), F32),
        grid_spec=grid_spec,
        compiler_params=_cparams("parallel", "arbitrary"),
        name="attn_prompt",
    )(qi_arr, ki_arr, z, z, z, *lam_params, gain)


def _attn_decode_kernel(pt_ref, *refs, n_pages, page, lam_init):
    k_refs = refs[:n_pages]
    v_refs = refs[n_pages:2 * n_pages]
    z_ref, lq1, lk1, lq2, lk2, gain_ref, o_ref = refs[2 * n_pages:]
    del pt_ref
    past = n_pages * page
    zrow = z_ref[0]
    q = zrow[:, COL_Q:COL_Q + 256] * (A_DH ** -0.5)
    k_own = zrow[:, COL_K:COL_K + 256]
    v_own = zrow[:, COL_V:COL_V + 256]
    lane = lax.broadcasted_iota(jnp.int32, (2 * A_HEADS, 256), 1)
    r = lax.broadcasted_iota(jnp.int32, (2 * A_HEADS, 256), 0)
    lo = (r % A_HEADS) * 2 * A_DH + (r // A_HEADS) * A_DH
    qm = jnp.where((lane >= lo) & (lane < lo + A_DH), jnp.broadcast_to(q, (2 * A_HEADS, 256)), 0.0)
    qmb = qm.astype(BF16)
    s = jnp.concatenate([_dot(qmb, kr[0, 0].astype(BF16)) for kr in k_refs], axis=1)
    rr = lax.broadcasted_iota(jnp.int32, (2 * A_HEADS, 1), 0) % A_HEADS
    slope = jnp.zeros((2 * A_HEADS, 1), F32)
    for h in range(A_HEADS):
        slope = jnp.where(rr == h, _SLOPES[h], slope)
    kpos = lax.broadcasted_iota(jnp.int32, (2 * A_HEADS, past), 1).astype(F32)
    s = s - slope * (float(past) - kpos)
    s_own = jnp.sum(qm * k_own, axis=-1, keepdims=True)
    m = jnp.maximum(jnp.max(s, axis=-1, keepdims=True), s_own)
    p = jnp.exp(s - m)
    p_own = jnp.exp(s_own - m)
    denom = jnp.sum(p, axis=-1, keepdims=True) + p_own
    pb = p.astype(BF16)
    o = p_own * v_own
    for j, vr in enumerate(v_refs):
        o = o + _dot_nt(pb[:, j * page:(j + 1) * page], vr[0, 0].astype(BF16))
    o = o / denom
    lam = _lam_value(lq1, lk1, lq2, lk2, lam_init)
    d = o[:A_HEADS] - lam * o[A_HEADS:]
    hl = lax.broadcasted_iota(jnp.int32, (A_HEADS, 256), 1) // A_DV
    hr = lax.broadcasted_iota(jnp.int32, (A_HEADS, 256), 0)
    own = hl == hr
    d = jnp.where(own, d, 0.0)
    ms = jnp.sum(d * d, axis=-1, keepdims=True) * (1.0 / A_DV)
    d = d * lax.rsqrt(ms + EPS)
    gain = jnp.concatenate([gain_ref[...]] * A_HEADS, axis=1)
    o_ref[0] = jnp.sum(d, axis=0, keepdims=True) * gain * (1.0 - lam_init)


def _attn_decode(page_table, ck, cv, layer, z3, lam_params, gain, lam_init):
    DB, n_pages = page_table.shape
    page = ck.shape[-1]
    kv_specs = [pl.BlockSpec((1, 1, 256, page), functools.partial(lambda b, pt, j: (layer, pt[b, j], 0, 0), j=j))
                for j in range(n_pages)]
    vec = pl.BlockSpec((1, A_DH), lambda b, pt: (0, 0))
    grid_spec = pltpu.PrefetchScalarGridSpec(
        num_scalar_prefetch=1,
        grid=(DB,),
        in_specs=kv_specs + kv_specs + [pl.BlockSpec((1, 1, Z_W), lambda b, pt: (b, 0, 0)), vec, vec, vec, vec,
                                        pl.BlockSpec((1, A_DV), lambda b, pt: (0, 0))],
        out_specs=pl.BlockSpec((1, 1, 256), lambda b, pt: (b, 0, 0)),
    )
    return pl.pallas_call(
        functools.partial(_attn_decode_kernel, n_pages=n_pages, page=page, lam_init=lam_init),
        out_shape=jax.ShapeDtypeStruct((DB, 1, 256), F32),
        grid_spec=grid_spec,
        compiler_params=_cparams("arbitrary"),
        name="attn_decode",
    )(page_table, *([ck] * n_pages), *([cv] * n_pages), z3, *lam_params, gain)


def _s5_params(lam_re, lam_im, log_dt, b_re, b_im, c_re, c_im):
    dt = jnp.exp(log_dt)[:, None]
    lr, li = lam_re, lam_im
    a, th = lr * dt, li * dt
    mag = jnp.exp(a)
    ab_re, ab_im = mag * jnp.cos(th), mag * jnp.sin(th)
    den = lr * lr + li * li
    nr, ni = ab_re - 1.0, ab_im
    f_re = (nr * lr + ni * li) / den
    f_im = (ni * lr - nr * li) / den
    bb_re = f_re[..., None] * b_re - f_im[..., None] * b_im
    bb_im = f_re[..., None] * b_im + f_im[..., None] * b_re
    eye = jnp.eye(S5_GROUPS, dtype=F32)
    expand_b = lambda m: jnp.einsum("gpn,gh->gnhp", m, eye).reshape(S5_GROUPS * S5_GROUP, S5_STATE)
    wb = jnp.concatenate([expand_b(bb_re), expand_b(bb_im)], axis=1)
    expand_c = lambda m: jnp.einsum("gnp,gh->gphn", m, eye).reshape(S5_STATE, S5_GROUPS * S5_GROUP)
    cm = jnp.concatenate([expand_c(c_re), -expand_c(c_im)], axis=0)
    pr, pi = ab_re.reshape(1, S5_STATE), ab_im.reshape(1, S5_STATE)
    res, ims = [pr], [pi]
    for _ in range(S5_LEVELS - 1):
        pr, pi = pr * pr - pi * pi, 2.0 * pr * pi
        res.append(pr)
        ims.append(pi)
    return dict(wb=wb, cm=cm, ab_re=res[0], ab_im=ims[0], pow_re=jnp.concatenate(res, axis=0),
                pow_im=jnp.concatenate(ims, axis=0))


def _s5_out(y, u, d_ref, wg_ref, bg_ref):
    y = _gelu_tanh(y + d_ref[...] * u)
    return y * _sigmoid(_dot(y.astype(BF16), wg_ref[...]) + bg_ref[...])


def _s5_prompt_kernel(u_ref, wb_ref, cm_ref, pr_ref, pi_ref, d_ref, wg_ref, bg_ref, o_ref, hre_ref, him_ref,
                      bu_scr, hb_scr, sr_scr, si_scr, hr_scr, hi_scr, *, ts):
    t = pl.program_id(1)
    n = S5_SCAN
    P = S5_STATE
    pad = S5_PAD

    @pl.when(t == 0)
    def _init():
        hr_scr[...] = jnp.zeros(hr_scr.shape, F32)
        hi_scr[...] = jnp.zeros(hi_scr.shape, F32)
        sr_scr[0:pad, :] = jnp.zeros((pad, P), F32)
        si_scr[0:pad, :] = jnp.zeros((pad, P), F32)

    u = u_ref[...]
    bu_scr[...] = _dot(u.astype(BF16), wb_ref[...])

    def chunk(c, carry):
        r0 = pl.multiple_of(c * n, n)
        sr_scr[pad:pad + n, :] = bu_scr[pl.ds(r0, n), 0:P]
        si_scr[pad:pad + n, :] = bu_scr[pl.ds(r0, n), P:2 * P]
        ar, ai = pr_ref[0:1, :], pi_ref[0:1, :]
        h0r, h0i = hr_scr[...], hi_scr[...]
        sr_scr[pad:pad + 1, :] = sr_scr[pad:pad + 1, :] + (ar * h0r - ai * h0i)
        si_scr[pad:pad + 1, :] = si_scr[pad:pad + 1, :] + (ar * h0i + ai * h0r)
        for lv in range(S5_LEVELS):
            d = 1 << lv
            ar, ai = pr_ref[lv:lv + 1, :], pi_ref[lv:lv + 1, :]
            cr, ci = sr_scr[pad:pad + n, :], si_scr[pad:pad + n, :]
            qr, qi = sr_scr[pad - d:pad - d + n, :], si_scr[pad - d:pad - d + n, :]
            nr, ni = cr + (ar * qr - ai * qi), ci + (ar * qi + ai * qr)
            if lv + 1 < S5_LEVELS:
                sr_scr[pad:pad + n, :] = nr
                si_scr[pad:pad + n, :] = ni
        hr_scr[...] = nr[n - 1:n, :]
        hi_scr[...] = ni[n - 1:n, :]
        hb_scr[pl.ds(r0, n), 0:P] = nr.astype(BF16)
        hb_scr[pl.ds(r0, n), P:2 * P] = ni.astype(BF16)
        return carry

    lax.fori_loop(0, ts // n, chunk, 0)
    y = _dot(hb_scr[...], cm_ref[...])
    o_ref[...] = _s5_out(y, u, d_ref, wg_ref, bg_ref)
    hre_ref[0] = hr_scr[...]
    him_ref[0] = hi_scr[...]


def _s5_prompt(z, sp, d, wg, bg, B, S, ts):
    nt = S // ts
    tab = _full_spec((S5_LEVELS, S5_STATE))
    st = pl.BlockSpec((1, 1, S5_STATE), lambda b, t: (b, 0, 0))
    return pl.pallas_call(
        functools.partial(_s5_prompt_kernel, ts=ts),
        out_shape=(jax.ShapeDtypeStruct((B * S, 256), F32), jax.ShapeDtypeStruct((B, 1, S5_STATE), F32),
                   jax.ShapeDtypeStruct((B, 1, S5_STATE), F32)),
        grid=(B, nt),
        in_specs=[pl.BlockSpec((ts, 256), lambda b, t: (b * nt + t, COL_U // 256)),
                  _full_spec((256, 2 * S5_STATE)), _full_spec((2 * S5_STATE, 256)), tab, tab,
                  _full_spec((1, 256)), _full_spec((256, 256)), _full_spec((1, 256))],
        out_specs=(pl.BlockSpec((ts, 256), lambda b, t: (b * nt + t, 0)), st, st),
        scratch_shapes=[pltpu.VMEM((ts, 2 * S5_STATE), F32), pltpu.VMEM((ts, 2 * S5_STATE), BF16),
                        pltpu.VMEM((S5_PAD + S5_SCAN, S5_STATE), F32), pltpu.VMEM((S5_PAD + S5_SCAN, S5_STATE), F32),
                        pltpu.VMEM((1, S5_STATE), F32), pltpu.VMEM((1, S5_STATE), F32)],
        compiler_params=_cparams("parallel", "arbitrary"),
        name="s5_prompt",
    )(z, sp["wb"].astype(BF16), sp["cm"].astype(BF16), sp["pow_re"], sp["pow_im"], d, wg, bg)


def _gla_log_a(small, wa_ref, ba_ref):
    x = _dot(small, wa_ref[...], precision=HI) + ba_ref[...]
    return (jnp.minimum(x, 0.0) - jnp.log(1.0 + jnp.exp(-jnp.abs(x)))) * (1.0 / GLA_TAU)


def _gla_prompt_kernel(qk_ref, v_ref, cg_ref, sm_ref, wa_ref, ba_ref, gain_ref, segx_ref, seg_ref, bmask_ref,
                       o_ref, st_ref, b_scr, o_scr, st_scr, *, ts):
    t = pl.program_id(1)
    n = GLA_SUB

    @pl.when(t == 0)
    def _init():
        st_scr[...] = jnp.zeros(st_scr.shape, F32)

    log_a = _gla_log_a(sm_ref[...], wa_ref, ba_ref)
    ri = lax.broadcasted_iota(jnp.int32, (ts, ts), 0)
    ci = lax.broadcasted_iota(jnp.int32, (ts, ts), 1)
    tri = jnp.where((ri >= ci) & (ri // n == ci // n), 1.0, 0.0).astype(BF16)
    b_scr[...] = _dot01_left(tri, log_a, 3)
    rows = lax.broadcasted_iota(jnp.int32, (n, 128), 0)

    def body(c, carry):
        r0 = pl.multiple_of(c * n, n)
        qk = qk_ref[pl.ds(r0, n), :]
        q = qk[:, :128] * (C_DK ** -0.5)
        k = qk[:, 128:]
        v = v_ref[pl.ds(r0, n), :]
        b = b_scr[pl.ds(r0, n), :]
        parts = []
        for s in range(n):
            e = jnp.exp(jnp.where(rows >= s, b - b[s:s + 1, :], NEG_INF))
            parts.append(q * k[s:s + 1, :] * e)
        pall = jnp.concatenate(parts, axis=0).astype(BF16)
        aexp = _dot(pall, segx_ref[...])
        acc = None
        for s in range(n):
            term = aexp[s * n:(s + 1) * n, :] * v[s:s + 1, :]
            acc = term if acc is None else acc + term
        st = st_scr[...]
        o = acc + _dot_nt((q * jnp.exp(b)).astype(BF16), st.astype(BF16))
        b_last = b[n - 1:n, :]
        kt = k * jnp.exp(b_last - b)
        upd = _dot_tn(v.astype(BF16), kt.astype(BF16))
        st_scr[...] = st * jnp.exp(b_last) + upd * bmask_ref[...]
        o_scr[pl.ds(r0, n), :] = o
        return carry

    lax.fori_loop(0, ts // n, body, 0)
    o = o_scr[...]
    cg = cg_ref[...]
    o_ref[...] = _head_rms(o, seg_ref, gain_ref[...]) * _silu(cg)
    st_ref[0] = st_scr[...]


def _gla_consts():
    hk = np.arange(128) // C_DK
    hv = np.arange(256) // C_DV
    segx = jnp.asarray((hk[:, None] == hv[None, :]).astype(np.float32)).astype(BF16)
    bmask = jnp.asarray((hv[:, None] == hk[None, :]).astype(np.float32))
    return segx, bmask


def _gla_prompt(z, wa, ba, gain, B, S, ts):
    nt = S // ts
    segx, bmask = _gla_consts()
    return pl.pallas_call(
        functools.partial(_gla_prompt_kernel, ts=ts),
        out_shape=(jax.ShapeDtypeStruct((B * S, 256), F32), jax.ShapeDtypeStruct((B, 256, 128), F32)),
        grid=(B, nt),
        in_specs=[pl.BlockSpec((ts, 256), lambda b, t: (b * nt + t, COL_CQK // 256)),
                  pl.BlockSpec((ts, 256), lambda b, t: (b * nt + t, COL_CV // 256)),
                  pl.BlockSpec((ts, 256), lambda b, t: (b * nt + t, COL_CG // 256)),
                  pl.BlockSpec((ts, SMALL_W), lambda b, t: (b * nt + t, COL_SMALL // SMALL_W)),
                  _full_spec((SMALL_W, 128)), _full_spec((1, 128)), _full_spec((1, 256)),
                  _full_spec((128, 256)), _full_spec((256, 256)), _full_spec((256, 128))],
        out_specs=(pl.BlockSpec((ts, 256), lambda b, t: (b * nt + t, 0)),
                   pl.BlockSpec((1, 256, 128), lambda b, t: (b, 0, 0))),
        scratch_shapes=[pltpu.VMEM((ts, 128), F32), pltpu.VMEM((ts, 256), F32), pltpu.VMEM((256, 128), F32)],
        compiler_params=_cparams("parallel", "arbitrary"),
        name="gla_prompt",
    )(z, z, z, z, wa, ba, gain, segx, _seg_matrix(256, 64), bmask)


def _dn_gates(small, acoef_ref, dtb_ref):
    beta = _sigmoid(small)
    g = acoef_ref[...] * _softplus(small + dtb_ref[...])
    return beta, g


def _dn_qkv(y, seg_ref):
    y = _silu(y)
    q, k, v = y[:, :256], y[:, 256:512], y[:, 512:768]
    nq = _dot01(q * q, seg_ref[...], 2)
    nk = _dot01(k * k, seg_ref[...], 2)
    q = q * lax.rsqrt(nq + EPS) * (D_DK ** -0.5)
    k = k * lax.rsqrt(nk + EPS)
    return q, k, v


def _dn_prompt_kernel(x_ref, dz_ref, sm_ref, cw_ref, acoef_ref, dtb_ref, gain_ref, seg_ref, esel_ref,
                      o_ref, st_ref, cs_ref, xp_scr, q_scr, k_scr, v_scr, gb_scr, ge_scr, be_scr,
                      u_scr, w_scr, att_scr, o_scr, st_scr, *, ts):
    t = pl.program_id(1)
    C = DN_CHUNK
    PADR = 8
    W = D_HEADS * D_DK

    @pl.when(t == 0)
    def _init():
        st_scr[...] = jnp.zeros(st_scr.shape, F32)
        xp_scr[0:PADR, :] = jnp.zeros((PADR, DN_QKV_W), F32)

    x = x_ref[...]
    xp_scr[PADR:PADR + ts, :] = x
    y = cw_ref[3:4, :] * x
    for i in range(CONV_W - 1):
        y = y + cw_ref[i:i + 1, :] * xp_scr[PADR - 3 + i:PADR - 3 + i + ts, :]
    xp_scr[PADR - 3:PADR, :] = x[ts - 3:ts, :]
    cs_ref[0] = x[ts - 3:ts, :]
    q, k, v = _dn_qkv(y, seg_ref)
    q_scr[...] = q
    k_scr[...] = k
    v_scr[...] = v
    beta, g = _dn_gates(sm_ref[...], acoef_ref, dtb_ref)
    ri = lax.broadcasted_iota(jnp.int32, (ts, ts), 0)
    ci = lax.broadcasted_iota(jnp.int32, (ts, ts), 1)
    tri = jnp.where((ri >= ci) & (ri // C == ci // C), 1.0, 0.0).astype(BF16)
    gcum = _dot01_left(tri, g, 3)
    lane = lax.broadcasted_iota(jnp.int32, (ts, 128), 1)
    gb = jnp.where(lane >= SM_A, gcum, beta)
    gb_scr[...] = gb
    ex = _dot01(gb, esel_ref[...], 3)
    ge_scr[...] = ex[:, :W]
    be_scr[...] = ex[:, W:]

    rr = lax.broadcasted_iota(jnp.int32, (W, W), 0)
    cc = lax.broadcasted_iota(jnp.int32, (W, W), 1)
    same_head = (rr // C) == (cc // C)
    lower_incl = same_head & (rr >= cc)
    lower_strict = same_head & (rr > cc)
    eye = jnp.where(rr == cc, 1.0, 0.0)
    blocks = [(rr // w) == (cc // w) for w in (8, 16, 32, C)]

    def stack_heads(a):
        return jnp.where(same_head, jnp.concatenate([a] * D_HEADS, axis=0), 0.0)

    def collapse(a):
        return a[0:C] + a[C:2 * C] + a[2 * C:3 * C] + a[3 * C:4 * C]

    def solve_body(c, carry):
        r0 = pl.multiple_of(c * C, C)
        qc = q_scr[pl.ds(r0, C), :]
        kc = k_scr[pl.ds(r0, C), :]
        vc = v_scr[pl.ds(r0, C), :]
        gbc = gb_scr[pl.ds(r0, C), :]
        ge = ge_scr[pl.ds(r0, C), :]
        be = be_scr[pl.ds(r0, C), :]
        gbt = gbc.T
        gcol = jnp.concatenate([gbc[:, SM_A + h:SM_A + h + 1] for h in range(D_HEADS)], axis=0)
        grow = jnp.concatenate([gbt[SM_A + h:SM_A + h + 1, :] for h in range(D_HEADS)], axis=1)
        decay = jnp.exp(jnp.where(lower_incl, gcol - grow, NEG_INF))
        kst = stack_heads(kc).astype(BF16)
        kq = _dot_nt(jnp.concatenate([stack_heads(kc * be), stack_heads(qc)], axis=0).astype(BF16), kst)
        m = jnp.where(lower_strict, -(kq[:W] * decay), 0.0)
        m1 = jnp.where(blocks[0], m, 0.0)
        m1b = m1.astype(BF16)
        m2 = _dot(m1b, m1b)
        s1 = eye + m1
        r2 = _dot(m2.astype(BF16), jnp.concatenate([m2, s1], axis=1).astype(BF16))
        s3 = s1 + r2[:, W:]
        tm = s3 + _dot(r2[:, :W].astype(BF16), s3.astype(BF16))
        for inner, outer in zip(blocks[:-1], blocks[1:]):
            tmb = tm.astype(BF16)
            off = jnp.where(outer & jnp.logical_not(inner), m, 0.0).astype(BF16)
            tm = tm + _dot(_dot(tmb, off).astype(BF16), tmb)
        rhs = jnp.concatenate([stack_heads(vc * be), stack_heads(kc * (be * jnp.exp(ge)))], axis=1)
        uw = _dot(tm.astype(BF16), rhs.astype(BF16))
        u_scr[c] = collapse(uw[:, :W])
        w_scr[c] = collapse(uw[:, W:])
        att_scr[c] = (kq[W:] * decay).astype(BF16)
        return carry

    lax.fori_loop(0, ts // C, solve_body, 0, unroll=2)

    def state_body(c, carry):
        r0 = pl.multiple_of(c * C, C)
        qc = q_scr[pl.ds(r0, C), :]
        kc = k_scr[pl.ds(r0, C), :]
        ge = ge_scr[pl.ds(r0, C), :]
        glast = ge[C - 1:C, :]
        s = st_scr[...]
        wq = jnp.concatenate([w_scr[c], qc * jnp.exp(ge)], axis=0)
        sw = _dot(wq.astype(BF16), s.astype(BF16))
        v_new = u_scr[c] - sw[:C]
        ov = _dot(att_scr[c], stack_heads(v_new).astype(BF16))
        o_scr[pl.ds(r0, C), :] = sw[C:] + collapse(ov)
        k_dec = kc * jnp.exp(glast - ge)
        upd = _dot_tn(k_dec.astype(BF16), v_new.astype(BF16))
        st_scr[...] = s * jnp.exp(glast) + jnp.where(same_head, upd, 0.0)
        return carry

    lax.fori_loop(0, ts // C, state_body, 0)
    o_ref[...] = _head_rms(o_scr[...], seg_ref, gain_ref[...]) * _silu(dz_ref[...])
    st_ref[0] = st_scr[...]


def _dn_esel():
    e = np.zeros((128, 512), np.float32)
    for h in range(D_HEADS):
        e[SM_A + h, h * 64:(h + 1) * 64] = 1.0
        e[SM_B + h, 256 + h * 64:256 + (h + 1) * 64] = 1.0
    return jnp.asarray(e).astype(BF16)


def _dn_prompt(z, cw, acoef, dtb, gain, B, S, ts):
    nt = S // ts
    nc = ts // DN_CHUNK
    W = D_HEADS * D_DK
    return pl.pallas_call(
        functools.partial(_dn_prompt_kernel, ts=ts),
        out_shape=(jax.ShapeDtypeStruct((B * S, 256), F32), jax.ShapeDtypeStruct((B, W, W), F32),
                   jax.ShapeDtypeStruct((B, CONV_W - 1, DN_QKV_W), F32)),
        grid=(B, nt),
        in_specs=[pl.BlockSpec((ts, DN_QKV_W), lambda b, t: (b * nt + t, COL_DQKV // DN_QKV_W)),
                  pl.BlockSpec((ts, 256), lambda b, t: (b * nt + t, COL_DZ // 256)),
                  pl.BlockSpec((ts, SMALL_W), lambda b, t: (b * nt + t, COL_SMALL // SMALL_W)),
                  _full_spec((CONV_W, DN_QKV_W)), _full_spec((1, 128)), _full_spec((1, 128)), _full_spec((1, 256)),
                  _full_spec((256, 256)), _full_spec((128, 512))],
        out_specs=(pl.BlockSpec((ts, 256), lambda b, t: (b * nt + t, 0)),
                   pl.BlockSpec((1, W, W), lambda b, t: (b, 0, 0)),
                   pl.BlockSpec((1, CONV_W - 1, DN_QKV_W), lambda b, t: (b, 0, 0))),
        scratch_shapes=[pltpu.VMEM((8 + ts, DN_QKV_W), F32), pltpu.VMEM((ts, 256), F32), pltpu.VMEM((ts, 256), F32),
                        pltpu.VMEM((ts, 256), F32), pltpu.VMEM((ts, 128), F32), pltpu.VMEM((ts, W), F32),
                        pltpu.VMEM((ts, W), F32), pltpu.VMEM((nc, DN_CHUNK, W), F32), pltpu.VMEM((nc, DN_CHUNK, W), F32),
                        pltpu.VMEM((nc, W, W), BF16), pltpu.VMEM((ts, 256), F32), pltpu.VMEM((W, W), F32)],
        compiler_params=_cparams("parallel", "arbitrary"),
        name="dn_prompt",
    )(z, z, z, cw, acoef, dtb, gain, _seg_matrix(256, 64), _dn_esel())


def _dec_prep_kernel(z_ref, h0r_ref, h0i_ref, conv_ref, wb_ref, cm_ref, abr_ref, abi_ref, d_ref, wg_ref, bg_ref,
                     wa_ref, ba_ref, cw_ref, acoef_ref, dtb_ref, seg_ref,
                     ob_ref, hr_ref, hi_ref, gq_ref, gk_ref, ga_ref, dq_ref, dk_ref, dv_ref, dgb_ref, cs_ref):
    z = z_ref[...]
    u = z[:, COL_U:COL_U + 256]
    bu = _dot(u, wb_ref[...], precision=HI)
    bur, bui = bu[:, :S5_STATE], bu[:, S5_STATE:]
    h0r, h0i = h0r_ref[...], h0i_ref[...]
    abr, abi = abr_ref[...], abi_ref[...]
    hr = abr * h0r - abi * h0i + bur
    hi = abr * h0i + abi * h0r + bui
    hr_ref[...] = hr
    hi_ref[...] = hi
    y = _dot(jnp.concatenate([hr, hi], axis=1).astype(BF16), cm_ref[...])
    ob_ref[...] = _s5_out(y, u, d_ref, wg_ref, bg_ref)
    small = z[:, COL_SMALL:COL_SMALL + SMALL_W]
    gq_ref[...] = z[:, COL_CQK:COL_CQK + 128] * (C_DK ** -0.5)
    gk_ref[...] = z[:, COL_CQK + 128:COL_CQK + 256]
    ga_ref[...] = _gla_log_a(small, wa_ref, ba_ref)
    x = z[:, COL_DQKV:COL_DQKV + DN_QKV_W]
    y = cw_ref[3:4, :] * x
    for i in range(CONV_W - 1):
        y = y + cw_ref[i:i + 1, :] * conv_ref[i]
    cs_ref[0] = conv_ref[1]
    cs_ref[1] = conv_ref[2]
    cs_ref[2] = x
    q, k, v = _dn_qkv(y, seg_ref)
    dq_ref[...] = q
    dk_ref[...] = k
    dv_ref[...] = v
    beta, g = _dn_gates(small, acoef_ref, dtb_ref)
    lane = lax.broadcasted_iota(jnp.int32, small.shape, 1)
    dgb_ref[...] = jnp.where(lane >= SM_A, jnp.exp(g), beta)


def _dec_prep(z, h0r, h0i, conv3, sp, d, wg, bg, wa, ba, cw, acoef, dtb):
    DB = z.shape[0]
    sds = lambda *s: jax.ShapeDtypeStruct(s, F32)
    out_shape = (sds(DB, 256), sds(DB, S5_STATE), sds(DB, S5_STATE), sds(DB, 128), sds(DB, 128), sds(DB, 128),
                 sds(DB, 256), sds(DB, 256), sds(DB, 256), sds(DB, 128), sds(CONV_W - 1, DB, DN_QKV_W))
    return pl.pallas_call(
        _dec_prep_kernel,
        out_shape=out_shape,
        compiler_params=pltpu.CompilerParams(vmem_limit_bytes=VMEM_LIMIT_BYTES),
        name="dec_prep",
    )(z, h0r, h0i, conv3, sp["wb"], sp["cm"].astype(BF16), sp["ab_re"], sp["ab_im"], d, wg, bg, wa, ba, cw, acoef, dtb,
      _seg_matrix(256, 64))


def _gla_step_kernel(s_ref, q_ref, k_ref, a_ref, v_ref, cg_ref, gain_ref, sn_ref, o_ref):
    v = v_ref[...]
    sn = s_ref[...] * jnp.exp(a_ref[...]) + k_ref[...] * v[:, None, :]
    sn_ref[...] = sn
    o = jnp.sum(q_ref[...] * sn, axis=1)
    o_ref[...] = _rms(o, gain_ref[...]) * _silu(cg_ref[...])


def _gla_step(s, qcol, kcol, acol, v, cg, gain, tb):
    n = s.shape[0]
    big = pl.BlockSpec((tb, C_DK, C_DV), lambda i: (i, 0, 0))
    row = pl.BlockSpec((tb, C_DV), lambda i: (i, 0))
    return pl.pallas_call(
        _gla_step_kernel,
        out_shape=(jax.ShapeDtypeStruct(s.shape, F32), jax.ShapeDtypeStruct((n, C_DV), F32)),
        grid=(n // tb,),
        in_specs=[big, big, big, big, row, row, _full_spec((1, C_DV))],
        out_specs=(big, row),
        compiler_params=_cparams("parallel"),
        name="gla_step",
    )(s, qcol, kcol, acol, v, cg, gain)


def _dn_step_kernel(s_ref, q_ref, k_ref, v_ref, eg_ref, beta_ref, dz_ref, gain_ref, sn_ref, o_ref):
    s = s_ref[...]
    kcol = k_ref[...]
    eg = eg_ref[...]
    ks = jnp.sum(kcol * s, axis=1)
    v_new = beta_ref[...] * (v_ref[...] - eg * ks)
    sn = s * eg[:, None, :] + kcol * v_new[:, None, :]
    sn_ref[...] = sn
    o = jnp.sum(q_ref[...] * sn, axis=1)
    o_ref[...] = _rms(o, gain_ref[...]) * _silu(dz_ref[...])


def _dn_step(s, qcol, kcol, v, eg, beta, dz, gain, tb):
    n = s.shape[0]
    big = pl.BlockSpec((tb, D_DK, D_DV), lambda i: (i, 0, 0))
    row = pl.BlockSpec((tb, D_DV), lambda i: (i, 0))
    return pl.pallas_call(
        _dn_step_kernel,
        out_shape=(jax.ShapeDtypeStruct(s.shape, F32), jax.ShapeDtypeStruct((n, D_DV), F32)),
        grid=(n // tb,),
        in_specs=[big, big, big, row, row, row, row, _full_spec((1, D_DV))],
        out_specs=(big, row),
        compiler_params=_cparams("parallel"),
        name="dn_step",
    )(s, qcol, kcol, v, eg, beta, dz, gain)


def _prep_w_in(w):
    o = np.cumsum([0, 256, 256, 256, 256, 128, 128, 256, 256, 16, 768, 4, 4, 256])
    q, k, v, u, cq, ck, cv, cg, lr, dqkv, db, da, dz = [w[:, o[i]:o[i + 1]] for i in range(13)]
    pad = jnp.zeros((w.shape[0], SMALL_W - 24), w.dtype)
    return jnp.concatenate([q, k, v, u, cq, ck, cv, dqkv, cg, dz, lr, db, da, pad], axis=1).astype(BF16)


def _lane_row(vals, offset):
    return jnp.zeros((1, 128), F32).at[0, offset:offset + vals.shape[0]].set(vals)


def kernel(x_prompt, x_sample, cache_k, cache_v, state_ssm_re, state_ssm_im, state_gla, state_delta, state_conv, page_table, p_prompt, p_sample, g_mix, w_in, lam_q1, lam_k1, lam_q2, lam_k2, attn_norm, s5_lam_re, s5_lam_im, s5_log_dt, s5_b_re, s5_b_im, s5_c_re, s5_c_im, s5_d, s5_w_glu, s5_b_glu, gla_w_a2, gla_b_a, gla_norm, dn_conv_w, dn_a_log, dn_dt_bias, dn_norm, w_branch_gate, b_branch_gate, w_branch, w_out, g_ffn, w_ff1, w_ff2, g_ple, w_ple_gate, w_ple_proj, g_final):
    B, S, _ = x_prompt.shape
    DB = x_sample.shape[0]
    T = B * S
    n_pool, page = cache_k.shape[1], cache_k.shape[2]
    hp = x_prompt.reshape(T, D_MODEL)
    hs = x_sample.reshape(DB, D_MODEL)
    row = lambda a: a.reshape(1, -1)
    ck_t = cache_k.transpose(0, 1, 3, 4, 2).reshape(DEPTH, n_pool, A_HEADS * 2 * A_DH, page)
    cv_t = cache_v.transpose(0, 1, 3, 4, 2).reshape(DEPTH, n_pool, A_HEADS * A_DV, page)
    st_p, st_s = [], []
    for l in range(DEPTH):
        lam_init = 0.8 - 0.6 * math.exp(-0.3 * l)
        final = l == DEPTH - 1
        w_in_l = _prep_w_in(w_in[l])
        wg, wb, wo = w_branch_gate[l].astype(BF16), w_branch[l].astype(BF16), w_out[l].astype(BF16)
        w1, w2 = w_ff1[l].astype(BF16), w_ff2[l].astype(BF16)
        wpg, wpp = w_ple_gate[l].astype(BF16), w_ple_proj[l].astype(BF16)
        lam_params = (row(lam_q1[l]), row(lam_k1[l]), row(lam_q2[l]), row(lam_k2[l]))
        sp = _s5_params(s5_lam_re[l], s5_lam_im[l], s5_log_dt[l], s5_b_re[l], s5_b_im[l], s5_c_re[l], s5_c_im[l])
        s5d, s5wg, s5bg = row(s5_d[l]), s5_w_glu[l].astype(BF16), row(s5_b_glu[l])
        wa = jnp.zeros((SMALL_W, 128), F32).at[SM_LR:SM_LR + C_RANK].set(gla_w_a2[l])
        ba = row(gla_b_a[l])
        gla_gain4 = row(jnp.tile(gla_norm[l], C_HEADS))
        dn_gain4 = row(jnp.tile(dn_norm[l], D_HEADS))
        acoef = _lane_row(-jnp.exp(dn_a_log[l]), SM_A)
        dtb = _lane_row(dn_dt_bias[l], SM_A)
        cw = dn_conv_w[l]

        zp = _inproj(hp, row(g_mix[l]), w_in_l, 512)
        oa = _attn_prompt(zp, lam_params, row(attn_norm[l]), B, S, 512, lam_init)
        ob, p_hre, p_him = _s5_prompt(zp, sp, s5d, s5wg, s5bg, B, S, 512)
        oc, p_gla_t = _gla_prompt(zp, wa, ba, gla_gain4, B, S, 512)
        od, p_dn_bd, p_conv = _dn_prompt(zp, cw, acoef, dtb, dn_gain4, B, S, 512)
        p_dn = jnp.stack([p_dn_bd[:, h * D_DK:(h + 1) * D_DK, h * D_DV:(h + 1) * D_DV] for h in range(D_HEADS)], axis=1)
        hp = _merge(hp, oa, ob, oc, od, row(g_mix[l]), wg, b_branch_gate[l], wb, wo, 512)
        hp = _ffn(hp, p_prompt[l].reshape(T, PLE_DIM), row(g_ffn[l]), w1, w2, row(g_ple[l]), wpg, wpp, row(g_final),
                  256, final)
        p_gla = jnp.stack([p_gla_t[:, h * C_DV:(h + 1) * C_DV, h * C_DK:(h + 1) * C_DK] for h in range(C_HEADS)],
                          axis=1).swapaxes(-1, -2)
        st_p.append((zp[:, COL_K:COL_K + 256].reshape(B, S, A_HEADS, 2 * A_DH),
                     zp[:, COL_V:COL_V + 256].reshape(B, S, A_HEADS, A_DV),
                     p_hre.reshape(B, S5_GROUPS, S5_P), p_him.reshape(B, S5_GROUPS, S5_P), p_gla, p_dn, p_conv))

        zs = _inproj(hs, row(g_mix[l]), w_in_l, DB)
        oa_s = _attn_decode(page_table, ck_t, cv_t, l, zs.reshape(DB, 1, Z_W), lam_params, row(attn_norm[l]),
                            lam_init).reshape(DB, 256)
        (ob_s, s_hre, s_him, gq, gk, ga, dq, dk, dv, dgb, s_conv3) = _dec_prep(
            zs, state_ssm_re[l].reshape(DB, S5_STATE), state_ssm_im[l].reshape(DB, S5_STATE),
            state_conv[l].swapaxes(0, 1), sp, s5d, s5wg, s5bg, wa, ba, cw, acoef, dtb)
        col = lambda a, h, dk, dv: jnp.broadcast_to(a.reshape(DB * h, dk, 1), (DB * h, dk, dv))
        s_gla, oc_s = _gla_step(state_gla[l].reshape(DB * C_HEADS, C_DK, C_DV), col(gq, C_HEADS, C_DK, C_DV),
                                col(gk, C_HEADS, C_DK, C_DV), col(ga, C_HEADS, C_DK, C_DV),
                                zs[:, COL_CV:COL_CV + 256].reshape(DB * C_HEADS, C_DV),
                                zs[:, COL_CG:COL_CG + 256].reshape(DB * C_HEADS, C_DV), row(gla_norm[l]), 128)
        lanes = lambda a, off: jnp.broadcast_to(a[:, off:off + D_HEADS].reshape(DB * D_HEADS, 1), (DB * D_HEADS, D_DV))
        s_dn, od_s = _dn_step(state_delta[l].reshape(DB * D_HEADS, D_DK, D_DV), col(dq, D_HEADS, D_DK, D_DV),
                              col(dk, D_HEADS, D_DK, D_DV), dv.reshape(DB * D_HEADS, D_DV), lanes(dgb, SM_A),
                              lanes(dgb, SM_B), zs[:, COL_DZ:COL_DZ + 256].reshape(DB * D_HEADS, D_DV),
                              row(dn_norm[l]), 64)
        hs = _merge(hs, oa_s, ob_s, oc_s.reshape(DB, 256), od_s.reshape(DB, 256), row(g_mix[l]), wg, b_branch_gate[l],
                    wb, wo, DB)
        hs = _ffn(hs, p_sample[l].reshape(DB, PLE_DIM), row(g_ffn[l]), w1, w2, row(g_ple[l]), wpg, wpp, row(g_final),
                  DB, final)
        st_s.append((zs[:, COL_K:COL_K + 256].reshape(DB, 1, A_HEADS, 2 * A_DH),
                     zs[:, COL_V:COL_V + 256].reshape(DB, 1, A_HEADS, A_DV),
                     s_hre.reshape(DB, S5_GROUPS, S5_P), s_him.reshape(DB, S5_GROUPS, S5_P),
                     s_gla.reshape(DB, C_HEADS, C_DK, C_DV), s_dn.reshape(DB, D_HEADS, D_DK, D_DV),
                     s_conv3.swapaxes(0, 1)))

    stk = lambda states, i: jnp.stack([s[i] for s in states])
    return (hp.reshape(B, S, D_MODEL), hs.reshape(DB, 1, D_MODEL),
            stk(st_p, 0), stk(st_p, 1), stk(st_p, 2), stk(st_p, 3), stk(st_p, 4), stk(st_p, 5), stk(st_p, 6),
            stk(st_s, 0), stk(st_s, 1), stk(st_s, 2), stk(st_s, 3), stk(st_s, 4), stk(st_s, 5), stk(st_s, 6))
```

```python
import functools
import math

import numpy as np
import jax
import jax.numpy as jnp
from jax import lax
from jax.experimental import pallas as pl
from jax.experimental.pallas import tpu as pltpu

F32 = jnp.float32
BF16 = jnp.bfloat16
HI = lax.Precision.HIGHEST

D_MODEL = 1024
DEPTH = 2
A_HEADS, A_DH, A_DV = 4, 32, 64
S5_GROUPS, S5_GROUP, S5_P = 16, 16, 64
S5_STATE = S5_GROUPS * S5_P
C_HEADS, C_DK, C_DV, C_RANK = 4, 32, 64, 16
GLA_TAU = 16.0
D_HEADS, D_DK, D_DV = 4, 64, 64
CONV_W = 4
DN_QKV_W = 768
D_FF = 4096
PLE_DIM = 256
EPS = 1e-6
BRANCH_W = 256

Z_W = 2944
COL_Q, COL_K, COL_V, COL_U = 0, 256, 512, 768
COL_CQK, COL_CV, COL_DQKV, COL_CG, COL_DZ, COL_SMALL = 1024, 1280, 1536, 2304, 2560, 2816
SMALL_W = 128
SM_LR, SM_B, SM_A = 0, 16, 20

S5_SCAN = 64
S5_LEVELS = 6
S5_PAD = S5_SCAN // 2
GLA_SUB = 16
DN_CHUNK = 64
VMEM_LIMIT_BYTES = 56 * 1024 * 1024
NEG_INF = float("-inf")


def _cparams(*sem):
    return pltpu.CompilerParams(dimension_semantics=sem, vmem_limit_bytes=VMEM_LIMIT_BYTES)


def _sigmoid(x):
    return 1.0 / (1.0 + jnp.exp(-x))


def _silu(x):
    return x * _sigmoid(x)


def _softplus(x):
    return jnp.maximum(x, 0.0) + jnp.log(1.0 + jnp.exp(-jnp.abs(x)))


def _gelu_tanh(x):
    return 0.5 * x * (1.0 + jnp.tanh(math.sqrt(2.0 / math.pi) * (x + 0.044715 * (x * x * x))))


def _rms(x, g):
    return x * lax.rsqrt(jnp.mean(x * x, axis=-1, keepdims=True) + EPS) * g


def _dot(a, b, **kw):
    return jnp.dot(a, b, preferred_element_type=F32, **kw)


def _dot_nt(a, b, **kw):
    return lax.dot_general(a, b, (((1,), (1,)), ((), ())), preferred_element_type=F32, **kw)


def _dot_tn(a, b, **kw):
    return lax.dot_general(a, b, (((0,), (0,)), ((), ())), preferred_element_type=F32, **kw)


def _full_spec(shape):
    nd = len(shape)
    return pl.BlockSpec(shape, lambda *_: (0,) * nd)


def _seg_matrix(width, seg):
    i = np.arange(width) // seg
    return jnp.asarray((i[:, None] == i[None, :]).astype(np.float32)).astype(BF16)


def _bf16_parts(x, parts):
    out = []
    for i in range(parts):
        p = x.astype(BF16)
        out.append(p)
        if i + 1 < parts:
            x = x - p.astype(F32)
    return out


def _dot01(x, e, parts):
    acc = None
    for p in _bf16_parts(x, parts):
        t = _dot(p, e)
        acc = t if acc is None else acc + t
    return acc


def _dot01_left(e, x, parts):
    acc = None
    for p in _bf16_parts(x, parts):
        t = _dot(e, p)
        acc = t if acc is None else acc + t
    return acc


def _head_rms(o, seg_ref, gain):
    ms = _dot01(o * o, seg_ref[...], 2) * (1.0 / 64.0)
    return o * lax.rsqrt(ms + EPS) * gain


def _inproj_kernel(x_ref, g_ref, w_ref, z_ref):
    hn = _rms(x_ref[...], g_ref[...]).astype(BF16)
    z_ref[...] = _dot(hn, w_ref[...])


def _inproj(x, g, w, tm):
    T = x.shape[0]
    return pl.pallas_call(
        _inproj_kernel,
        out_shape=jax.ShapeDtypeStruct((T, Z_W), F32),
        grid=(T // tm,),
        in_specs=[pl.BlockSpec((tm, D_MODEL), lambda i: (i, 0)), _full_spec((1, D_MODEL)), _full_spec((D_MODEL, Z_W))],
        out_specs=pl.BlockSpec((tm, Z_W), lambda i: (i, 0)),
        compiler_params=_cparams("parallel"),
        name="inproj",
    )(x, g, w)


def _merge_kernel(h_ref, oa_ref, ob_ref, oc_ref, od_ref, g_ref, wg_ref, bg_ref, wb_ref, wo_ref, out_ref):
    h = h_ref[...]
    hn = _rms(h, g_ref[...]).astype(BF16)
    merged = None
    for n, br in enumerate((oa_ref, ob_ref, oc_ref, od_ref)):
        gate = _sigmoid(_dot(hn, wg_ref[n]) + bg_ref[n:n + 1, :])
        term = gate * _dot(br[...].astype(BF16), wb_ref[n])
        merged = term if merged is None else merged + term
    out_ref[...] = h + _dot(merged.astype(BF16), wo_ref[...])


def _merge(h, oa, ob, oc, od, g, wg, bg, wb, wo, tm):
    T = h.shape[0]
    row = lambda w: pl.BlockSpec((tm, w), lambda i: (i, 0))
    return pl.pallas_call(
        _merge_kernel,
        out_shape=jax.ShapeDtypeStruct((T, D_MODEL), F32),
        grid=(T // tm,),
        in_specs=[row(D_MODEL), row(BRANCH_W), row(BRANCH_W), row(BRANCH_W), row(BRANCH_W),
                  _full_spec((1, D_MODEL)), _full_spec((4, D_MODEL, D_MODEL)), _full_spec((4, D_MODEL)),
                  _full_spec((4, BRANCH_W, D_MODEL)), _full_spec((D_MODEL, D_MODEL))],
        out_specs=row(D_MODEL),
        compiler_params=_cparams("parallel"),
        name="merge",
    )(h, oa, ob, oc, od, g, wg, bg, wb, wo)


def _ffn_kernel(h_ref, p_ref, gf_ref, w1_ref, w2_ref, gp_ref, wpg_ref, wpp_ref, gfin_ref, out_ref, *, final):
    h = h_ref[...]
    hf = _rms(h, gf_ref[...]).astype(BF16)
    acc = h
    step = 1024
    for j in range(D_FF // step):
        a = jnp.maximum(_dot(hf, w1_ref[:, j * step:(j + 1) * step]), 0.0)
        acc = acc + _dot((a * a).astype(BF16), w2_ref[j * step:(j + 1) * step, :])
    pg = _sigmoid(_dot(_rms(acc, gp_ref[...]).astype(BF16), wpg_ref[...]))
    out = acc + pg * _dot(p_ref[...].astype(BF16), wpp_ref[...])
    if final:
        out = _rms(out, gfin_ref[...])
    out_ref[...] = out


def _ffn(h, p, gf, w1, w2, gp, wpg, wpp, gfin, tm, final):
    T = h.shape[0]
    row = lambda w: pl.BlockSpec((tm, w), lambda i: (i, 0))
    return pl.pallas_call(
        functools.partial(_ffn_kernel, final=final),
        out_shape=jax.ShapeDtypeStruct((T, D_MODEL), F32),
        grid=(T // tm,),
        in_specs=[row(D_MODEL), row(PLE_DIM), _full_spec((1, D_MODEL)), _full_spec((D_MODEL, D_FF)),
                  _full_spec((D_FF, D_MODEL)), _full_spec((1, D_MODEL)), _full_spec((D_MODEL, D_MODEL)),
                  _full_spec((PLE_DIM, D_MODEL)), _full_spec((1, D_MODEL))],
        out_specs=row(D_MODEL),
        compiler_params=_cparams("parallel"),
        name="ffn_ple",
    )(h, p, gf, w1, w2, gp, wpg, wpp, gfin)


_SLOPES = [float(s) for s in 2.0 ** (-8.0 * np.arange(1, A_HEADS + 1) / A_HEADS)]
LOG2E = 1.0 / math.log(2.0)


def _lam_value(lq1, lk1, lq2, lk2, lam_init):
    return (jnp.exp(jnp.sum(lq1[...] * lk1[...], axis=-1, keepdims=True))
            - jnp.exp(jnp.sum(lq2[...] * lk2[...], axis=-1, keepdims=True)) + lam_init)


def _attn_prompt_kernel(qi_ref, ki_ref, q_ref, k_ref, v_ref, lq1, lk1, lq2, lk2, gain_ref, o_ref,
                        qm_scr, m_scr, acc_scr, *, tq, lam_init):
    pair = pl.program_id(1)
    qi = qi_ref[pair]
    ki = ki_ref[pair]

    @pl.when(ki == 0)
    def _init():
        q = q_ref[...] * (A_DH ** -0.5 * LOG2E)
        lane = lax.broadcasted_iota(jnp.int32, q.shape, 1)
        for i in range(2 * A_HEADS):
            lo = (i // 2) * 2 * A_DH + (i % 2) * A_DH
            qm_scr[i] = jnp.where((lane >= lo) & (lane < lo + A_DH), q, 0.0).astype(BF16)
        m_scr[...] = jnp.full(m_scr.shape, NEG_INF, F32)
        acc_scr[...] = jnp.zeros(acc_scr.shape, F32)

    def _step(diag):
        k = k_ref[...].astype(BF16)
        vt = v_ref[...].T
        kpos = (lax.broadcasted_iota(jnp.int32, (tq, 1), 0) + (ki - qi) * tq).astype(F32)
        ones_row = jnp.where(lax.broadcasted_iota(jnp.int32, (A_DV, tq), 0) == 0, 1.0, 0.0)
        if diag:
            causal = (lax.broadcasted_iota(jnp.int32, (tq, tq), 0) <= lax.broadcasted_iota(jnp.int32, (tq, tq), 1))
        qq = jnp.concatenate([qm_scr[i] for i in range(2 * A_HEADS)], axis=0)
        s_all = _dot_nt(k, qq)
        for h in range(A_HEADS):
            vext = jnp.concatenate([vt[h * A_DV:(h + 1) * A_DV, :], ones_row], axis=0).astype(BF16)
            bias = kpos * (_SLOPES[h] * LOG2E)
            if diag:
                bias = jnp.where(causal, bias, NEG_INF)
                bias = jnp.concatenate([bias, bias], axis=1)
            i = 2 * h
            s = s_all[:, i * tq:(i + 2) * tq] + bias
            m_prev = jnp.concatenate([m_scr[i], m_scr[i + 1]], axis=1)
            m_new = jnp.maximum(m_prev, jnp.max(s, axis=0, keepdims=True))
            alpha = jnp.exp2(m_prev - m_new)
            p = jnp.exp2(s - m_new)
            pv = _dot(vext, p.astype(BF16))
            acc_scr[i] = alpha[:, :tq] * acc_scr[i] + pv[:, :tq]
            acc_scr[i + 1] = alpha[:, tq:] * acc_scr[i + 1] + pv[:, tq:]
            m_scr[i] = m_new[:, :tq]
            m_scr[i + 1] = m_new[:, tq:]

    @pl.when(ki < qi)
    def _off_diagonal():
        _step(False)

    @pl.when(ki == qi)
    def _diagonal():
        _step(True)
        lam = _lam_value(lq1, lk1, lq2, lk2, lam_init)
        outs = []
        for h in range(A_HEADS):
            a1, a2 = acc_scr[2 * h], acc_scr[2 * h + 1]
            o = a1[:A_DV, :] / a1[A_DV:A_DV + 1, :] - lam * (a2[:A_DV, :] / a2[A_DV:A_DV + 1, :])
            outs.append(o * lax.rsqrt(jnp.mean(o * o, axis=0, keepdims=True) + EPS))
        gain = jnp.concatenate([gain_ref[...]] * A_HEADS, axis=1)
        o_ref[...] = jnp.concatenate(outs, axis=0).T * gain * (1.0 - lam_init)


def _attn_prompt(z, lam_params, gain, B, S, tq, lam_init):
    nq = S // tq
    pairs = [(qi, ki) for qi in range(nq) for ki in range(qi + 1)]
    qi_arr = jnp.asarray([p[0] for p in pairs], jnp.int32)
    ki_arr = jnp.asarray([p[1] for p in pairs], jnp.int32)
    vec = pl.BlockSpec((1, A_DH), lambda b, p, qa, ka: (0, 0))
    grid_spec = pltpu.PrefetchScalarGridSpec(
        num_scalar_prefetch=2,
        grid=(B, len(pairs)),
        in_specs=[pl.BlockSpec((tq, 256), lambda b, p, qa, ka: (b * nq + qa[p], COL_Q // 256)),
                  pl.BlockSpec((tq, 256), lambda b, p, qa, ka: (b * nq + ka[p], COL_K // 256)),
                  pl.BlockSpec((tq, 256), lambda b, p, qa, ka: (b * nq + ka[p], COL_V // 256)),
                  vec, vec, vec, vec, pl.BlockSpec((1, A_DV), lambda b, p, qa, ka: (0, 0))],
        out_specs=pl.BlockSpec((tq, 256), lambda b, p, qa, ka: (b * nq + qa[p], 0)),
        scratch_shapes=[pltpu.VMEM((2 * A_HEADS, tq, 256), BF16), pltpu.VMEM((2 * A_HEADS, 1, tq), F32),
                        pltpu.VMEM((2 * A_HEADS, 128, tq), F32)],
    )
    return pl.pallas_call(
        functools.partial(_attn_prompt_kernel, tq=tq, lam_init=lam_init),
        out_shape=jax.ShapeDtypeStruct((B * S, A_HEADS * A_DV), F32),
        grid_spec=grid_spec,
        compiler_params=_cparams("parallel", "arbitrary"),
        name="attn_prompt",
    )(qi_arr, ki_arr, z, z, z, *lam_params, gain)


def _attn_decode_kernel(pt_ref, *refs, n_pages, page, lam_init):
    k_refs = refs[:n_pages]
    v_refs = refs[n_pages:2 * n_pages]
    z_ref, lq1, lk1, lq2, lk2, gain_ref, o_ref = refs[2 * n_pages:]
    del pt_ref
    past = n_pages * page
    zrow = z_ref[0]
    q = zrow[:, COL_Q:COL_Q + 256] * (A_DH ** -0.5)
    k_own = zrow[:, COL_K:COL_K + 256]
    v_own = zrow[:, COL_V:COL_V + 256]
    lane = lax.broadcasted_iota(jnp.int32, (2 * A_HEADS, 256), 1)
    r = lax.broadcasted_iota(jnp.int32, (2 * A_HEADS, 256), 0)
    lo = (r % A_HEADS) * 2 * A_DH + (r // A_HEADS) * A_DH
    qm = jnp.where((lane >= lo) & (lane < lo + A_DH), jnp.broadcast_to(q, (2 * A_HEADS, 256)), 0.0)
    qmb = qm.astype(BF16)
    s = jnp.concatenate([_dot(qmb, kr[0, 0].astype(BF16)) for kr in k_refs], axis=1)
    rr = lax.broadcasted_iota(jnp.int32, (2 * A_HEADS, 1), 0) % A_HEADS
    slope = jnp.zeros((2 * A_HEADS, 1), F32)
    for h in range(A_HEADS):
        slope = jnp.where(rr == h, _SLOPES[h], slope)
    kpos = lax.broadcasted_iota(jnp.int32, (2 * A_HEADS, past), 1).astype(F32)
    s = s - slope * (float(past) - kpos)
    s_own = jnp.sum(qm * k_own, axis=-1, keepdims=True)
    m = jnp.maximum(jnp.max(s, axis=-1, keepdims=True), s_own)
    p = jnp.exp(s - m)
    p_own = jnp.exp(s_own - m)
    denom = jnp.sum(p, axis=-1, keepdims=True) + p_own
    pb = p.astype(BF16)
    o = p_own * v_own
    for j, vr in enumerate(v_refs):
        o = o + _dot_nt(pb[:, j * page:(j + 1) * page], vr[0, 0].astype(BF16))
    o = o / denom
    lam = _lam_value(lq1, lk1, lq2, lk2, lam_init)
    d = o[:A_HEADS] - lam * o[A_HEADS:]
    hl = lax.broadcasted_iota(jnp.int32, (A_HEADS, 256), 1) // A_DV
    hr = lax.broadcasted_iota(jnp.int32, (A_HEADS, 256), 0)
    own = hl == hr
    d = jnp.where(own, d, 0.0)
    ms = jnp.sum(d * d, axis=-1, keepdims=True) * (1.0 / A_DV)
    d = d * lax.rsqrt(ms + EPS)
    gain = jnp.concatenate([gain_ref[...]] * A_HEADS, axis=1)
    o_ref[0] = jnp.sum(d, axis=0, keepdims=True) * gain * (1.0 - lam_init)


def _attn_decode(page_table, ck, cv, layer, z3, lam_params, gain, lam_init):
    DB, n_pages = page_table.shape
    page = ck.shape[-1]
    kv_specs = [pl.BlockSpec((1, 1, 256, page), functools.partial(lambda b, pt, j: (layer, pt[b, j], 0, 0), j=j))
                for j in range(n_pages)]
    vec = pl.BlockSpec((1, A_DH), lambda b, pt: (0, 0))
    grid_spec = pltpu.PrefetchScalarGridSpec(
        num_scalar_prefetch=1,
        grid=(DB,),
        in_specs=kv_specs + kv_specs + [pl.BlockSpec((1, 1, Z_W), lambda b, pt: (b, 0, 0)), vec, vec, vec, vec,
                                        pl.BlockSpec((1, A_DV), lambda b, pt: (0, 0))],
        out_specs=pl.BlockSpec((1, 1, 256), lambda b, pt: (b, 0, 0)),
    )
    return pl.pallas_call(
        functools.partial(_attn_decode_kernel, n_pages=n_pages, page=page, lam_init=lam_init),
        out_shape=jax.ShapeDtypeStruct((DB, 1, 256), F32),
        grid_spec=grid_spec,
        compiler_params=_cparams("arbitrary"),
        name="attn_decode",
    )(page_table, *([ck] * n_pages), *([cv] * n_pages), z3, *lam_params, gain)


def _s5_params(lam_re, lam_im, log_dt, b_re, b_im, c_re, c_im):
    dt = jnp.exp(log_dt)[:, None]
    lr, li = lam_re, lam_im
    a, th = lr * dt, li * dt
    mag = jnp.exp(a)
    ab_re, ab_im = mag * jnp.cos(th), mag * jnp.sin(th)
    den = lr * lr + li * li
    nr, ni = ab_re - 1.0, ab_im
    f_re = (nr * lr + ni * li) / den
    f_im = (ni * lr - nr * li) / den
    bb_re = f_re[..., None] * b_re - f_im[..., None] * b_im
    bb_im = f_re[..., None] * b_im + f_im[..., None] * b_re
    eye = jnp.eye(S5_GROUPS, dtype=F32)
    expand_b = lambda m: jnp.einsum("gpn,gh->gnhp", m, eye).reshape(S5_GROUPS * S5_GROUP, S5_STATE)
    wb = jnp.concatenate([expand_b(bb_re), expand_b(bb_im)], axis=1)
    expand_c = lambda m: jnp.einsum("gnp,gh->gphn", m, eye).reshape(S5_STATE, S5_GROUPS * S5_GROUP)
    cm = jnp.concatenate([expand_c(c_re), -expand_c(c_im)], axis=0)
    pr, pi = ab_re.reshape(1, S5_STATE), ab_im.reshape(1, S5_STATE)
    res, ims = [pr], [pi]
    for _ in range(S5_LEVELS - 1):
        pr, pi = pr * pr - pi * pi, 2.0 * pr * pi
        res.append(pr)
        ims.append(pi)
    return dict(wb=wb, cm=cm, ab_re=res[0], ab_im=ims[0], pow_re=jnp.concatenate(res, axis=0),
                pow_im=jnp.concatenate(ims, axis=0))


def _s5_out(y, u, d_ref, wg_ref, bg_ref):
    y = _gelu_tanh(y + d_ref[...] * u)
    return y * _sigmoid(_dot(y.astype(BF16), wg_ref[...]) + bg_ref[...])


def _s5_prompt_kernel(u_ref, wb_ref, cm_ref, pr_ref, pi_ref, d_ref, wg_ref, bg_ref, o_ref, hre_ref, him_ref,
                      bu_scr, hb_scr, sr_scr, si_scr, hr_scr, hi_scr, *, ts):
    t = pl.program_id(1)
    n = S5_SCAN
    P = S5_STATE
    pad = S5_PAD

    @pl.when(t == 0)
    def _init():
        hr_scr[...] = jnp.zeros(hr_scr.shape, F32)
        hi_scr[...] = jnp.zeros(hi_scr.shape, F32)
        sr_scr[0:pad, :] = jnp.zeros((pad, P), F32)
        si_scr[0:pad, :] = jnp.zeros((pad, P), F32)

    u = u_ref[...]
    bu_scr[...] = _dot(u.astype(BF16), wb_ref[...])

    def chunk(c, carry):
        r0 = pl.multiple_of(c * n, n)
        sr_scr[pad:pad + n, :] = bu_scr[pl.ds(r0, n), 0:P]
        si_scr[pad:pad + n, :] = bu_scr[pl.ds(r0, n), P:2 * P]
        ar, ai = pr_ref[0:1, :], pi_ref[0:1, :]
        h0r, h0i = hr_scr[...], hi_scr[...]
        sr_scr[pad:pad + 1, :] = sr_scr[pad:pad + 1, :] + (ar * h0r - ai * h0i)
        si_scr[pad:pad + 1, :] = si_scr[pad:pad + 1, :] + (ar * h0i + ai * h0r)
        for lv in range(S5_LEVELS):
            d = 1 << lv
            ar, ai = pr_ref[lv:lv + 1, :], pi_ref[lv:lv + 1, :]
            cr, ci = sr_scr[pad:pad + n, :], si_scr[pad:pad + n, :]
            qr, qi = sr_scr[pad - d:pad - d + n, :], si_scr[pad - d:pad - d + n, :]
            nr, ni = cr + (ar * qr - ai * qi), ci + (ar * qi + ai * qr)
            if lv + 1 < S5_LEVELS:
                sr_scr[pad:pad + n, :] = nr
                si_scr[pad:pad + n, :] = ni
        hr_scr[...] = nr[n - 1:n, :]
        hi_scr[...] = ni[n - 1:n, :]
        hb_scr[pl.ds(r0, n), 0:P] = nr.astype(BF16)
        hb_scr[pl.ds(r0, n), P:2 * P] = ni.astype(BF16)
        return carry

    lax.fori_loop(0, ts // n, chunk, 0)
    y = _dot(hb_scr[...], cm_ref[...])
    o_ref[...] = _s5_out(y, u, d_ref, wg_ref, bg_ref)
    hre_ref[0] = hr_scr[...]
    him_ref[0] = hi_scr[...]


def _s5_prompt(z, sp, d, wg, bg, B, S, ts):
    nt = S // ts
    tab = _full_spec((S5_LEVELS, S5_STATE))
    st = pl.BlockSpec((1, 1, S5_STATE), lambda b, t: (b, 0, 0))
    return pl.pallas_call(
        functools.partial(_s5_prompt_kernel, ts=ts),
        out_shape=(jax.ShapeDtypeStruct((B * S, 256), F32), jax.ShapeDtypeStruct((B, 1, S5_STATE), F32),
                   jax.ShapeDtypeStruct((B, 1, S5_STATE), F32)),
        grid=(B, nt),
        in_specs=[pl.BlockSpec((ts, 256), lambda b, t: (b * nt + t, COL_U // 256)),
                  _full_spec((256, 2 * S5_STATE)), _full_spec((2 * S5_STATE, 256)), tab, tab,
                  _full_spec((1, 256)), _full_spec((256, 256)), _full_spec((1, 256))],
        out_specs=(pl.BlockSpec((ts, 256), lambda b, t: (b * nt + t, 0)), st, st),
        scratch_shapes=[pltpu.VMEM((ts, 2 * S5_STATE), F32), pltpu.VMEM((ts, 2 * S5_STATE), BF16),
                        pltpu.VMEM((S5_PAD + S5_SCAN, S5_STATE), F32), pltpu.VMEM((S5_PAD + S5_SCAN, S5_STATE), F32),
                        pltpu.VMEM((1, S5_STATE), F32), pltpu.VMEM((1, S5_STATE), F32)],
        compiler_params=_cparams("parallel", "arbitrary"),
        name="s5_prompt",
    )(z, sp["wb"].astype(BF16), sp["cm"].astype(BF16), sp["pow_re"], sp["pow_im"], d, wg, bg)


def _gla_log_a(small, wa_ref, ba_ref):
    x = _dot(small, wa_ref[...], precision=HI) + ba_ref[...]
    return (jnp.minimum(x, 0.0) - jnp.log(1.0 + jnp.exp(-jnp.abs(x)))) * (1.0 / GLA_TAU)


def _gla_prompt_kernel(qk_ref, v_ref, cg_ref, sm_ref, wa_ref, ba_ref, gain_ref, segx_ref, seg_ref, bmask_ref,
                       o_ref, st_ref, b_scr, o_scr, st_scr, *, ts):
    t = pl.program_id(1)
    n = GLA_SUB

    @pl.when(t == 0)
    def _init():
        st_scr[...] = jnp.zeros(st_scr.shape, F32)

    log_a = _gla_log_a(sm_ref[...], wa_ref, ba_ref)
    ri = lax.broadcasted_iota(jnp.int32, (ts, ts), 0)
    ci = lax.broadcasted_iota(jnp.int32, (ts, ts), 1)
    tri = jnp.where((ri >= ci) & (ri // n == ci // n), 1.0, 0.0).astype(BF16)
    b_scr[...] = _dot01_left(tri, log_a, 3)
    rows = lax.broadcasted_iota(jnp.int32, (n, 128), 0)

    def body(c, carry):
        r0 = pl.multiple_of(c * n, n)
        qk = qk_ref[pl.ds(r0, n), :]
        q = qk[:, :128] * (C_DK ** -0.5)
        k = qk[:, 128:]
        v = v_ref[pl.ds(r0, n), :]
        b = b_scr[pl.ds(r0, n), :]
        parts = []
        for s in range(n):
            e = jnp.exp(jnp.where(rows >= s, b - b[s:s + 1, :], NEG_INF))
            parts.append(q * k[s:s + 1, :] * e)
        pall = jnp.concatenate(parts, axis=0).astype(BF16)
        aexp = _dot(pall, segx_ref[...])
        acc = None
        for s in range(n):
            term = aexp[s * n:(s + 1) * n, :] * v[s:s + 1, :]
            acc = term if acc is None else acc + term
        st = st_scr[...]
        o = acc + _dot_nt((q * jnp.exp(b)).astype(BF16), st.astype(BF16))
        b_last = b[n - 1:n, :]
        kt = k * jnp.exp(b_last - b)
        upd = _dot_tn(v.astype(BF16), kt.astype(BF16))
        st_scr[...] = st * jnp.exp(b_last) + upd * bmask_ref[...]
        o_scr[pl.ds(r0, n), :] = o
        return carry

    lax.fori_loop(0, ts // n, body, 0)
    o = o_scr[...]
    cg = cg_ref[...]
    o_ref[...] = _head_rms(o, seg_ref, gain_ref[...]) * _silu(cg)
    st_ref[0] = st_scr[...]


def _gla_consts():
    hk = np.arange(128) // C_DK
    hv = np.arange(256) // C_DV
    segx = jnp.asarray((hk[:, None] == hv[None, :]).astype(np.float32)).astype(BF16)
    bmask = jnp.asarray((hv[:, None] == hk[None, :]).astype(np.float32))
    return segx, bmask


def _gla_prompt(z, wa, ba, gain, B, S, ts):
    nt = S // ts
    segx, bmask = _gla_consts()
    return pl.pallas_call(
        functools.partial(_gla_prompt_kernel, ts=ts),
        out_shape=(jax.ShapeDtypeStruct((B * S, 256), F32), jax.ShapeDtypeStruct((B, 256, 128), F32)),
        grid=(B, nt),
        in_specs=[pl.BlockSpec((ts, 256), lambda b, t: (b * nt + t, COL_CQK // 256)),
                  pl.BlockSpec((ts, 256), lambda b, t: (b * nt + t, COL_CV // 256)),
                  pl.BlockSpec((ts, 256), lambda b, t: (b * nt + t, COL_CG // 256)),
                  pl.BlockSpec((ts, SMALL_W), lambda b, t: (b * nt + t, COL_SMALL // SMALL_W)),
                  _full_spec((SMALL_W, 128)), _full_spec((1, 128)), _full_spec((1, 256)),
                  _full_spec((128, 256)), _full_spec((256, 256)), _full_spec((256, 128))],
        out_specs=(pl.BlockSpec((ts, 256), lambda b, t: (b * nt + t, 0)),
                   pl.BlockSpec((1, 256, 128), lambda b, t: (b, 0, 0))),
        scratch_shapes=[pltpu.VMEM((ts, 128), F32), pltpu.VMEM((ts, 256), F32), pltpu.VMEM((256, 128), F32)],
        compiler_params=_cparams("parallel", "arbitrary"),
        name="gla_prompt",
    )(z, z, z, z, wa, ba, gain, segx, _seg_matrix(256, 64), bmask)


def _dn_gates(small, acoef_ref, dtb_ref):
    beta = _sigmoid(small)
    g = acoef_ref[...] * _softplus(small + dtb_ref[...])
    return beta, g


def _dn_qkv(y, seg_ref):
    y = _silu(y)
    q, k, v = y[:, :256], y[:, 256:512], y[:, 512:768]
    nq = _dot01(q * q, seg_ref[...], 2)
    nk = _dot01(k * k, seg_ref[...], 2)
    q = q * lax.rsqrt(nq + EPS) * (D_DK ** -0.5)
    k = k * lax.rsqrt(nk + EPS)
    return q, k, v


def _dn_prompt_kernel(x_ref, dz_ref, sm_ref, cw_ref, acoef_ref, dtb_ref, gain_ref, seg_ref, esel_ref,
                      o_ref, st_ref, cs_ref, xp_scr, q_scr, k_scr, v_scr, gb_scr, ge_scr, be_scr,
                      u_scr, w_scr, att_scr, o_scr, st_scr, *, ts):
    t = pl.program_id(1)
    C = DN_CHUNK
    PADR = 8
    W = D_HEADS * D_DK

    @pl.when(t == 0)
    def _init():
        st_scr[...] = jnp.zeros(st_scr.shape, F32)
        xp_scr[0:PADR, :] = jnp.zeros((PADR, DN_QKV_W), F32)

    x = x_ref[...]
    xp_scr[PADR:PADR + ts, :] = x
    y = cw_ref[3:4, :] * x
    for i in range(CONV_W - 1):
        y = y + cw_ref[i:i + 1, :] * xp_scr[PADR - 3 + i:PADR - 3 + i + ts, :]
    xp_scr[PADR - 3:PADR, :] = x[ts - 3:ts, :]
    cs_ref[0] = x[ts - 3:ts, :]
    q, k, v = _dn_qkv(y, seg_ref)
    q_scr[...] = q
    k_scr[...] = k
    v_scr[...] = v
    beta, g = _dn_gates(sm_ref[...], acoef_ref, dtb_ref)
    ri = lax.broadcasted_iota(jnp.int32, (ts, ts), 0)
    ci = lax.broadcasted_iota(jnp.int32, (ts, ts), 1)
    tri = jnp.where((ri >= ci) & (ri // C == ci // C), 1.0, 0.0).astype(BF16)
    gcum = _dot01_left(tri, g, 3)
    lane = lax.broadcasted_iota(jnp.int32, (ts, 128), 1)
    gb = jnp.where(lane >= SM_A, gcum, beta)
    gb_scr[...] = gb
    ex = _dot01(gb, esel_ref[...], 3)
    ge_scr[...] = ex[:, :W]
    be_scr[...] = ex[:, W:]

    rr = lax.broadcasted_iota(jnp.int32, (W, W), 0)
    cc = lax.broadcasted_iota(jnp.int32, (W, W), 1)
    same_head = (rr // C) == (cc // C)
    lower_incl = same_head & (rr >= cc)
    lower_strict = same_head & (rr > cc)
    eye = jnp.where(rr == cc, 1.0, 0.0)
    blocks = [(rr // w) == (cc // w) for w in (8, 16, 32, C)]

    def stack_heads(a):
        return jnp.where(same_head, jnp.concatenate([a] * D_HEADS, axis=0), 0.0)

    def collapse(a):
        return a[0:C] + a[C:2 * C] + a[2 * C:3 * C] + a[3 * C:4 * C]

    def solve_body(c, carry):
        r0 = pl.multiple_of(c * C, C)
        qc = q_scr[pl.ds(r0, C), :]
        kc = k_scr[pl.ds(r0, C), :]
        vc = v_scr[pl.ds(r0, C), :]
        gbc = gb_scr[pl.ds(r0, C), :]
        ge = ge_scr[pl.ds(r0, C), :]
        be = be_scr[pl.ds(r0, C), :]
        gbt = gbc.T
        gcol = jnp.concatenate([gbc[:, SM_A + h:SM_A + h + 1] for h in range(D_HEADS)], axis=0)
        grow = jnp.concatenate([gbt[SM_A + h:SM_A + h + 1, :] for h in range(D_HEADS)], axis=1)
        decay = jnp.exp(jnp.where(lower_incl, gcol - grow, NEG_INF))
        kst = stack_heads(kc).astype(BF16)
        kq = _dot_nt(jnp.concatenate([stack_heads(kc * be), stack_heads(qc)], axis=0).astype(BF16), kst)
        m = jnp.where(lower_strict, -(kq[:W] * decay), 0.0)
        m1 = jnp.where(blocks[0], m, 0.0)
        m1b = m1.astype(BF16)
        m2 = _dot(m1b, m1b)
        s1 = eye + m1
        r2 = _dot(m2.astype(BF16), jnp.concatenate([m2, s1], axis=1).astype(BF16))
        s3 = s1 + r2[:, W:]
        tm = s3 + _dot(r2[:, :W].astype(BF16), s3.astype(BF16))
        for inner, outer in zip(blocks[:-1], blocks[1:]):
            tmb = tm.astype(BF16)
            off = jnp.where(outer & jnp.logical_not(inner), m, 0.0).astype(BF16)
            tm = tm + _dot(_dot(tmb, off).astype(BF16), tmb)
        rhs = jnp.concatenate([stack_heads(vc * be), stack_heads(kc * (be * jnp.exp(ge)))], axis=1)
        uw = _dot(tm.astype(BF16), rhs.astype(BF16))
        u_scr[c] = collapse(uw[:, :W])
        w_scr[c] = collapse(uw[:, W:])
        att_scr[c] = (kq[W:] * decay).astype(BF16)
        return carry

    lax.fori_loop(0, ts // C, solve_body, 0, unroll=2)

    def state_body(c, carry):
        r0 = pl.multiple_of(c * C, C)
        qc = q_scr[pl.ds(r0, C), :]
        kc = k_scr[pl.ds(r0, C), :]
        ge = ge_scr[pl.ds(r0, C), :]
        glast = ge[C - 1:C, :]
        s = st_scr[...]
        wq = jnp.concatenate([w_scr[c], qc * jnp.exp(ge)], axis=0)
        sw = _dot(wq.astype(BF16), s.astype(BF16))
        v_new = u_scr[c] - sw[:C]
        ov = _dot(att_scr[c], stack_heads(v_new).astype(BF16))
        o_scr[pl.ds(r0, C), :] = sw[C:] + collapse(ov)
        k_dec = kc * jnp.exp(glast - ge)
        upd = _dot_tn(k_dec.astype(BF16), v_new.astype(BF16))
        st_scr[...] = s * jnp.exp(glast) + jnp.where(same_head, upd, 0.0)
        return carry

    lax.fori_loop(0, ts // C, state_body, 0)
    o_ref[...] = _head_rms(o_scr[...], seg_ref, gain_ref[...]) * _silu(dz_ref[...])
    st_ref[0] = st_scr[...]


def _dn_esel():
    e = np.zeros((128, 512), np.float32)
    for h in range(D_HEADS):
        e[SM_A + h, h * 64:(h + 1) * 64] = 1.0
        e[SM_B + h, 256 + h * 64:256 + (h + 1) * 64] = 1.0
    return jnp.asarray(e).astype(BF16)


def _dn_prompt(z, cw, acoef, dtb, gain, B, S, ts):
    nt = S // ts
    nc = ts // DN_CHUNK
    W = D_HEADS * D_DK
    return pl.pallas_call(
        functools.partial(_dn_prompt_kernel, ts=ts),
        out_shape=(jax.ShapeDtypeStruct((B * S, 256), F32), jax.ShapeDtypeStruct((B, W, W), F32),
                   jax.ShapeDtypeStruct((B, CONV_W - 1, DN_QKV_W), F32)),
        grid=(B, nt),
        in_specs=[pl.BlockSpec((ts, DN_QKV_W), lambda b, t: (b * nt + t, COL_DQKV // DN_QKV_W)),
                  pl.BlockSpec((ts, 256), lambda b, t: (b * nt + t, COL_DZ // 256)),
                  pl.BlockSpec((ts, SMALL_W), lambda b, t: (b * nt + t, COL_SMALL // SMALL_W)),
                  _full_spec((CONV_W, DN_QKV_W)), _full_spec((1, 128)), _full_spec((1, 128)), _full_spec((1, 256)),
                  _full_spec((256, 256)), _full_spec((128, 512))],
        out_specs=(pl.BlockSpec((ts, 256), lambda b, t: (b * nt + t, 0)),
                   pl.BlockSpec((1, W, W), lambda b, t: (b, 0, 0)),
                   pl.BlockSpec((1, CONV_W - 1, DN_QKV_W), lambda b, t: (b, 0, 0))),
        scratch_shapes=[pltpu.VMEM((8 + ts, DN_QKV_W), F32), pltpu.VMEM((ts, 256), F32), pltpu.VMEM((ts, 256), F32),
                        pltpu.VMEM((ts, 256), F32), pltpu.VMEM((ts, 128), F32), pltpu.VMEM((ts, W), F32),
                        pltpu.VMEM((ts, W), F32), pltpu.VMEM((nc, DN_CHUNK, W), F32), pltpu.VMEM((nc, DN_CHUNK, W), F32),
                        pltpu.VMEM((nc, W, W), BF16), pltpu.VMEM((ts, 256), F32), pltpu.VMEM((W, W), F32)],
        compiler_params=_cparams("parallel", "arbitrary"),
        name="dn_prompt",
    )(z, z, z, cw, acoef, dtb, gain, _seg_matrix(256, 64), _dn_esel())


def _dec_prep_kernel(z_ref, h0r_ref, h0i_ref, conv_ref, wb_ref, cm_ref, abr_ref, abi_ref, d_ref, wg_ref, bg_ref,
                     wa_ref, ba_ref, cw_ref, acoef_ref, dtb_ref, seg_ref,
                     ob_ref, hr_ref, hi_ref, gq_ref, gk_ref, ga_ref, dq_ref, dk_ref, dv_ref, dgb_ref, cs_ref):
    z = z_ref[...]
    u = z[:, COL_U:COL_U + 256]
    bu = _dot(u, wb_ref[...], precision=HI)
    bur, bui = bu[:, :S5_STATE], bu[:, S5_STATE:]
    h0r, h0i = h0r_ref[...], h0i_ref[...]
    abr, abi = abr_ref[...], abi_ref[...]
    hr = abr * h0r - abi * h0i + bur
    hi = abr * h0i + abi * h0r + bui
    hr_ref[...] = hr
    hi_ref[...] = hi
    y = _dot(jnp.concatenate([hr, hi], axis=1).astype(BF16), cm_ref[...])
    ob_ref[...] = _s5_out(y, u, d_ref, wg_ref, bg_ref)
    small = z[:, COL_SMALL:COL_SMALL + SMALL_W]
    gq_ref[...] = z[:, COL_CQK:COL_CQK + 128] * (C_DK ** -0.5)
    gk_ref[...] = z[:, COL_CQK + 128:COL_CQK + 256]
    ga_ref[...] = _gla_log_a(small, wa_ref, ba_ref)
    x = z[:, COL_DQKV:COL_DQKV + DN_QKV_W]
    y = cw_ref[3:4, :] * x
    for i in range(CONV_W - 1):
        y = y + cw_ref[i:i + 1, :] * conv_ref[i]
    cs_ref[0] = conv_ref[1]
    cs_ref[1] = conv_ref[2]
    cs_ref[2] = x
    q, k, v = _dn_qkv(y, seg_ref)
    dq_ref[...] = q
    dk_ref[...] = k
    dv_ref[...] = v
    beta, g = _dn_gates(small, acoef_ref, dtb_ref)
    lane = lax.broadcasted_iota(jnp.int32, small.shape, 1)
    dgb_ref[...] = jnp.where(lane >= SM_A, jnp.exp(g), beta)


def _dec_prep(z, h0r, h0i, conv3, sp, d, wg, bg, wa, ba, cw, acoef, dtb):
    DB = z.shape[0]
    sds = lambda *s: jax.ShapeDtypeStruct(s, F32)
    out_shape = (sds(DB, 256), sds(DB, S5_STATE), sds(DB, S5_STATE), sds(DB, 128), sds(DB, 128), sds(DB, 128),
                 sds(DB, 256), sds(DB, 256), sds(DB, 256), sds(DB, 128), sds(CONV_W - 1, DB, DN_QKV_W))
    return pl.pallas_call(
        _dec_prep_kernel,
        out_shape=out_shape,
        compiler_params=pltpu.CompilerParams(vmem_limit_bytes=VMEM_LIMIT_BYTES),
        name="dec_prep",
    )(z, h0r, h0i, conv3, sp["wb"], sp["cm"].astype(BF16), sp["ab_re"], sp["ab_im"], d, wg, bg, wa, ba, cw, acoef, dtb,
      _seg_matrix(256, 64))


def _gla_step_kernel(s_ref, q_ref, k_ref, a_ref, v_ref, cg_ref, gain_ref, sn_ref, o_ref):
    v = v_ref[...]
    sn = s_ref[...] * jnp.exp(a_ref[...]) + k_ref[...] * v[:, None, :]
    sn_ref[...] = sn
    o = jnp.sum(q_ref[...] * sn, axis=1)
    o_ref[...] = _rms(o, gain_ref[...]) * _silu(cg_ref[...])


def _gla_step(s, qcol, kcol, acol, v, cg, gain, tb):
    n = s.shape[0]
    big = pl.BlockSpec((tb, C_DK, C_DV), lambda i: (i, 0, 0))
    row = pl.BlockSpec((tb, C_DV), lambda i: (i, 0))
    return pl.pallas_call(
        _gla_step_kernel,
        out_shape=(jax.ShapeDtypeStruct(s.shape, F32), jax.ShapeDtypeStruct((n, C_DV), F32)),
        grid=(n // tb,),
        in_specs=[big, big, big, big, row, row, _full_spec((1, C_DV))],
        out_specs=(big, row),
        compiler_params=_cparams("parallel"),
        name="gla_step",
    )(s, qcol, kcol, acol, v, cg, gain)


def _dn_step_kernel(s_ref, q_ref, k_ref, v_ref, eg_ref, beta_ref, dz_ref, gain_ref, sn_ref, o_ref):
    s = s_ref[...]
    kcol = k_ref[...]
    eg = eg_ref[...]
    ks = jnp.sum(kcol * s, axis=1)
    v_new = beta_ref[...] * (v_ref[...] - eg * ks)
    sn = s * eg[:, None, :] + kcol * v_new[:, None, :]
    sn_ref[...] = sn
    o = jnp.sum(q_ref[...] * sn, axis=1)
    o_ref[...] = _rms(o, gain_ref[...]) * _silu(dz_ref[...])


def _dn_step(s, qcol, kcol, v, eg, beta, dz, gain, tb):
    n = s.shape[0]
    big = pl.BlockSpec((tb, D_DK, D_DV), lambda i: (i, 0, 0))
    row = pl.BlockSpec((tb, D_DV), lambda i: (i, 0))
    return pl.pallas_call(
        _dn_step_kernel,
        out_shape=(jax.ShapeDtypeStruct(s.shape, F32), jax.ShapeDtypeStruct((n, D_DV), F32)),
        grid=(n // tb,),
        in_specs=[big, big, big, row, row, row, row, _full_spec((1, D_DV))],
        out_specs=(big, row),
        compiler_params=_cparams("parallel"),
        name="dn_step",
    )(s, qcol, kcol, v, eg, beta, dz, gain)


def _prep_w_in(w):
    o = np.cumsum([0, 256, 256, 256, 256, 128, 128, 256, 256, 16, 768, 4, 4, 256])
    q, k, v, u, cq, ck, cv, cg, lr, dqkv, db, da, dz = [w[:, o[i]:o[i + 1]] for i in range(13)]
    pad = jnp.zeros((w.shape[0], SMALL_W - 24), w.dtype)
    return jnp.concatenate([q, k, v, u, cq, ck, cv, dqkv, cg, dz, lr, db, da, pad], axis=1).astype(BF16)


def _lane_row(vals, offset):
    return jnp.zeros((1, 128), F32).at[0, offset:offset + vals.shape[0]].set(vals)


def kernel(x_prompt, x_sample, cache_k, cache_v, state_ssm_re, state_ssm_im, state_gla, state_delta, state_conv, page_table, p_prompt, p_sample, g_mix, w_in, lam_q1, lam_k1, lam_q2, lam_k2, attn_norm, s5_lam_re, s5_lam_im, s5_log_dt, s5_b_re, s5_b_im, s5_c_re, s5_c_im, s5_d, s5_w_glu, s5_b_glu, gla_w_a2, gla_b_a, gla_norm, dn_conv_w, dn_a_log, dn_dt_bias, dn_norm, w_branch_gate, b_branch_gate, w_branch, w_out, g_ffn, w_ff1, w_ff2, g_ple, w_ple_gate, w_ple_proj, g_final):
    B, S, _ = x_prompt.shape
    DB = x_sample.shape[0]
    T = B * S
    n_pool, page = cache_k.shape[1], cache_k.shape[2]
    hp = x_prompt.reshape(T, D_MODEL)
    hs = x_sample.reshape(DB, D_MODEL)
    row = lambda a: a.reshape(1, -1)
    ck_t = cache_k.transpose(0, 1, 3, 4, 2).reshape(DEPTH, n_pool, A_HEADS * 2 * A_DH, page)
    cv_t = cache_v.transpose(0, 1, 3, 4, 2).reshape(DEPTH, n_pool, A_HEADS * A_DV, page)
    st_p, st_s = [], []
    for l in range(DEPTH):
        lam_init = 0.8 - 0.6 * math.exp(-0.3 * l)
        final = l == DEPTH - 1
        w_in_l = _prep_w_in(w_in[l])
        wg, wb, wo = w_branch_gate[l].astype(BF16), w_branch[l].astype(BF16), w_out[l].astype(BF16)
        w1, w2 = w_ff1[l].astype(BF16), w_ff2[l].astype(BF16)
        wpg, wpp = w_ple_gate[l].astype(BF16), w_ple_proj[l].astype(BF16)
        lam_params = (row(lam_q1[l]), row(lam_k1[l]), row(lam_q2[l]), row(lam_k2[l]))
        sp = _s5_params(s5_lam_re[l], s5_lam_im[l], s5_log_dt[l], s5_b_re[l], s5_b_im[l], s5_c_re[l], s5_c_im[l])
        s5d, s5wg, s5bg = row(s5_d[l]), s5_w_glu[l].astype(BF16), row(s5_b_glu[l])
        wa = jnp.zeros((SMALL_W, 128), F32).at[SM_LR:SM_LR + C_RANK].set(gla_w_a2[l])
        ba = row(gla_b_a[l])
        gla_gain4 = row(jnp.tile(gla_norm[l], C_HEADS))
        dn_gain4 = row(jnp.tile(dn_norm[l], D_HEADS))
        acoef = _lane_row(-jnp.exp(dn_a_log[l]), SM_A)
        dtb = _lane_row(dn_dt_bias[l], SM_A)
        cw = dn_conv_w[l]

        zp = _inproj(hp, row(g_mix[l]), w_in_l, 512)
        oa = _attn_prompt(zp, lam_params, row(attn_norm[l]), B, S, 512, lam_init)
        ob, p_hre, p_him = _s5_prompt(zp, sp, s5d, s5wg, s5bg, B, S, 512)
        oc, p_gla_t = _gla_prompt(zp, wa, ba, gla_gain4, B, S, 512)
        od, p_dn_bd, p_conv = _dn_prompt(zp, cw, acoef, dtb, dn_gain4, B, S, 512)
        p_dn = jnp.stack([p_dn_bd[:, h * D_DK:(h + 1) * D_DK, h * D_DV:(h + 1) * D_DV] for h in range(D_HEADS)], axis=1)
        hp = _merge(hp, oa, ob, oc, od, row(g_mix[l]), wg, b_branch_gate[l], wb, wo, 512)
        hp = _ffn(hp, p_prompt[l].reshape(T, PLE_DIM), row(g_ffn[l]), w1, w2, row(g_ple[l]), wpg, wpp, row(g_final),
                  256, final)
        p_gla = jnp.stack([p_gla_t[:, h * C_DV:(h + 1) * C_DV, h * C_DK:(h + 1) * C_DK] for h in range(C_HEADS)],
                          axis=1).swapaxes(-1, -2)
        st_p.append((zp[:, COL_K:COL_K + 256].reshape(B, S, A_HEADS, 2 * A_DH),
                     zp[:, COL_V:COL_V + 256].reshape(B, S, A_HEADS, A_DV),
                     p_hre.reshape(B, S5_GROUPS, S5_P), p_him.reshape(B, S5_GROUPS, S5_P), p_gla, p_dn, p_conv))

        zs = _inproj(hs, row(g_mix[l]), w_in_l, DB)
        oa_s = _attn_decode(page_table, ck_t, cv_t, l, zs.reshape(DB, 1, Z_W), lam_params, row(attn_norm[l]),
                            lam_init).reshape(DB, 256)
        (ob_s, s_hre, s_him, gq, gk, ga, dq, dk, dv, dgb, s_conv3) = _dec_prep(
            zs, state_ssm_re[l].reshape(DB, S5_STATE), state_ssm_im[l].reshape(DB, S5_STATE),
            state_conv[l].swapaxes(0, 1), sp, s5d, s5wg, s5bg, wa, ba, cw, acoef, dtb)
        col = lambda a, h, dk, dv: jnp.broadcast_to(a.reshape(DB * h, dk, 1), (DB * h, dk, dv))
        s_gla, oc_s = _gla_step(state_gla[l].reshape(DB * C_HEADS, C_DK, C_DV), col(gq, C_HEADS, C_DK, C_DV),
                                col(gk, C_HEADS, C_DK, C_DV), col(ga, C_HEADS, C_DK, C_DV),
                                zs[:, COL_CV:COL_CV + 256].reshape(DB * C_HEADS, C_DV),
                                zs[:, COL_CG:COL_CG + 256].reshape(DB * C_HEADS, C_DV), row(gla_norm[l]), 128)
        lanes = lambda a, off: jnp.broadcast_to(a[:, off:off + D_HEADS].reshape(DB * D_HEADS, 1), (DB * D_HEADS, D_DV))
        s_dn, od_s = _dn_step(state_delta[l].reshape(DB * D_HEADS, D_DK, D_DV), col(dq, D_HEADS, D_DK, D_DV),
                              col(dk, D_HEADS, D_DK, D_DV), dv.reshape(DB * D_HEADS, D_DV), lanes(dgb, SM_A),
                              lanes(dgb, SM_B), zs[:, COL_DZ:COL_DZ + 256].reshape(DB * D_HEADS, D_DV),
                              row(dn_norm[l]), 64)
        hs = _merge(hs, oa_s, ob_s, oc_s.reshape(DB, 256), od_s.reshape(DB, 256), row(g_mix[l]), wg, b_branch_gate[l],
                    wb, wo, DB)
        hs = _ffn(hs, p_sample[l].reshape(DB, PLE_DIM), row(g_ffn[l]), w1, w2, row(g_ple[l]), wpg, wpp, row(g_final),
                  DB, final)
        st_s.append((zs[:, COL_K:COL_K + 256].reshape(DB, 1, A_HEADS, 2 * A_DH),
                     zs[:, COL_V:COL_V + 256].reshape(DB, 1, A_HEADS, A_DV),
                     s_hre.reshape(DB, S5_GROUPS, S5_P), s_him.reshape(DB, S5_GROUPS, S5_P),
                     s_gla.reshape(DB, C_HEADS, C_DK, C_DV), s_dn.reshape(DB, D_HEADS, D_DK, D_DV),
                     s_conv3.swapaxes(0, 1)))

    stk = lambda states, i: jnp.stack([s[i] for s in states])
    return (hp.reshape(B, S, D_MODEL), hs.reshape(DB, 1, D_MODEL),
            stk(st_p, 0), stk(st_p, 1), stk(st_p, 2), stk(st_p, 3), stk(st_p, 4), stk(st_p, 5), stk(st_p, 6),
            stk(st_s, 0), stk(st_s, 1), stk(st_s, 2), stk(st_s, 3), stk(st_s, 4), stk(st_s, 5), stk(st_s, 6))
```

```python
import functools
import math

import numpy as np
import jax
import jax.numpy as jnp
from jax import lax
from jax.experimental import pallas as pl
from jax.experimental.pallas import tpu as pltpu

F32 = jnp.float32
BF16 = jnp.bfloat16
HI = lax.Precision.HIGHEST

D_MODEL = 1024
DEPTH = 2
A_HEADS, A_DH, A_DV = 4, 32, 64
S5_GROUPS, S5_GROUP, S5_P = 16, 16, 64
S5_STATE = S5_GROUPS * S5_P
C_HEADS, C_DK, C_DV, C_RANK = 4, 32, 64, 16
GLA_TAU = 16.0
D_HEADS, D_DK, D_DV = 4, 64, 64
CONV_W = 4
DN_QKV_W = 768
D_FF = 4096
PLE_DIM = 256
EPS = 1e-6
BRANCH_W = 256

Z_W = 2944
COL_Q, COL_K, COL_V, COL_U = 0, 256, 512, 768
COL_CQK, COL_CV, COL_DQKV, COL_CG, COL_DZ, COL_SMALL = 1024, 1280, 1536, 2304, 2560, 2816
SMALL_W = 128
SM_LR, SM_B, SM_A = 0, 16, 20

S5_SCAN = 64
S5_LEVELS = 6
S5_PAD = S5_SCAN // 2
GLA_SUB = 16
DN_CHUNK = 64
VMEM_LIMIT_BYTES = 56 * 1024 * 1024
NEG_INF = float("-inf")


def _cparams(*sem):
    return pltpu.CompilerParams(dimension_semantics=sem, vmem_limit_bytes=VMEM_LIMIT_BYTES)


def _sigmoid(x):
    return 1.0 / (1.0 + jnp.exp(-x))


def _silu(x):
    return x * _sigmoid(x)


def _softplus(x):
    return jnp.maximum(x, 0.0) + jnp.log(1.0 + jnp.exp(-jnp.abs(x)))


def _gelu_tanh(x):
    return 0.5 * x * (1.0 + jnp.tanh(math.sqrt(2.0 / math.pi) * (x + 0.044715 * (x * x * x))))


def _rms(x, g):
    return x * lax.rsqrt(jnp.mean(x * x, axis=-1, keepdims=True) + EPS) * g


def _dot(a, b, **kw):
    return jnp.dot(a, b, preferred_element_type=F32, **kw)


def _dot_nt(a, b, **kw):
    return lax.dot_general(a, b, (((1,), (1,)), ((), ())), preferred_element_type=F32, **kw)


def _dot_tn(a, b, **kw):
    return lax.dot_general(a, b, (((0,), (0,)), ((), ())), preferred_element_type=F32, **kw)


def _full_spec(shape):
    nd = len(shape)
    return pl.BlockSpec(shape, lambda *_: (0,) * nd)


def _seg_matrix(width, seg):
    i = np.arange(width) // seg
    return jnp.asarray((i[:, None] == i[None, :]).astype(np.float32)).astype(BF16)


def _bf16_parts(x, parts):
    out = []
    for i in range(parts):
        p = x.astype(BF16)
        out.append(p)
        if i + 1 < parts:
            x = x - p.astype(F32)
    return out


def _dot01(x, e, parts):
    acc = None
    for p in _bf16_parts(x, parts):
        t = _dot(p, e)
        acc = t if acc is None else acc + t
    return acc


def _dot01_left(e, x, parts):
    acc = None
    for p in _bf16_parts(x, parts):
        t = _dot(e, p)
        acc = t if acc is None else acc + t
    return acc


def _head_rms(o, seg_ref, gain):
    ms = _dot01(o * o, seg_ref[...], 2) * (1.0 / 64.0)
    return o * lax.rsqrt(ms + EPS) * gain


def _inproj_kernel(x_ref, g_ref, w_ref, z_ref):
    hn = _rms(x_ref[...], g_ref[...]).astype(BF16)
    z_ref[...] = _dot(hn, w_ref[...])


def _inproj(x, g, w, tm):
    T = x.shape[0]
    return pl.pallas_call(
        _inproj_kernel,
        out_shape=jax.ShapeDtypeStruct((T, Z_W), F32),
        grid=(T // tm,),
        in_specs=[pl.BlockSpec((tm, D_MODEL), lambda i: (i, 0)), _full_spec((1, D_MODEL)), _full_spec((D_MODEL, Z_W))],
        out_specs=pl.BlockSpec((tm, Z_W), lambda i: (i, 0)),
        compiler_params=_cparams("parallel"),
        name="inproj",
    )(x, g, w)


def _merge_kernel(h_ref, oa_ref, ob_ref, oc_ref, od_ref, g_ref, wg_ref, bg_ref, wb_ref, wo_ref, out_ref):
    h = h_ref[...]
    hn = _rms(h, g_ref[...]).astype(BF16)
    merged = None
    for n, br in enumerate((oa_ref, ob_ref, oc_ref, od_ref)):
        gate = _sigmoid(_dot(hn, wg_ref[n]) + bg_ref[n:n + 1, :])
        term = gate * _dot(br[...].astype(BF16), wb_ref[n])
        merged = term if merged is None else merged + term
    out_ref[...] = h + _dot(merged.astype(BF16), wo_ref[...])


def _merge(h, oa, ob, oc, od, g, wg, bg, wb, wo, tm):
    T = h.shape[0]
    row = lambda w: pl.BlockSpec((tm, w), lambda i: (i, 0))
    return pl.pallas_call(
        _merge_kernel,
        out_shape=jax.ShapeDtypeStruct((T, D_MODEL), F32),
        grid=(T // tm,),
        in_specs=[row(D_MODEL), row(BRANCH_W), row(BRANCH_W), row(BRANCH_W), row(BRANCH_W),
                  _full_spec((1, D_MODEL)), _full_spec((4, D_MODEL, D_MODEL)), _full_spec((4, D_MODEL)),
                  _full_spec((4, BRANCH_W, D_MODEL)), _full_spec((D_MODEL, D_MODEL))],
        out_specs=row(D_MODEL),
        compiler_params=_cparams("parallel"),
        name="merge",
    )(h, oa, ob, oc, od, g, wg, bg, wb, wo)


def _ffn_kernel(h_ref, p_ref, gf_ref, w1_ref, w2_ref, gp_ref, wpg_ref, wpp_ref, gfin_ref, out_ref, *, final):
    h = h_ref[...]
    hf = _rms(h, gf_ref[...]).astype(BF16)
    acc = h
    step = 1024
    for j in range(D_FF // step):
        a = jnp.maximum(_dot(hf, w1_ref[:, j * step:(j + 1) * step]), 0.0)
        acc = acc + _dot((a * a).astype(BF16), w2_ref[j * step:(j + 1) * step, :])
    pg = _sigmoid(_dot(_rms(acc, gp_ref[...]).astype(BF16), wpg_ref[...]))
    out = acc + pg * _dot(p_ref[...].astype(BF16), wpp_ref[...])
    if final:
        out = _rms(out, gfin_ref[...])
    out_ref[...] = out


def _ffn(h, p, gf, w1, w2, gp, wpg, wpp, gfin, tm, final):
    T = h.shape[0]
    row = lambda w: pl.BlockSpec((tm, w), lambda i: (i, 0))
    return pl.pallas_call(
        functools.partial(_ffn_kernel, final=final),
        out_shape=jax.ShapeDtypeStruct((T, D_MODEL), F32),
        grid=(T // tm,),
        in_specs=[row(D_MODEL), row(PLE_DIM), _full_spec((1, D_MODEL)), _full_spec((D_MODEL, D_FF)),
                  _full_spec((D_FF, D_MODEL)), _full_spec((1, D_MODEL)), _full_spec((D_MODEL, D_MODEL)),
                  _full_spec((PLE_DIM, D_MODEL)), _full_spec((1, D_MODEL))],
        out_specs=row(D_MODEL),
        compiler_params=_cparams("parallel"),
        name="ffn_ple",
    )(h, p, gf, w1, w2, gp, wpg, wpp, gfin)


_SLOPES = [float(s) for s in 2.0 ** (-8.0 * np.arange(1, A_HEADS + 1) / A_HEADS)]
LOG2E = 1.0 / math.log(2.0)


def _lam_value(lq1, lk1, lq2, lk2, lam_init):
    return (jnp.exp(jnp.sum(lq1[...] * lk1[...], axis=-1, keepdims=True))
            - jnp.exp(jnp.sum(lq2[...] * lk2[...], axis=-1, keepdims=True)) + lam_init)


def _attn_prompt_kernel(qi_ref, ki_ref, q_ref, k_ref, v_ref, lq1, lk1, lq2, lk2, gain_ref, o_ref,
                        qm_scr, m_scr, acc_scr, *, tq, lam_init):
    pair = pl.program_id(1)
    qi = qi_ref[pair]
    ki = ki_ref[pair]

    @pl.when(ki == 0)
    def _init():
        q = q_ref[...] * (A_DH ** -0.5 * LOG2E)
        lane = lax.broadcasted_iota(jnp.int32, q.shape, 1)
        for i in range(2 * A_HEADS):
            lo = (i // 2) * 2 * A_DH + (i % 2) * A_DH
            qm_scr[i] = jnp.where((lane >= lo) & (lane < lo + A_DH), q, 0.0).astype(BF16)
        m_scr[...] = jnp.full(m_scr.shape, NEG_INF, F32)
        acc_scr[...] = jnp.zeros(acc_scr.shape, F32)

    def _step(diag):
        k = k_ref[...].astype(BF16)
        vt = v_ref[...].T
        kpos = (lax.broadcasted_iota(jnp.int32, (tq, 1), 0) + (ki - qi) * tq).astype(F32)
        ones_row = jnp.where(lax.broadcasted_iota(jnp.int32, (A_DV, tq), 0) == 0, 1.0, 0.0)
        if diag:
            causal = (lax.broadcasted_iota(jnp.int32, (tq, tq), 0) <= lax.broadcasted_iota(jnp.int32, (tq, tq), 1))
        qq = jnp.concatenate([qm_scr[i] for i in range(2 * A_HEADS)], axis=0)
        s_all = _dot_nt(k, qq)
        for h in range(A_HEADS):
            vext = jnp.concatenate([vt[h * A_DV:(h + 1) * A_DV, :], ones_row], axis=0).astype(BF16)
            bias = kpos * (_SLOPES[h] * LOG2E)
            if diag:
                bias = jnp.where(causal, bias, NEG_INF)
                bias = jnp.concatenate([bias, bias], axis=1)
            i = 2 * h
            s = s_all[:, i * tq:(i + 2) * tq] + bias
            m_prev = jnp.concatenate([m_scr[i], m_scr[i + 1]], axis=1)
            m_new = jnp.maximum(m_prev, jnp.max(s, axis=0, keepdims=True))
            alpha = jnp.exp2(m_prev - m_new)
            p = jnp.exp2(s - m_new)
            pv = _dot(vext, p.astype(BF16))
            acc_scr[i] = alpha[:, :tq] * acc_scr[i] + pv[:, :tq]
            acc_scr[i + 1] = alpha[:, tq:] * acc_scr[i + 1] + pv[:, tq:]
            m_scr[i] = m_new[:, :tq]
            m_scr[i + 1] = m_new[:, tq:]

    @pl.when(ki < qi)
    def _off_diagonal():
        _step(False)

    @pl.when(ki == qi)
    def _diagonal():
        _step(True)
        lam = _lam_value(lq1, lk1, lq2, lk2, lam_init)
        outs = []
        for h in range(A_HEADS):
            a1, a2 = acc_scr[2 * h], acc_scr[2 * h + 1]
            o = a1[:A_DV, :] / a1[A_DV:A_DV + 1, :] - lam * (a2[:A_DV, :] / a2[A_DV:A_DV + 1, :])
            outs.append(o * lax.rsqrt(jnp.mean(o * o, axis=0, keepdims=True) + EPS))
        gain = jnp.concatenate([gain_ref[...]] * A_HEADS, axis=1)
        o_ref[...] = jnp.concatenate(outs, axis=0).T * gain * (1.0 - lam_init)


def _attn_prompt(z, lam_params, gain, B, S, tq, lam_init):
    nq = S // tq
    pairs = [(qi, ki) for qi in range(nq) for ki in range(qi + 1)]
    qi_arr = jnp.asarray([p[0] for p in pairs], jnp.int32)
    ki_arr = jnp.asarray([p[1] for p in pairs], jnp.int32)
    vec = pl.BlockSpec((1, A_DH), lambda b, p, qa, ka: (0, 0))
    grid_spec = pltpu.PrefetchScalarGridSpec(
        num_scalar_prefetch=2,
        grid=(B, len(pairs)),
        in_specs=[pl.BlockSpec((tq, 256), lambda b, p, qa, ka: (b * nq + qa[p], COL_Q // 256)),
                  pl.BlockSpec((tq, 256), lambda b, p, qa, ka: (b * nq + ka[p], COL_K // 256)),
                  pl.BlockSpec((tq, 256), lambda b, p, qa, ka: (b * nq + ka[p], COL_V // 256)),
                  vec, vec, vec, vec, pl.BlockSpec((1, A_DV), lambda b, p, qa, ka: (0, 0))],
        out_specs=pl.BlockSpec((tq, 256), lambda b, p, qa, ka: (b * nq + qa[p], 0)),
        scratch_shapes=[pltpu.VMEM((2 * A_HEADS, tq, 256), BF16), pltpu.VMEM((2 * A_HEADS, 1, tq), F32),
                        pltpu.VMEM((2 * A_HEADS, 128, tq), F32)],
    )
    return pl.pallas_call(
        functools.partial(_attn_prompt_kernel, tq=tq, lam_init=lam_init),
        out_shape=jax.ShapeDtypeStruct((B * S, A_HEADS * A_DV), F32),
        grid_spec=grid_spec,
        compiler_params=_cparams("parallel", "arbitrary"),
        name="attn_prompt",
    )(qi_arr, ki_arr, z, z, z, *lam_params, gain)


def _attn_decode_kernel(pt_ref, *refs, n_pages, page, lam_init):
    k_refs = refs[:n_pages]
    v_refs = refs[n_pages:2 * n_pages]
    z_ref, lq1, lk1, lq2, lk2, gain_ref, o_ref = refs[2 * n_pages:]
    del pt_ref
    past = n_pages * page
    zrow = z_ref[0]
    q = zrow[:, COL_Q:COL_Q + 256] * (A_DH ** -0.5)
    k_own = zrow[:, COL_K:COL_K + 256]
    v_own = zrow[:, COL_V:COL_V + 256]
    lane = lax.broadcasted_iota(jnp.int32, (2 * A_HEADS, 256), 1)
    r = lax.broadcasted_iota(jnp.int32, (2 * A_HEADS, 256), 0)
    lo = (r % A_HEADS) * 2 * A_DH + (r // A_HEADS) * A_DH
    qm = jnp.where((lane >= lo) & (lane < lo + A_DH), jnp.broadcast_to(q, (2 * A_HEADS, 256)), 0.0)
    qmb = qm.astype(BF16)
    s = jnp.concatenate([_dot(qmb, kr[0, 0].astype(BF16)) for kr in k_refs], axis=1)
    rr = lax.broadcasted_iota(jnp.int32, (2 * A_HEADS, 1), 0) % A_HEADS
    slope = jnp.zeros((2 * A_HEADS, 1), F32)
    for h in range(A_HEADS):
        slope = jnp.where(rr == h, _SLOPES[h], slope)
    kpos = lax.broadcasted_iota(jnp.int32, (2 * A_HEADS, past), 1).astype(F32)
    s = s - slope * (float(past) - kpos)
    s_own = jnp.sum(qm * k_own, axis=-1, keepdims=True)
    m = jnp.maximum(jnp.max(s, axis=-1, keepdims=True), s_own)
    p = jnp.exp(s - m)
    p_own = jnp.exp(s_own - m)
    denom = jnp.sum(p, axis=-1, keepdims=True) + p_own
    pb = p.astype(BF16)
    o = p_own * v_own
    for j, vr in enumerate(v_refs):
        o = o + _dot_nt(pb[:, j * page:(j + 1) * page], vr[0, 0].astype(BF16))
    o = o / denom
    lam = _lam_value(lq1, lk1, lq2, lk2, lam_init)
    d = o[:A_HEADS] - lam * o[A_HEADS:]
    hl = lax.broadcasted_iota(jnp.int32, (A_HEADS, 256), 1) // A_DV
    hr = lax.broadcasted_iota(jnp.int32, (A_HEADS, 256), 0)
    own = hl == hr
    d = jnp.where(own, d, 0.0)
    ms = jnp.sum(d * d, axis=-1, keepdims=True) * (1.0 / A_DV)
    d = d * lax.rsqrt(ms + EPS)
    gain = jnp.concatenate([gain_ref[...]] * A_HEADS, axis=1)
    o_ref[0] = jnp.sum(d, axis=0, keepdims=True) * gain * (1.0 - lam_init)


def _attn_decode(page_table, ck, cv, layer, z3, lam_params, gain, lam_init):
    DB, n_pages = page_table.shape
    page = ck.shape[-1]
    kv_specs = [pl.BlockSpec((1, 1, 256, page), functools.partial(lambda b, pt, j: (layer, pt[b, j], 0, 0), j=j))
                for j in range(n_pages)]
    vec = pl.BlockSpec((1, A_DH), lambda b, pt: (0, 0))
    grid_spec = pltpu.PrefetchScalarGridSpec(
        num_scalar_prefetch=1,
        grid=(DB,),
        in_specs=kv_specs + kv_specs + [pl.BlockSpec((1, 1, Z_W), lambda b, pt: (b, 0, 0)), vec, vec, vec, vec,
                                        pl.BlockSpec((1, A_DV), lambda b, pt: (0, 0))],
        out_specs=pl.BlockSpec((1, 1, 256), lambda b, pt: (b, 0, 0)),
    )
    return pl.pallas_call(
        functools.partial(_attn_decode_kernel, n_pages=n_pages, page=page, lam_init=lam_init),
        out_shape=jax.ShapeDtypeStruct((DB, 1, 256), F32),
        grid_spec=grid_spec,
        compiler_params=_cparams("arbitrary"),
        name="attn_decode",
    )(page_table, *([ck] * n_pages), *([cv] * n_pages), z3, *lam_params, gain)


def _s5_params(lam_re, lam_im, log_dt, b_re, b_im, c_re, c_im):
    dt = jnp.exp(log_dt)[:, None]
    lr, li = lam_re, lam_im
    a, th = lr * dt, li * dt
    mag = jnp.exp(a)
    ab_re, ab_im = mag * jnp.cos(th), mag * jnp.sin(th)
    den = lr * lr + li * li
    nr, ni = ab_re - 1.0, ab_im
    f_re = (nr * lr + ni * li) / den
    f_im = (ni * lr - nr * li) / den
    bb_re = f_re[..., None] * b_re - f_im[..., None] * b_im
    bb_im = f_re[..., None] * b_im + f_im[..., None] * b_re
    eye = jnp.eye(S5_GROUPS, dtype=F32)
    expand_b = lambda m: jnp.einsum("gpn,gh->gnhp", m, eye).reshape(S5_GROUPS * S5_GROUP, S5_STATE)
    wb = jnp.concatenate([expand_b(bb_re), expand_b(bb_im)], axis=1)
    expand_c = lambda m: jnp.einsum("gnp,gh->gphn", m, eye).reshape(S5_STATE, S5_GROUPS * S5_GROUP)
    cm = jnp.concatenate([expand_c(c_re), -expand_c(c_im)], axis=0)
    pr, pi = ab_re.reshape(1, S5_STATE), ab_im.reshape(1, S5_STATE)
    res, ims = [pr], [pi]
    for _ in range(S5_LEVELS - 1):
        pr, pi = pr * pr - pi * pi, 2.0 * pr * pi
        res.append(pr)
        ims.append(pi)
    return dict(wb=wb, cm=cm, ab_re=res[0], ab_im=ims[0], pow_re=jnp.concatenate(res, axis=0),
                pow_im=jnp.concatenate(ims, axis=0))


def _s5_out(y, u, d_ref, wg_ref, bg_ref):
    y = _gelu_tanh(y + d_ref[...] * u)
    return y * _sigmoid(_dot(y.astype(BF16), wg_ref[...]) + bg_ref[...])


def _s5_prompt_kernel(u_ref, wb_ref, cm_ref, pr_ref, pi_ref, d_ref, wg_ref, bg_ref, o_ref, hre_ref, him_ref,
                      bu_scr, hb_scr, sr_scr, si_scr, hr_scr, hi_scr, *, ts):
    t = pl.program_id(1)
    n = S5_SCAN
    P = S5_STATE
    pad = S5_PAD

    @pl.when(t == 0)
    def _init():
        hr_scr[...] = jnp.zeros(hr_scr.shape, F32)
        hi_scr[...] = jnp.zeros(hi_scr.shape, F32)
        sr_scr[0:pad, :] = jnp.zeros((pad, P), F32)
        si_scr[0:pad, :] = jnp.zeros((pad, P), F32)

    u = u_ref[...]
    bu_scr[...] = _dot(u.astype(BF16), wb_ref[...])

    def chunk(c, carry):
        r0 = pl.multiple_of(c * n, n)
        sr_scr[pad:pad + n, :] = bu_scr[pl.ds(r0, n), 0:P]
        si_scr[pad:pad + n, :] = bu_scr[pl.ds(r0, n), P:2 * P]
        ar, ai = pr_ref[0:1, :], pi_ref[0:1, :]
        h0r, h0i = hr_scr[...], hi_scr[...]
        sr_scr[pad:pad + 1, :] = sr_scr[pad:pad + 1, :] + (ar * h0r - ai * h0i)
        si_scr[pad:pad + 1, :] = si_scr[pad:pad + 1, :] + (ar * h0i + ai * h0r)
        for lv in range(S5_LEVELS):
            d = 1 << lv
            ar, ai = pr_ref[lv:lv + 1, :], pi_ref[lv:lv + 1, :]
            cr, ci = sr_scr[pad:pad + n, :], si_scr[pad:pad + n, :]
            qr, qi = sr_scr[pad - d:pad - d + n, :], si_scr[pad - d:pad - d + n, :]
            nr, ni = cr + (ar * qr - ai * qi), ci + (ar * qi + ai * qr)
            if lv + 1 < S5_LEVELS:
                sr_scr[pad:pad + n, :] = nr
                si_scr[pad:pad + n, :] = ni
        hr_scr[...] = nr[n - 1:n, :]
        hi_scr[...] = ni[n - 1:n, :]
        hb_scr[pl.ds(r0, n), 0:P] = nr.astype(BF16)
        hb_scr[pl.ds(r0, n), P:2 * P] = ni.astype(BF16)
        return carry

    lax.fori_loop(0, ts // n, chunk, 0)
    y = _dot(hb_scr[...], cm_ref[...])
    o_ref[...] = _s5_out(y, u, d_ref, wg_ref, bg_ref)
    hre_ref[0] = hr_scr[...]
    him_ref[0] = hi_scr[...]


def _s5_prompt(z, sp, d, wg, bg, B, S, ts):
    nt = S // ts
    tab = _full_spec((S5_LEVELS, S5_STATE))
    st = pl.BlockSpec((1, 1, S5_STATE), lambda b, t: (b, 0, 0))
    return pl.pallas_call(
        functools.partial(_s5_prompt_kernel, ts=ts),
        out_shape=(jax.ShapeDtypeStruct((B * S, 256), F32), jax.ShapeDtypeStruct((B, 1, S5_STATE), F32),
                   jax.ShapeDtypeStruct((B, 1, S5_STATE), F32)),
        grid=(B, nt),
        in_specs=[pl.BlockSpec((ts, 256), lambda b, t: (b * nt + t, COL_U // 256)),
                  _full_spec((256, 2 * S5_STATE)), _full_spec((2 * S5_STATE, 256)), tab, tab,
                  _full_spec((1, 256)), _full_spec((256, 256)), _full_spec((1, 256))],
        out_specs=(pl.BlockSpec((ts, 256), lambda b, t: (b * nt + t, 0)), st, st),
        scratch_shapes=[pltpu.VMEM((ts, 2 * S5_STATE), F32), pltpu.VMEM((ts, 2 * S5_STATE), BF16),
                        pltpu.VMEM((S5_PAD + S5_SCAN, S5_STATE), F32), pltpu.VMEM((S5_PAD + S5_SCAN, S5_STATE), F32),
                        pltpu.VMEM((1, S5_STATE), F32), pltpu.VMEM((1, S5_STATE), F32)],
        compiler_params=_cparams("parallel", "arbitrary"),
        name="s5_prompt",
    )(z, sp["wb"].astype(BF16), sp["cm"].astype(BF16), sp["pow_re"], sp["pow_im"], d, wg, bg)


def _gla_log_a(small, wa_ref, ba_ref):
    x = _dot(small, wa_ref[...], precision=HI) + ba_ref[...]
    return (jnp.minimum(x, 0.0) - jnp.log(1.0 + jnp.exp(-jnp.abs(x)))) * (1.0 / GLA_TAU)


def _gla_prompt_kernel(qk_ref, v_ref, cg_ref, sm_ref, wa_ref, ba_ref, gain_ref, segx_ref, seg_ref, bmask_ref,
                       o_ref, st_ref, b_scr, o_scr, st_scr, *, ts, nb):
    t = pl.program_id(0)
    n = GLA_SUB
    seqs = range(nb)

    @pl.when(t == 0)
    def _init():
        st_scr[...] = jnp.zeros(st_scr.shape, F32)

    ri = lax.broadcasted_iota(jnp.int32, (ts, ts), 0)
    ci = lax.broadcasted_iota(jnp.int32, (ts, ts), 1)
    tri = jnp.where((ri >= ci) & (ri // n == ci // n), 1.0, 0.0).astype(BF16)
    for i in seqs:
        log_a = _gla_log_a(sm_ref[i], wa_ref, ba_ref)
        b_scr[i] = _dot01_left(tri, log_a, 3)
    rows = lax.broadcasted_iota(jnp.int32, (n, 128), 0)
    segx = segx_ref[...]
    bmask = bmask_ref[...]

    def body(c, carry):
        r0 = pl.multiple_of(c * n, n)
        qk = [qk_ref[i, pl.ds(r0, n), :] for i in seqs]
        q = [x[:, :128] * (C_DK ** -0.5) for x in qk]
        k = [x[:, 128:] for x in qk]
        v = [v_ref[i, pl.ds(r0, n), :] for i in seqs]
        b = [b_scr[i, pl.ds(r0, n), :] for i in seqs]
        pall = []
        for i in seqs:
            parts = []
            for s in range(n):
                e = jnp.exp(jnp.where(rows >= s, b[i] - b[i][s:s + 1, :], NEG_INF))
                parts.append(q[i] * k[i][s:s + 1, :] * e)
            pall.append(jnp.concatenate(parts, axis=0).astype(BF16))
        aexp = [_dot(pall[i], segx) for i in seqs]
        st = [st_scr[i] for i in seqs]
        inter = [_dot_nt((q[i] * jnp.exp(b[i])).astype(BF16), st[i].astype(BF16)) for i in seqs]
        b_last = [b[i][n - 1:n, :] for i in seqs]
        upd = [_dot_tn(v[i].astype(BF16), (k[i] * jnp.exp(b_last[i] - b[i])).astype(BF16)) for i in seqs]
        for i in seqs:
            acc = inter[i]
            for s in range(n):
                acc = acc + aexp[i][s * n:(s + 1) * n, :] * v[i][s:s + 1, :]
            o_scr[i, pl.ds(r0, n), :] = acc
            st_scr[i] = st[i] * jnp.exp(b_last[i]) + upd[i] * bmask
        return carry

    lax.fori_loop(0, ts // n, body, 0)
    for i in seqs:
        o_ref[i] = _head_rms(o_scr[i], seg_ref, gain_ref[...]) * _silu(cg_ref[i])
    st_ref[...] = st_scr[...]


def _gla_consts():
    hk = np.arange(128) // C_DK
    hv = np.arange(256) // C_DV
    segx = jnp.asarray((hk[:, None] == hv[None, :]).astype(np.float32)).astype(BF16)
    bmask = jnp.asarray((hv[:, None] == hk[None, :]).astype(np.float32))
    return segx, bmask


def _gla_prompt(z, wa, ba, gain, B, S, ts):
    nt = S // ts
    segx, bmask = _gla_consts()
    blk = lambda w, col: pl.BlockSpec((B, ts, w), lambda t: (0, t, col // w))
    return pl.pallas_call(
        functools.partial(_gla_prompt_kernel, ts=ts, nb=B),
        out_shape=(jax.ShapeDtypeStruct((B, S, 256), F32), jax.ShapeDtypeStruct((B, 256, 128), F32)),
        grid=(nt,),
        in_specs=[blk(256, COL_CQK), blk(256, COL_CV), blk(256, COL_CG), blk(SMALL_W, COL_SMALL),
                  _full_spec((SMALL_W, 128)), _full_spec((1, 128)), _full_spec((1, 256)),
                  _full_spec((128, 256)), _full_spec((256, 256)), _full_spec((256, 128))],
        out_specs=(pl.BlockSpec((B, ts, 256), lambda t: (0, t, 0)), _full_spec((B, 256, 128))),
        scratch_shapes=[pltpu.VMEM((B, ts, 128), F32), pltpu.VMEM((B, ts, 256), F32), pltpu.VMEM((B, 256, 128), F32)],
        compiler_params=_cparams("arbitrary"),
        name="gla_prompt",
    )(z, z, z, z, wa, ba, gain, segx, _seg_matrix(256, 64), bmask)


def _dn_gates(small, acoef_ref, dtb_ref):
    beta = _sigmoid(small)
    g = acoef_ref[...] * _softplus(small + dtb_ref[...])
    return beta, g


def _dn_qkv(y, seg_ref):
    y = _silu(y)
    q, k, v = y[:, :256], y[:, 256:512], y[:, 512:768]
    nq = _dot01(q * q, seg_ref[...], 2)
    nk = _dot01(k * k, seg_ref[...], 2)
    q = q * lax.rsqrt(nq + EPS) * (D_DK ** -0.5)
    k = k * lax.rsqrt(nk + EPS)
    return q, k, v


def _dn_prompt_kernel(x_ref, dz_ref, sm_ref, cw_ref, acoef_ref, dtb_ref, gain_ref, seg_ref, esel_ref,
                      o_ref, st_ref, cs_ref, xp_scr, q_scr, k_scr, v_scr, gb_scr, ge_scr, be_scr,
                      u_scr, w_scr, att_scr, o_scr, st_scr, *, ts):
    t = pl.program_id(1)
    C = DN_CHUNK
    PADR = 8
    W = D_HEADS * D_DK

    @pl.when(t == 0)
    def _init():
        st_scr[...] = jnp.zeros(st_scr.shape, F32)
        xp_scr[0:PADR, :] = jnp.zeros((PADR, DN_QKV_W), F32)

    x = x_ref[...]
    xp_scr[PADR:PADR + ts, :] = x
    y = cw_ref[3:4, :] * x
    for i in range(CONV_W - 1):
        y = y + cw_ref[i:i + 1, :] * xp_scr[PADR - 3 + i:PADR - 3 + i + ts, :]
    xp_scr[PADR - 3:PADR, :] = x[ts - 3:ts, :]
    cs_ref[0] = x[ts - 3:ts, :]
    q, k, v = _dn_qkv(y, seg_ref)
    q_scr[...] = q
    k_scr[...] = k
    v_scr[...] = v
    beta, g = _dn_gates(sm_ref[...], acoef_ref, dtb_ref)
    ri = lax.broadcasted_iota(jnp.int32, (ts, ts), 0)
    ci = lax.broadcasted_iota(jnp.int32, (ts, ts), 1)
    tri = jnp.where((ri >= ci) & (ri // C == ci // C), 1.0, 0.0).astype(BF16)
    gcum = _dot01_left(tri, g, 3)
    lane = lax.broadcasted_iota(jnp.int32, (ts, 128), 1)
    gb = jnp.where(lane >= SM_A, gcum, beta)
    gb_scr[...] = gb
    ex = _dot01(gb, esel_ref[...], 3)
    ge_scr[...] = ex[:, :W]
    be_scr[...] = ex[:, W:]

    rr = lax.broadcasted_iota(jnp.int32, (W, W), 0)
    cc = lax.broadcasted_iota(jnp.int32, (W, W), 1)
    same_head = (rr // C) == (cc // C)
    lower_incl = same_head & (rr >= cc)
    lower_strict = same_head & (rr > cc)
    eye = jnp.where(rr == cc, 1.0, 0.0)
    blocks = [(rr // w) == (cc // w) for w in (8, 16, 32, C)]

    def stack_heads(a):
        return jnp.where(same_head, jnp.concatenate([a] * D_HEADS, axis=0), 0.0)

    def collapse(a):
        return a[0:C] + a[C:2 * C] + a[2 * C:3 * C] + a[3 * C:4 * C]

    NI = 4
    both = range(NI)

    def solve_body(c4, carry):
        cs = [NI * c4 + j for j in both]
        r0s = [pl.multiple_of(c * C, C) for c in cs]
        qc = [q_scr[pl.ds(r0, C), :] for r0 in r0s]
        kc = [k_scr[pl.ds(r0, C), :] for r0 in r0s]
        vc = [v_scr[pl.ds(r0, C), :] for r0 in r0s]
        gbc = [gb_scr[pl.ds(r0, C), :] for r0 in r0s]
        ge = [ge_scr[pl.ds(r0, C), :] for r0 in r0s]
        be = [be_scr[pl.ds(r0, C), :] for r0 in r0s]
        decay, kst = [], []
        for j in both:
            gbt = gbc[j].T
            gcol = jnp.concatenate([gbc[j][:, SM_A + h:SM_A + h + 1] for h in range(D_HEADS)], axis=0)
            grow = jnp.concatenate([gbt[SM_A + h:SM_A + h + 1, :] for h in range(D_HEADS)], axis=1)
            decay.append(jnp.exp(jnp.where(lower_incl, gcol - grow, NEG_INF)))
            kst.append(stack_heads(kc[j]).astype(BF16))
        kq = [_dot_nt(jnp.concatenate([stack_heads(kc[j] * be[j]), stack_heads(qc[j])], axis=0).astype(BF16), kst[j])
              for j in both]
        m = [jnp.where(lower_strict, -(kq[j][:W] * decay[j]), 0.0) for j in both]
        m1 = [jnp.where(blocks[0], m[j], 0.0) for j in both]
        m1b = [x.astype(BF16) for x in m1]
        m2 = [_dot(m1b[j], m1b[j]) for j in both]
        s1 = [eye + m1[j] for j in both]
        r2 = [_dot(m2[j].astype(BF16), jnp.concatenate([m2[j], s1[j]], axis=1).astype(BF16)) for j in both]
        s3 = [s1[j] + r2[j][:, W:] for j in both]
        tm = [s3[j] + _dot(r2[j][:, :W].astype(BF16), s3[j].astype(BF16)) for j in both]
        for inner, outer in zip(blocks[:-1], blocks[1:]):
            tmb = [x.astype(BF16) for x in tm]
            off = [jnp.where(outer & jnp.logical_not(inner), m[j], 0.0).astype(BF16) for j in both]
            t1 = [_dot(tmb[j], off[j]).astype(BF16) for j in both]
            tm = [tm[j] + _dot(t1[j], tmb[j]) for j in both]
        rhs = [jnp.concatenate([stack_heads(vc[j] * be[j]), stack_heads(kc[j] * (be[j] * jnp.exp(ge[j])))], axis=1)
               for j in both]
        uw = [_dot(tm[j].astype(BF16), rhs[j].astype(BF16)) for j in both]
        for j in both:
            u_scr[cs[j]] = collapse(uw[j][:, :W])
            w_scr[cs[j]] = collapse(uw[j][:, W:])
            att_scr[cs[j]] = (kq[j][W:] * decay[j]).astype(BF16)
        return carry

    lax.fori_loop(0, ts // (NI * C), solve_body, 0)

    def state_body(c, carry):
        r0 = pl.multiple_of(c * C, C)
        qc = q_scr[pl.ds(r0, C), :]
        kc = k_scr[pl.ds(r0, C), :]
        ge = ge_scr[pl.ds(r0, C), :]
        glast = ge[C - 1:C, :]
        s = st_scr[...]
        wq = jnp.concatenate([w_scr[c], qc * jnp.exp(ge)], axis=0)
        sw = _dot(wq.astype(BF16), s.astype(BF16))
        v_new = u_scr[c] - sw[:C]
        ov = _dot(att_scr[c], stack_heads(v_new).astype(BF16))
        o_scr[pl.ds(r0, C), :] = sw[C:] + collapse(ov)
        k_dec = kc * jnp.exp(glast - ge)
        upd = _dot_tn(k_dec.astype(BF16), v_new.astype(BF16))
        st_scr[...] = s * jnp.exp(glast) + jnp.where(same_head, upd, 0.0)
        return carry

    lax.fori_loop(0, ts // C, state_body, 0)
    o_ref[...] = _head_rms(o_scr[...], seg_ref, gain_ref[...]) * _silu(dz_ref[...])
    st_ref[0] = st_scr[...]


def _dn_esel():
    e = np.zeros((128, 512), np.float32)
    for h in range(D_HEADS):
        e[SM_A + h, h * 64:(h + 1) * 64] = 1.0
        e[SM_B + h, 256 + h * 64:256 + (h + 1) * 64] = 1.0
    return jnp.asarray(e).astype(BF16)


def _dn_prompt(z, cw, acoef, dtb, gain, B, S, ts):
    nt = S // ts
    nc = ts // DN_CHUNK
    W = D_HEADS * D_DK
    return pl.pallas_call(
        functools.partial(_dn_prompt_kernel, ts=ts),
        out_shape=(jax.ShapeDtypeStruct((B * S, 256), F32), jax.ShapeDtypeStruct((B, W, W), F32),
                   jax.ShapeDtypeStruct((B, CONV_W - 1, DN_QKV_W), F32)),
        grid=(B, nt),
        in_specs=[pl.BlockSpec((ts, DN_QKV_W), lambda b, t: (b * nt + t, COL_DQKV // DN_QKV_W)),
                  pl.BlockSpec((ts, 256), lambda b, t: (b * nt + t, COL_DZ // 256)),
                  pl.BlockSpec((ts, SMALL_W), lambda b, t: (b * nt + t, COL_SMALL // SMALL_W)),
                  _full_spec((CONV_W, DN_QKV_W)), _full_spec((1, 128)), _full_spec((1, 128)), _full_spec((1, 256)),
                  _full_spec((256, 256)), _full_spec((128, 512))],
        out_specs=(pl.BlockSpec((ts, 256), lambda b, t: (b * nt + t, 0)),
                   pl.BlockSpec((1, W, W), lambda b, t: (b, 0, 0)),
                   pl.BlockSpec((1, CONV_W - 1, DN_QKV_W), lambda b, t: (b, 0, 0))),
        scratch_shapes=[pltpu.VMEM((8 + ts, DN_QKV_W), F32), pltpu.VMEM((ts, 256), F32), pltpu.VMEM((ts, 256), F32),
                        pltpu.VMEM((ts, 256), F32), pltpu.VMEM((ts, 128), F32), pltpu.VMEM((ts, W), F32),
                        pltpu.VMEM((ts, W), F32), pltpu.VMEM((nc, DN_CHUNK, W), F32), pltpu.VMEM((nc, DN_CHUNK, W), F32),
                        pltpu.VMEM((nc, W, W), BF16), pltpu.VMEM((ts, 256), F32), pltpu.VMEM((W, W), F32)],
        compiler_params=_cparams("parallel", "arbitrary"),
        name="dn_prompt",
    )(z, z, z, cw, acoef, dtb, gain, _seg_matrix(256, 64), _dn_esel())


def _dec_prep_kernel(z_ref, h0r_ref, h0i_ref, conv_ref, wb_ref, cm_ref, abr_ref, abi_ref, d_ref, wg_ref, bg_ref,
                     wa_ref, ba_ref, cw_ref, acoef_ref, dtb_ref, seg_ref,
                     ob_ref, hr_ref, hi_ref, gq_ref, gk_ref, ga_ref, dq_ref, dk_ref, dv_ref, dgb_ref, cs_ref):
    z = z_ref[...]
    u = z[:, COL_U:COL_U + 256]
    bu = _dot(u, wb_ref[...], precision=HI)
    bur, bui = bu[:, :S5_STATE], bu[:, S5_STATE:]
    h0r, h0i = h0r_ref[...], h0i_ref[...]
    abr, abi = abr_ref[...], abi_ref[...]
    hr = abr * h0r - abi * h0i + bur
    hi = abr * h0i + abi * h0r + bui
    hr_ref[...] = hr
    hi_ref[...] = hi
    y = _dot(jnp.concatenate([hr, hi], axis=1).astype(BF16), cm_ref[...])
    ob_ref[...] = _s5_out(y, u, d_ref, wg_ref, bg_ref)
    small = z[:, COL_SMALL:COL_SMALL + SMALL_W]
    gq_ref[...] = z[:, COL_CQK:COL_CQK + 128] * (C_DK ** -0.5)
    gk_ref[...] = z[:, COL_CQK + 128:COL_CQK + 256]
    ga_ref[...] = _gla_log_a(small, wa_ref, ba_ref)
    x = z[:, COL_DQKV:COL_DQKV + DN_QKV_W]
    y = cw_ref[3:4, :] * x
    for i in range(CONV_W - 1):
        y = y + cw_ref[i:i + 1, :] * conv_ref[i]
    cs_ref[0] = conv_ref[1]
    cs_ref[1] = conv_ref[2]
    cs_ref[2] = x
    q, k, v = _dn_qkv(y, seg_ref)
    dq_ref[...] = q
    dk_ref[...] = k
    dv_ref[...] = v
    beta, g = _dn_gates(small, acoef_ref, dtb_ref)
    lane = lax.broadcasted_iota(jnp.int32, small.shape, 1)
    dgb_ref[...] = jnp.where(lane >= SM_A, jnp.exp(g), beta)


def _dec_prep(z, h0r, h0i, conv3, sp, d, wg, bg, wa, ba, cw, acoef, dtb):
    DB = z.shape[0]
    sds = lambda *s: jax.ShapeDtypeStruct(s, F32)
    out_shape = (sds(DB, 256), sds(DB, S5_STATE), sds(DB, S5_STATE), sds(DB, 128), sds(DB, 128), sds(DB, 128),
                 sds(DB, 256), sds(DB, 256), sds(DB, 256), sds(DB, 128), sds(CONV_W - 1, DB, DN_QKV_W))
    return pl.pallas_call(
        _dec_prep_kernel,
        out_shape=out_shape,
        compiler_params=pltpu.CompilerParams(vmem_limit_bytes=VMEM_LIMIT_BYTES),
        name="dec_prep",
    )(z, h0r, h0i, conv3, sp["wb"], sp["cm"].astype(BF16), sp["ab_re"], sp["ab_im"], d, wg, bg, wa, ba, cw, acoef, dtb,
      _seg_matrix(256, 64))


def _gla_step_kernel(s_ref, q_ref, k_ref, a_ref, v_ref, cg_ref, gain_ref, sn_ref, o_ref):
    v = v_ref[...]
    sn = s_ref[...] * jnp.exp(a_ref[...]) + k_ref[...] * v[:, None, :]
    sn_ref[...] = sn
    o = jnp.sum(q_ref[...] * sn, axis=1)
    o_ref[...] = _rms(o, gain_ref[...]) * _silu(cg_ref[...])


def _gla_step(s, qcol, kcol, acol, v, cg, gain, tb):
    n = s.shape[0]
    big = pl.BlockSpec((tb, C_DK, C_DV), lambda i: (i, 0, 0))
    row = pl.BlockSpec((tb, C_DV), lambda i: (i, 0))
    return pl.pallas_call(
        _gla_step_kernel,
        out_shape=(jax.ShapeDtypeStruct(s.shape, F32), jax.ShapeDtypeStruct((n, C_DV), F32)),
        grid=(n // tb,),
        in_specs=[big, big, big, big, row, row, _full_spec((1, C_DV))],
        out_specs=(big, row),
        compiler_params=_cparams("parallel"),
        name="gla_step",
    )(s, qcol, kcol, acol, v, cg, gain)


def _dn_step_kernel(s_ref, q_ref, k_ref, v_ref, eg_ref, beta_ref, dz_ref, gain_ref, sn_ref, o_ref):
    s = s_ref[...]
    kcol = k_ref[...]
    eg = eg_ref[...]
    ks = jnp.sum(kcol * s, axis=1)
    v_new = beta_ref[...] * (v_ref[...] - eg * ks)
    sn = s * eg[:, None, :] + kcol * v_new[:, None, :]
    sn_ref[...] = sn
    o = jnp.sum(q_ref[...] * sn, axis=1)
    o_ref[...] = _rms(o, gain_ref[...]) * _silu(dz_ref[...])


def _dn_step(s, qcol, kcol, v, eg, beta, dz, gain, tb):
    n = s.shape[0]
    big = pl.BlockSpec((tb, D_DK, D_DV), lambda i: (i, 0, 0))
    row = pl.BlockSpec((tb, D_DV), lambda i: (i, 0))
    return pl.pallas_call(
        _dn_step_kernel,
        out_shape=(jax.ShapeDtypeStruct(s.shape, F32), jax.ShapeDtypeStruct((n, D_DV), F32)),
        grid=(n // tb,),
        in_specs=[big, big, big, row, row, row, row, _full_spec((1, D_DV))],
        out_specs=(big, row),
        compiler_params=_cparams("parallel"),
        name="dn_step",
    )(s, qcol, kcol, v, eg, beta, dz, gain)


def _prep_w_in(w):
    o = np.cumsum([0, 256, 256, 256, 256, 128, 128, 256, 256, 16, 768, 4, 4, 256])
    q, k, v, u, cq, ck, cv, cg, lr, dqkv, db, da, dz = [w[:, o[i]:o[i + 1]] for i in range(13)]
    pad = jnp.zeros((w.shape[0], SMALL_W - 24), w.dtype)
    return jnp.concatenate([q, k, v, u, cq, ck, cv, dqkv, cg, dz, lr, db, da, pad], axis=1).astype(BF16)


def _lane_row(vals, offset):
    return jnp.zeros((1, 128), F32).at[0, offset:offset + vals.shape[0]].set(vals)


def kernel(x_prompt, x_sample, cache_k, cache_v, state_ssm_re, state_ssm_im, state_gla, state_delta, state_conv, page_table, p_prompt, p_sample, g_mix, w_in, lam_q1, lam_k1, lam_q2, lam_k2, attn_norm, s5_lam_re, s5_lam_im, s5_log_dt, s5_b_re, s5_b_im, s5_c_re, s5_c_im, s5_d, s5_w_glu, s5_b_glu, gla_w_a2, gla_b_a, gla_norm, dn_conv_w, dn_a_log, dn_dt_bias, dn_norm, w_branch_gate, b_branch_gate, w_branch, w_out, g_ffn, w_ff1, w_ff2, g_ple, w_ple_gate, w_ple_proj, g_final):
    B, S, _ = x_prompt.shape
    DB = x_sample.shape[0]
    T = B * S
    n_pool, page = cache_k.shape[1], cache_k.shape[2]
    hp = x_prompt.reshape(T, D_MODEL)
    hs = x_sample.reshape(DB, D_MODEL)
    row = lambda a: a.reshape(1, -1)
    ck_t = cache_k.transpose(0, 1, 3, 4, 2).reshape(DEPTH, n_pool, A_HEADS * 2 * A_DH, page)
    cv_t = cache_v.transpose(0, 1, 3, 4, 2).reshape(DEPTH, n_pool, A_HEADS * A_DV, page)
    st_p, st_s = [], []
    for l in range(DEPTH):
        lam_init = 0.8 - 0.6 * math.exp(-0.3 * l)
        final = l == DEPTH - 1
        w_in_l = _prep_w_in(w_in[l])
        wg, wb, wo = w_branch_gate[l].astype(BF16), w_branch[l].astype(BF16), w_out[l].astype(BF16)
        w1, w2 = w_ff1[l].astype(BF16), w_ff2[l].astype(BF16)
        wpg, wpp = w_ple_gate[l].astype(BF16), w_ple_proj[l].astype(BF16)
        lam_params = (row(lam_q1[l]), row(lam_k1[l]), row(lam_q2[l]), row(lam_k2[l]))
        sp = _s5_params(s5_lam_re[l], s5_lam_im[l], s5_log_dt[l], s5_b_re[l], s5_b_im[l], s5_c_re[l], s5_c_im[l])
        s5d, s5wg, s5bg = row(s5_d[l]), s5_w_glu[l].astype(BF16), row(s5_b_glu[l])
        wa = jnp.zeros((SMALL_W, 128), F32).at[SM_LR:SM_LR + C_RANK].set(gla_w_a2[l])
        ba = row(gla_b_a[l])
        gla_gain4 = row(jnp.tile(gla_norm[l], C_HEADS))
        dn_gain4 = row(jnp.tile(dn_norm[l], D_HEADS))
        acoef = _lane_row(-jnp.exp(dn_a_log[l]), SM_A)
        dtb = _lane_row(dn_dt_bias[l], SM_A)
        cw = dn_conv_w[l]

        zp = _inproj(hp, row(g_mix[l]), w_in_l, 512)
        oa = _attn_prompt(zp, lam_params, row(attn_norm[l]), B, S, 512, lam_init)
        ob, p_hre, p_him = _s5_prompt(zp, sp, s5d, s5wg, s5bg, B, S, 512)
        oc, p_gla_t = _gla_prompt(zp.reshape(B, S, Z_W), wa, ba, gla_gain4, B, S, 256)
        oc = oc.reshape(T, 256)
        od, p_dn_bd, p_conv = _dn_prompt(zp, cw, acoef, dtb, dn_gain4, B, S, 512)
        p_dn = jnp.stack([p_dn_bd[:, h * D_DK:(h + 1) * D_DK, h * D_DV:(h + 1) * D_DV] for h in range(D_HEADS)], axis=1)
        hp = _merge(hp, oa, ob, oc, od, row(g_mix[l]), wg, b_branch_gate[l], wb, wo, 512)
        hp = _ffn(hp, p_prompt[l].reshape(T, PLE_DIM), row(g_ffn[l]), w1, w2, row(g_ple[l]), wpg, wpp, row(g_final),
                  256, final)
        p_gla = jnp.stack([p_gla_t[:, h * C_DV:(h + 1) * C_DV, h * C_DK:(h + 1) * C_DK] for h in range(C_HEADS)],
                          axis=1).swapaxes(-1, -2)
        st_p.append((zp[:, COL_K:COL_K + 256].reshape(B, S, A_HEADS, 2 * A_DH),
                     zp[:, COL_V:COL_V + 256].reshape(B, S, A_HEADS, A_DV),
                     p_hre.reshape(B, S5_GROUPS, S5_P), p_him.reshape(B, S5_GROUPS, S5_P), p_gla, p_dn, p_conv))

        zs = _inproj(hs, row(g_mix[l]), w_in_l, DB)
        oa_s = _attn_decode(page_table, ck_t, cv_t, l, zs.reshape(DB, 1, Z_W), lam_params, row(attn_norm[l]),
                            lam_init).reshape(DB, 256)
        (ob_s, s_hre, s_him, gq, gk, ga, dq, dk, dv, dgb, s_conv3) = _dec_prep(
            zs, state_ssm_re[l].reshape(DB, S5_STATE), state_ssm_im[l].reshape(DB, S5_STATE),
            state_conv[l].swapaxes(0, 1), sp, s5d, s5wg, s5bg, wa, ba, cw, acoef, dtb)
        col = lambda a, h, dk, dv: jnp.broadcast_to(a.reshape(DB * h, dk, 1), (DB * h, dk, dv))
        s_gla, oc_s = _gla_step(state_gla[l].reshape(DB * C_HEADS, C_DK, C_DV), col(gq, C_HEADS, C_DK, C_DV),
                                col(gk, C_HEADS, C_DK, C_DV), col(ga, C_HEADS, C_DK, C_DV),
                                zs[:, COL_CV:COL_CV + 256].reshape(DB * C_HEADS, C_DV),
                                zs[:, COL_CG:COL_CG + 256].reshape(DB * C_HEADS, C_DV), row(gla_norm[l]), 128)
        lanes = lambda a, off: jnp.broadcast_to(a[:, off:off + D_HEADS].reshape(DB * D_HEADS, 1), (DB * D_HEADS, D_DV))
        s_dn, od_s = _dn_step(state_delta[l].reshape(DB * D_HEADS, D_DK, D_DV), col(dq, D_HEADS, D_DK, D_DV),
                              col(dk, D_HEADS, D_DK, D_DV), dv.reshape(DB * D_HEADS, D_DV), lanes(dgb, SM_A),
                              lanes(dgb, SM_B), zs[:, COL_DZ:COL_DZ + 256].reshape(DB * D_HEADS, D_DV),
                              row(dn_norm[l]), 64)
        hs = _merge(hs, oa_s, ob_s, oc_s.reshape(DB, 256), od_s.reshape(DB, 256), row(g_mix[l]), wg, b_branch_gate[l],
                    wb, wo, DB)
        hs = _ffn(hs, p_sample[l].reshape(DB, PLE_DIM), row(g_ffn[l]), w1, w2, row(g_ple[l]), wpg, wpp, row(g_final),
                  DB, final)
        st_s.append((zs[:, COL_K:COL_K + 256].reshape(DB, 1, A_HEADS, 2 * A_DH),
                     zs[:, COL_V:COL_V + 256].reshape(DB, 1, A_HEADS, A_DV),
                     s_hre.reshape(DB, S5_GROUPS, S5_P), s_him.reshape(DB, S5_GROUPS, S5_P),
                     s_gla.reshape(DB, C_HEADS, C_DK, C_DV), s_dn.reshape(DB, D_HEADS, D_DK, D_DV),
                     s_conv3.swapaxes(0, 1)))

    stk = lambda states, i: jnp.stack([s[i] for s in states])
    return (hp.reshape(B, S, D_MODEL), hs.reshape(DB, 1, D_MODEL),
            stk(st_p, 0), stk(st_p, 1), stk(st_p, 2), stk(st_p, 3), stk(st_p, 4), stk(st_p, 5), stk(st_p, 6),
            stk(st_s, 0), stk(st_s, 1), stk(st_s, 2), stk(st_s, 3), stk(st_s, 4), stk(st_s, 5), stk(st_s, 6))
```

```python
import functools
import math

import numpy as np
import jax
import jax.numpy as jnp
from jax import lax
from jax.experimental import pallas as pl
from jax.experimental.pallas import tpu as pltpu

F32 = jnp.float32
BF16 = jnp.bfloat16
HI = lax.Precision.HIGHEST

D_MODEL = 1024
DEPTH = 2
A_HEADS, A_DH, A_DV = 4, 32, 64
S5_GROUPS, S5_GROUP, S5_P = 16, 16, 64
S5_STATE = S5_GROUPS * S5_P
C_HEADS, C_DK, C_DV, C_RANK = 4, 32, 64, 16
GLA_TAU = 16.0
D_HEADS, D_DK, D_DV = 4, 64, 64
CONV_W = 4
DN_QKV_W = 768
D_FF = 4096
PLE_DIM = 256
EPS = 1e-6
BRANCH_W = 256

Z_W = 2944
COL_Q, COL_K, COL_V, COL_U = 0, 256, 512, 768
COL_CQK, COL_CV, COL_DQKV, COL_CG, COL_DZ, COL_SMALL = 1024, 1280, 1536, 2304, 2560, 2816
SMALL_W = 128
SM_LR, SM_B, SM_A = 0, 16, 20

S5_SCAN = 64
S5_LEVELS = 6
S5_PAD = S5_SCAN // 2
GLA_SUB = 16
DN_CHUNK = 64
VMEM_LIMIT_BYTES = 56 * 1024 * 1024
NEG_INF = float("-inf")


def _cparams(*sem):
    return pltpu.CompilerParams(dimension_semantics=sem, vmem_limit_bytes=VMEM_LIMIT_BYTES)


def _sigmoid(x):
    return 1.0 / (1.0 + jnp.exp(-x))


def _silu(x):
    return x * _sigmoid(x)


def _softplus(x):
    return jnp.maximum(x, 0.0) + jnp.log(1.0 + jnp.exp(-jnp.abs(x)))


def _gelu_tanh(x):
    return 0.5 * x * (1.0 + jnp.tanh(math.sqrt(2.0 / math.pi) * (x + 0.044715 * (x * x * x))))


def _rms(x, g):
    return x * lax.rsqrt(jnp.mean(x * x, axis=-1, keepdims=True) + EPS) * g


def _dot(a, b, **kw):
    return jnp.dot(a, b, preferred_element_type=F32, **kw)


def _dot_nt(a, b, **kw):
    return lax.dot_general(a, b, (((1,), (1,)), ((), ())), preferred_element_type=F32, **kw)


def _dot_tn(a, b, **kw):
    return lax.dot_general(a, b, (((0,), (0,)), ((), ())), preferred_element_type=F32, **kw)


def _full_spec(shape):
    nd = len(shape)
    return pl.BlockSpec(shape, lambda *_: (0,) * nd)


def _seg_matrix(width, seg):
    i = np.arange(width) // seg
    return jnp.asarray((i[:, None] == i[None, :]).astype(np.float32)).astype(BF16)


def _bf16_parts(x, parts):
    out = []
    for i in range(parts):
        p = x.astype(BF16)
        out.append(p)
        if i + 1 < parts:
            x = x - p.astype(F32)
    return out


def _dot01(x, e, parts):
    acc = None
    for p in _bf16_parts(x, parts):
        t = _dot(p, e)
        acc = t if acc is None else acc + t
    return acc


def _dot01_left(e, x, parts):
    acc = None
    for p in _bf16_parts(x, parts):
        t = _dot(e, p)
        acc = t if acc is None else acc + t
    return acc


def _head_rms(o, seg_ref, gain):
    ms = _dot01(o * o, seg_ref[...], 2) * (1.0 / 64.0)
    return o * lax.rsqrt(ms + EPS) * gain


def _inproj_kernel(x_ref, g_ref, w_ref, z_ref, *kv_refs):
    hn = _rms(x_ref[...], g_ref[...]).astype(BF16)
    z = _dot(hn, w_ref[...])
    z_ref[...] = z
    if kv_refs:
        kt_ref, vt_ref = kv_refs
        kt_ref[0] = z[:, COL_K:COL_K + 256].T
        vt_ref[0] = z[:, COL_V:COL_V + 256].T


def _inproj(x, g, w, tm, seq_len=None):
    T = x.shape[0]
    out_shape = [jax.ShapeDtypeStruct((T, Z_W), F32)]
    out_specs = [pl.BlockSpec((tm, Z_W), lambda i: (i, 0))]
    if seq_len is not None:
        nt = seq_len // tm
        out_shape += [jax.ShapeDtypeStruct((T // seq_len, 256, seq_len), F32)] * 2
        out_specs += [pl.BlockSpec((1, 256, tm), lambda i: (i // nt, 0, i % nt))] * 2
    return pl.pallas_call(
        _inproj_kernel,
        out_shape=tuple(out_shape),
        grid=(T // tm,),
        in_specs=[pl.BlockSpec((tm, D_MODEL), lambda i: (i, 0)), _full_spec((1, D_MODEL)), _full_spec((D_MODEL, Z_W))],
        out_specs=tuple(out_specs),
        compiler_params=_cparams("parallel"),
        name="inproj",
    )(x, g, w)


def _merge_kernel(h_ref, oa_ref, ob_ref, oc_ref, od_ref, g_ref, wg_ref, bg_ref, wb_ref, wo_ref, out_ref):
    h = h_ref[...]
    hn = _rms(h, g_ref[...]).astype(BF16)
    merged = None
    for n, br in enumerate((oa_ref, ob_ref, oc_ref, od_ref)):
        gate = _sigmoid(_dot(hn, wg_ref[n]) + bg_ref[n:n + 1, :])
        term = gate * _dot(br[...].astype(BF16), wb_ref[n])
        merged = term if merged is None else merged + term
    out_ref[...] = h + _dot(merged.astype(BF16), wo_ref[...])


def _merge(h, oa, ob, oc, od, g, wg, bg, wb, wo, tm):
    T = h.shape[0]
    row = lambda w: pl.BlockSpec((tm, w), lambda i: (i, 0))
    return pl.pallas_call(
        _merge_kernel,
        out_shape=jax.ShapeDtypeStruct((T, D_MODEL), F32),
        grid=(T // tm,),
        in_specs=[row(D_MODEL), row(BRANCH_W), row(BRANCH_W), row(BRANCH_W), row(BRANCH_W),
                  _full_spec((1, D_MODEL)), _full_spec((4, D_MODEL, D_MODEL)), _full_spec((4, D_MODEL)),
                  _full_spec((4, BRANCH_W, D_MODEL)), _full_spec((D_MODEL, D_MODEL))],
        out_specs=row(D_MODEL),
        compiler_params=_cparams("parallel"),
        name="merge",
    )(h, oa, ob, oc, od, g, wg, bg, wb, wo)


def _ffn_kernel(h_ref, p_ref, gf_ref, w1_ref, w2_ref, gp_ref, wpg_ref, wpp_ref, gfin_ref, out_ref, *, final):
    h = h_ref[...]
    hf = _rms(h, gf_ref[...]).astype(BF16)
    acc = h
    step = 1024
    for j in range(D_FF // step):
        a = jnp.maximum(_dot(hf, w1_ref[:, j * step:(j + 1) * step]), 0.0)
        acc = acc + _dot((a * a).astype(BF16), w2_ref[j * step:(j + 1) * step, :])
    pg = _sigmoid(_dot(_rms(acc, gp_ref[...]).astype(BF16), wpg_ref[...]))
    out = acc + pg * _dot(p_ref[...].astype(BF16), wpp_ref[...])
    if final:
        out = _rms(out, gfin_ref[...])
    out_ref[...] = out


def _ffn(h, p, gf, w1, w2, gp, wpg, wpp, gfin, tm, final):
    T = h.shape[0]
    row = lambda w: pl.BlockSpec((tm, w), lambda i: (i, 0))
    return pl.pallas_call(
        functools.partial(_ffn_kernel, final=final),
        out_shape=jax.ShapeDtypeStruct((T, D_MODEL), F32),
        grid=(T // tm,),
        in_specs=[row(D_MODEL), row(PLE_DIM), _full_spec((1, D_MODEL)), _full_spec((D_MODEL, D_FF)),
                  _full_spec((D_FF, D_MODEL)), _full_spec((1, D_MODEL)), _full_spec((D_MODEL, D_MODEL)),
                  _full_spec((PLE_DIM, D_MODEL)), _full_spec((1, D_MODEL))],
        out_specs=row(D_MODEL),
        compiler_params=_cparams("parallel"),
        name="ffn_ple",
    )(h, p, gf, w1, w2, gp, wpg, wpp, gfin)


_SLOPES = [float(s) for s in 2.0 ** (-8.0 * np.arange(1, A_HEADS + 1) / A_HEADS)]
LOG2E = 1.0 / math.log(2.0)


def _lam_value(lq1, lk1, lq2, lk2, lam_init):
    return (jnp.exp(jnp.sum(lq1[...] * lk1[...], axis=-1, keepdims=True))
            - jnp.exp(jnp.sum(lq2[...] * lk2[...], axis=-1, keepdims=True)) + lam_init)


def _attn_prompt_kernel(qi_ref, ki_ref, q_ref, k_ref, v_ref, lq1, lk1, lq2, lk2, gain_ref, o_ref,
                        qm_scr, m_scr, acc_scr, *, tq, lam_init):
    pair = pl.program_id(1)
    qi = qi_ref[pair]
    ki = ki_ref[pair]

    @pl.when(ki == 0)
    def _init():
        q = q_ref[...] * (A_DH ** -0.5 * LOG2E)
        lane = lax.broadcasted_iota(jnp.int32, q.shape, 1)
        for i in range(2 * A_HEADS):
            lo = (i // 2) * 2 * A_DH + (i % 2) * A_DH
            qm_scr[i] = jnp.where((lane >= lo) & (lane < lo + A_DH), q, 0.0).astype(BF16)
        m_scr[...] = jnp.full(m_scr.shape, NEG_INF, F32)
        acc_scr[...] = jnp.zeros(acc_scr.shape, F32)

    def _step(diag):
        k = k_ref[...].astype(BF16)
        vt = v_ref[...].T
        kpos = (lax.broadcasted_iota(jnp.int32, (tq, 1), 0) + (ki - qi) * tq).astype(F32)
        ones_row = jnp.where(lax.broadcasted_iota(jnp.int32, (A_DV, tq), 0) == 0, 1.0, 0.0)
        if diag:
            causal = (lax.broadcasted_iota(jnp.int32, (tq, tq), 0) <= lax.broadcasted_iota(jnp.int32, (tq, tq), 1))
        qq = jnp.concatenate([qm_scr[i] for i in range(2 * A_HEADS)], axis=0)
        s_all = _dot_nt(k, qq)
        for h in range(A_HEADS):
            vext = jnp.concatenate([vt[h * A_DV:(h + 1) * A_DV, :], ones_row], axis=0).astype(BF16)
            bias = kpos * (_SLOPES[h] * LOG2E)
            if diag:
                bias = jnp.where(causal, bias, NEG_INF)
                bias = jnp.concatenate([bias, bias], axis=1)
            i = 2 * h
            s = s_all[:, i * tq:(i + 2) * tq] + bias
            m_prev = jnp.concatenate([m_scr[i], m_scr[i + 1]], axis=1)
            m_new = jnp.maximum(m_prev, jnp.max(s, axis=0, keepdims=True))
            alpha = jnp.exp2(m_prev - m_new)
            p = jnp.exp2(s - m_new)
            pv = _dot(vext, p.astype(BF16))
            acc_scr[i] = alpha[:, :tq] * acc_scr[i] + pv[:, :tq]
            acc_scr[i + 1] = alpha[:, tq:] * acc_scr[i + 1] + pv[:, tq:]
            m_scr[i] = m_new[:, :tq]
            m_scr[i + 1] = m_new[:, tq:]

    @pl.when(ki < qi)
    def _off_diagonal():
        _step(False)

    @pl.when(ki == qi)
    def _diagonal():
        _step(True)
        lam = _lam_value(lq1, lk1, lq2, lk2, lam_init)
        outs = []
        for h in range(A_HEADS):
            a1, a2 = acc_scr[2 * h], acc_scr[2 * h + 1]
            o = a1[:A_DV, :] / a1[A_DV:A_DV + 1, :] - lam * (a2[:A_DV, :] / a2[A_DV:A_DV + 1, :])
            outs.append(o * lax.rsqrt(jnp.mean(o * o, axis=0, keepdims=True) + EPS))
        gain = jnp.concatenate([gain_ref[...]] * A_HEADS, axis=1)
        o_ref[...] = jnp.concatenate(outs, axis=0).T * gain * (1.0 - lam_init)


def _attn_prompt(z, lam_params, gain, B, S, tq, lam_init):
    nq = S // tq
    pairs = [(qi, ki) for qi in range(nq) for ki in range(qi + 1)]
    qi_arr = jnp.asarray([p[0] for p in pairs], jnp.int32)
    ki_arr = jnp.asarray([p[1] for p in pairs], jnp.int32)
    vec = pl.BlockSpec((1, A_DH), lambda b, p, qa, ka: (0, 0))
    grid_spec = pltpu.PrefetchScalarGridSpec(
        num_scalar_prefetch=2,
        grid=(B, len(pairs)),
        in_specs=[pl.BlockSpec((tq, 256), lambda b, p, qa, ka: (b * nq + qa[p], COL_Q // 256)),
                  pl.BlockSpec((tq, 256), lambda b, p, qa, ka: (b * nq + ka[p], COL_K // 256)),
                  pl.BlockSpec((tq, 256), lambda b, p, qa, ka: (b * nq + ka[p], COL_V // 256)),
                  vec, vec, vec, vec, pl.BlockSpec((1, A_DV), lambda b, p, qa, ka: (0, 0))],
        out_specs=pl.BlockSpec((tq, 256), lambda b, p, qa, ka: (b * nq + qa[p], 0)),
        scratch_shapes=[pltpu.VMEM((2 * A_HEADS, tq, 256), BF16), pltpu.VMEM((2 * A_HEADS, 1, tq), F32),
                        pltpu.VMEM((2 * A_HEADS, 128, tq), F32)],
    )
    return pl.pallas_call(
        functools.partial(_attn_prompt_kernel, tq=tq, lam_init=lam_init),
        out_shape=jax.ShapeDtypeStruct((B * S, A_HEADS * A_DV), F32),
        grid_spec=grid_spec,
        compiler_params=_cparams("parallel", "arbitrary"),
        name="attn_prompt",
    )(qi_arr, ki_arr, z, z, z, *lam_params, gain)


def _attn_decode_kernel(pt_ref, *refs, n_pages, page, lam_init):
    k_refs = refs[:n_pages]
    v_refs = refs[n_pages:2 * n_pages]
    z_ref, lq1, lk1, lq2, lk2, gain_ref, o_ref = refs[2 * n_pages:]
    del pt_ref
    past = n_pages * page
    zrow = z_ref[0]
    q = zrow[:, COL_Q:COL_Q + 256] * (A_DH ** -0.5)
    k_own = zrow[:, COL_K:COL_K + 256]
    v_own = zrow[:, COL_V:COL_V + 256]
    lane = lax.broadcasted_iota(jnp.int32, (2 * A_HEADS, 256), 1)
    r = lax.broadcasted_iota(jnp.int32, (2 * A_HEADS, 256), 0)
    lo = (r % A_HEADS) * 2 * A_DH + (r // A_HEADS) * A_DH
    qm = jnp.where((lane >= lo) & (lane < lo + A_DH), jnp.broadcast_to(q, (2 * A_HEADS, 256)), 0.0)
    qmb = qm.astype(BF16)
    s = jnp.concatenate([_dot(qmb, kr[0, 0].astype(BF16)) for kr in k_refs], axis=1)
    rr = lax.broadcasted_iota(jnp.int32, (2 * A_HEADS, 1), 0) % A_HEADS
    slope = jnp.zeros((2 * A_HEADS, 1), F32)
    for h in range(A_HEADS):
        slope = jnp.where(rr == h, _SLOPES[h], slope)
    kpos = lax.broadcasted_iota(jnp.int32, (2 * A_HEADS, past), 1).astype(F32)
    s = s - slope * (float(past) - kpos)
    s_own = jnp.sum(qm * k_own, axis=-1, keepdims=True)
    m = jnp.maximum(jnp.max(s, axis=-1, keepdims=True), s_own)
    p = jnp.exp(s - m)
    p_own = jnp.exp(s_own - m)
    denom = jnp.sum(p, axis=-1, keepdims=True) + p_own
    pb = p.astype(BF16)
    o = p_own * v_own
    for j, vr in enumerate(v_refs):
        o = o + _dot_nt(pb[:, j * page:(j + 1) * page], vr[0, 0].astype(BF16))
    o = o / denom
    lam = _lam_value(lq1, lk1, lq2, lk2, lam_init)
    d = o[:A_HEADS] - lam * o[A_HEADS:]
    hl = lax.broadcasted_iota(jnp.int32, (A_HEADS, 256), 1) // A_DV
    hr = lax.broadcasted_iota(jnp.int32, (A_HEADS, 256), 0)
    own = hl == hr
    d = jnp.where(own, d, 0.0)
    ms = jnp.sum(d * d, axis=-1, keepdims=True) * (1.0 / A_DV)
    d = d * lax.rsqrt(ms + EPS)
    gain = jnp.concatenate([gain_ref[...]] * A_HEADS, axis=1)
    o_ref[0] = jnp.sum(d, axis=0, keepdims=True) * gain * (1.0 - lam_init)


def _attn_decode(page_table, ck, cv, layer, z3, lam_params, gain, lam_init):
    DB, n_pages = page_table.shape
    page = ck.shape[-1]
    kv_specs = [pl.BlockSpec((1, 1, 256, page), functools.partial(lambda b, pt, j: (layer, pt[b, j], 0, 0), j=j))
                for j in range(n_pages)]
    vec = pl.BlockSpec((1, A_DH), lambda b, pt: (0, 0))
    grid_spec = pltpu.PrefetchScalarGridSpec(
        num_scalar_prefetch=1,
        grid=(DB,),
        in_specs=kv_specs + kv_specs + [pl.BlockSpec((1, 1, Z_W), lambda b, pt: (b, 0, 0)), vec, vec, vec, vec,
                                        pl.BlockSpec((1, A_DV), lambda b, pt: (0, 0))],
        out_specs=pl.BlockSpec((1, 1, 256), lambda b, pt: (b, 0, 0)),
    )
    return pl.pallas_call(
        functools.partial(_attn_decode_kernel, n_pages=n_pages, page=page, lam_init=lam_init),
        out_shape=jax.ShapeDtypeStruct((DB, 1, 256), F32),
        grid_spec=grid_spec,
        compiler_params=_cparams("arbitrary"),
        name="attn_decode",
    )(page_table, *([ck] * n_pages), *([cv] * n_pages), z3, *lam_params, gain)


def _s5_params(lam_re, lam_im, log_dt, b_re, b_im, c_re, c_im):
    dt = jnp.exp(log_dt)[:, None]
    lr, li = lam_re, lam_im
    a, th = lr * dt, li * dt
    mag = jnp.exp(a)
    ab_re, ab_im = mag * jnp.cos(th), mag * jnp.sin(th)
    den = lr * lr + li * li
    nr, ni = ab_re - 1.0, ab_im
    f_re = (nr * lr + ni * li) / den
    f_im = (ni * lr - nr * li) / den
    bb_re = f_re[..., None] * b_re - f_im[..., None] * b_im
    bb_im = f_re[..., None] * b_im + f_im[..., None] * b_re
    eye = jnp.eye(S5_GROUPS, dtype=F32)
    expand_b = lambda m: jnp.einsum("gpn,gh->gnhp", m, eye).reshape(S5_GROUPS * S5_GROUP, S5_STATE)
    wb = jnp.concatenate([expand_b(bb_re), expand_b(bb_im)], axis=1)
    expand_c = lambda m: jnp.einsum("gnp,gh->gphn", m, eye).reshape(S5_STATE, S5_GROUPS * S5_GROUP)
    cm = jnp.concatenate([expand_c(c_re), -expand_c(c_im)], axis=0)
    pr, pi = ab_re.reshape(1, S5_STATE), ab_im.reshape(1, S5_STATE)
    res, ims = [pr], [pi]
    for _ in range(S5_LEVELS - 1):
        pr, pi = pr * pr - pi * pi, 2.0 * pr * pi
        res.append(pr)
        ims.append(pi)
    return dict(wb=wb, cm=cm, ab_re=res[0], ab_im=ims[0], pow_re=jnp.concatenate(res, axis=0),
                pow_im=jnp.concatenate(ims, axis=0))


def _s5_out(y, u, d_ref, wg_ref, bg_ref):
    y = _gelu_tanh(y + d_ref[...] * u)
    return y * _sigmoid(_dot(y.astype(BF16), wg_ref[...]) + bg_ref[...])


def _s5_prompt_kernel(u_ref, wb_ref, cm_ref, pr_ref, pi_ref, d_ref, wg_ref, bg_ref, o_ref, hre_ref, him_ref,
                      bu_scr, hb_scr, sr_scr, si_scr, hr_scr, hi_scr, *, ts):
    t = pl.program_id(1)
    n = S5_SCAN
    P = S5_STATE
    pad = S5_PAD

    @pl.when(t == 0)
    def _init():
        hr_scr[...] = jnp.zeros(hr_scr.shape, F32)
        hi_scr[...] = jnp.zeros(hi_scr.shape, F32)
        sr_scr[0:pad, :] = jnp.zeros((pad, P), F32)
        si_scr[0:pad, :] = jnp.zeros((pad, P), F32)

    u = u_ref[...]
    bu_scr[...] = _dot(u.astype(BF16), wb_ref[...])

    def chunk(c, carry):
        r0 = pl.multiple_of(c * n, n)
        sr_scr[pad:pad + n, :] = bu_scr[pl.ds(r0, n), 0:P]
        si_scr[pad:pad + n, :] = bu_scr[pl.ds(r0, n), P:2 * P]
        ar, ai = pr_ref[0:1, :], pi_ref[0:1, :]
        h0r, h0i = hr_scr[...], hi_scr[...]
        sr_scr[pad:pad + 1, :] = sr_scr[pad:pad + 1, :] + (ar * h0r - ai * h0i)
        si_scr[pad:pad + 1, :] = si_scr[pad:pad + 1, :] + (ar * h0i + ai * h0r)
        for lv in range(S5_LEVELS):
            d = 1 << lv
            ar, ai = pr_ref[lv:lv + 1, :], pi_ref[lv:lv + 1, :]
            cr, ci = sr_scr[pad:pad + n, :], si_scr[pad:pad + n, :]
            qr, qi = sr_scr[pad - d:pad - d + n, :], si_scr[pad - d:pad - d + n, :]
            nr, ni = cr + (ar * qr - ai * qi), ci + (ar * qi + ai * qr)
            if lv + 1 < S5_LEVELS:
                sr_scr[pad:pad + n, :] = nr
                si_scr[pad:pad + n, :] = ni
        hr_scr[...] = nr[n - 1:n, :]
        hi_scr[...] = ni[n - 1:n, :]
        hb_scr[pl.ds(r0, n), 0:P] = nr.astype(BF16)
        hb_scr[pl.ds(r0, n), P:2 * P] = ni.astype(BF16)
        return carry

    lax.fori_loop(0, ts // n, chunk, 0)
    y = _dot(hb_scr[...], cm_ref[...])
    o_ref[...] = _s5_out(y, u, d_ref, wg_ref, bg_ref)
    hre_ref[0] = hr_scr[...]
    him_ref[0] = hi_scr[...]


def _s5_prompt(z, sp, d, wg, bg, B, S, ts):
    nt = S // ts
    tab = _full_spec((S5_LEVELS, S5_STATE))
    st = pl.BlockSpec((1, 1, S5_STATE), lambda b, t: (b, 0, 0))
    return pl.pallas_call(
        functools.partial(_s5_prompt_kernel, ts=ts),
        out_shape=(jax.ShapeDtypeStruct((B * S, 256), F32), jax.ShapeDtypeStruct((B, 1, S5_STATE), F32),
                   jax.ShapeDtypeStruct((B, 1, S5_STATE), F32)),
        grid=(B, nt),
        in_specs=[pl.BlockSpec((ts, 256), lambda b, t: (b * nt + t, COL_U // 256)),
                  _full_spec((256, 2 * S5_STATE)), _full_spec((2 * S5_STATE, 256)), tab, tab,
                  _full_spec((1, 256)), _full_spec((256, 256)), _full_spec((1, 256))],
        out_specs=(pl.BlockSpec((ts, 256), lambda b, t: (b * nt + t, 0)), st, st),
        scratch_shapes=[pltpu.VMEM((ts, 2 * S5_STATE), F32), pltpu.VMEM((ts, 2 * S5_STATE), BF16),
                        pltpu.VMEM((S5_PAD + S5_SCAN, S5_STATE), F32), pltpu.VMEM((S5_PAD + S5_SCAN, S5_STATE), F32),
                        pltpu.VMEM((1, S5_STATE), F32), pltpu.VMEM((1, S5_STATE), F32)],
        compiler_params=_cparams("parallel", "arbitrary"),
        name="s5_prompt",
    )(z, sp["wb"].astype(BF16), sp["cm"].astype(BF16), sp["pow_re"], sp["pow_im"], d, wg, bg)


def _gla_log_a(small, wa_ref, ba_ref):
    x = _dot(small, wa_ref[...], precision=HI) + ba_ref[...]
    return (jnp.minimum(x, 0.0) - jnp.log(1.0 + jnp.exp(-jnp.abs(x)))) * (1.0 / GLA_TAU)


def _gla_prompt_kernel(qk_ref, v_ref, cg_ref, sm_ref, wa_ref, ba_ref, gain_ref, segx_ref, seg_ref, bmask_ref,
                       o_ref, st_ref, b_scr, o_scr, st_scr, *, ts, nb):
    t = pl.program_id(0)
    n = GLA_SUB
    seqs = range(nb)

    @pl.when(t == 0)
    def _init():
        st_scr[...] = jnp.zeros(st_scr.shape, F32)

    ri = lax.broadcasted_iota(jnp.int32, (ts, ts), 0)
    ci = lax.broadcasted_iota(jnp.int32, (ts, ts), 1)
    tri = jnp.where((ri >= ci) & (ri // n == ci // n), 1.0, 0.0).astype(BF16)
    for i in seqs:
        log_a = _gla_log_a(sm_ref[i], wa_ref, ba_ref)
        b_scr[i] = _dot01_left(tri, log_a, 3)
    rows = lax.broadcasted_iota(jnp.int32, (n, 128), 0)
    segx = segx_ref[...]
    bmask = bmask_ref[...]

    def body(c, carry):
        r0 = pl.multiple_of(c * n, n)
        qk = [qk_ref[i, pl.ds(r0, n), :] for i in seqs]
        q = [x[:, :128] * (C_DK ** -0.5) for x in qk]
        k = [x[:, 128:] for x in qk]
        v = [v_ref[i, pl.ds(r0, n), :] for i in seqs]
        b = [b_scr[i, pl.ds(r0, n), :] for i in seqs]
        pall = []
        for i in seqs:
            parts = []
            for s in range(n):
                e = jnp.exp(jnp.where(rows >= s, b[i] - b[i][s:s + 1, :], NEG_INF))
                parts.append(q[i] * k[i][s:s + 1, :] * e)
            pall.append(jnp.concatenate(parts, axis=0).astype(BF16))
        aexp = [_dot(pall[i], segx) for i in seqs]
        st = [st_scr[i] for i in seqs]
        inter = [_dot_nt((q[i] * jnp.exp(b[i])).astype(BF16), st[i].astype(BF16)) for i in seqs]
        b_last = [b[i][n - 1:n, :] for i in seqs]
        upd = [_dot_tn(v[i].astype(BF16), (k[i] * jnp.exp(b_last[i] - b[i])).astype(BF16)) for i in seqs]
        for i in seqs:
            acc = inter[i]
            for s in range(n):
                acc = acc + aexp[i][s * n:(s + 1) * n, :] * v[i][s:s + 1, :]
            o_scr[i, pl.ds(r0, n), :] = acc
            st_scr[i] = st[i] * jnp.exp(b_last[i]) + upd[i] * bmask
        return carry

    lax.fori_loop(0, ts // n, body, 0)
    for i in seqs:
        o_ref[i] = _head_rms(o_scr[i], seg_ref, gain_ref[...]) * _silu(cg_ref[i])
    st_ref[...] = st_scr[...]


def _gla_consts():
    hk = np.arange(128) // C_DK
    hv = np.arange(256) // C_DV
    segx = jnp.asarray((hk[:, None] == hv[None, :]).astype(np.float32)).astype(BF16)
    bmask = jnp.asarray((hv[:, None] == hk[None, :]).astype(np.float32))
    return segx, bmask


def _gla_prompt(z, wa, ba, gain, B, S, ts):
    nt = S // ts
    segx, bmask = _gla_consts()
    blk = lambda w, col: pl.BlockSpec((B, ts, w), lambda t: (0, t, col // w))
    return pl.pallas_call(
        functools.partial(_gla_prompt_kernel, ts=ts, nb=B),
        out_shape=(jax.ShapeDtypeStruct((B, S, 256), F32), jax.ShapeDtypeStruct((B, 256, 128), F32)),
        grid=(nt,),
        in_specs=[blk(256, COL_CQK), blk(256, COL_CV), blk(256, COL_CG), blk(SMALL_W, COL_SMALL),
                  _full_spec((SMALL_W, 128)), _full_spec((1, 128)), _full_spec((1, 256)),
                  _full_spec((128, 256)), _full_spec((256, 256)), _full_spec((256, 128))],
        out_specs=(pl.BlockSpec((B, ts, 256), lambda t: (0, t, 0)), _full_spec((B, 256, 128))),
        scratch_shapes=[pltpu.VMEM((B, ts, 128), F32), pltpu.VMEM((B, ts, 256), F32), pltpu.VMEM((B, 256, 128), F32)],
        compiler_params=_cparams("arbitrary"),
        name="gla_prompt",
    )(z, z, z, z, wa, ba, gain, segx, _seg_matrix(256, 64), bmask)


def _dn_gates(small, acoef_ref, dtb_ref):
    beta = _sigmoid(small)
    g = acoef_ref[...] * _softplus(small + dtb_ref[...])
    return beta, g


def _dn_qkv(y, seg_ref):
    y = _silu(y)
    q, k, v = y[:, :256], y[:, 256:512], y[:, 512:768]
    nq = _dot01(q * q, seg_ref[...], 2)
    nk = _dot01(k * k, seg_ref[...], 2)
    q = q * lax.rsqrt(nq + EPS) * (D_DK ** -0.5)
    k = k * lax.rsqrt(nk + EPS)
    return q, k, v


def _dn_prompt_kernel(x_ref, dz_ref, sm_ref, cw_ref, acoef_ref, dtb_ref, gain_ref, seg_ref, esel_ref,
                      o_ref, st_ref, cs_ref, xp_scr, q_scr, k_scr, v_scr, gb_scr, ge_scr, be_scr,
                      u_scr, w_scr, att_scr, o_scr, st_scr, *, ts, nb):
    t = pl.program_id(0)
    C = DN_CHUNK
    PADR = 8
    W = D_HEADS * D_DK
    seqs = range(nb)

    @pl.when(t == 0)
    def _init():
        st_scr[...] = jnp.zeros(st_scr.shape, F32)
        for i in seqs:
            xp_scr[i, 0:PADR, :] = jnp.zeros((PADR, DN_QKV_W), F32)

    ri = lax.broadcasted_iota(jnp.int32, (ts, ts), 0)
    ci = lax.broadcasted_iota(jnp.int32, (ts, ts), 1)
    tri = jnp.where((ri >= ci) & (ri // C == ci // C), 1.0, 0.0).astype(BF16)
    lane = lax.broadcasted_iota(jnp.int32, (ts, 128), 1)
    for i in seqs:
        x = x_ref[i]
        xp_scr[i, PADR:PADR + ts, :] = x
        y = cw_ref[3:4, :] * x
        for j in range(CONV_W - 1):
            y = y + cw_ref[j:j + 1, :] * xp_scr[i, PADR - 3 + j:PADR - 3 + j + ts, :]
        xp_scr[i, PADR - 3:PADR, :] = x[ts - 3:ts, :]
        cs_ref[i] = x[ts - 3:ts, :]
        q, k, v = _dn_qkv(y, seg_ref)
        q_scr[i] = q
        k_scr[i] = k
        v_scr[i] = v
        beta, g = _dn_gates(sm_ref[i], acoef_ref, dtb_ref)
        gcum = _dot01_left(tri, g, 3)
        gb = jnp.where(lane >= SM_A, gcum, beta)
        gb_scr[i] = gb
        ex = _dot01(gb, esel_ref[...], 3)
        ge_scr[i] = ex[:, :W]
        be_scr[i] = ex[:, W:]

    rr = lax.broadcasted_iota(jnp.int32, (W, W), 0)
    cc = lax.broadcasted_iota(jnp.int32, (W, W), 1)
    same_head = (rr // C) == (cc // C)
    lower_incl = same_head & (rr >= cc)
    lower_strict = same_head & (rr > cc)
    eye = jnp.where(rr == cc, 1.0, 0.0)
    blocks = [(rr // w) == (cc // w) for w in (8, 16, 32, C)]

    def stack_heads(a):
        return jnp.where(same_head, jnp.concatenate([a] * D_HEADS, axis=0), 0.0)

    def collapse(a):
        return a[0:C] + a[C:2 * C] + a[2 * C:3 * C] + a[3 * C:4 * C]

    both = seqs

    def solve_body(c, carry):
        r0 = pl.multiple_of(c * C, C)
        qc = [q_scr[i, pl.ds(r0, C), :] for i in seqs]
        kc = [k_scr[i, pl.ds(r0, C), :] for i in seqs]
        vc = [v_scr[i, pl.ds(r0, C), :] for i in seqs]
        gbc = [gb_scr[i, pl.ds(r0, C), :] for i in seqs]
        ge = [ge_scr[i, pl.ds(r0, C), :] for i in seqs]
        be = [be_scr[i, pl.ds(r0, C), :] for i in seqs]
        decay, kst = [], []
        for j in both:
            gbt = gbc[j].T
            gcol = jnp.concatenate([gbc[j][:, SM_A + h:SM_A + h + 1] for h in range(D_HEADS)], axis=0)
            grow = jnp.concatenate([gbt[SM_A + h:SM_A + h + 1, :] for h in range(D_HEADS)], axis=1)
            decay.append(jnp.exp(jnp.where(lower_incl, gcol - grow, NEG_INF)))
            kst.append(stack_heads(kc[j]).astype(BF16))
        kq = [_dot_nt(jnp.concatenate([stack_heads(kc[j] * be[j]), stack_heads(qc[j])], axis=0).astype(BF16), kst[j])
              for j in both]
        m = [jnp.where(lower_strict, -(kq[j][:W] * decay[j]), 0.0) for j in both]
        m1 = [jnp.where(blocks[0], m[j], 0.0) for j in both]
        m1b = [x.astype(BF16) for x in m1]
        m2 = [_dot(m1b[j], m1b[j]) for j in both]
        s1 = [eye + m1[j] for j in both]
        r2 = [_dot(m2[j].astype(BF16), jnp.concatenate([m2[j], s1[j]], axis=1).astype(BF16)) for j in both]
        s3 = [s1[j] + r2[j][:, W:] for j in both]
        tm = [s3[j] + _dot(r2[j][:, :W].astype(BF16), s3[j].astype(BF16)) for j in both]
        for inner, outer in zip(blocks[:-1], blocks[1:]):
            tmb = [x.astype(BF16) for x in tm]
            off = [jnp.where(outer & jnp.logical_not(inner), m[j], 0.0).astype(BF16) for j in both]
            t1 = [_dot(tmb[j], off[j]).astype(BF16) for j in both]
            tm = [tm[j] + _dot(t1[j], tmb[j]) for j in both]
        rhs = [jnp.concatenate([stack_heads(vc[j] * be[j]), stack_heads(kc[j] * (be[j] * jnp.exp(ge[j])))], axis=1)
               for j in both]
        uw = [_dot(tm[j].astype(BF16), rhs[j].astype(BF16)) for j in both]
        for j in both:
            u_scr[j, c] = collapse(uw[j][:, :W])
            w_scr[j, c] = collapse(uw[j][:, W:])
            att_scr[j, c] = (kq[j][W:] * decay[j]).astype(BF16)
        return carry

    lax.fori_loop(0, ts // C, solve_body, 0)

    def state_body(c, carry):
        r0 = pl.multiple_of(c * C, C)
        qc = [q_scr[i, pl.ds(r0, C), :] for i in seqs]
        kc = [k_scr[i, pl.ds(r0, C), :] for i in seqs]
        ge = [ge_scr[i, pl.ds(r0, C), :] for i in seqs]
        glast = [x[C - 1:C, :] for x in ge]
        s = [st_scr[i] for i in seqs]
        wq = [jnp.concatenate([w_scr[i, c], qc[i] * jnp.exp(ge[i])], axis=0) for i in seqs]
        sw = [_dot(wq[i].astype(BF16), s[i].astype(BF16)) for i in seqs]
        v_new = [u_scr[i, c] - sw[i][:C] for i in seqs]
        ov = [_dot(att_scr[i, c], stack_heads(v_new[i]).astype(BF16)) for i in seqs]
        upd = [_dot_tn((kc[i] * jnp.exp(glast[i] - ge[i])).astype(BF16), v_new[i].astype(BF16)) for i in seqs]
        for i in seqs:
            o_scr[i, pl.ds(r0, C), :] = sw[i][C:] + collapse(ov[i])
            st_scr[i] = s[i] * jnp.exp(glast[i]) + jnp.where(same_head, upd[i], 0.0)
        return carry

    lax.fori_loop(0, ts // C, state_body, 0)
    for i in seqs:
        o_ref[i] = _head_rms(o_scr[i], seg_ref, gain_ref[...]) * _silu(dz_ref[i])
    st_ref[...] = st_scr[...]


def _dn_esel():
    e = np.zeros((128, 512), np.float32)
    for h in range(D_HEADS):
        e[SM_A + h, h * 64:(h + 1) * 64] = 1.0
        e[SM_B + h, 256 + h * 64:256 + (h + 1) * 64] = 1.0
    return jnp.asarray(e).astype(BF16)


def _dn_prompt(z, cw, acoef, dtb, gain, B, S, ts):
    nt = S // ts
    nc = ts // DN_CHUNK
    W = D_HEADS * D_DK
    blk = lambda w, col: pl.BlockSpec((B, ts, w), lambda t: (0, t, col // w))
    return pl.pallas_call(
        functools.partial(_dn_prompt_kernel, ts=ts, nb=B),
        out_shape=(jax.ShapeDtypeStruct((B, S, 256), F32), jax.ShapeDtypeStruct((B, W, W), F32),
                   jax.ShapeDtypeStruct((B, CONV_W - 1, DN_QKV_W), F32)),
        grid=(nt,),
        in_specs=[blk(DN_QKV_W, COL_DQKV), blk(256, COL_DZ), blk(SMALL_W, COL_SMALL),
                  _full_spec((CONV_W, DN_QKV_W)), _full_spec((1, 128)), _full_spec((1, 128)), _full_spec((1, 256)),
                  _full_spec((256, 256)), _full_spec((128, 512))],
        out_specs=(pl.BlockSpec((B, ts, 256), lambda t: (0, t, 0)), _full_spec((B, W, W)),
                   _full_spec((B, CONV_W - 1, DN_QKV_W))),
        scratch_shapes=[pltpu.VMEM((B, 8 + ts, DN_QKV_W), F32), pltpu.VMEM((B, ts, 256), F32),
                        pltpu.VMEM((B, ts, 256), F32), pltpu.VMEM((B, ts, 256), F32), pltpu.VMEM((B, ts, 128), F32),
                        pltpu.VMEM((B, ts, W), F32), pltpu.VMEM((B, ts, W), F32),
                        pltpu.VMEM((B, nc, DN_CHUNK, W), F32), pltpu.VMEM((B, nc, DN_CHUNK, W), F32),
                        pltpu.VMEM((B, nc, W, W), BF16), pltpu.VMEM((B, ts, 256), F32), pltpu.VMEM((B, W, W), F32)],
        compiler_params=_cparams("arbitrary"),
        name="dn_prompt",
    )(z, z, z, cw, acoef, dtb, gain, _seg_matrix(256, 64), _dn_esel())


def _dec_prep_kernel(z_ref, h0r_ref, h0i_ref, conv_ref, wb_ref, cm_ref, abr_ref, abi_ref, d_ref, wg_ref, bg_ref,
                     wa_ref, ba_ref, cw_ref, acoef_ref, dtb_ref, seg_ref,
                     ob_ref, hr_ref, hi_ref, gq_ref, gk_ref, ga_ref, dq_ref, dk_ref, dv_ref, dgb_ref, cs_ref):
    z = z_ref[...]
    u = z[:, COL_U:COL_U + 256]
    bu = _dot(u, wb_ref[...], precision=HI)
    bur, bui = bu[:, :S5_STATE], bu[:, S5_STATE:]
    h0r, h0i = h0r_ref[...], h0i_ref[...]
    abr, abi = abr_ref[...], abi_ref[...]
    hr = abr * h0r - abi * h0i + bur
    hi = abr * h0i + abi * h0r + bui
    hr_ref[...] = hr
    hi_ref[...] = hi
    y = _dot(jnp.concatenate([hr, hi], axis=1).astype(BF16), cm_ref[...])
    ob_ref[...] = _s5_out(y, u, d_ref, wg_ref, bg_ref)
    small = z[:, COL_SMALL:COL_SMALL + SMALL_W]
    gq_ref[...] = z[:, COL_CQK:COL_CQK + 128] * (C_DK ** -0.5)
    gk_ref[...] = z[:, COL_CQK + 128:COL_CQK + 256]
    ga_ref[...] = _gla_log_a(small, wa_ref, ba_ref)
    x = z[:, COL_DQKV:COL_DQKV + DN_QKV_W]
    y = cw_ref[3:4, :] * x
    for i in range(CONV_W - 1):
        y = y + cw_ref[i:i + 1, :] * conv_ref[i]
    cs_ref[0] = conv_ref[1]
    cs_ref[1] = conv_ref[2]
    cs_ref[2] = x
    q, k, v = _dn_qkv(y, seg_ref)
    dq_ref[...] = q
    dk_ref[...] = k
    dv_ref[...] = v
    beta, g = _dn_gates(small, acoef_ref, dtb_ref)
    lane = lax.broadcasted_iota(jnp.int32, small.shape, 1)
    dgb_ref[...] = jnp.where(lane >= SM_A, jnp.exp(g), beta)


def _dec_prep(z, h0r, h0i, conv3, sp, d, wg, bg, wa, ba, cw, acoef, dtb):
    DB = z.shape[0]
    sds = lambda *s: jax.ShapeDtypeStruct(s, F32)
    out_shape = (sds(DB, 256), sds(DB, S5_STATE), sds(DB, S5_STATE), sds(DB, 128), sds(DB, 128), sds(DB, 128),
                 sds(DB, 256), sds(DB, 256), sds(DB, 256), sds(DB, 128), sds(CONV_W - 1, DB, DN_QKV_W))
    return pl.pallas_call(
        _dec_prep_kernel,
        out_shape=out_shape,
        compiler_params=pltpu.CompilerParams(vmem_limit_bytes=VMEM_LIMIT_BYTES),
        name="dec_prep",
    )(z, h0r, h0i, conv3, sp["wb"], sp["cm"].astype(BF16), sp["ab_re"], sp["ab_im"], d, wg, bg, wa, ba, cw, acoef, dtb,
      _seg_matrix(256, 64))


def _gla_step_kernel(s_ref, q_ref, k_ref, a_ref, v_ref, cg_ref, gain_ref, sn_ref, o_ref):
    v = v_ref[...]
    sn = s_ref[...] * jnp.exp(a_ref[...]) + k_ref[...] * v[:, None, :]
    sn_ref[...] = sn
    o = jnp.sum(q_ref[...] * sn, axis=1)
    o_ref[...] = _rms(o, gain_ref[...]) * _silu(cg_ref[...])


def _gla_step(s, qcol, kcol, acol, v, cg, gain, tb):
    n = s.shape[0]
    big = pl.BlockSpec((tb, C_DK, C_DV), lambda i: (i, 0, 0))
    row = pl.BlockSpec((tb, C_DV), lambda i: (i, 0))
    return pl.pallas_call(
        _gla_step_kernel,
        out_shape=(jax.ShapeDtypeStruct(s.shape, F32), jax.ShapeDtypeStruct((n, C_DV), F32)),
        grid=(n // tb,),
        in_specs=[big, big, big, big, row, row, _full_spec((1, C_DV))],
        out_specs=(big, row),
        compiler_params=_cparams("parallel"),
        name="gla_step",
    )(s, qcol, kcol, acol, v, cg, gain)


def _dn_step_kernel(s_ref, q_ref, k_ref, v_ref, eg_ref, beta_ref, dz_ref, gain_ref, sn_ref, o_ref):
    s = s_ref[...]
    kcol = k_ref[...]
    eg = eg_ref[...]
    ks = jnp.sum(kcol * s, axis=1)
    v_new = beta_ref[...] * (v_ref[...] - eg * ks)
    sn = s * eg[:, None, :] + kcol * v_new[:, None, :]
    sn_ref[...] = sn
    o = jnp.sum(q_ref[...] * sn, axis=1)
    o_ref[...] = _rms(o, gain_ref[...]) * _silu(dz_ref[...])


def _dn_step(s, qcol, kcol, v, eg, beta, dz, gain, tb):
    n = s.shape[0]
    big = pl.BlockSpec((tb, D_DK, D_DV), lambda i: (i, 0, 0))
    row = pl.BlockSpec((tb, D_DV), lambda i: (i, 0))
    return pl.pallas_call(
        _dn_step_kernel,
        out_shape=(jax.ShapeDtypeStruct(s.shape, F32), jax.ShapeDtypeStruct((n, D_DV), F32)),
        grid=(n // tb,),
        in_specs=[big, big, big, row, row, row, row, _full_spec((1, D_DV))],
        out_specs=(big, row),
        compiler_params=_cparams("parallel"),
        name="dn_step",
    )(s, qcol, kcol, v, eg, beta, dz, gain)


def _prep_w_in(w):
    o = np.cumsum([0, 256, 256, 256, 256, 128, 128, 256, 256, 16, 768, 4, 4, 256])
    q, k, v, u, cq, ck, cv, cg, lr, dqkv, db, da, dz = [w[:, o[i]:o[i + 1]] for i in range(13)]
    pad = jnp.zeros((w.shape[0], SMALL_W - 24), w.dtype)
    return jnp.concatenate([q, k, v, u, cq, ck, cv, dqkv, cg, dz, lr, db, da, pad], axis=1).astype(BF16)


def _lane_row(vals, offset):
    return jnp.zeros((1, 128), F32).at[0, offset:offset + vals.shape[0]].set(vals)


def kernel(x_prompt, x_sample, cache_k, cache_v, state_ssm_re, state_ssm_im, state_gla, state_delta, state_conv, page_table, p_prompt, p_sample, g_mix, w_in, lam_q1, lam_k1, lam_q2, lam_k2, attn_norm, s5_lam_re, s5_lam_im, s5_log_dt, s5_b_re, s5_b_im, s5_c_re, s5_c_im, s5_d, s5_w_glu, s5_b_glu, gla_w_a2, gla_b_a, gla_norm, dn_conv_w, dn_a_log, dn_dt_bias, dn_norm, w_branch_gate, b_branch_gate, w_branch, w_out, g_ffn, w_ff1, w_ff2, g_ple, w_ple_gate, w_ple_proj, g_final):
    B, S, _ = x_prompt.shape
    DB = x_sample.shape[0]
    T = B * S
    n_pool, page = cache_k.shape[1], cache_k.shape[2]
    hp = x_prompt.reshape(T, D_MODEL)
    hs = x_sample.reshape(DB, D_MODEL)
    row = lambda a: a.reshape(1, -1)
    ck_t = cache_k.transpose(0, 1, 3, 4, 2).reshape(DEPTH, n_pool, A_HEADS * 2 * A_DH, page)
    cv_t = cache_v.transpose(0, 1, 3, 4, 2).reshape(DEPTH, n_pool, A_HEADS * A_DV, page)
    st_p, st_s = [], []
    for l in range(DEPTH):
        lam_init = 0.8 - 0.6 * math.exp(-0.3 * l)
        final = l == DEPTH - 1
        w_in_l = _prep_w_in(w_in[l])
        wg, wb, wo = w_branch_gate[l].astype(BF16), w_branch[l].astype(BF16), w_out[l].astype(BF16)
        w1, w2 = w_ff1[l].astype(BF16), w_ff2[l].astype(BF16)
        wpg, wpp = w_ple_gate[l].astype(BF16), w_ple_proj[l].astype(BF16)
        lam_params = (row(lam_q1[l]), row(lam_k1[l]), row(lam_q2[l]), row(lam_k2[l]))
        sp = _s5_params(s5_lam_re[l], s5_lam_im[l], s5_log_dt[l], s5_b_re[l], s5_b_im[l], s5_c_re[l], s5_c_im[l])
        s5d, s5wg, s5bg = row(s5_d[l]), s5_w_glu[l].astype(BF16), row(s5_b_glu[l])
        wa = jnp.zeros((SMALL_W, 128), F32).at[SM_LR:SM_LR + C_RANK].set(gla_w_a2[l])
        ba = row(gla_b_a[l])
        gla_gain4 = row(jnp.tile(gla_norm[l], C_HEADS))
        dn_gain4 = row(jnp.tile(dn_norm[l], D_HEADS))
        acoef = _lane_row(-jnp.exp(dn_a_log[l]), SM_A)
        dtb = _lane_row(dn_dt_bias[l], SM_A)
        cw = dn_conv_w[l]

        zp, kt_p, vt_p = _inproj(hp, row(g_mix[l]), w_in_l, 512, seq_len=S)
        oa = _attn_prompt(zp, lam_params, row(attn_norm[l]), B, S, 512, lam_init)
        ob, p_hre, p_him = _s5_prompt(zp, sp, s5d, s5wg, s5bg, B, S, 512)
        oc, p_gla_t = _gla_prompt(zp.reshape(B, S, Z_W), wa, ba, gla_gain4, B, S, 256)
        oc = oc.reshape(T, 256)
        od, p_dn_bd, p_conv = _dn_prompt(zp.reshape(B, S, Z_W), cw, acoef, dtb, dn_gain4, B, S, 256)
        od = od.reshape(T, 256)
        p_dn = jnp.stack([p_dn_bd[:, h * D_DK:(h + 1) * D_DK, h * D_DV:(h + 1) * D_DV] for h in range(D_HEADS)], axis=1)
        hp = _merge(hp, oa, ob, oc, od, row(g_mix[l]), wg, b_branch_gate[l], wb, wo, 512)
        hp = _ffn(hp, p_prompt[l].reshape(T, PLE_DIM), row(g_ffn[l]), w1, w2, row(g_ple[l]), wpg, wpp, row(g_final),
                  256, final)
        p_gla = jnp.stack([p_gla_t[:, h * C_DV:(h + 1) * C_DV, h * C_DK:(h + 1) * C_DK] for h in range(C_HEADS)],
                          axis=1).swapaxes(-1, -2)
        st_p.append((kt_p.reshape(B, A_HEADS, 2 * A_DH, S).transpose(0, 3, 1, 2),
                     vt_p.reshape(B, A_HEADS, A_DV, S).transpose(0, 3, 1, 2),
                     p_hre.reshape(B, S5_GROUPS, S5_P), p_him.reshape(B, S5_GROUPS, S5_P), p_gla, p_dn, p_conv))

        zs, = _inproj(hs, row(g_mix[l]), w_in_l, DB)
        oa_s = _attn_decode(page_table, ck_t, cv_t, l, zs.reshape(DB, 1, Z_W), lam_params, row(attn_norm[l]),
                            lam_init).reshape(DB, 256)
        (ob_s, s_hre, s_him, gq, gk, ga, dq, dk, dv, dgb, s_conv3) = _dec_prep(
            zs, state_ssm_re[l].reshape(DB, S5_STATE), state_ssm_im[l].reshape(DB, S5_STATE),
            state_conv[l].swapaxes(0, 1), sp, s5d, s5wg, s5bg, wa, ba, cw, acoef, dtb)
        col = lambda a, h, dk, dv: jnp.broadcast_to(a.reshape(DB * h, dk, 1), (DB * h, dk, dv))
        s_gla, oc_s = _gla_step(state_gla[l].reshape(DB * C_HEADS, C_DK, C_DV), col(gq, C_HEADS, C_DK, C_DV),
                                col(gk, C_HEADS, C_DK, C_DV), col(ga, C_HEADS, C_DK, C_DV),
                                zs[:, COL_CV:COL_CV + 256].reshape(DB * C_HEADS, C_DV),
                                zs[:, COL_CG:COL_CG + 256].reshape(DB * C_HEADS, C_DV), row(gla_norm[l]), 128)
        lanes = lambda a, off: jnp.broadcast_to(a[:, off:off + D_HEADS].reshape(DB * D_HEADS, 1), (DB * D_HEADS, D_DV))
        s_dn, od_s = _dn_step(state_delta[l].reshape(DB * D_HEADS, D_DK, D_DV), col(dq, D_HEADS, D_DK, D_DV),
                              col(dk, D_HEADS, D_DK, D_DV), dv.reshape(DB * D_HEADS, D_DV), lanes(dgb, SM_A),
                              lanes(dgb, SM_B), zs[:, COL_DZ:COL_DZ + 256].reshape(DB * D_HEADS, D_DV),
                              row(dn_norm[l]), 64)
        hs = _merge(hs, oa_s, ob_s, oc_s.reshape(DB, 256), od_s.reshape(DB, 256), row(g_mix[l]), wg, b_branch_gate[l],
                    wb, wo, DB)
        hs = _ffn(hs, p_sample[l].reshape(DB, PLE_DIM), row(g_ffn[l]), w1, w2, row(g_ple[l]), wpg, wpp, row(g_final),
                  DB, final)
        st_s.append((zs[:, COL_K:COL_K + 256].reshape(DB, 1, A_HEADS, 2 * A_DH),
                     zs[:, COL_V:COL_V + 256].reshape(DB, 1, A_HEADS, A_DV),
                     s_hre.reshape(DB, S5_GROUPS, S5_P), s_him.reshape(DB, S5_GROUPS, S5_P),
                     s_gla.reshape(DB, C_HEADS, C_DK, C_DV), s_dn.reshape(DB, D_HEADS, D_DK, D_DV),
                     s_conv3.swapaxes(0, 1)))

    stk = lambda states, i: jnp.stack([s[i] for s in states])
    return (hp.reshape(B, S, D_MODEL), hs.reshape(DB, 1, D_MODEL),
            stk(st_p, 0), stk(st_p, 1), stk(st_p, 2), stk(st_p, 3), stk(st_p, 4), stk(st_p, 5), stk(st_p, 6),
            stk(st_s, 0), stk(st_s, 1), stk(st_s, 2), stk(st_s, 3), stk(st_s, 4), stk(st_s, 5), stk(st_s, 6))
```

```python
import functools
import math

import numpy as np
import jax
import jax.numpy as jnp
from jax import lax
from jax.experimental import pallas as pl
from jax.experimental.pallas import tpu as pltpu

F32 = jnp.float32
BF16 = jnp.bfloat16
HI = lax.Precision.HIGHEST

D_MODEL = 1024
DEPTH = 2
A_HEADS, A_DH, A_DV = 4, 32, 64
S5_GROUPS, S5_GROUP, S5_P = 16, 16, 64
S5_STATE = S5_GROUPS * S5_P
C_HEADS, C_DK, C_DV, C_RANK = 4, 32, 64, 16
GLA_TAU = 16.0
D_HEADS, D_DK, D_DV = 4, 64, 64
CONV_W = 4
DN_QKV_W = 768
D_FF = 4096
PLE_DIM = 256
EPS = 1e-6
BRANCH_W = 256

Z_W = 2944
COL_Q, COL_K, COL_V, COL_U = 0, 256, 512, 768
COL_CQK, COL_CV, COL_DQKV, COL_CG, COL_DZ, COL_SMALL = 1024, 1280, 1536, 2304, 2560, 2816
SMALL_W = 128
SM_LR, SM_B, SM_A = 0, 16, 20

S5_SCAN = 64
S5_LEVELS = 6
S5_PAD = S5_SCAN // 2
GLA_SUB = 16
DN_CHUNK = 64
VMEM_LIMIT_BYTES = 56 * 1024 * 1024

TM_PROJ = 512
TM_FFN = 256
TQ_ATTN = 512
TS_S5 = 512
TS_SEQS = 256
TB_GLA_STEP = 128
TB_DN_STEP = 64
NEG_INF = float("-inf")


def _cparams(*sem):
    return pltpu.CompilerParams(dimension_semantics=sem, vmem_limit_bytes=VMEM_LIMIT_BYTES)


def _sigmoid(x):
    return 1.0 / (1.0 + jnp.exp(-x))


def _silu(x):
    return x * _sigmoid(x)


def _softplus(x):
    return jnp.maximum(x, 0.0) + jnp.log(1.0 + jnp.exp(-jnp.abs(x)))


def _gelu_tanh(x):
    return 0.5 * x * (1.0 + jnp.tanh(math.sqrt(2.0 / math.pi) * (x + 0.044715 * (x * x * x))))


def _rms(x, g):
    return x * lax.rsqrt(jnp.mean(x * x, axis=-1, keepdims=True) + EPS) * g


def _dot(a, b, **kw):
    return jnp.dot(a, b, preferred_element_type=F32, **kw)


def _dot_nt(a, b, **kw):
    return lax.dot_general(a, b, (((1,), (1,)), ((), ())), preferred_element_type=F32, **kw)


def _dot_tn(a, b, **kw):
    return lax.dot_general(a, b, (((0,), (0,)), ((), ())), preferred_element_type=F32, **kw)


def _full_spec(shape):
    nd = len(shape)
    return pl.BlockSpec(shape, lambda *_: (0,) * nd)


def _seg_matrix(width, seg):
    i = np.arange(width) // seg
    return jnp.asarray((i[:, None] == i[None, :]).astype(np.float32)).astype(BF16)


def _bf16_parts(x, parts):
    out = []
    for i in range(parts):
        p = x.astype(BF16)
        out.append(p)
        if i + 1 < parts:
            x = x - p.astype(F32)
    return out


def _dot01(x, e, parts):
    acc = None
    for p in _bf16_parts(x, parts):
        t = _dot(p, e)
        acc = t if acc is None else acc + t
    return acc


def _dot01_left(e, x, parts):
    acc = None
    for p in _bf16_parts(x, parts):
        t = _dot(e, p)
        acc = t if acc is None else acc + t
    return acc


def _head_rms(o, seg_ref, gain):
    ms = _dot01(o * o, seg_ref[...], 2) * (1.0 / 64.0)
    return o * lax.rsqrt(ms + EPS) * gain


def _inproj_kernel(x_ref, g_ref, w_ref, z_ref, *kv_refs):
    hn = _rms(x_ref[...], g_ref[...]).astype(BF16)
    z = _dot(hn, w_ref[...])
    z_ref[...] = z
    if kv_refs:
        kt_ref, vt_ref = kv_refs
        kt_ref[0] = z[:, COL_K:COL_K + 256].T
        vt_ref[0] = z[:, COL_V:COL_V + 256].T


def _inproj(x, g, w, tm, seq_len=None):
    T = x.shape[0]
    out_shape = [jax.ShapeDtypeStruct((T, Z_W), F32)]
    out_specs = [pl.BlockSpec((tm, Z_W), lambda i: (i, 0))]
    if seq_len is not None:
        nt = seq_len // tm
        out_shape += [jax.ShapeDtypeStruct((T // seq_len, 256, seq_len), F32)] * 2
        out_specs += [pl.BlockSpec((1, 256, tm), lambda i: (i // nt, 0, i % nt))] * 2
    return pl.pallas_call(
        _inproj_kernel,
        out_shape=tuple(out_shape),
        grid=(T // tm,),
        in_specs=[pl.BlockSpec((tm, D_MODEL), lambda i: (i, 0)), _full_spec((1, D_MODEL)), _full_spec((D_MODEL, Z_W))],
        out_specs=tuple(out_specs),
        compiler_params=_cparams("parallel"),
        name="inproj",
    )(x, g, w)


def _merge_kernel(h_ref, oa_ref, ob_ref, oc_ref, od_ref, g_ref, wg_ref, bg_ref, wb_ref, wo_ref, out_ref):
    h = h_ref[...]
    hn = _rms(h, g_ref[...]).astype(BF16)
    merged = None
    for n, br in enumerate((oa_ref, ob_ref, oc_ref, od_ref)):
        gate = _sigmoid(_dot(hn, wg_ref[n]) + bg_ref[n:n + 1, :])
        term = gate * _dot(br[...].astype(BF16), wb_ref[n])
        merged = term if merged is None else merged + term
    out_ref[...] = h + _dot(merged.astype(BF16), wo_ref[...])


def _merge(h, oa, ob, oc, od, g, wg, bg, wb, wo, tm):
    T = h.shape[0]
    row = lambda w: pl.BlockSpec((tm, w), lambda i: (i, 0))
    return pl.pallas_call(
        _merge_kernel,
        out_shape=jax.ShapeDtypeStruct((T, D_MODEL), F32),
        grid=(T // tm,),
        in_specs=[row(D_MODEL), row(BRANCH_W), row(BRANCH_W), row(BRANCH_W), row(BRANCH_W),
                  _full_spec((1, D_MODEL)), _full_spec((4, D_MODEL, D_MODEL)), _full_spec((4, D_MODEL)),
                  _full_spec((4, BRANCH_W, D_MODEL)), _full_spec((D_MODEL, D_MODEL))],
        out_specs=row(D_MODEL),
        compiler_params=_cparams("parallel"),
        name="merge",
    )(h, oa, ob, oc, od, g, wg, bg, wb, wo)


def _ffn_kernel(h_ref, p_ref, gf_ref, w1_ref, w2_ref, gp_ref, wpg_ref, wpp_ref, gfin_ref, out_ref, *, final):
    h = h_ref[...]
    hf = _rms(h, gf_ref[...]).astype(BF16)
    acc = h
    step = 1024
    for j in range(D_FF // step):
        a = jnp.maximum(_dot(hf, w1_ref[:, j * step:(j + 1) * step]), 0.0)
        acc = acc + _dot((a * a).astype(BF16), w2_ref[j * step:(j + 1) * step, :])
    pg = _sigmoid(_dot(_rms(acc, gp_ref[...]).astype(BF16), wpg_ref[...]))
    out = acc + pg * _dot(p_ref[...].astype(BF16), wpp_ref[...])
    if final:
        out = _rms(out, gfin_ref[...])
    out_ref[...] = out


def _ffn(h, p, gf, w1, w2, gp, wpg, wpp, gfin, tm, final):
    T = h.shape[0]
    row = lambda w: pl.BlockSpec((tm, w), lambda i: (i, 0))
    return pl.pallas_call(
        functools.partial(_ffn_kernel, final=final),
        out_shape=jax.ShapeDtypeStruct((T, D_MODEL), F32),
        grid=(T // tm,),
        in_specs=[row(D_MODEL), row(PLE_DIM), _full_spec((1, D_MODEL)), _full_spec((D_MODEL, D_FF)),
                  _full_spec((D_FF, D_MODEL)), _full_spec((1, D_MODEL)), _full_spec((D_MODEL, D_MODEL)),
                  _full_spec((PLE_DIM, D_MODEL)), _full_spec((1, D_MODEL))],
        out_specs=row(D_MODEL),
        compiler_params=_cparams("parallel"),
        name="ffn_ple",
    )(h, p, gf, w1, w2, gp, wpg, wpp, gfin)


_SLOPES = [float(s) for s in 2.0 ** (-8.0 * np.arange(1, A_HEADS + 1) / A_HEADS)]
LOG2E = 1.0 / math.log(2.0)


def _lam_value(lq1, lk1, lq2, lk2, lam_init):
    return (jnp.exp(jnp.sum(lq1[...] * lk1[...], axis=-1, keepdims=True))
            - jnp.exp(jnp.sum(lq2[...] * lk2[...], axis=-1, keepdims=True)) + lam_init)


def _attn_prompt_kernel(qi_ref, ki_ref, q_ref, k_ref, v_ref, lq1, lk1, lq2, lk2, gain_ref, o_ref,
                        qm_scr, m_scr, acc_scr, *, tq, lam_init):
    pair = pl.program_id(1)
    qi = qi_ref[pair]
    ki = ki_ref[pair]

    @pl.when(ki == 0)
    def _init():
        q = q_ref[...] * (A_DH ** -0.5 * LOG2E)
        lane = lax.broadcasted_iota(jnp.int32, q.shape, 1)
        for i in range(2 * A_HEADS):
            lo = (i // 2) * 2 * A_DH + (i % 2) * A_DH
            qm_scr[i] = jnp.where((lane >= lo) & (lane < lo + A_DH), q, 0.0).astype(BF16)
        m_scr[...] = jnp.full(m_scr.shape, NEG_INF, F32)
        acc_scr[...] = jnp.zeros(acc_scr.shape, F32)

    def _step(diag):
        k = k_ref[...].astype(BF16)
        vt = v_ref[...].T
        kpos = (lax.broadcasted_iota(jnp.int32, (tq, 1), 0) + (ki - qi) * tq).astype(F32)
        ones_row = jnp.where(lax.broadcasted_iota(jnp.int32, (A_DV, tq), 0) == 0, 1.0, 0.0)
        if diag:
            causal = (lax.broadcasted_iota(jnp.int32, (tq, tq), 0) <= lax.broadcasted_iota(jnp.int32, (tq, tq), 1))
        qq = jnp.concatenate([qm_scr[i] for i in range(2 * A_HEADS)], axis=0)
        s_all = _dot_nt(k, qq)
        for h in range(A_HEADS):
            vext = jnp.concatenate([vt[h * A_DV:(h + 1) * A_DV, :], ones_row], axis=0).astype(BF16)
            bias = kpos * (_SLOPES[h] * LOG2E)
            if diag:
                bias = jnp.where(causal, bias, NEG_INF)
                bias = jnp.concatenate([bias, bias], axis=1)
            i = 2 * h
            s = s_all[:, i * tq:(i + 2) * tq] + bias
            m_prev = jnp.concatenate([m_scr[i], m_scr[i + 1]], axis=1)
            m_new = jnp.maximum(m_prev, jnp.max(s, axis=0, keepdims=True))
            alpha = jnp.exp2(m_prev - m_new)
            p = jnp.exp2(s - m_new)
            pv = _dot(vext, p.astype(BF16))
            acc_scr[i] = alpha[:, :tq] * acc_scr[i] + pv[:, :tq]
            acc_scr[i + 1] = alpha[:, tq:] * acc_scr[i + 1] + pv[:, tq:]
            m_scr[i] = m_new[:, :tq]
            m_scr[i + 1] = m_new[:, tq:]

    @pl.when(ki < qi)
    def _off_diagonal():
        _step(False)

    @pl.when(ki == qi)
    def _diagonal():
        _step(True)
        lam = _lam_value(lq1, lk1, lq2, lk2, lam_init)
        outs = []
        for h in range(A_HEADS):
            a1, a2 = acc_scr[2 * h], acc_scr[2 * h + 1]
            o = a1[:A_DV, :] / a1[A_DV:A_DV + 1, :] - lam * (a2[:A_DV, :] / a2[A_DV:A_DV + 1, :])
            outs.append(o * lax.rsqrt(jnp.mean(o * o, axis=0, keepdims=True) + EPS))
        gain = jnp.concatenate([gain_ref[...]] * A_HEADS, axis=1)
        o_ref[...] = jnp.concatenate(outs, axis=0).T * gain * (1.0 - lam_init)


def _attn_prompt(z, lam_params, gain, B, S, tq, lam_init):
    nq = S // tq
    pairs = [(qi, ki) for qi in range(nq) for ki in range(qi + 1)]
    qi_arr = jnp.asarray([p[0] for p in pairs], jnp.int32)
    ki_arr = jnp.asarray([p[1] for p in pairs], jnp.int32)
    vec = pl.BlockSpec((1, A_DH), lambda b, p, qa, ka: (0, 0))
    grid_spec = pltpu.PrefetchScalarGridSpec(
        num_scalar_prefetch=2,
        grid=(B, len(pairs)),
        in_specs=[pl.BlockSpec((tq, 256), lambda b, p, qa, ka: (b * nq + qa[p], COL_Q // 256)),
                  pl.BlockSpec((tq, 256), lambda b, p, qa, ka: (b * nq + ka[p], COL_K // 256)),
                  pl.BlockSpec((tq, 256), lambda b, p, qa, ka: (b * nq + ka[p], COL_V // 256)),
                  vec, vec, vec, vec, pl.BlockSpec((1, A_DV), lambda b, p, qa, ka: (0, 0))],
        out_specs=pl.BlockSpec((tq, 256), lambda b, p, qa, ka: (b * nq + qa[p], 0)),
        scratch_shapes=[pltpu.VMEM((2 * A_HEADS, tq, 256), BF16), pltpu.VMEM((2 * A_HEADS, 1, tq), F32),
                        pltpu.VMEM((2 * A_HEADS, 128, tq), F32)],
    )
    return pl.pallas_call(
        functools.partial(_attn_prompt_kernel, tq=tq, lam_init=lam_init),
        out_shape=jax.ShapeDtypeStruct((B * S, A_HEADS * A_DV), F32),
        grid_spec=grid_spec,
        compiler_params=_cparams("parallel", "arbitrary"),
        name="attn_prompt",
    )(qi_arr, ki_arr, z, z, z, *lam_params, gain)


def _attn_decode_kernel(pt_ref, *refs, n_pages, page, lam_init):
    k_refs = refs[:n_pages]
    v_refs = refs[n_pages:2 * n_pages]
    z_ref, lq1, lk1, lq2, lk2, gain_ref, o_ref = refs[2 * n_pages:]
    del pt_ref
    past = n_pages * page
    zrow = z_ref[0]
    q = zrow[:, COL_Q:COL_Q + 256] * (A_DH ** -0.5)
    k_own = zrow[:, COL_K:COL_K + 256]
    v_own = zrow[:, COL_V:COL_V + 256]
    lane = lax.broadcasted_iota(jnp.int32, (2 * A_HEADS, 256), 1)
    r = lax.broadcasted_iota(jnp.int32, (2 * A_HEADS, 256), 0)
    lo = (r % A_HEADS) * 2 * A_DH + (r // A_HEADS) * A_DH
    qm = jnp.where((lane >= lo) & (lane < lo + A_DH), jnp.broadcast_to(q, (2 * A_HEADS, 256)), 0.0)
    qmb = qm.astype(BF16)
    s = jnp.concatenate([_dot(qmb, kr[0, 0].astype(BF16)) for kr in k_refs], axis=1)
    rr = lax.broadcasted_iota(jnp.int32, (2 * A_HEADS, 1), 0) % A_HEADS
    slope = jnp.zeros((2 * A_HEADS, 1), F32)
    for h in range(A_HEADS):
        slope = jnp.where(rr == h, _SLOPES[h], slope)
    kpos = lax.broadcasted_iota(jnp.int32, (2 * A_HEADS, past), 1).astype(F32)
    s = s - slope * (float(past) - kpos)
    s_own = jnp.sum(qm * k_own, axis=-1, keepdims=True)
    m = jnp.maximum(jnp.max(s, axis=-1, keepdims=True), s_own)
    p = jnp.exp(s - m)
    p_own = jnp.exp(s_own - m)
    denom = jnp.sum(p, axis=-1, keepdims=True) + p_own
    pb = p.astype(BF16)
    o = p_own * v_own
    for j, vr in enumerate(v_refs):
        o = o + _dot_nt(pb[:, j * page:(j + 1) * page], vr[0, 0].astype(BF16))
    o = o / denom
    lam = _lam_value(lq1, lk1, lq2, lk2, lam_init)
    d = o[:A_HEADS] - lam * o[A_HEADS:]
    hl = lax.broadcasted_iota(jnp.int32, (A_HEADS, 256), 1) // A_DV
    hr = lax.broadcasted_iota(jnp.int32, (A_HEADS, 256), 0)
    own = hl == hr
    d = jnp.where(own, d, 0.0)
    ms = jnp.sum(d * d, axis=-1, keepdims=True) * (1.0 / A_DV)
    d = d * lax.rsqrt(ms + EPS)
    gain = jnp.concatenate([gain_ref[...]] * A_HEADS, axis=1)
    o_ref[0] = jnp.sum(d, axis=0, keepdims=True) * gain * (1.0 - lam_init)


def _attn_decode(page_table, ck, cv, layer, z3, lam_params, gain, lam_init):
    DB, n_pages = page_table.shape
    page = ck.shape[-1]
    kv_specs = [pl.BlockSpec((1, 1, 256, page), functools.partial(lambda b, pt, j: (layer, pt[b, j], 0, 0), j=j))
                for j in range(n_pages)]
    vec = pl.BlockSpec((1, A_DH), lambda b, pt: (0, 0))
    grid_spec = pltpu.PrefetchScalarGridSpec(
        num_scalar_prefetch=1,
        grid=(DB,),
        in_specs=kv_specs + kv_specs + [pl.BlockSpec((1, 1, Z_W), lambda b, pt: (b, 0, 0)), vec, vec, vec, vec,
                                        pl.BlockSpec((1, A_DV), lambda b, pt: (0, 0))],
        out_specs=pl.BlockSpec((1, 1, 256), lambda b, pt: (b, 0, 0)),
    )
    return pl.pallas_call(
        functools.partial(_attn_decode_kernel, n_pages=n_pages, page=page, lam_init=lam_init),
        out_shape=jax.ShapeDtypeStruct((DB, 1, 256), F32),
        grid_spec=grid_spec,
        compiler_params=_cparams("arbitrary"),
        name="attn_decode",
    )(page_table, *([ck] * n_pages), *([cv] * n_pages), z3, *lam_params, gain)


def _s5_params(lam_re, lam_im, log_dt, b_re, b_im, c_re, c_im):
    dt = jnp.exp(log_dt)[:, None]
    lr, li = lam_re, lam_im
    a, th = lr * dt, li * dt
    mag = jnp.exp(a)
    ab_re, ab_im = mag * jnp.cos(th), mag * jnp.sin(th)
    den = lr * lr + li * li
    nr, ni = ab_re - 1.0, ab_im
    f_re = (nr * lr + ni * li) / den
    f_im = (ni * lr - nr * li) / den
    bb_re = f_re[..., None] * b_re - f_im[..., None] * b_im
    bb_im = f_re[..., None] * b_im + f_im[..., None] * b_re
    eye = jnp.eye(S5_GROUPS, dtype=F32)
    expand_b = lambda m: jnp.einsum("gpn,gh->gnhp", m, eye).reshape(S5_GROUPS * S5_GROUP, S5_STATE)
    wb = jnp.concatenate([expand_b(bb_re), expand_b(bb_im)], axis=1)
    expand_c = lambda m: jnp.einsum("gnp,gh->gphn", m, eye).reshape(S5_STATE, S5_GROUPS * S5_GROUP)
    cm = jnp.concatenate([expand_c(c_re), -expand_c(c_im)], axis=0)
    pr, pi = ab_re.reshape(1, S5_STATE), ab_im.reshape(1, S5_STATE)
    res, ims = [pr], [pi]
    for _ in range(S5_LEVELS - 1):
        pr, pi = pr * pr - pi * pi, 2.0 * pr * pi
        res.append(pr)
        ims.append(pi)
    return dict(wb=wb, cm=cm, ab_re=res[0], ab_im=ims[0], pow_re=jnp.concatenate(res, axis=0),
                pow_im=jnp.concatenate(ims, axis=0))


def _s5_out(y, u, d_ref, wg_ref, bg_ref):
    y = _gelu_tanh(y + d_ref[...] * u)
    return y * _sigmoid(_dot(y.astype(BF16), wg_ref[...]) + bg_ref[...])


def _s5_prompt_kernel(u_ref, wb_ref, cm_ref, pr_ref, pi_ref, d_ref, wg_ref, bg_ref, o_ref, hre_ref, him_ref,
                      bu_scr, hb_scr, sr_scr, si_scr, hr_scr, hi_scr, *, ts):
    t = pl.program_id(1)
    n = S5_SCAN
    P = S5_STATE
    pad = S5_PAD

    @pl.when(t == 0)
    def _init():
        hr_scr[...] = jnp.zeros(hr_scr.shape, F32)
        hi_scr[...] = jnp.zeros(hi_scr.shape, F32)
        sr_scr[0:pad, :] = jnp.zeros((pad, P), F32)
        si_scr[0:pad, :] = jnp.zeros((pad, P), F32)

    u = u_ref[...]
    bu_scr[...] = _dot(u.astype(BF16), wb_ref[...])

    def chunk(c, carry):
        r0 = pl.multiple_of(c * n, n)
        sr_scr[pad:pad + n, :] = bu_scr[pl.ds(r0, n), 0:P]
        si_scr[pad:pad + n, :] = bu_scr[pl.ds(r0, n), P:2 * P]
        ar, ai = pr_ref[0:1, :], pi_ref[0:1, :]
        h0r, h0i = hr_scr[...], hi_scr[...]
        sr_scr[pad:pad + 1, :] = sr_scr[pad:pad + 1, :] + (ar * h0r - ai * h0i)
        si_scr[pad:pad + 1, :] = si_scr[pad:pad + 1, :] + (ar * h0i + ai * h0r)
        for lv in range(S5_LEVELS):
            d = 1 << lv
            ar, ai = pr_ref[lv:lv + 1, :], pi_ref[lv:lv + 1, :]
            cr, ci = sr_scr[pad:pad + n, :], si_scr[pad:pad + n, :]
            qr, qi = sr_scr[pad - d:pad - d + n, :], si_scr[pad - d:pad - d + n, :]
            nr, ni = cr + (ar * qr - ai * qi), ci + (ar * qi + ai * qr)
            if lv + 1 < S5_LEVELS:
                sr_scr[pad:pad + n, :] = nr
                si_scr[pad:pad + n, :] = ni
        hr_scr[...] = nr[n - 1:n, :]
        hi_scr[...] = ni[n - 1:n, :]
        hb_scr[pl.ds(r0, n), 0:P] = nr.astype(BF16)
        hb_scr[pl.ds(r0, n), P:2 * P] = ni.astype(BF16)
        return carry

    lax.fori_loop(0, ts // n, chunk, 0)
    y = _dot(hb_scr[...], cm_ref[...])
    o_ref[...] = _s5_out(y, u, d_ref, wg_ref, bg_ref)
    hre_ref[0] = hr_scr[...]
    him_ref[0] = hi_scr[...]


def _s5_prompt(z, sp, d, wg, bg, B, S, ts):
    nt = S // ts
    tab = _full_spec((S5_LEVELS, S5_STATE))
    st = pl.BlockSpec((1, 1, S5_STATE), lambda b, t: (b, 0, 0))
    return pl.pallas_call(
        functools.partial(_s5_prompt_kernel, ts=ts),
        out_shape=(jax.ShapeDtypeStruct((B * S, 256), F32), jax.ShapeDtypeStruct((B, 1, S5_STATE), F32),
                   jax.ShapeDtypeStruct((B, 1, S5_STATE), F32)),
        grid=(B, nt),
        in_specs=[pl.BlockSpec((ts, 256), lambda b, t: (b * nt + t, COL_U // 256)),
                  _full_spec((256, 2 * S5_STATE)), _full_spec((2 * S5_STATE, 256)), tab, tab,
                  _full_spec((1, 256)), _full_spec((256, 256)), _full_spec((1, 256))],
        out_specs=(pl.BlockSpec((ts, 256), lambda b, t: (b * nt + t, 0)), st, st),
        scratch_shapes=[pltpu.VMEM((ts, 2 * S5_STATE), F32), pltpu.VMEM((ts, 2 * S5_STATE), BF16),
                        pltpu.VMEM((S5_PAD + S5_SCAN, S5_STATE), F32), pltpu.VMEM((S5_PAD + S5_SCAN, S5_STATE), F32),
                        pltpu.VMEM((1, S5_STATE), F32), pltpu.VMEM((1, S5_STATE), F32)],
        compiler_params=_cparams("parallel", "arbitrary"),
        name="s5_prompt",
    )(z, sp["wb"].astype(BF16), sp["cm"].astype(BF16), sp["pow_re"], sp["pow_im"], d, wg, bg)


def _gla_log_a(small, wa_ref, ba_ref):
    x = _dot(small, wa_ref[...], precision=HI) + ba_ref[...]
    return (jnp.minimum(x, 0.0) - jnp.log(1.0 + jnp.exp(-jnp.abs(x)))) * (1.0 / GLA_TAU)


def _gla_prompt_kernel(qk_ref, v_ref, cg_ref, sm_ref, wa_ref, ba_ref, gain_ref, segx_ref, seg_ref, bmask_ref,
                       o_ref, st_ref, b_scr, o_scr, st_scr, *, ts, nb):
    t = pl.program_id(0)
    n = GLA_SUB
    seqs = range(nb)

    @pl.when(t == 0)
    def _init():
        st_scr[...] = jnp.zeros(st_scr.shape, F32)

    ri = lax.broadcasted_iota(jnp.int32, (ts, ts), 0)
    ci = lax.broadcasted_iota(jnp.int32, (ts, ts), 1)
    tri = jnp.where((ri >= ci) & (ri // n == ci // n), 1.0, 0.0).astype(BF16)
    for i in seqs:
        log_a = _gla_log_a(sm_ref[i], wa_ref, ba_ref)
        b_scr[i] = _dot01_left(tri, log_a, 3)
    rows = lax.broadcasted_iota(jnp.int32, (n, 128), 0)
    segx = segx_ref[...]
    bmask = bmask_ref[...]

    def body(c, carry):
        r0 = pl.multiple_of(c * n, n)
        qk = [qk_ref[i, pl.ds(r0, n), :] for i in seqs]
        q = [x[:, :128] * (C_DK ** -0.5) for x in qk]
        k = [x[:, 128:] for x in qk]
        v = [v_ref[i, pl.ds(r0, n), :] for i in seqs]
        b = [b_scr[i, pl.ds(r0, n), :] for i in seqs]
        pall = []
        for i in seqs:
            parts = []
            for s in range(n):
                e = jnp.exp(jnp.where(rows >= s, b[i] - b[i][s:s + 1, :], NEG_INF))
                parts.append(q[i] * k[i][s:s + 1, :] * e)
            pall.append(jnp.concatenate(parts, axis=0).astype(BF16))
        aexp = [_dot(pall[i], segx) for i in seqs]
        st = [st_scr[i] for i in seqs]
        inter = [_dot_nt((q[i] * jnp.exp(b[i])).astype(BF16), st[i].astype(BF16)) for i in seqs]
        b_last = [b[i][n - 1:n, :] for i in seqs]
        upd = [_dot_tn(v[i].astype(BF16), (k[i] * jnp.exp(b_last[i] - b[i])).astype(BF16)) for i in seqs]
        for i in seqs:
            acc = inter[i]
            for s in range(n):
                acc = acc + aexp[i][s * n:(s + 1) * n, :] * v[i][s:s + 1, :]
            o_scr[i, pl.ds(r0, n), :] = acc
            st_scr[i] = st[i] * jnp.exp(b_last[i]) + upd[i] * bmask
        return carry

    lax.fori_loop(0, ts // n, body, 0)
    for i in seqs:
        o_ref[i] = _head_rms(o_scr[i], seg_ref, gain_ref[...]) * _silu(cg_ref[i])
    st_ref[...] = st_scr[...]


def _gla_consts():
    hk = np.arange(128) // C_DK
    hv = np.arange(256) // C_DV
    segx = jnp.asarray((hk[:, None] == hv[None, :]).astype(np.float32)).astype(BF16)
    bmask = jnp.asarray((hv[:, None] == hk[None, :]).astype(np.float32))
    return segx, bmask


def _gla_prompt(z, wa, ba, gain, B, S, ts):
    nt = S // ts
    segx, bmask = _gla_consts()
    blk = lambda w, col: pl.BlockSpec((B, ts, w), lambda t: (0, t, col // w))
    return pl.pallas_call(
        functools.partial(_gla_prompt_kernel, ts=ts, nb=B),
        out_shape=(jax.ShapeDtypeStruct((B, S, 256), F32), jax.ShapeDtypeStruct((B, 256, 128), F32)),
        grid=(nt,),
        in_specs=[blk(256, COL_CQK), blk(256, COL_CV), blk(256, COL_CG), blk(SMALL_W, COL_SMALL),
                  _full_spec((SMALL_W, 128)), _full_spec((1, 128)), _full_spec((1, 256)),
                  _full_spec((128, 256)), _full_spec((256, 256)), _full_spec((256, 128))],
        out_specs=(pl.BlockSpec((B, ts, 256), lambda t: (0, t, 0)), _full_spec((B, 256, 128))),
        scratch_shapes=[pltpu.VMEM((B, ts, 128), F32), pltpu.VMEM((B, ts, 256), F32), pltpu.VMEM((B, 256, 128), F32)],
        compiler_params=_cparams("arbitrary"),
        name="gla_prompt",
    )(z, z, z, z, wa, ba, gain, segx, _seg_matrix(256, 64), bmask)


def _dn_gates(small, acoef_ref, dtb_ref):
    beta = _sigmoid(small)
    g = acoef_ref[...] * _softplus(small + dtb_ref[...])
    return beta, g


def _dn_qkv(y, seg_ref):
    y = _silu(y)
    q, k, v = y[:, :256], y[:, 256:512], y[:, 512:768]
    nq = _dot01(q * q, seg_ref[...], 2)
    nk = _dot01(k * k, seg_ref[...], 2)
    q = q * lax.rsqrt(nq + EPS) * (D_DK ** -0.5)
    k = k * lax.rsqrt(nk + EPS)
    return q, k, v


def _dn_prompt_kernel(x_ref, dz_ref, sm_ref, cw_ref, acoef_ref, dtb_ref, gain_ref, seg_ref, esel_ref,
                      o_ref, st_ref, cs_ref, xp_scr, q_scr, k_scr, v_scr, gb_scr, ge_scr, be_scr,
                      u_scr, w_scr, att_scr, o_scr, st_scr, *, ts, nb):
    t = pl.program_id(0)
    C = DN_CHUNK
    PADR = 8
    W = D_HEADS * D_DK
    seqs = range(nb)

    @pl.when(t == 0)
    def _init():
        st_scr[...] = jnp.zeros(st_scr.shape, F32)
        for i in seqs:
            xp_scr[i, 0:PADR, :] = jnp.zeros((PADR, DN_QKV_W), F32)

    ri = lax.broadcasted_iota(jnp.int32, (ts, ts), 0)
    ci = lax.broadcasted_iota(jnp.int32, (ts, ts), 1)
    tri = jnp.where((ri >= ci) & (ri // C == ci // C), 1.0, 0.0).astype(BF16)
    lane = lax.broadcasted_iota(jnp.int32, (ts, 128), 1)
    for i in seqs:
        x = x_ref[i]
        xp_scr[i, PADR:PADR + ts, :] = x
        y = cw_ref[3:4, :] * x
        for j in range(CONV_W - 1):
            y = y + cw_ref[j:j + 1, :] * xp_scr[i, PADR - 3 + j:PADR - 3 + j + ts, :]
        xp_scr[i, PADR - 3:PADR, :] = x[ts - 3:ts, :]
        cs_ref[i] = x[ts - 3:ts, :]
        q, k, v = _dn_qkv(y, seg_ref)
        q_scr[i] = q
        k_scr[i] = k
        v_scr[i] = v
        beta, g = _dn_gates(sm_ref[i], acoef_ref, dtb_ref)
        gcum = _dot01_left(tri, g, 3)
        gb = jnp.where(lane >= SM_A, gcum, beta)
        gb_scr[i] = gb
        ex = _dot01(gb, esel_ref[...], 3)
        ge_scr[i] = ex[:, :W]
        be_scr[i] = ex[:, W:]

    rr = lax.broadcasted_iota(jnp.int32, (W, W), 0)
    cc = lax.broadcasted_iota(jnp.int32, (W, W), 1)
    same_head = (rr // C) == (cc // C)
    lower_incl = same_head & (rr >= cc)
    lower_strict = same_head & (rr > cc)
    eye = jnp.where(rr == cc, 1.0, 0.0)
    blocks = [(rr // w) == (cc // w) for w in (8, 16, 32, C)]

    def stack_heads(a):
        return jnp.where(same_head, jnp.concatenate([a] * D_HEADS, axis=0), 0.0)

    def collapse(a):
        return a[0:C] + a[C:2 * C] + a[2 * C:3 * C] + a[3 * C:4 * C]

    def solve_body(c, carry):
        r0 = pl.multiple_of(c * C, C)
        qc = [q_scr[i, pl.ds(r0, C), :] for i in seqs]
        kc = [k_scr[i, pl.ds(r0, C), :] for i in seqs]
        vc = [v_scr[i, pl.ds(r0, C), :] for i in seqs]
        gbc = [gb_scr[i, pl.ds(r0, C), :] for i in seqs]
        ge = [ge_scr[i, pl.ds(r0, C), :] for i in seqs]
        be = [be_scr[i, pl.ds(r0, C), :] for i in seqs]
        decay, kst = [], []
        for j in seqs:
            gbt = gbc[j].T
            gcol = jnp.concatenate([gbc[j][:, SM_A + h:SM_A + h + 1] for h in range(D_HEADS)], axis=0)
            grow = jnp.concatenate([gbt[SM_A + h:SM_A + h + 1, :] for h in range(D_HEADS)], axis=1)
            decay.append(jnp.exp(jnp.where(lower_incl, gcol - grow, NEG_INF)))
            kst.append(stack_heads(kc[j]).astype(BF16))
        kq = [_dot_nt(jnp.concatenate([stack_heads(kc[j] * be[j]), stack_heads(qc[j])], axis=0).astype(BF16), kst[j])
              for j in seqs]
        m = [jnp.where(lower_strict, -(kq[j][:W] * decay[j]), 0.0) for j in seqs]
        m1 = [jnp.where(blocks[0], m[j], 0.0) for j in seqs]
        m1b = [x.astype(BF16) for x in m1]
        m2 = [_dot(m1b[j], m1b[j]) for j in seqs]
        s1 = [eye + m1[j] for j in seqs]
        r2 = [_dot(m2[j].astype(BF16), jnp.concatenate([m2[j], s1[j]], axis=1).astype(BF16)) for j in seqs]
        s3 = [s1[j] + r2[j][:, W:] for j in seqs]
        tm = [s3[j] + _dot(r2[j][:, :W].astype(BF16), s3[j].astype(BF16)) for j in seqs]
        for inner, outer in zip(blocks[:-1], blocks[1:]):
            tmb = [x.astype(BF16) for x in tm]
            off = [jnp.where(outer & jnp.logical_not(inner), m[j], 0.0).astype(BF16) for j in seqs]
            t1 = [_dot(tmb[j], off[j]).astype(BF16) for j in seqs]
            tm = [tm[j] + _dot(t1[j], tmb[j]) for j in seqs]
        rhs = [jnp.concatenate([stack_heads(vc[j] * be[j]), stack_heads(kc[j] * (be[j] * jnp.exp(ge[j])))], axis=1)
               for j in seqs]
        uw = [_dot(tm[j].astype(BF16), rhs[j].astype(BF16)) for j in seqs]
        for j in seqs:
            u_scr[j, c] = collapse(uw[j][:, :W])
            w_scr[j, c] = collapse(uw[j][:, W:])
            att_scr[j, c] = (kq[j][W:] * decay[j]).astype(BF16)
        return carry

    lax.fori_loop(0, ts // C, solve_body, 0)

    def state_body(c, carry):
        r0 = pl.multiple_of(c * C, C)
        qc = [q_scr[i, pl.ds(r0, C), :] for i in seqs]
        kc = [k_scr[i, pl.ds(r0, C), :] for i in seqs]
        ge = [ge_scr[i, pl.ds(r0, C), :] for i in seqs]
        glast = [x[C - 1:C, :] for x in ge]
        s = [st_scr[i] for i in seqs]
        wq = [jnp.concatenate([w_scr[i, c], qc[i] * jnp.exp(ge[i])], axis=0) for i in seqs]
        sw = [_dot(wq[i].astype(BF16), s[i].astype(BF16)) for i in seqs]
        v_new = [u_scr[i, c] - sw[i][:C] for i in seqs]
        ov = [_dot(att_scr[i, c], stack_heads(v_new[i]).astype(BF16)) for i in seqs]
        upd = [_dot_tn((kc[i] * jnp.exp(glast[i] - ge[i])).astype(BF16), v_new[i].astype(BF16)) for i in seqs]
        for i in seqs:
            o_scr[i, pl.ds(r0, C), :] = sw[i][C:] + collapse(ov[i])
            st_scr[i] = s[i] * jnp.exp(glast[i]) + jnp.where(same_head, upd[i], 0.0)
        return carry

    lax.fori_loop(0, ts // C, state_body, 0)
    for i in seqs:
        o_ref[i] = _head_rms(o_scr[i], seg_ref, gain_ref[...]) * _silu(dz_ref[i])
    st_ref[...] = st_scr[...]


def _dn_esel():
    e = np.zeros((128, 512), np.float32)
    for h in range(D_HEADS):
        e[SM_A + h, h * 64:(h + 1) * 64] = 1.0
        e[SM_B + h, 256 + h * 64:256 + (h + 1) * 64] = 1.0
    return jnp.asarray(e).astype(BF16)


def _dn_prompt(z, cw, acoef, dtb, gain, B, S, ts):
    nt = S // ts
    nc = ts // DN_CHUNK
    W = D_HEADS * D_DK
    blk = lambda w, col: pl.BlockSpec((B, ts, w), lambda t: (0, t, col // w))
    return pl.pallas_call(
        functools.partial(_dn_prompt_kernel, ts=ts, nb=B),
        out_shape=(jax.ShapeDtypeStruct((B, S, 256), F32), jax.ShapeDtypeStruct((B, W, W), F32),
                   jax.ShapeDtypeStruct((B, CONV_W - 1, DN_QKV_W), F32)),
        grid=(nt,),
        in_specs=[blk(DN_QKV_W, COL_DQKV), blk(256, COL_DZ), blk(SMALL_W, COL_SMALL),
                  _full_spec((CONV_W, DN_QKV_W)), _full_spec((1, 128)), _full_spec((1, 128)), _full_spec((1, 256)),
                  _full_spec((256, 256)), _full_spec((128, 512))],
        out_specs=(pl.BlockSpec((B, ts, 256), lambda t: (0, t, 0)), _full_spec((B, W, W)),
                   _full_spec((B, CONV_W - 1, DN_QKV_W))),
        scratch_shapes=[pltpu.VMEM((B, 8 + ts, DN_QKV_W), F32), pltpu.VMEM((B, ts, 256), F32),
                        pltpu.VMEM((B, ts, 256), F32), pltpu.VMEM((B, ts, 256), F32), pltpu.VMEM((B, ts, 128), F32),
                        pltpu.VMEM((B, ts, W), F32), pltpu.VMEM((B, ts, W), F32),
                        pltpu.VMEM((B, nc, DN_CHUNK, W), F32), pltpu.VMEM((B, nc, DN_CHUNK, W), F32),
                        pltpu.VMEM((B, nc, W, W), BF16), pltpu.VMEM((B, ts, 256), F32), pltpu.VMEM((B, W, W), F32)],
        compiler_params=_cparams("arbitrary"),
        name="dn_prompt",
    )(z, z, z, cw, acoef, dtb, gain, _seg_matrix(256, 64), _dn_esel())


def _dec_prep_kernel(z_ref, h0r_ref, h0i_ref, conv_ref, wb_ref, cm_ref, abr_ref, abi_ref, d_ref, wg_ref, bg_ref,
                     wa_ref, ba_ref, cw_ref, acoef_ref, dtb_ref, seg_ref,
                     ob_ref, hr_ref, hi_ref, gq_ref, gk_ref, ga_ref, dq_ref, dk_ref, dv_ref, dgb_ref, cs_ref):
    z = z_ref[...]
    u = z[:, COL_U:COL_U + 256]
    bu = _dot(u, wb_ref[...], precision=HI)
    bur, bui = bu[:, :S5_STATE], bu[:, S5_STATE:]
    h0r, h0i = h0r_ref[...], h0i_ref[...]
    abr, abi = abr_ref[...], abi_ref[...]
    hr = abr * h0r - abi * h0i + bur
    hi = abr * h0i + abi * h0r + bui
    hr_ref[...] = hr
    hi_ref[...] = hi
    y = _dot(jnp.concatenate([hr, hi], axis=1).astype(BF16), cm_ref[...])
    ob_ref[...] = _s5_out(y, u, d_ref, wg_ref, bg_ref)
    small = z[:, COL_SMALL:COL_SMALL + SMALL_W]
    gq_ref[...] = z[:, COL_CQK:COL_CQK + 128] * (C_DK ** -0.5)
    gk_ref[...] = z[:, COL_CQK + 128:COL_CQK + 256]
    ga_ref[...] = _gla_log_a(small, wa_ref, ba_ref)
    x = z[:, COL_DQKV:COL_DQKV + DN_QKV_W]
    y = cw_ref[3:4, :] * x
    for i in range(CONV_W - 1):
        y = y + cw_ref[i:i + 1, :] * conv_ref[i]
    cs_ref[0] = conv_ref[1]
    cs_ref[1] = conv_ref[2]
    cs_ref[2] = x
    q, k, v = _dn_qkv(y, seg_ref)
    dq_ref[...] = q
    dk_ref[...] = k
    dv_ref[...] = v
    beta, g = _dn_gates(small, acoef_ref, dtb_ref)
    lane = lax.broadcasted_iota(jnp.int32, small.shape, 1)
    dgb_ref[...] = jnp.where(lane >= SM_A, jnp.exp(g), beta)


def _dec_prep(z, h0r, h0i, conv3, sp, d, wg, bg, wa, ba, cw, acoef, dtb):
    DB = z.shape[0]
    sds = lambda *s: jax.ShapeDtypeStruct(s, F32)
    out_shape = (sds(DB, 256), sds(DB, S5_STATE), sds(DB, S5_STATE), sds(DB, 128), sds(DB, 128), sds(DB, 128),
                 sds(DB, 256), sds(DB, 256), sds(DB, 256), sds(DB, 128), sds(CONV_W - 1, DB, DN_QKV_W))
    return pl.pallas_call(
        _dec_prep_kernel,
        out_shape=out_shape,
        compiler_params=pltpu.CompilerParams(vmem_limit_bytes=VMEM_LIMIT_BYTES),
        name="dec_prep",
    )(z, h0r, h0i, conv3, sp["wb"], sp["cm"].astype(BF16), sp["ab_re"], sp["ab_im"], d, wg, bg, wa, ba, cw, acoef, dtb,
      _seg_matrix(256, 64))


def _gla_step_kernel(s_ref, q_ref, k_ref, a_ref, v_ref, cg_ref, gain_ref, sn_ref, o_ref):
    v = v_ref[...]
    sn = s_ref[...] * jnp.exp(a_ref[...]) + k_ref[...] * v[:, None, :]
    sn_ref[...] = sn
    o = jnp.sum(q_ref[...] * sn, axis=1)
    o_ref[...] = _rms(o, gain_ref[...]) * _silu(cg_ref[...])


def _gla_step(s, qcol, kcol, acol, v, cg, gain, tb):
    n = s.shape[0]
    big = pl.BlockSpec((tb, C_DK, C_DV), lambda i: (i, 0, 0))
    row = pl.BlockSpec((tb, C_DV), lambda i: (i, 0))
    return pl.pallas_call(
        _gla_step_kernel,
        out_shape=(jax.ShapeDtypeStruct(s.shape, F32), jax.ShapeDtypeStruct((n, C_DV), F32)),
        grid=(n // tb,),
        in_specs=[big, big, big, big, row, row, _full_spec((1, C_DV))],
        out_specs=(big, row),
        compiler_params=_cparams("parallel"),
        name="gla_step",
    )(s, qcol, kcol, acol, v, cg, gain)


def _dn_step_kernel(s_ref, q_ref, k_ref, v_ref, eg_ref, beta_ref, dz_ref, gain_ref, sn_ref, o_ref):
    s = s_ref[...]
    kcol = k_ref[...]
    eg = eg_ref[...]
    ks = jnp.sum(kcol * s, axis=1)
    v_new = beta_ref[...] * (v_ref[...] - eg * ks)
    sn = s * eg[:, None, :] + kcol * v_new[:, None, :]
    sn_ref[...] = sn
    o = jnp.sum(q_ref[...] * sn, axis=1)
    o_ref[...] = _rms(o, gain_ref[...]) * _silu(dz_ref[...])


def _dn_step(s, qcol, kcol, v, eg, beta, dz, gain, tb):
    n = s.shape[0]
    big = pl.BlockSpec((tb, D_DK, D_DV), lambda i: (i, 0, 0))
    row = pl.BlockSpec((tb, D_DV), lambda i: (i, 0))
    return pl.pallas_call(
        _dn_step_kernel,
        out_shape=(jax.ShapeDtypeStruct(s.shape, F32), jax.ShapeDtypeStruct((n, D_DV), F32)),
        grid=(n // tb,),
        in_specs=[big, big, big, row, row, row, row, _full_spec((1, D_DV))],
        out_specs=(big, row),
        compiler_params=_cparams("parallel"),
        name="dn_step",
    )(s, qcol, kcol, v, eg, beta, dz, gain)


def _prep_w_in(w):
    o = np.cumsum([0, 256, 256, 256, 256, 128, 128, 256, 256, 16, 768, 4, 4, 256])
    q, k, v, u, cq, ck, cv, cg, lr, dqkv, db, da, dz = [w[:, o[i]:o[i + 1]] for i in range(13)]
    pad = jnp.zeros((w.shape[0], SMALL_W - 24), w.dtype)
    return jnp.concatenate([q, k, v, u, cq, ck, cv, dqkv, cg, dz, lr, db, da, pad], axis=1).astype(BF16)


def _lane_row(vals, offset):
    return jnp.zeros((1, 128), F32).at[0, offset:offset + vals.shape[0]].set(vals)


def kernel(x_prompt, x_sample, cache_k, cache_v, state_ssm_re, state_ssm_im, state_gla, state_delta, state_conv, page_table, p_prompt, p_sample, g_mix, w_in, lam_q1, lam_k1, lam_q2, lam_k2, attn_norm, s5_lam_re, s5_lam_im, s5_log_dt, s5_b_re, s5_b_im, s5_c_re, s5_c_im, s5_d, s5_w_glu, s5_b_glu, gla_w_a2, gla_b_a, gla_norm, dn_conv_w, dn_a_log, dn_dt_bias, dn_norm, w_branch_gate, b_branch_gate, w_branch, w_out, g_ffn, w_ff1, w_ff2, g_ple, w_ple_gate, w_ple_proj, g_final):
    B, S, _ = x_prompt.shape
    DB = x_sample.shape[0]
    T = B * S
    n_pool, page = cache_k.shape[1], cache_k.shape[2]
    hp = x_prompt.reshape(T, D_MODEL)
    hs = x_sample.reshape(DB, D_MODEL)
    row = lambda a: a.reshape(1, -1)
    ck_t = cache_k.transpose(0, 1, 3, 4, 2).reshape(DEPTH, n_pool, A_HEADS * 2 * A_DH, page)
    cv_t = cache_v.transpose(0, 1, 3, 4, 2).reshape(DEPTH, n_pool, A_HEADS * A_DV, page)
    st_p, st_s = [], []
    for l in range(DEPTH):
        lam_init = 0.8 - 0.6 * math.exp(-0.3 * l)
        final = l == DEPTH - 1
        w_in_l = _prep_w_in(w_in[l])
        wg, wb, wo = w_branch_gate[l].astype(BF16), w_branch[l].astype(BF16), w_out[l].astype(BF16)
        w1, w2 = w_ff1[l].astype(BF16), w_ff2[l].astype(BF16)
        wpg, wpp = w_ple_gate[l].astype(BF16), w_ple_proj[l].astype(BF16)
        lam_params = (row(lam_q1[l]), row(lam_k1[l]), row(lam_q2[l]), row(lam_k2[l]))
        sp = _s5_params(s5_lam_re[l], s5_lam_im[l], s5_log_dt[l], s5_b_re[l], s5_b_im[l], s5_c_re[l], s5_c_im[l])
        s5d, s5wg, s5bg = row(s5_d[l]), s5_w_glu[l].astype(BF16), row(s5_b_glu[l])
        wa = jnp.zeros((SMALL_W, 128), F32).at[SM_LR:SM_LR + C_RANK].set(gla_w_a2[l])
        ba = row(gla_b_a[l])
        gla_gain4 = row(jnp.tile(gla_norm[l], C_HEADS))
        dn_gain4 = row(jnp.tile(dn_norm[l], D_HEADS))
        acoef = _lane_row(-jnp.exp(dn_a_log[l]), SM_A)
        dtb = _lane_row(dn_dt_bias[l], SM_A)
        cw = dn_conv_w[l]

        zp, kt_p, vt_p = _inproj(hp, row(g_mix[l]), w_in_l, TM_PROJ, seq_len=S)
        oa = _attn_prompt(zp, lam_params, row(attn_norm[l]), B, S, TQ_ATTN, lam_init)
        ob, p_hre, p_him = _s5_prompt(zp, sp, s5d, s5wg, s5bg, B, S, TS_S5)
        oc, p_gla_t = _gla_prompt(zp.reshape(B, S, Z_W), wa, ba, gla_gain4, B, S, TS_SEQS)
        oc = oc.reshape(T, 256)
        od, p_dn_bd, p_conv = _dn_prompt(zp.reshape(B, S, Z_W), cw, acoef, dtb, dn_gain4, B, S, TS_SEQS)
        od = od.reshape(T, 256)
        p_dn = jnp.stack([p_dn_bd[:, h * D_DK:(h + 1) * D_DK, h * D_DV:(h + 1) * D_DV] for h in range(D_HEADS)], axis=1)
        hp = _merge(hp, oa, ob, oc, od, row(g_mix[l]), wg, b_branch_gate[l], wb, wo, TM_PROJ)
        hp = _ffn(hp, p_prompt[l].reshape(T, PLE_DIM), row(g_ffn[l]), w1, w2, row(g_ple[l]), wpg, wpp, row(g_final),
                  TM_FFN, final)
        p_gla = jnp.stack([p_gla_t[:, h * C_DV:(h + 1) * C_DV, h * C_DK:(h + 1) * C_DK] for h in range(C_HEADS)],
                          axis=1).swapaxes(-1, -2)
        st_p.append((kt_p.reshape(B, A_HEADS, 2 * A_DH, S).transpose(0, 3, 1, 2),
                     vt_p.reshape(B, A_HEADS, A_DV, S).transpose(0, 3, 1, 2),
                     p_hre.reshape(B, S5_GROUPS, S5_P), p_him.reshape(B, S5_GROUPS, S5_P), p_gla, p_dn, p_conv))

        zs, = _inproj(hs, row(g_mix[l]), w_in_l, DB)
        oa_s = _attn_decode(page_table, ck_t, cv_t, l, zs.reshape(DB, 1, Z_W), lam_params, row(attn_norm[l]),
                            lam_init).reshape(DB, 256)
        (ob_s, s_hre, s_him, gq, gk, ga, dq, dk, dv, dgb, s_conv3) = _dec_prep(
            zs, state_ssm_re[l].reshape(DB, S5_STATE), state_ssm_im[l].reshape(DB, S5_STATE),
            state_conv[l].swapaxes(0, 1), sp, s5d, s5wg, s5bg, wa, ba, cw, acoef, dtb)
        col = lambda a, h, dk, dv: jnp.broadcast_to(a.reshape(DB * h, dk, 1), (DB * h, dk, dv))
        s_gla, oc_s = _gla_step(state_gla[l].reshape(DB * C_HEADS, C_DK, C_DV), col(gq, C_HEADS, C_DK, C_DV),
                                col(gk, C_HEADS, C_DK, C_DV), col(ga, C_HEADS, C_DK, C_DV),
                                zs[:, COL_CV:COL_CV + 256].reshape(DB * C_HEADS, C_DV),
                                zs[:, COL_CG:COL_CG + 256].reshape(DB * C_HEADS, C_DV), row(gla_norm[l]), TB_GLA_STEP)
        lanes = lambda a, off: jnp.broadcast_to(a[:, off:off + D_HEADS].reshape(DB * D_HEADS, 1), (DB * D_HEADS, D_DV))
        s_dn, od_s = _dn_step(state_delta[l].reshape(DB * D_HEADS, D_DK, D_DV), col(dq, D_HEADS, D_DK, D_DV),
                              col(dk, D_HEADS, D_DK, D_DV), dv.reshape(DB * D_HEADS, D_DV), lanes(dgb, SM_A),
                              lanes(dgb, SM_B), zs[:, COL_DZ:COL_DZ + 256].reshape(DB * D_HEADS, D_DV),
                              row(dn_norm[l]), TB_DN_STEP)
        hs = _merge(hs, oa_s, ob_s, oc_s.reshape(DB, 256), od_s.reshape(DB, 256), row(g_mix[l]), wg, b_branch_gate[l],
                    wb, wo, DB)
        hs = _ffn(hs, p_sample[l].reshape(DB, PLE_DIM), row(g_ffn[l]), w1, w2, row(g_ple[l]), wpg, wpp, row(g_final),
                  DB, final)
        st_s.append((zs[:, COL_K:COL_K + 256].reshape(DB, 1, A_HEADS, 2 * A_DH),
                     zs[:, COL_V:COL_V + 256].reshape(DB, 1, A_HEADS, A_DV),
                     s_hre.reshape(DB, S5_GROUPS, S5_P), s_him.reshape(DB, S5_GROUPS, S5_P),
                     s_gla.reshape(DB, C_HEADS, C_DK, C_DV), s_dn.reshape(DB, D_HEADS, D_DK, D_DV),
                     s_conv3.swapaxes(0, 1)))

    stk = lambda states, i: jnp.stack([s[i] for s in states])
    return (hp.reshape(B, S, D_MODEL), hs.reshape(DB, 1, D_MODEL),
            stk(st_p, 0), stk(st_p, 1), stk(st_p, 2), stk(st_p, 3), stk(st_p, 4), stk(st_p, 5), stk(st_p, 6),
            stk(st_s, 0), stk(st_s, 1), stk(st_s, 2), stk(st_s, 3), stk(st_s, 4), stk(st_s, 5), stk(st_s, 6))
```

```python
import functools
import math

import numpy as np
import jax
import jax.numpy as jnp
from jax import lax
from jax.experimental import pallas as pl
from jax.experimental.pallas import tpu as pltpu

F32 = jnp.float32
BF16 = jnp.bfloat16
HI = lax.Precision.HIGHEST

D_MODEL = 1024
DEPTH = 2
A_HEADS, A_DH, A_DV = 4, 32, 64
S5_GROUPS, S5_GROUP, S5_P = 16, 16, 64
S5_STATE = S5_GROUPS * S5_P
C_HEADS, C_DK, C_DV, C_RANK = 4, 32, 64, 16
GLA_TAU = 16.0
D_HEADS, D_DK, D_DV = 4, 64, 64
CONV_W = 4
DN_QKV_W = 768
D_FF = 4096
PLE_DIM = 256
EPS = 1e-6
BRANCH_W = 256

Z_W = 2944
COL_Q, COL_K, COL_V, COL_U = 0, 256, 512, 768
COL_CQK, COL_CV, COL_DQKV, COL_CG, COL_DZ, COL_SMALL = 1024, 1280, 1536, 2304, 2560, 2816
SMALL_W = 128
SM_LR, SM_B, SM_A = 0, 16, 20

S5_SCAN = 64
S5_LEVELS = 6
S5_PAD = S5_SCAN // 2
GLA_SUB = 16
DN_CHUNK = 64
VMEM_LIMIT_BYTES = 56 * 1024 * 1024

TM_PROJ = 512
TM_FFN = 256
TQ_ATTN = 512
TS_S5 = 512
TS_SEQS = 256
TB_GLA_STEP = 128
TB_DN_STEP = 64
NEG_INF = float("-inf")


def _cparams(*sem):
    return pltpu.CompilerParams(dimension_semantics=sem, vmem_limit_bytes=VMEM_LIMIT_BYTES)


def _sigmoid(x):
    return 1.0 / (1.0 + jnp.exp(-x))


def _silu(x):
    return x * _sigmoid(x)


def _softplus(x):
    return jnp.maximum(x, 0.0) + jnp.log(1.0 + jnp.exp(-jnp.abs(x)))


def _gelu_tanh(x):
    return 0.5 * x * (1.0 + jnp.tanh(math.sqrt(2.0 / math.pi) * (x + 0.044715 * (x * x * x))))


def _rms(x, g):
    return x * lax.rsqrt(jnp.mean(x * x, axis=-1, keepdims=True) + EPS) * g


def _dot(a, b, **kw):
    return jnp.dot(a, b, preferred_element_type=F32, **kw)


def _dot_nt(a, b, **kw):
    return lax.dot_general(a, b, (((1,), (1,)), ((), ())), preferred_element_type=F32, **kw)


def _dot_tn(a, b, **kw):
    return lax.dot_general(a, b, (((0,), (0,)), ((), ())), preferred_element_type=F32, **kw)


def _full_spec(shape):
    nd = len(shape)
    return pl.BlockSpec(shape, lambda *_: (0,) * nd)


def _seg_matrix(width, seg):
    i = np.arange(width) // seg
    return jnp.asarray((i[:, None] == i[None, :]).astype(np.float32)).astype(BF16)


def _bf16_parts(x, parts):
    out = []
    for i in range(parts):
        p = x.astype(BF16)
        out.append(p)
        if i + 1 < parts:
            x = x - p.astype(F32)
    return out


def _dot01(x, e, parts):
    acc = None
    for p in _bf16_parts(x, parts):
        t = _dot(p, e)
        acc = t if acc is None else acc + t
    return acc


def _dot01_left(e, x, parts):
    acc = None
    for p in _bf16_parts(x, parts):
        t = _dot(e, p)
        acc = t if acc is None else acc + t
    return acc


def _head_rms(o, seg_ref, gain):
    ms = _dot01(o * o, seg_ref[...], 2) * (1.0 / 64.0)
    return o * lax.rsqrt(ms + EPS) * gain


def _inproj_kernel(x_ref, g_ref, w_ref, z_ref, *kv_refs):
    hn = _rms(x_ref[...], g_ref[...]).astype(BF16)
    z = _dot(hn, w_ref[...])
    z_ref[...] = z
    if kv_refs:
        kt_ref, vt_ref = kv_refs
        kt_ref[0] = z[:, COL_K:COL_K + 256].T
        vt_ref[0] = z[:, COL_V:COL_V + 256].T


def _inproj(x, g, w, tm, seq_len=None):
    T = x.shape[0]
    out_shape = [jax.ShapeDtypeStruct((T, Z_W), F32)]
    out_specs = [pl.BlockSpec((tm, Z_W), lambda i: (i, 0))]
    if seq_len is not None:
        nt = seq_len // tm
        out_shape += [jax.ShapeDtypeStruct((T // seq_len, 256, seq_len), F32)] * 2
        out_specs += [pl.BlockSpec((1, 256, tm), lambda i: (i // nt, 0, i % nt))] * 2
    return pl.pallas_call(
        _inproj_kernel,
        out_shape=tuple(out_shape),
        grid=(T // tm,),
        in_specs=[pl.BlockSpec((tm, D_MODEL), lambda i: (i, 0)), _full_spec((1, D_MODEL)), _full_spec((D_MODEL, Z_W))],
        out_specs=tuple(out_specs),
        compiler_params=_cparams("parallel"),
        name="inproj",
    )(x, g, w)


def _merge_kernel(h_ref, oa_ref, ob_ref, oc_ref, od_ref, g_ref, wg_ref, bg_ref, wb_ref, wo_ref, out_ref):
    h = h_ref[...]
    hn = _rms(h, g_ref[...]).astype(BF16)
    merged = None
    for n, br in enumerate((oa_ref, ob_ref, oc_ref, od_ref)):
        gate = _sigmoid(_dot(hn, wg_ref[n]) + bg_ref[n:n + 1, :])
        term = gate * _dot(br[...].astype(BF16), wb_ref[n])
        merged = term if merged is None else merged + term
    out_ref[...] = h + _dot(merged.astype(BF16), wo_ref[...])


def _merge(h, oa, ob, oc, od, g, wg, bg, wb, wo, tm):
    T = h.shape[0]
    row = lambda w: pl.BlockSpec((tm, w), lambda i: (i, 0))
    return pl.pallas_call(
        _merge_kernel,
        out_shape=jax.ShapeDtypeStruct((T, D_MODEL), F32),
        grid=(T // tm,),
        in_specs=[row(D_MODEL), row(BRANCH_W), row(BRANCH_W), row(BRANCH_W), row(BRANCH_W),
                  _full_spec((1, D_MODEL)), _full_spec((4, D_MODEL, D_MODEL)), _full_spec((4, D_MODEL)),
                  _full_spec((4, BRANCH_W, D_MODEL)), _full_spec((D_MODEL, D_MODEL))],
        out_specs=row(D_MODEL),
        compiler_params=_cparams("parallel"),
        name="merge",
    )(h, oa, ob, oc, od, g, wg, bg, wb, wo)


def _ffn_kernel(h_ref, p_ref, gf_ref, w1_ref, w2_ref, gp_ref, wpg_ref, wpp_ref, gfin_ref, out_ref, *, final):
    h = h_ref[...]
    hf = _rms(h, gf_ref[...]).astype(BF16)
    acc = h
    step = 1024
    for j in range(D_FF // step):
        a = jnp.maximum(_dot(hf, w1_ref[:, j * step:(j + 1) * step]), 0.0)
        acc = acc + _dot((a * a).astype(BF16), w2_ref[j * step:(j + 1) * step, :])
    pg = _sigmoid(_dot(_rms(acc, gp_ref[...]).astype(BF16), wpg_ref[...]))
    out = acc + pg * _dot(p_ref[...].astype(BF16), wpp_ref[...])
    if final:
        out = _rms(out, gfin_ref[...])
    out_ref[...] = out


def _ffn(h, p, gf, w1, w2, gp, wpg, wpp, gfin, tm, final):
    T = h.shape[0]
    row = lambda w: pl.BlockSpec((tm, w), lambda i: (i, 0))
    return pl.pallas_call(
        functools.partial(_ffn_kernel, final=final),
        out_shape=jax.ShapeDtypeStruct((T, D_MODEL), F32),
        grid=(T // tm,),
        in_specs=[row(D_MODEL), row(PLE_DIM), _full_spec((1, D_MODEL)), _full_spec((D_MODEL, D_FF)),
                  _full_spec((D_FF, D_MODEL)), _full_spec((1, D_MODEL)), _full_spec((D_MODEL, D_MODEL)),
                  _full_spec((PLE_DIM, D_MODEL)), _full_spec((1, D_MODEL))],
        out_specs=row(D_MODEL),
        compiler_params=_cparams("parallel"),
        name="ffn_ple",
    )(h, p, gf, w1, w2, gp, wpg, wpp, gfin)


_SLOPES = [float(s) for s in 2.0 ** (-8.0 * np.arange(1, A_HEADS + 1) / A_HEADS)]
LOG2E = 1.0 / math.log(2.0)


def _lam_value(lq1, lk1, lq2, lk2, lam_init):
    return (jnp.exp(jnp.sum(lq1[...] * lk1[...], axis=-1, keepdims=True))
            - jnp.exp(jnp.sum(lq2[...] * lk2[...], axis=-1, keepdims=True)) + lam_init)


def _attn_prompt_kernel(qi_ref, ki_ref, q_ref, k_ref, v_ref, lq1, lk1, lq2, lk2, gain_ref, o_ref,
                        qm_scr, m_scr, acc_scr, *, tq, lam_init):
    pair = pl.program_id(1)
    qi = qi_ref[pair]
    ki = ki_ref[pair]

    @pl.when(ki == 0)
    def _init():
        q = q_ref[...] * (A_DH ** -0.5 * LOG2E)
        lane = lax.broadcasted_iota(jnp.int32, q.shape, 1)
        for i in range(2 * A_HEADS):
            lo = (i // 2) * 2 * A_DH + (i % 2) * A_DH
            qm_scr[i] = jnp.where((lane >= lo) & (lane < lo + A_DH), q, 0.0).astype(BF16)
        m_scr[...] = jnp.full(m_scr.shape, NEG_INF, F32)
        acc_scr[...] = jnp.zeros(acc_scr.shape, F32)

    def _step(diag):
        k = k_ref[...].astype(BF16)
        vt = v_ref[...].T
        kpos = (lax.broadcasted_iota(jnp.int32, (tq, 1), 0) + (ki - qi) * tq).astype(F32)
        ones_row = jnp.where(lax.broadcasted_iota(jnp.int32, (A_DV, tq), 0) == 0, 1.0, 0.0)
        if diag:
            causal = (lax.broadcasted_iota(jnp.int32, (tq, tq), 0) <= lax.broadcasted_iota(jnp.int32, (tq, tq), 1))
        qq = jnp.concatenate([qm_scr[i] for i in range(2 * A_HEADS)], axis=0)
        s_all = _dot_nt(k, qq)
        for h in range(A_HEADS):
            vext = jnp.concatenate([vt[h * A_DV:(h + 1) * A_DV, :], ones_row], axis=0).astype(BF16)
            bias = kpos * (_SLOPES[h] * LOG2E)
            if diag:
                bias = jnp.where(causal, bias, NEG_INF)
                bias = jnp.concatenate([bias, bias], axis=1)
            i = 2 * h
            s = s_all[:, i * tq:(i + 2) * tq] + bias
            m_prev = jnp.concatenate([m_scr[i], m_scr[i + 1]], axis=1)
            m_new = jnp.maximum(m_prev, jnp.max(s, axis=0, keepdims=True))
            alpha = jnp.exp2(m_prev - m_new)
            p = jnp.exp2(s - m_new)
            pv = _dot(vext, p.astype(BF16))
            acc_scr[i] = alpha[:, :tq] * acc_scr[i] + pv[:, :tq]
            acc_scr[i + 1] = alpha[:, tq:] * acc_scr[i + 1] + pv[:, tq:]
            m_scr[i] = m_new[:, :tq]
            m_scr[i + 1] = m_new[:, tq:]

    @pl.when(ki < qi)
    def _off_diagonal():
        _step(False)

    @pl.when(ki == qi)
    def _diagonal():
        _step(True)
        lam = _lam_value(lq1, lk1, lq2, lk2, lam_init)
        outs = []
        for h in range(A_HEADS):
            a1, a2 = acc_scr[2 * h], acc_scr[2 * h + 1]
            o = a1[:A_DV, :] / a1[A_DV:A_DV + 1, :] - lam * (a2[:A_DV, :] / a2[A_DV:A_DV + 1, :])
            outs.append(o * lax.rsqrt(jnp.mean(o * o, axis=0, keepdims=True) + EPS))
        gain = jnp.concatenate([gain_ref[...]] * A_HEADS, axis=1)
        o_ref[...] = jnp.concatenate(outs, axis=0).T * gain * (1.0 - lam_init)


def _attn_prompt(z, lam_params, gain, B, S, tq, lam_init):
    nq = S // tq
    pairs = [(qi, ki) for qi in range(nq) for ki in range(qi + 1)]
    qi_arr = jnp.asarray([p[0] for p in pairs], jnp.int32)
    ki_arr = jnp.asarray([p[1] for p in pairs], jnp.int32)
    vec = pl.BlockSpec((1, A_DH), lambda b, p, qa, ka: (0, 0))
    grid_spec = pltpu.PrefetchScalarGridSpec(
        num_scalar_prefetch=2,
        grid=(B, len(pairs)),
        in_specs=[pl.BlockSpec((tq, 256), lambda b, p, qa, ka: (b * nq + qa[p], COL_Q // 256)),
                  pl.BlockSpec((tq, 256), lambda b, p, qa, ka: (b * nq + ka[p], COL_K // 256)),
                  pl.BlockSpec((tq, 256), lambda b, p, qa, ka: (b * nq + ka[p], COL_V // 256)),
                  vec, vec, vec, vec, pl.BlockSpec((1, A_DV), lambda b, p, qa, ka: (0, 0))],
        out_specs=pl.BlockSpec((tq, 256), lambda b, p, qa, ka: (b * nq + qa[p], 0)),
        scratch_shapes=[pltpu.VMEM((2 * A_HEADS, tq, 256), BF16), pltpu.VMEM((2 * A_HEADS, 1, tq), F32),
                        pltpu.VMEM((2 * A_HEADS, 128, tq), F32)],
    )
    return pl.pallas_call(
        functools.partial(_attn_prompt_kernel, tq=tq, lam_init=lam_init),
        out_shape=jax.ShapeDtypeStruct((B * S, A_HEADS * A_DV), F32),
        grid_spec=grid_spec,
        compiler_params=_cparams("parallel", "arbitrary"),
        name="attn_prompt",
    )(qi_arr, ki_arr, z, z, z, *lam_params, gain)


def _attn_decode_kernel(pt_ref, ck_ref, cv_ref, z_ref, lq1, lk1, lq2, lk2, gain_ref, o_ref, kbuf, vbuf, sem,
                        *, layer, n_pages, page, lam_init):
    b = pl.program_id(0)
    slot = b % 2

    def page_copies(seq, into):
        cps = []
        for j in range(n_pages):
            pid = pt_ref[seq, j]
            cps.append(pltpu.make_async_copy(ck_ref.at[layer, pid], kbuf.at[into, j], sem.at[into, 0]))
            cps.append(pltpu.make_async_copy(cv_ref.at[layer, pid], vbuf.at[into, j], sem.at[into, 1]))
        return cps

    @pl.when(b == 0)
    def _first():
        for cp in page_copies(0, 0):
            cp.start()

    @pl.when(b + 1 < pl.num_programs(0))
    def _next():
        for cp in page_copies(b + 1, 1 - slot):
            cp.start()

    for cp in page_copies(b, slot):
        cp.wait()
    past = n_pages * page
    zrow = z_ref[0]
    q = zrow[:, COL_Q:COL_Q + 256] * (A_DH ** -0.5)
    k_own = zrow[:, COL_K:COL_K + 256]
    v_own = zrow[:, COL_V:COL_V + 256]
    lane = lax.broadcasted_iota(jnp.int32, (2 * A_HEADS, 256), 1)
    r = lax.broadcasted_iota(jnp.int32, (2 * A_HEADS, 256), 0)
    lo = (r % A_HEADS) * 2 * A_DH + (r // A_HEADS) * A_DH
    qm = jnp.where((lane >= lo) & (lane < lo + A_DH), jnp.broadcast_to(q, (2 * A_HEADS, 256)), 0.0)
    qmb = qm.astype(BF16)
    kcat = jnp.concatenate([kbuf[slot, j].astype(BF16) for j in range(n_pages)], axis=1)
    s = _dot(qmb, kcat)
    rr = lax.broadcasted_iota(jnp.int32, (2 * A_HEADS, 1), 0) % A_HEADS
    slope = jnp.zeros((2 * A_HEADS, 1), F32)
    for h in range(A_HEADS):
        slope = jnp.where(rr == h, _SLOPES[h], slope)
    kpos = lax.broadcasted_iota(jnp.int32, (2 * A_HEADS, past), 1).astype(F32)
    s = s - slope * (float(past) - kpos)
    s_own = jnp.sum(qm * k_own, axis=-1, keepdims=True)
    m = jnp.maximum(jnp.max(s, axis=-1, keepdims=True), s_own)
    p = jnp.exp(s - m)
    p_own = jnp.exp(s_own - m)
    denom = jnp.sum(p, axis=-1, keepdims=True) + p_own
    pb = p.astype(BF16)
    vcat = jnp.concatenate([vbuf[slot, j].astype(BF16) for j in range(n_pages)], axis=1)
    o = (p_own * v_own + _dot_nt(pb, vcat)) / denom
    lam = _lam_value(lq1, lk1, lq2, lk2, lam_init)
    d = o[:A_HEADS] - lam * o[A_HEADS:]
    hl = lax.broadcasted_iota(jnp.int32, (A_HEADS, 256), 1) // A_DV
    hr = lax.broadcasted_iota(jnp.int32, (A_HEADS, 256), 0)
    own = hl == hr
    d = jnp.where(own, d, 0.0)
    ms = jnp.sum(d * d, axis=-1, keepdims=True) * (1.0 / A_DV)
    d = d * lax.rsqrt(ms + EPS)
    gain = jnp.concatenate([gain_ref[...]] * A_HEADS, axis=1)
    o_ref[0] = jnp.sum(d, axis=0, keepdims=True) * gain * (1.0 - lam_init)


def _attn_decode(page_table, ck, cv, layer, z3, lam_params, gain, lam_init):
    DB, n_pages = page_table.shape
    page = ck.shape[-1]
    hbm = pl.BlockSpec(memory_space=pl.ANY)
    vec = pl.BlockSpec((1, A_DH), lambda b, pt: (0, 0))
    grid_spec = pltpu.PrefetchScalarGridSpec(
        num_scalar_prefetch=1,
        grid=(DB,),
        in_specs=[hbm, hbm, pl.BlockSpec((1, 1, Z_W), lambda b, pt: (b, 0, 0)), vec, vec, vec, vec,
                  pl.BlockSpec((1, A_DV), lambda b, pt: (0, 0))],
        out_specs=pl.BlockSpec((1, 1, 256), lambda b, pt: (b, 0, 0)),
        scratch_shapes=[pltpu.VMEM((2, n_pages, 256, page), F32), pltpu.VMEM((2, n_pages, 256, page), F32),
                        pltpu.SemaphoreType.DMA((2, 2))],
    )
    return pl.pallas_call(
        functools.partial(_attn_decode_kernel, layer=layer, n_pages=n_pages, page=page, lam_init=lam_init),
        out_shape=jax.ShapeDtypeStruct((DB, 1, 256), F32),
        grid_spec=grid_spec,
        compiler_params=_cparams("arbitrary"),
        name="attn_decode",
    )(page_table, ck, cv, z3, *lam_params, gain)


def _s5_params(lam_re, lam_im, log_dt, b_re, b_im, c_re, c_im):
    dt = jnp.exp(log_dt)[:, None]
    lr, li = lam_re, lam_im
    a, th = lr * dt, li * dt
    mag = jnp.exp(a)
    ab_re, ab_im = mag * jnp.cos(th), mag * jnp.sin(th)
    den = lr * lr + li * li
    nr, ni = ab_re - 1.0, ab_im
    f_re = (nr * lr + ni * li) / den
    f_im = (ni * lr - nr * li) / den
    bb_re = f_re[..., None] * b_re - f_im[..., None] * b_im
    bb_im = f_re[..., None] * b_im + f_im[..., None] * b_re
    eye = jnp.eye(S5_GROUPS, dtype=F32)
    expand_b = lambda m: jnp.einsum("gpn,gh->gnhp", m, eye).reshape(S5_GROUPS * S5_GROUP, S5_STATE)
    wb = jnp.concatenate([expand_b(bb_re), expand_b(bb_im)], axis=1)
    expand_c = lambda m: jnp.einsum("gnp,gh->gphn", m, eye).reshape(S5_STATE, S5_GROUPS * S5_GROUP)
    cm = jnp.concatenate([expand_c(c_re), -expand_c(c_im)], axis=0)
    pr, pi = ab_re.reshape(1, S5_STATE), ab_im.reshape(1, S5_STATE)
    res, ims = [pr], [pi]
    for _ in range(S5_LEVELS - 1):
        pr, pi = pr * pr - pi * pi, 2.0 * pr * pi
        res.append(pr)
        ims.append(pi)
    return dict(wb=wb, cm=cm, ab_re=res[0], ab_im=ims[0], pow_re=jnp.concatenate(res, axis=0),
                pow_im=jnp.concatenate(ims, axis=0))


def _s5_out(y, u, d_ref, wg_ref, bg_ref):
    y = _gelu_tanh(y + d_ref[...] * u)
    return y * _sigmoid(_dot(y.astype(BF16), wg_ref[...]) + bg_ref[...])


def _s5_prompt_kernel(u_ref, wb_ref, cm_ref, pr_ref, pi_ref, d_ref, wg_ref, bg_ref, o_ref, hre_ref, him_ref,
                      bu_scr, hb_scr, sr_scr, si_scr, hr_scr, hi_scr, *, ts):
    t = pl.program_id(1)
    n = S5_SCAN
    P = S5_STATE
    pad = S5_PAD

    @pl.when(t == 0)
    def _init():
        hr_scr[...] = jnp.zeros(hr_scr.shape, F32)
        hi_scr[...] = jnp.zeros(hi_scr.shape, F32)
        sr_scr[0:pad, :] = jnp.zeros((pad, P), F32)
        si_scr[0:pad, :] = jnp.zeros((pad, P), F32)

    u = u_ref[...]
    bu_scr[...] = _dot(u.astype(BF16), wb_ref[...])

    def chunk(c, carry):
        r0 = pl.multiple_of(c * n, n)
        sr_scr[pad:pad + n, :] = bu_scr[pl.ds(r0, n), 0:P]
        si_scr[pad:pad + n, :] = bu_scr[pl.ds(r0, n), P:2 * P]
        ar, ai = pr_ref[0:1, :], pi_ref[0:1, :]
        h0r, h0i = hr_scr[...], hi_scr[...]
        sr_scr[pad:pad + 1, :] = sr_scr[pad:pad + 1, :] + (ar * h0r - ai * h0i)
        si_scr[pad:pad + 1, :] = si_scr[pad:pad + 1, :] + (ar * h0i + ai * h0r)
        for lv in range(S5_LEVELS):
            d = 1 << lv
            ar, ai = pr_ref[lv:lv + 1, :], pi_ref[lv:lv + 1, :]
            cr, ci = sr_scr[pad:pad + n, :], si_scr[pad:pad + n, :]
            qr, qi = sr_scr[pad - d:pad - d + n, :], si_scr[pad - d:pad - d + n, :]
            nr, ni = cr + (ar * qr - ai * qi), ci + (ar * qi + ai * qr)
            if lv + 1 < S5_LEVELS:
                sr_scr[pad:pad + n, :] = nr
                si_scr[pad:pad + n, :] = ni
        hr_scr[...] = nr[n - 1:n, :]
        hi_scr[...] = ni[n - 1:n, :]
        hb_scr[pl.ds(r0, n), 0:P] = nr.astype(BF16)
        hb_scr[pl.ds(r0, n), P:2 * P] = ni.astype(BF16)
        return carry

    lax.fori_loop(0, ts // n, chunk, 0)
    y = _dot(hb_scr[...], cm_ref[...])
    o_ref[...] = _s5_out(y, u, d_ref, wg_ref, bg_ref)
    hre_ref[0] = hr_scr[...]
    him_ref[0] = hi_scr[...]


def _s5_prompt(z, sp, d, wg, bg, B, S, ts):
    nt = S // ts
    tab = _full_spec((S5_LEVELS, S5_STATE))
    st = pl.BlockSpec((1, 1, S5_STATE), lambda b, t: (b, 0, 0))
    return pl.pallas_call(
        functools.partial(_s5_prompt_kernel, ts=ts),
        out_shape=(jax.ShapeDtypeStruct((B * S, 256), F32), jax.ShapeDtypeStruct((B, 1, S5_STATE), F32),
                   jax.ShapeDtypeStruct((B, 1, S5_STATE), F32)),
        grid=(B, nt),
        in_specs=[pl.BlockSpec((ts, 256), lambda b, t: (b * nt + t, COL_U // 256)),
                  _full_spec((256, 2 * S5_STATE)), _full_spec((2 * S5_STATE, 256)), tab, tab,
                  _full_spec((1, 256)), _full_spec((256, 256)), _full_spec((1, 256))],
        out_specs=(pl.BlockSpec((ts, 256), lambda b, t: (b * nt + t, 0)), st, st),
        scratch_shapes=[pltpu.VMEM((ts, 2 * S5_STATE), F32), pltpu.VMEM((ts, 2 * S5_STATE), BF16),
                        pltpu.VMEM((S5_PAD + S5_SCAN, S5_STATE), F32), pltpu.VMEM((S5_PAD + S5_SCAN, S5_STATE), F32),
                        pltpu.VMEM((1, S5_STATE), F32), pltpu.VMEM((1, S5_STATE), F32)],
        compiler_params=_cparams("parallel", "arbitrary"),
        name="s5_prompt",
    )(z, sp["wb"].astype(BF16), sp["cm"].astype(BF16), sp["pow_re"], sp["pow_im"], d, wg, bg)


def _gla_log_a(small, wa_ref, ba_ref):
    x = _dot(small, wa_ref[...], precision=HI) + ba_ref[...]
    return (jnp.minimum(x, 0.0) - jnp.log(1.0 + jnp.exp(-jnp.abs(x)))) * (1.0 / GLA_TAU)


def _gla_prompt_kernel(qk_ref, v_ref, cg_ref, sm_ref, wa_ref, ba_ref, gain_ref, segx_ref, seg_ref, bmask_ref,
                       o_ref, st_ref, b_scr, o_scr, st_scr, *, ts, nb):
    t = pl.program_id(0)
    n = GLA_SUB
    seqs = range(nb)

    @pl.when(t == 0)
    def _init():
        st_scr[...] = jnp.zeros(st_scr.shape, F32)

    ri = lax.broadcasted_iota(jnp.int32, (ts, ts), 0)
    ci = lax.broadcasted_iota(jnp.int32, (ts, ts), 1)
    tri = jnp.where((ri >= ci) & (ri // n == ci // n), 1.0, 0.0).astype(BF16)
    for i in seqs:
        log_a = _gla_log_a(sm_ref[i], wa_ref, ba_ref)
        b_scr[i] = _dot01_left(tri, log_a, 3)
    rows = lax.broadcasted_iota(jnp.int32, (n, 128), 0)
    segx = segx_ref[...]
    bmask = bmask_ref[...]

    def body(c, carry):
        r0 = pl.multiple_of(c * n, n)
        qk = [qk_ref[i, pl.ds(r0, n), :] for i in seqs]
        q = [x[:, :128] * (C_DK ** -0.5) for x in qk]
        k = [x[:, 128:] for x in qk]
        v = [v_ref[i, pl.ds(r0, n), :] for i in seqs]
        b = [b_scr[i, pl.ds(r0, n), :] for i in seqs]
        pall = []
        for i in seqs:
            parts = []
            for s in range(n):
                e = jnp.exp(jnp.where(rows >= s, b[i] - b[i][s:s + 1, :], NEG_INF))
                parts.append(q[i] * k[i][s:s + 1, :] * e)
            pall.append(jnp.concatenate(parts, axis=0).astype(BF16))
        aexp = [_dot(pall[i], segx) for i in seqs]
        st = [st_scr[i] for i in seqs]
        inter = [_dot_nt((q[i] * jnp.exp(b[i])).astype(BF16), st[i].astype(BF16)) for i in seqs]
        b_last = [b[i][n - 1:n, :] for i in seqs]
        upd = [_dot_tn(v[i].astype(BF16), (k[i] * jnp.exp(b_last[i] - b[i])).astype(BF16)) for i in seqs]
        for i in seqs:
            acc = inter[i]
            for s in range(n):
                acc = acc + aexp[i][s * n:(s + 1) * n, :] * v[i][s:s + 1, :]
            o_scr[i, pl.ds(r0, n), :] = acc
            st_scr[i] = st[i] * jnp.exp(b_last[i]) + upd[i] * bmask
        return carry

    lax.fori_loop(0, ts // n, body, 0)
    for i in seqs:
        o_ref[i] = _head_rms(o_scr[i], seg_ref, gain_ref[...]) * _silu(cg_ref[i])
    st_ref[...] = st_scr[...]


def _gla_consts():
    hk = np.arange(128) // C_DK
    hv = np.arange(256) // C_DV
    segx = jnp.asarray((hk[:, None] == hv[None, :]).astype(np.float32)).astype(BF16)
    bmask = jnp.asarray((hv[:, None] == hk[None, :]).astype(np.float32))
    return segx, bmask


def _gla_prompt(z, wa, ba, gain, B, S, ts):
    nt = S // ts
    segx, bmask = _gla_consts()
    blk = lambda w, col: pl.BlockSpec((B, ts, w), lambda t: (0, t, col // w))
    return pl.pallas_call(
        functools.partial(_gla_prompt_kernel, ts=ts, nb=B),
        out_shape=(jax.ShapeDtypeStruct((B, S, 256), F32), jax.ShapeDtypeStruct((B, 256, 128), F32)),
        grid=(nt,),
        in_specs=[blk(256, COL_CQK), blk(256, COL_CV), blk(256, COL_CG), blk(SMALL_W, COL_SMALL),
                  _full_spec((SMALL_W, 128)), _full_spec((1, 128)), _full_spec((1, 256)),
                  _full_spec((128, 256)), _full_spec((256, 256)), _full_spec((256, 128))],
        out_specs=(pl.BlockSpec((B, ts, 256), lambda t: (0, t, 0)), _full_spec((B, 256, 128))),
        scratch_shapes=[pltpu.VMEM((B, ts, 128), F32), pltpu.VMEM((B, ts, 256), F32), pltpu.VMEM((B, 256, 128), F32)],
        compiler_params=_cparams("arbitrary"),
        name="gla_prompt",
    )(z, z, z, z, wa, ba, gain, segx, _seg_matrix(256, 64), bmask)


def _dn_gates(small, acoef_ref, dtb_ref):
    beta = _sigmoid(small)
    g = acoef_ref[...] * _softplus(small + dtb_ref[...])
    return beta, g


def _dn_qkv(y, seg_ref):
    y = _silu(y)
    q, k, v = y[:, :256], y[:, 256:512], y[:, 512:768]
    nq = _dot01(q * q, seg_ref[...], 2)
    nk = _dot01(k * k, seg_ref[...], 2)
    q = q * lax.rsqrt(nq + EPS) * (D_DK ** -0.5)
    k = k * lax.rsqrt(nk + EPS)
    return q, k, v


def _dn_prompt_kernel(x_ref, dz_ref, sm_ref, cw_ref, acoef_ref, dtb_ref, gain_ref, seg_ref, esel_ref,
                      o_ref, st_ref, cs_ref, xp_scr, q_scr, k_scr, v_scr, gb_scr, ge_scr, be_scr,
                      u_scr, w_scr, att_scr, o_scr, st_scr, *, ts, nb):
    t = pl.program_id(0)
    C = DN_CHUNK
    PADR = 8
    W = D_HEADS * D_DK
    seqs = range(nb)

    @pl.when(t == 0)
    def _init():
        st_scr[...] = jnp.zeros(st_scr.shape, F32)
        for i in seqs:
            xp_scr[i, 0:PADR, :] = jnp.zeros((PADR, DN_QKV_W), F32)

    ri = lax.broadcasted_iota(jnp.int32, (ts, ts), 0)
    ci = lax.broadcasted_iota(jnp.int32, (ts, ts), 1)
    tri = jnp.where((ri >= ci) & (ri // C == ci // C), 1.0, 0.0).astype(BF16)
    lane = lax.broadcasted_iota(jnp.int32, (ts, 128), 1)
    for i in seqs:
        x = x_ref[i]
        xp_scr[i, PADR:PADR + ts, :] = x
        y = cw_ref[3:4, :] * x
        for j in range(CONV_W - 1):
            y = y + cw_ref[j:j + 1, :] * xp_scr[i, PADR - 3 + j:PADR - 3 + j + ts, :]
        xp_scr[i, PADR - 3:PADR, :] = x[ts - 3:ts, :]
        cs_ref[i] = x[ts - 3:ts, :]
        q, k, v = _dn_qkv(y, seg_ref)
        q_scr[i] = q
        k_scr[i] = k
        v_scr[i] = v
        beta, g = _dn_gates(sm_ref[i], acoef_ref, dtb_ref)
        gcum = _dot01_left(tri, g, 3)
        gb = jnp.where(lane >= SM_A, gcum, beta)
        gb_scr[i] = gb
        ex = _dot01(gb, esel_ref[...], 3)
        ge_scr[i] = ex[:, :W]
        be_scr[i] = ex[:, W:]

    rr = lax.broadcasted_iota(jnp.int32, (W, W), 0)
    cc = lax.broadcasted_iota(jnp.int32, (W, W), 1)
    same_head = (rr // C) == (cc // C)
    lower_incl = same_head & (rr >= cc)
    lower_strict = same_head & (rr > cc)
    eye = jnp.where(rr == cc, 1.0, 0.0)
    blocks = [(rr // w) == (cc // w) for w in (8, 16, 32, C)]

    def stack_heads(a):
        return jnp.where(same_head, jnp.concatenate([a] * D_HEADS, axis=0), 0.0)

    def collapse(a):
        return a[0:C] + a[C:2 * C] + a[2 * C:3 * C] + a[3 * C:4 * C]

    def solve_body(c, carry):
        r0 = pl.multiple_of(c * C, C)
        qc = [q_scr[i, pl.ds(r0, C), :] for i in seqs]
        kc = [k_scr[i, pl.ds(r0, C), :] for i in seqs]
        vc = [v_scr[i, pl.ds(r0, C), :] for i in seqs]
        gbc = [gb_scr[i, pl.ds(r0, C), :] for i in seqs]
        ge = [ge_scr[i, pl.ds(r0, C), :] for i in seqs]
        be = [be_scr[i, pl.ds(r0, C), :] for i in seqs]
        decay, kst = [], []
        for j in seqs:
            gbt = gbc[j].T
            gcol = jnp.concatenate([gbc[j][:, SM_A + h:SM_A + h + 1] for h in range(D_HEADS)], axis=0)
            grow = jnp.concatenate([gbt[SM_A + h:SM_A + h + 1, :] for h in range(D_HEADS)], axis=1)
            decay.append(jnp.exp(jnp.where(lower_incl, gcol - grow, NEG_INF)))
            kst.append(stack_heads(kc[j]).astype(BF16))
        kq = [_dot_nt(jnp.concatenate([stack_heads(kc[j] * be[j]), stack_heads(qc[j])], axis=0).astype(BF16), kst[j])
              for j in seqs]
        m = [jnp.where(lower_strict, -(kq[j][:W] * decay[j]), 0.0) for j in seqs]
        m1 = [jnp.where(blocks[0], m[j], 0.0) for j in seqs]
        m1b = [x.astype(BF16) for x in m1]
        m2 = [_dot(m1b[j], m1b[j]) for j in seqs]
        s1 = [eye + m1[j] for j in seqs]
        r2 = [_dot(m2[j].astype(BF16), jnp.concatenate([m2[j], s1[j]], axis=1).astype(BF16)) for j in seqs]
        s3 = [s1[j] + r2[j][:, W:] for j in seqs]
        tm = [s3[j] + _dot(r2[j][:, :W].astype(BF16), s3[j].astype(BF16)) for j in seqs]
        for inner, outer in zip(blocks[:-1], blocks[1:]):
            tmb = [x.astype(BF16) for x in tm]
            off = [jnp.where(outer & jnp.logical_not(inner), m[j], 0.0).astype(BF16) for j in seqs]
            t1 = [_dot(tmb[j], off[j]).astype(BF16) for j in seqs]
            tm = [tm[j] + _dot(t1[j], tmb[j]) for j in seqs]
        rhs = [jnp.concatenate([stack_heads(vc[j] * be[j]), stack_heads(kc[j] * (be[j] * jnp.exp(ge[j])))], axis=1)
               for j in seqs]
        uw = [_dot(tm[j].astype(BF16), rhs[j].astype(BF16)) for j in seqs]
        for j in seqs:
            u_scr[j, c] = collapse(uw[j][:, :W])
            w_scr[j, c] = collapse(uw[j][:, W:])
            att_scr[j, c] = (kq[j][W:] * decay[j]).astype(BF16)
        return carry

    lax.fori_loop(0, ts // C, solve_body, 0)

    def state_body(c, carry):
        r0 = pl.multiple_of(c * C, C)
        qc = [q_scr[i, pl.ds(r0, C), :] for i in seqs]
        kc = [k_scr[i, pl.ds(r0, C), :] for i in seqs]
        ge = [ge_scr[i, pl.ds(r0, C), :] for i in seqs]
        glast = [x[C - 1:C, :] for x in ge]
        s = [st_scr[i] for i in seqs]
        wq = [jnp.concatenate([w_scr[i, c], qc[i] * jnp.exp(ge[i])], axis=0) for i in seqs]
        sw = [_dot(wq[i].astype(BF16), s[i].astype(BF16)) for i in seqs]
        v_new = [u_scr[i, c] - sw[i][:C] for i in seqs]
        ov = [_dot(att_scr[i, c], stack_heads(v_new[i]).astype(BF16)) for i in seqs]
        upd = [_dot_tn((kc[i] * jnp.exp(glast[i] - ge[i])).astype(BF16), v_new[i].astype(BF16)) for i in seqs]
        for i in seqs:
            o_scr[i, pl.ds(r0, C), :] = sw[i][C:] + collapse(ov[i])
            st_scr[i] = s[i] * jnp.exp(glast[i]) + jnp.where(same_head, upd[i], 0.0)
        return carry

    lax.fori_loop(0, ts // C, state_body, 0)
    for i in seqs:
        o_ref[i] = _head_rms(o_scr[i], seg_ref, gain_ref[...]) * _silu(dz_ref[i])
    st_ref[...] = st_scr[...]


def _dn_esel():
    e = np.zeros((128, 512), np.float32)
    for h in range(D_HEADS):
        e[SM_A + h, h * 64:(h + 1) * 64] = 1.0
        e[SM_B + h, 256 + h * 64:256 + (h + 1) * 64] = 1.0
    return jnp.asarray(e).astype(BF16)


def _dn_prompt(z, cw, acoef, dtb, gain, B, S, ts):
    nt = S // ts
    nc = ts // DN_CHUNK
    W = D_HEADS * D_DK
    blk = lambda w, col: pl.BlockSpec((B, ts, w), lambda t: (0, t, col // w))
    return pl.pallas_call(
        functools.partial(_dn_prompt_kernel, ts=ts, nb=B),
        out_shape=(jax.ShapeDtypeStruct((B, S, 256), F32), jax.ShapeDtypeStruct((B, W, W), F32),
                   jax.ShapeDtypeStruct((B, CONV_W - 1, DN_QKV_W), F32)),
        grid=(nt,),
        in_specs=[blk(DN_QKV_W, COL_DQKV), blk(256, COL_DZ), blk(SMALL_W, COL_SMALL),
                  _full_spec((CONV_W, DN_QKV_W)), _full_spec((1, 128)), _full_spec((1, 128)), _full_spec((1, 256)),
                  _full_spec((256, 256)), _full_spec((128, 512))],
        out_specs=(pl.BlockSpec((B, ts, 256), lambda t: (0, t, 0)), _full_spec((B, W, W)),
                   _full_spec((B, CONV_W - 1, DN_QKV_W))),
        scratch_shapes=[pltpu.VMEM((B, 8 + ts, DN_QKV_W), F32), pltpu.VMEM((B, ts, 256), F32),
                        pltpu.VMEM((B, ts, 256), F32), pltpu.VMEM((B, ts, 256), F32), pltpu.VMEM((B, ts, 128), F32),
                        pltpu.VMEM((B, ts, W), F32), pltpu.VMEM((B, ts, W), F32),
                        pltpu.VMEM((B, nc, DN_CHUNK, W), F32), pltpu.VMEM((B, nc, DN_CHUNK, W), F32),
                        pltpu.VMEM((B, nc, W, W), BF16), pltpu.VMEM((B, ts, 256), F32), pltpu.VMEM((B, W, W), F32)],
        compiler_params=_cparams("arbitrary"),
        name="dn_prompt",
    )(z, z, z, cw, acoef, dtb, gain, _seg_matrix(256, 64), _dn_esel())


def _dec_prep_kernel(z_ref, h0r_ref, h0i_ref, conv_ref, wb_ref, cm_ref, abr_ref, abi_ref, d_ref, wg_ref, bg_ref,
                     wa_ref, ba_ref, cw_ref, acoef_ref, dtb_ref, seg_ref,
                     ob_ref, hr_ref, hi_ref, gq_ref, gk_ref, ga_ref, dq_ref, dk_ref, dv_ref, dgb_ref, cs_ref):
    z = z_ref[...]
    u = z[:, COL_U:COL_U + 256]
    bu = _dot(u, wb_ref[...], precision=HI)
    bur, bui = bu[:, :S5_STATE], bu[:, S5_STATE:]
    h0r, h0i = h0r_ref[...], h0i_ref[...]
    abr, abi = abr_ref[...], abi_ref[...]
    hr = abr * h0r - abi * h0i + bur
    hi = abr * h0i + abi * h0r + bui
    hr_ref[...] = hr
    hi_ref[...] = hi
    y = _dot(jnp.concatenate([hr, hi], axis=1).astype(BF16), cm_ref[...])
    ob_ref[...] = _s5_out(y, u, d_ref, wg_ref, bg_ref)
    small = z[:, COL_SMALL:COL_SMALL + SMALL_W]
    gq_ref[...] = z[:, COL_CQK:COL_CQK + 128] * (C_DK ** -0.5)
    gk_ref[...] = z[:, COL_CQK + 128:COL_CQK + 256]
    ga_ref[...] = _gla_log_a(small, wa_ref, ba_ref)
    x = z[:, COL_DQKV:COL_DQKV + DN_QKV_W]
    y = cw_ref[3:4, :] * x
    for i in range(CONV_W - 1):
        y = y + cw_ref[i:i + 1, :] * conv_ref[i]
    cs_ref[0] = conv_ref[1]
    cs_ref[1] = conv_ref[2]
    cs_ref[2] = x
    q, k, v = _dn_qkv(y, seg_ref)
    dq_ref[...] = q
    dk_ref[...] = k
    dv_ref[...] = v
    beta, g = _dn_gates(small, acoef_ref, dtb_ref)
    lane = lax.broadcasted_iota(jnp.int32, small.shape, 1)
    dgb_ref[...] = jnp.where(lane >= SM_A, jnp.exp(g), beta)


def _dec_prep(z, h0r, h0i, conv3, sp, d, wg, bg, wa, ba, cw, acoef, dtb):
    DB = z.shape[0]
    sds = lambda *s: jax.ShapeDtypeStruct(s, F32)
    out_shape = (sds(DB, 256), sds(DB, S5_STATE), sds(DB, S5_STATE), sds(DB, 128), sds(DB, 128), sds(DB, 128),
                 sds(DB, 256), sds(DB, 256), sds(DB, 256), sds(DB, 128), sds(CONV_W - 1, DB, DN_QKV_W))
    return pl.pallas_call(
        _dec_prep_kernel,
        out_shape=out_shape,
        compiler_params=pltpu.CompilerParams(vmem_limit_bytes=VMEM_LIMIT_BYTES),
        name="dec_prep",
    )(z, h0r, h0i, conv3, sp["wb"], sp["cm"].astype(BF16), sp["ab_re"], sp["ab_im"], d, wg, bg, wa, ba, cw, acoef, dtb,
      _seg_matrix(256, 64))


def _gla_step_kernel(s_ref, q_ref, k_ref, a_ref, v_ref, cg_ref, gain_ref, sn_ref, o_ref):
    v = v_ref[...]
    sn = s_ref[...] * jnp.exp(a_ref[...]) + k_ref[...] * v[:, None, :]
    sn_ref[...] = sn
    o = jnp.sum(q_ref[...] * sn, axis=1)
    o_ref[...] = _rms(o, gain_ref[...]) * _silu(cg_ref[...])


def _gla_step(s, qcol, kcol, acol, v, cg, gain, tb):
    n = s.shape[0]
    big = pl.BlockSpec((tb, C_DK, C_DV), lambda i: (i, 0, 0))
    row = pl.BlockSpec((tb, C_DV), lambda i: (i, 0))
    return pl.pallas_call(
        _gla_step_kernel,
        out_shape=(jax.ShapeDtypeStruct(s.shape, F32), jax.ShapeDtypeStruct((n, C_DV), F32)),
        grid=(n // tb,),
        in_specs=[big, big, big, big, row, row, _full_spec((1, C_DV))],
        out_specs=(big, row),
        compiler_params=_cparams("parallel"),
        name="gla_step",
    )(s, qcol, kcol, acol, v, cg, gain)


def _dn_step_kernel(s_ref, q_ref, k_ref, v_ref, eg_ref, beta_ref, dz_ref, gain_ref, sn_ref, o_ref):
    s = s_ref[...]
    kcol = k_ref[...]
    eg = eg_ref[...]
    ks = jnp.sum(kcol * s, axis=1)
    v_new = beta_ref[...] * (v_ref[...] - eg * ks)
    sn = s * eg[:, None, :] + kcol * v_new[:, None, :]
    sn_ref[...] = sn
    o = jnp.sum(q_ref[...] * sn, axis=1)
    o_ref[...] = _rms(o, gain_ref[...]) * _silu(dz_ref[...])


def _dn_step(s, qcol, kcol, v, eg, beta, dz, gain, tb):
    n = s.shape[0]
    big = pl.BlockSpec((tb, D_DK, D_DV), lambda i: (i, 0, 0))
    row = pl.BlockSpec((tb, D_DV), lambda i: (i, 0))
    return pl.pallas_call(
        _dn_step_kernel,
        out_shape=(jax.ShapeDtypeStruct(s.shape, F32), jax.ShapeDtypeStruct((n, D_DV), F32)),
        grid=(n // tb,),
        in_specs=[big, big, big, row, row, row, row, _full_spec((1, D_DV))],
        out_specs=(big, row),
        compiler_params=_cparams("parallel"),
        name="dn_step",
    )(s, qcol, kcol, v, eg, beta, dz, gain)


def _prep_w_in(w):
    o = np.cumsum([0, 256, 256, 256, 256, 128, 128, 256, 256, 16, 768, 4, 4, 256])
    q, k, v, u, cq, ck, cv, cg, lr, dqkv, db, da, dz = [w[:, o[i]:o[i + 1]] for i in range(13)]
    pad = jnp.zeros((w.shape[0], SMALL_W - 24), w.dtype)
    return jnp.concatenate([q, k, v, u, cq, ck, cv, dqkv, cg, dz, lr, db, da, pad], axis=1).astype(BF16)


def _lane_row(vals, offset):
    return jnp.zeros((1, 128), F32).at[0, offset:offset + vals.shape[0]].set(vals)


def kernel(x_prompt, x_sample, cache_k, cache_v, state_ssm_re, state_ssm_im, state_gla, state_delta, state_conv, page_table, p_prompt, p_sample, g_mix, w_in, lam_q1, lam_k1, lam_q2, lam_k2, attn_norm, s5_lam_re, s5_lam_im, s5_log_dt, s5_b_re, s5_b_im, s5_c_re, s5_c_im, s5_d, s5_w_glu, s5_b_glu, gla_w_a2, gla_b_a, gla_norm, dn_conv_w, dn_a_log, dn_dt_bias, dn_norm, w_branch_gate, b_branch_gate, w_branch, w_out, g_ffn, w_ff1, w_ff2, g_ple, w_ple_gate, w_ple_proj, g_final):
    B, S, _ = x_prompt.shape
    DB = x_sample.shape[0]
    T = B * S
    n_pool, page = cache_k.shape[1], cache_k.shape[2]
    hp = x_prompt.reshape(T, D_MODEL)
    hs = x_sample.reshape(DB, D_MODEL)
    row = lambda a: a.reshape(1, -1)
    ck_t = cache_k.transpose(0, 1, 3, 4, 2).reshape(DEPTH, n_pool, A_HEADS * 2 * A_DH, page)
    cv_t = cache_v.transpose(0, 1, 3, 4, 2).reshape(DEPTH, n_pool, A_HEADS * A_DV, page)
    st_p, st_s = [], []
    for l in range(DEPTH):
        lam_init = 0.8 - 0.6 * math.exp(-0.3 * l)
        final = l == DEPTH - 1
        w_in_l = _prep_w_in(w_in[l])
        wg, wb, wo = w_branch_gate[l].astype(BF16), w_branch[l].astype(BF16), w_out[l].astype(BF16)
        w1, w2 = w_ff1[l].astype(BF16), w_ff2[l].astype(BF16)
        wpg, wpp = w_ple_gate[l].astype(BF16), w_ple_proj[l].astype(BF16)
        lam_params = (row(lam_q1[l]), row(lam_k1[l]), row(lam_q2[l]), row(lam_k2[l]))
        sp = _s5_params(s5_lam_re[l], s5_lam_im[l], s5_log_dt[l], s5_b_re[l], s5_b_im[l], s5_c_re[l], s5_c_im[l])
        s5d, s5wg, s5bg = row(s5_d[l]), s5_w_glu[l].astype(BF16), row(s5_b_glu[l])
        wa = jnp.zeros((SMALL_W, 128), F32).at[SM_LR:SM_LR + C_RANK].set(gla_w_a2[l])
        ba = row(gla_b_a[l])
        gla_gain4 = row(jnp.tile(gla_norm[l], C_HEADS))
        dn_gain4 = row(jnp.tile(dn_norm[l], D_HEADS))
        acoef = _lane_row(-jnp.exp(dn_a_log[l]), SM_A)
        dtb = _lane_row(dn_dt_bias[l], SM_A)
        cw = dn_conv_w[l]

        zp, kt_p, vt_p = _inproj(hp, row(g_mix[l]), w_in_l, TM_PROJ, seq_len=S)
        oa = _attn_prompt(zp, lam_params, row(attn_norm[l]), B, S, TQ_ATTN, lam_init)
        ob, p_hre, p_him = _s5_prompt(zp, sp, s5d, s5wg, s5bg, B, S, TS_S5)
        oc, p_gla_t = _gla_prompt(zp.reshape(B, S, Z_W), wa, ba, gla_gain4, B, S, TS_SEQS)
        oc = oc.reshape(T, 256)
        od, p_dn_bd, p_conv = _dn_prompt(zp.reshape(B, S, Z_W), cw, acoef, dtb, dn_gain4, B, S, TS_SEQS)
        od = od.reshape(T, 256)
        p_dn = jnp.stack([p_dn_bd[:, h * D_DK:(h + 1) * D_DK, h * D_DV:(h + 1) * D_DV] for h in range(D_HEADS)], axis=1)
        hp = _merge(hp, oa, ob, oc, od, row(g_mix[l]), wg, b_branch_gate[l], wb, wo, TM_PROJ)
        hp = _ffn(hp, p_prompt[l].reshape(T, PLE_DIM), row(g_ffn[l]), w1, w2, row(g_ple[l]), wpg, wpp, row(g_final),
                  TM_FFN, final)
        p_gla = jnp.stack([p_gla_t[:, h * C_DV:(h + 1) * C_DV, h * C_DK:(h + 1) * C_DK] for h in range(C_HEADS)],
                          axis=1).swapaxes(-1, -2)
        st_p.append((kt_p.reshape(B, A_HEADS, 2 * A_DH, S).transpose(0, 3, 1, 2),
                     vt_p.reshape(B, A_HEADS, A_DV, S).transpose(0, 3, 1, 2),
                     p_hre.reshape(B, S5_GROUPS, S5_P), p_him.reshape(B, S5_GROUPS, S5_P), p_gla, p_dn, p_conv))

        zs, = _inproj(hs, row(g_mix[l]), w_in_l, DB)
        oa_s = _attn_decode(page_table, ck_t, cv_t, l, zs.reshape(DB, 1, Z_W), lam_params, row(attn_norm[l]),
                            lam_init).reshape(DB, 256)
        (ob_s, s_hre, s_him, gq, gk, ga, dq, dk, dv, dgb, s_conv3) = _dec_prep(
            zs, state_ssm_re[l].reshape(DB, S5_STATE), state_ssm_im[l].reshape(DB, S5_STATE),
            state_conv[l].swapaxes(0, 1), sp, s5d, s5wg, s5bg, wa, ba, cw, acoef, dtb)
        col = lambda a, h, dk, dv: jnp.broadcast_to(a.reshape(DB * h, dk, 1), (DB * h, dk, dv))
        s_gla, oc_s = _gla_step(state_gla[l].reshape(DB * C_HEADS, C_DK, C_DV), col(gq, C_HEADS, C_DK, C_DV),
                                col(gk, C_HEADS, C_DK, C_DV), col(ga, C_HEADS, C_DK, C_DV),
                                zs[:, COL_CV:COL_CV + 256].reshape(DB * C_HEADS, C_DV),
                                zs[:, COL_CG:COL_CG + 256].reshape(DB * C_HEADS, C_DV), row(gla_norm[l]), TB_GLA_STEP)
        lanes = lambda a, off: jnp.broadcast_to(a[:, off:off + D_HEADS].reshape(DB * D_HEADS, 1), (DB * D_HEADS, D_DV))
        s_dn, od_s = _dn_step(state_delta[l].reshape(DB * D_HEADS, D_DK, D_DV), col(dq, D_HEADS, D_DK, D_DV),
                              col(dk, D_HEADS, D_DK, D_DV), dv.reshape(DB * D_HEADS, D_DV), lanes(dgb, SM_A),
                              lanes(dgb, SM_B), zs[:, COL_DZ:COL_DZ + 256].reshape(DB * D_HEADS, D_DV),
                              row(dn_norm[l]), TB_DN_STEP)
        hs = _merge(hs, oa_s, ob_s, oc_s.reshape(DB, 256), od_s.reshape(DB, 256), row(g_mix[l]), wg, b_branch_gate[l],
                    wb, wo, DB)
        hs = _ffn(hs, p_sample[l].reshape(DB, PLE_DIM), row(g_ffn[l]), w1, w2, row(g_ple[l]), wpg, wpp, row(g_final),
                  DB, final)
        st_s.append((zs[:, COL_K:COL_K + 256].reshape(DB, 1, A_HEADS, 2 * A_DH),
                     zs[:, COL_V:COL_V + 256].reshape(DB, 1, A_HEADS, A_DV),
                     s_hre.reshape(DB, S5_GROUPS, S5_P), s_him.reshape(DB, S5_GROUPS, S5_P),
                     s_gla.reshape(DB, C_HEADS, C_DK, C_DV), s_dn.reshape(DB, D_HEADS, D_DK, D_DV),
                     s_conv3.swapaxes(0, 1)))

    stk = lambda states, i: jnp.stack([s[i] for s in states])
    return (hp.reshape(B, S, D_MODEL), hs.reshape(DB, 1, D_MODEL),
            stk(st_p, 0), stk(st_p, 1), stk(st_p, 2), stk(st_p, 3), stk(st_p, 4), stk(st_p, 5), stk(st_p, 6),
            stk(st_s, 0), stk(st_s, 1), stk(st_s, 2), stk(st_s, 3), stk(st_s, 4), stk(st_s, 5), stk(st_s, 6))
```

```python
import functools
import math

import numpy as np
import jax
import jax.numpy as jnp
from jax import lax
from jax.experimental import pallas as pl
from jax.experimental.pallas import tpu as pltpu

F32 = jnp.float32
BF16 = jnp.bfloat16
HI = lax.Precision.HIGHEST

D_MODEL = 1024
DEPTH = 2
A_HEADS, A_DH, A_DV = 4, 32, 64
S5_GROUPS, S5_GROUP, S5_P = 16, 16, 64
S5_STATE = S5_GROUPS * S5_P
C_HEADS, C_DK, C_DV, C_RANK = 4, 32, 64, 16
GLA_TAU = 16.0
D_HEADS, D_DK, D_DV = 4, 64, 64
CONV_W = 4
DN_QKV_W = 768
D_FF = 4096
PLE_DIM = 256
EPS = 1e-6
BRANCH_W = 256

Z_W = 2944
COL_Q, COL_K, COL_V, COL_U = 0, 256, 512, 768
COL_CQK, COL_CV, COL_DQKV, COL_CG, COL_DZ, COL_SMALL = 1024, 1280, 1536, 2304, 2560, 2816
SMALL_W = 128
SM_LR, SM_B, SM_A = 0, 16, 20

S5_SCAN = 64
S5_LEVELS = 6
S5_PAD = S5_SCAN // 2
GLA_SUB = 16
DN_CHUNK = 64
VMEM_LIMIT_BYTES = 56 * 1024 * 1024

TM_PROJ = 512
TM_FFN = 256
TQ_ATTN = 512
TS_S5 = 512
TS_SEQS = 256
TB_GLA_STEP = 128
TB_DN_STEP = 64
NEG_INF = float("-inf")


def _cparams(*sem):
    return pltpu.CompilerParams(dimension_semantics=sem, vmem_limit_bytes=VMEM_LIMIT_BYTES)


def _sigmoid(x):
    return 1.0 / (1.0 + jnp.exp(-x))


def _silu(x):
    return x * _sigmoid(x)


def _softplus(x):
    return jnp.maximum(x, 0.0) + jnp.log(1.0 + jnp.exp(-jnp.abs(x)))


def _gelu_tanh(x):
    return 0.5 * x * (1.0 + jnp.tanh(math.sqrt(2.0 / math.pi) * (x + 0.044715 * (x * x * x))))


def _rms(x, g):
    return x * lax.rsqrt(jnp.mean(x * x, axis=-1, keepdims=True) + EPS) * g


def _dot(a, b, **kw):
    return jnp.dot(a, b, preferred_element_type=F32, **kw)


def _dot_nt(a, b, **kw):
    return lax.dot_general(a, b, (((1,), (1,)), ((), ())), preferred_element_type=F32, **kw)


def _dot_tn(a, b, **kw):
    return lax.dot_general(a, b, (((0,), (0,)), ((), ())), preferred_element_type=F32, **kw)


def _full_spec(shape):
    nd = len(shape)
    return pl.BlockSpec(shape, lambda *_: (0,) * nd)


def _seg_matrix(width, seg):
    i = np.arange(width) // seg
    return jnp.asarray((i[:, None] == i[None, :]).astype(np.float32)).astype(BF16)


def _bf16_parts(x, parts):
    out = []
    for i in range(parts):
        p = x.astype(BF16)
        out.append(p)
        if i + 1 < parts:
            x = x - p.astype(F32)
    return out


def _dot01(x, e, parts):
    acc = None
    for p in _bf16_parts(x, parts):
        t = _dot(p, e)
        acc = t if acc is None else acc + t
    return acc


def _dot01_left(e, x, parts):
    acc = None
    for p in _bf16_parts(x, parts):
        t = _dot(e, p)
        acc = t if acc is None else acc + t
    return acc


def _head_rms(o, seg_ref, gain):
    ms = _dot01(o * o, seg_ref[...], 2) * (1.0 / 64.0)
    return o * lax.rsqrt(ms + EPS) * gain


def _inproj_kernel(x_ref, g_ref, w_ref, z_ref, *kv_refs):
    hn = _rms(x_ref[...], g_ref[...]).astype(BF16)
    z = _dot(hn, w_ref[...])
    z_ref[...] = z
    if kv_refs:
        kt_ref, vt_ref = kv_refs
        kt_ref[0] = z[:, COL_K:COL_K + 256].T
        vt_ref[0] = z[:, COL_V:COL_V + 256].T


def _inproj(x, g, w, tm, seq_len=None):
    T = x.shape[0]
    out_shape = [jax.ShapeDtypeStruct((T, Z_W), F32)]
    out_specs = [pl.BlockSpec((tm, Z_W), lambda i: (i, 0))]
    if seq_len is not None:
        nt = seq_len // tm
        out_shape += [jax.ShapeDtypeStruct((T // seq_len, 256, seq_len), F32)] * 2
        out_specs += [pl.BlockSpec((1, 256, tm), lambda i: (i // nt, 0, i % nt))] * 2
    return pl.pallas_call(
        _inproj_kernel,
        out_shape=tuple(out_shape),
        grid=(T // tm,),
        in_specs=[pl.BlockSpec((tm, D_MODEL), lambda i: (i, 0)), _full_spec((1, D_MODEL)), _full_spec((D_MODEL, Z_W))],
        out_specs=tuple(out_specs),
        compiler_params=_cparams("parallel"),
        name="inproj",
    )(x, g, w)


def _merge_kernel(h_ref, oa_ref, ob_ref, oc_ref, od_ref, g_ref, wg_ref, bg_ref, wb_ref, wo_ref, out_ref):
    h = h_ref[...]
    hn = _rms(h, g_ref[...]).astype(BF16)
    merged = None
    for n, br in enumerate((oa_ref, ob_ref, oc_ref, od_ref)):
        gate = _sigmoid(_dot(hn, wg_ref[n]) + bg_ref[n:n + 1, :])
        term = gate * _dot(br[...].astype(BF16), wb_ref[n])
        merged = term if merged is None else merged + term
    out_ref[...] = h + _dot(merged.astype(BF16), wo_ref[...])


def _merge(h, oa, ob, oc, od, g, wg, bg, wb, wo, tm):
    T = h.shape[0]
    row = lambda w: pl.BlockSpec((tm, w), lambda i: (i, 0))
    return pl.pallas_call(
        _merge_kernel,
        out_shape=jax.ShapeDtypeStruct((T, D_MODEL), F32),
        grid=(T // tm,),
        in_specs=[row(D_MODEL), row(BRANCH_W), row(BRANCH_W), row(BRANCH_W), row(BRANCH_W),
                  _full_spec((1, D_MODEL)), _full_spec((4, D_MODEL, D_MODEL)), _full_spec((4, D_MODEL)),
                  _full_spec((4, BRANCH_W, D_MODEL)), _full_spec((D_MODEL, D_MODEL))],
        out_specs=row(D_MODEL),
        compiler_params=_cparams("parallel"),
        name="merge",
    )(h, oa, ob, oc, od, g, wg, bg, wb, wo)


def _ffn_kernel(h_ref, p_ref, gf_ref, w1_ref, w2_ref, gp_ref, wpg_ref, wpp_ref, gfin_ref, out_ref, *, final):
    h = h_ref[...]
    hf = _rms(h, gf_ref[...]).astype(BF16)
    acc = h
    step = 1024
    for j in range(D_FF // step):
        a = jnp.maximum(_dot(hf, w1_ref[:, j * step:(j + 1) * step]), 0.0)
        acc = acc + _dot((a * a).astype(BF16), w2_ref[j * step:(j + 1) * step, :])
    pg = _sigmoid(_dot(_rms(acc, gp_ref[...]).astype(BF16), wpg_ref[...]))
    out = acc + pg * _dot(p_ref[...].astype(BF16), wpp_ref[...])
    if final:
        out = _rms(out, gfin_ref[...])
    out_ref[...] = out


def _ffn(h, p, gf, w1, w2, gp, wpg, wpp, gfin, tm, final):
    T = h.shape[0]
    row = lambda w: pl.BlockSpec((tm, w), lambda i: (i, 0))
    return pl.pallas_call(
        functools.partial(_ffn_kernel, final=final),
        out_shape=jax.ShapeDtypeStruct((T, D_MODEL), F32),
        grid=(T // tm,),
        in_specs=[row(D_MODEL), row(PLE_DIM), _full_spec((1, D_MODEL)), _full_spec((D_MODEL, D_FF)),
                  _full_spec((D_FF, D_MODEL)), _full_spec((1, D_MODEL)), _full_spec((D_MODEL, D_MODEL)),
                  _full_spec((PLE_DIM, D_MODEL)), _full_spec((1, D_MODEL))],
        out_specs=row(D_MODEL),
        compiler_params=_cparams("parallel"),
        name="ffn_ple",
    )(h, p, gf, w1, w2, gp, wpg, wpp, gfin)


_SLOPES = [float(s) for s in 2.0 ** (-8.0 * np.arange(1, A_HEADS + 1) / A_HEADS)]
LOG2E = 1.0 / math.log(2.0)


def _lam_value(lq1, lk1, lq2, lk2, lam_init):
    return (jnp.exp(jnp.sum(lq1[...] * lk1[...], axis=-1, keepdims=True))
            - jnp.exp(jnp.sum(lq2[...] * lk2[...], axis=-1, keepdims=True)) + lam_init)


def _attn_prompt_kernel(qi_ref, ki_ref, q_ref, k_ref, v_ref, lq1, lk1, lq2, lk2, gain_ref, o_ref,
                        qm_scr, m_scr, acc_scr, *, tq, lam_init):
    pair = pl.program_id(1)
    qi = qi_ref[pair]
    ki = ki_ref[pair]

    @pl.when(ki == 0)
    def _init():
        q = q_ref[...] * (A_DH ** -0.5 * LOG2E)
        lane = lax.broadcasted_iota(jnp.int32, q.shape, 1)
        for i in range(2 * A_HEADS):
            lo = (i // 2) * 2 * A_DH + (i % 2) * A_DH
            qm_scr[i] = jnp.where((lane >= lo) & (lane < lo + A_DH), q, 0.0).astype(BF16)
        m_scr[...] = jnp.full(m_scr.shape, NEG_INF, F32)
        acc_scr[...] = jnp.zeros(acc_scr.shape, F32)

    def _step(diag):
        k = k_ref[...].astype(BF16)
        vt = v_ref[...].T
        kpos = (lax.broadcasted_iota(jnp.int32, (tq, 1), 0) + (ki - qi) * tq).astype(F32)
        ones_row = jnp.where(lax.broadcasted_iota(jnp.int32, (A_DV, tq), 0) == 0, 1.0, 0.0)
        if diag:
            causal = (lax.broadcasted_iota(jnp.int32, (tq, tq), 0) <= lax.broadcasted_iota(jnp.int32, (tq, tq), 1))
        qq = jnp.concatenate([qm_scr[i] for i in range(2 * A_HEADS)], axis=0)
        s_all = _dot_nt(k, qq)
        for h in range(A_HEADS):
            vext = jnp.concatenate([vt[h * A_DV:(h + 1) * A_DV, :], ones_row], axis=0).astype(BF16)
            bias = kpos * (_SLOPES[h] * LOG2E)
            if diag:
                bias = jnp.where(causal, bias, NEG_INF)
                bias = jnp.concatenate([bias, bias], axis=1)
            i = 2 * h
            s = s_all[:, i * tq:(i + 2) * tq] + bias
            m_prev = jnp.concatenate([m_scr[i], m_scr[i + 1]], axis=1)
            m_new = jnp.maximum(m_prev, jnp.max(s, axis=0, keepdims=True))
            alpha = jnp.exp2(m_prev - m_new)
            p = jnp.exp2(s - m_new)
            pv = _dot(vext, p.astype(BF16))
            acc_scr[i] = alpha[:, :tq] * acc_scr[i] + pv[:, :tq]
            acc_scr[i + 1] = alpha[:, tq:] * acc_scr[i + 1] + pv[:, tq:]
            m_scr[i] = m_new[:, :tq]
            m_scr[i + 1] = m_new[:, tq:]

    @pl.when(ki < qi)
    def _off_diagonal():
        _step(False)

    @pl.when(ki == qi)
    def _diagonal():
        _step(True)
        lam = _lam_value(lq1, lk1, lq2, lk2, lam_init)
        outs = []
        for h in range(A_HEADS):
            a1, a2 = acc_scr[2 * h], acc_scr[2 * h + 1]
            o = a1[:A_DV, :] / a1[A_DV:A_DV + 1, :] - lam * (a2[:A_DV, :] / a2[A_DV:A_DV + 1, :])
            outs.append(o * lax.rsqrt(jnp.mean(o * o, axis=0, keepdims=True) + EPS))
        gain = jnp.concatenate([gain_ref[...]] * A_HEADS, axis=1)
        o_ref[...] = jnp.concatenate(outs, axis=0).T * gain * (1.0 - lam_init)


def _attn_prompt(z, lam_params, gain, B, S, tq, lam_init):
    nq = S // tq
    pairs = [(qi, ki) for qi in range(nq) for ki in range(qi + 1)]
    qi_arr = jnp.asarray([p[0] for p in pairs], jnp.int32)
    ki_arr = jnp.asarray([p[1] for p in pairs], jnp.int32)
    vec = pl.BlockSpec((1, A_DH), lambda b, p, qa, ka: (0, 0))
    grid_spec = pltpu.PrefetchScalarGridSpec(
        num_scalar_prefetch=2,
        grid=(B, len(pairs)),
        in_specs=[pl.BlockSpec((tq, 256), lambda b, p, qa, ka: (b * nq + qa[p], COL_Q // 256)),
                  pl.BlockSpec((tq, 256), lambda b, p, qa, ka: (b * nq + ka[p], COL_K // 256)),
                  pl.BlockSpec((tq, 256), lambda b, p, qa, ka: (b * nq + ka[p], COL_V // 256)),
                  vec, vec, vec, vec, pl.BlockSpec((1, A_DV), lambda b, p, qa, ka: (0, 0))],
        out_specs=pl.BlockSpec((tq, 256), lambda b, p, qa, ka: (b * nq + qa[p], 0)),
        scratch_shapes=[pltpu.VMEM((2 * A_HEADS, tq, 256), BF16), pltpu.VMEM((2 * A_HEADS, 1, tq), F32),
                        pltpu.VMEM((2 * A_HEADS, 128, tq), F32)],
    )
    return pl.pallas_call(
        functools.partial(_attn_prompt_kernel, tq=tq, lam_init=lam_init),
        out_shape=jax.ShapeDtypeStruct((B * S, A_HEADS * A_DV), F32),
        grid_spec=grid_spec,
        compiler_params=_cparams("parallel", "arbitrary"),
        name="attn_prompt",
    )(qi_arr, ki_arr, z, z, z, *lam_params, gain)


def _attn_decode_kernel(pt_ref, ck_ref, cv_ref, z_ref, lq1, lk1, lq2, lk2, gain_ref, o_ref, kbuf, vbuf, sem,
                        *, layer, n_pages, page, lam_init):
    b = pl.program_id(0)
    slot = b % 2

    def page_copies(seq, into):
        cps = []
        for j in range(n_pages):
            pid = pt_ref[seq, j]
            cps.append(pltpu.make_async_copy(ck_ref.at[layer, pid], kbuf.at[into, j], sem.at[into, 0]))
            cps.append(pltpu.make_async_copy(cv_ref.at[layer, pid], vbuf.at[into, j], sem.at[into, 1]))
        return cps

    def start_all(cps):
        for i, cp in enumerate(cps):
            cp.start(priority=i % 2)

    @pl.when(b == 0)
    def _first():
        start_all(page_copies(0, 0))

    @pl.when(b + 1 < pl.num_programs(0))
    def _next():
        start_all(page_copies(b + 1, 1 - slot))

    for cp in page_copies(b, slot):
        cp.wait()
    past = n_pages * page
    zrow = z_ref[0]
    q = zrow[:, COL_Q:COL_Q + 256] * (A_DH ** -0.5)
    k_own = zrow[:, COL_K:COL_K + 256]
    v_own = zrow[:, COL_V:COL_V + 256]
    lane = lax.broadcasted_iota(jnp.int32, (2 * A_HEADS, 256), 1)
    r = lax.broadcasted_iota(jnp.int32, (2 * A_HEADS, 256), 0)
    lo = (r % A_HEADS) * 2 * A_DH + (r // A_HEADS) * A_DH
    qm = jnp.where((lane >= lo) & (lane < lo + A_DH), jnp.broadcast_to(q, (2 * A_HEADS, 256)), 0.0)
    qmb = qm.astype(BF16)
    kcat = jnp.concatenate([kbuf[slot, j].astype(BF16) for j in range(n_pages)], axis=1)
    s = _dot(qmb, kcat)
    rr = lax.broadcasted_iota(jnp.int32, (2 * A_HEADS, 1), 0) % A_HEADS
    slope = jnp.zeros((2 * A_HEADS, 1), F32)
    for h in range(A_HEADS):
        slope = jnp.where(rr == h, _SLOPES[h], slope)
    kpos = lax.broadcasted_iota(jnp.int32, (2 * A_HEADS, past), 1).astype(F32)
    s = s - slope * (float(past) - kpos)
    s_own = jnp.sum(qm * k_own, axis=-1, keepdims=True)
    m = jnp.maximum(jnp.max(s, axis=-1, keepdims=True), s_own)
    p = jnp.exp(s - m)
    p_own = jnp.exp(s_own - m)
    denom = jnp.sum(p, axis=-1, keepdims=True) + p_own
    pb = p.astype(BF16)
    vcat = jnp.concatenate([vbuf[slot, j].astype(BF16) for j in range(n_pages)], axis=1)
    o = (p_own * v_own + _dot_nt(pb, vcat)) / denom
    lam = _lam_value(lq1, lk1, lq2, lk2, lam_init)
    d = o[:A_HEADS] - lam * o[A_HEADS:]
    hl = lax.broadcasted_iota(jnp.int32, (A_HEADS, 256), 1) // A_DV
    hr = lax.broadcasted_iota(jnp.int32, (A_HEADS, 256), 0)
    own = hl == hr
    d = jnp.where(own, d, 0.0)
    ms = jnp.sum(d * d, axis=-1, keepdims=True) * (1.0 / A_DV)
    d = d * lax.rsqrt(ms + EPS)
    gain = jnp.concatenate([gain_ref[...]] * A_HEADS, axis=1)
    o_ref[0] = jnp.sum(d, axis=0, keepdims=True) * gain * (1.0 - lam_init)


def _attn_decode(page_table, ck, cv, layer, z3, lam_params, gain, lam_init):
    DB, n_pages = page_table.shape
    page = ck.shape[-1]
    hbm = pl.BlockSpec(memory_space=pl.ANY)
    vec = pl.BlockSpec((1, A_DH), lambda b, pt: (0, 0))
    grid_spec = pltpu.PrefetchScalarGridSpec(
        num_scalar_prefetch=1,
        grid=(DB,),
        in_specs=[hbm, hbm, pl.BlockSpec((1, 1, Z_W), lambda b, pt: (b, 0, 0)), vec, vec, vec, vec,
                  pl.BlockSpec((1, A_DV), lambda b, pt: (0, 0))],
        out_specs=pl.BlockSpec((1, 1, 256), lambda b, pt: (b, 0, 0)),
        scratch_shapes=[pltpu.VMEM((2, n_pages, 256, page), F32), pltpu.VMEM((2, n_pages, 256, page), F32),
                        pltpu.SemaphoreType.DMA((2, 2))],
    )
    return pl.pallas_call(
        functools.partial(_attn_decode_kernel, layer=layer, n_pages=n_pages, page=page, lam_init=lam_init),
        out_shape=jax.ShapeDtypeStruct((DB, 1, 256), F32),
        grid_spec=grid_spec,
        compiler_params=_cparams("arbitrary"),
        name="attn_decode",
    )(page_table, ck, cv, z3, *lam_params, gain)


def _s5_params(lam_re, lam_im, log_dt, b_re, b_im, c_re, c_im):
    dt = jnp.exp(log_dt)[:, None]
    lr, li = lam_re, lam_im
    a, th = lr * dt, li * dt
    mag = jnp.exp(a)
    ab_re, ab_im = mag * jnp.cos(th), mag * jnp.sin(th)
    den = lr * lr + li * li
    nr, ni = ab_re - 1.0, ab_im
    f_re = (nr * lr + ni * li) / den
    f_im = (ni * lr - nr * li) / den
    bb_re = f_re[..., None] * b_re - f_im[..., None] * b_im
    bb_im = f_re[..., None] * b_im + f_im[..., None] * b_re
    eye = jnp.eye(S5_GROUPS, dtype=F32)
    expand_b = lambda m: jnp.einsum("gpn,gh->gnhp", m, eye).reshape(S5_GROUPS * S5_GROUP, S5_STATE)
    wb = jnp.concatenate([expand_b(bb_re), expand_b(bb_im)], axis=1)
    expand_c = lambda m: jnp.einsum("gnp,gh->gphn", m, eye).reshape(S5_STATE, S5_GROUPS * S5_GROUP)
    cm = jnp.concatenate([expand_c(c_re), -expand_c(c_im)], axis=0)
    pr, pi = ab_re.reshape(1, S5_STATE), ab_im.reshape(1, S5_STATE)
    res, ims = [pr], [pi]
    for _ in range(S5_LEVELS - 1):
        pr, pi = pr * pr - pi * pi, 2.0 * pr * pi
        res.append(pr)
        ims.append(pi)
    return dict(wb=wb, cm=cm, ab_re=res[0], ab_im=ims[0], pow_re=jnp.concatenate(res, axis=0),
                pow_im=jnp.concatenate(ims, axis=0))


def _s5_out(y, u, d_ref, wg_ref, bg_ref):
    y = _gelu_tanh(y + d_ref[...] * u)
    return y * _sigmoid(_dot(y.astype(BF16), wg_ref[...]) + bg_ref[...])


def _s5_prompt_kernel(u_ref, wb_ref, cm_ref, pr_ref, pi_ref, d_ref, wg_ref, bg_ref, o_ref, hre_ref, him_ref,
                      bu_scr, hb_scr, sr_scr, si_scr, hr_scr, hi_scr, *, ts):
    t = pl.program_id(1)
    n = S5_SCAN
    P = S5_STATE
    pad = S5_PAD

    @pl.when(t == 0)
    def _init():
        hr_scr[...] = jnp.zeros(hr_scr.shape, F32)
        hi_scr[...] = jnp.zeros(hi_scr.shape, F32)
        sr_scr[0:pad, :] = jnp.zeros((pad, P), F32)
        si_scr[0:pad, :] = jnp.zeros((pad, P), F32)

    u = u_ref[...]
    bu_scr[...] = _dot(u.astype(BF16), wb_ref[...])

    def chunk(c, carry):
        r0 = pl.multiple_of(c * n, n)
        sr_scr[pad:pad + n, :] = bu_scr[pl.ds(r0, n), 0:P]
        si_scr[pad:pad + n, :] = bu_scr[pl.ds(r0, n), P:2 * P]
        ar, ai = pr_ref[0:1, :], pi_ref[0:1, :]
        h0r, h0i = hr_scr[...], hi_scr[...]
        sr_scr[pad:pad + 1, :] = sr_scr[pad:pad + 1, :] + (ar * h0r - ai * h0i)
        si_scr[pad:pad + 1, :] = si_scr[pad:pad + 1, :] + (ar * h0i + ai * h0r)
        for lv in range(S5_LEVELS):
            d = 1 << lv
            ar, ai = pr_ref[lv:lv + 1, :], pi_ref[lv:lv + 1, :]
            cr, ci = sr_scr[pad:pad + n, :], si_scr[pad:pad + n, :]
            qr, qi = sr_scr[pad - d:pad - d + n, :], si_scr[pad - d:pad - d + n, :]
            nr, ni = cr + (ar * qr - ai * qi), ci + (ar * qi + ai * qr)
            if lv + 1 < S5_LEVELS:
                sr_scr[pad:pad + n, :] = nr
                si_scr[pad:pad + n, :] = ni
        hr_scr[...] = nr[n - 1:n, :]
        hi_scr[...] = ni[n - 1:n, :]
        hb_scr[pl.ds(r0, n), 0:P] = nr.astype(BF16)
        hb_scr[pl.ds(r0, n), P:2 * P] = ni.astype(BF16)
        return carry

    lax.fori_loop(0, ts // n, chunk, 0)
    y = _dot(hb_scr[...], cm_ref[...])
    o_ref[...] = _s5_out(y, u, d_ref, wg_ref, bg_ref)
    hre_ref[0] = hr_scr[...]
    him_ref[0] = hi_scr[...]


def _s5_prompt(z, sp, d, wg, bg, B, S, ts):
    nt = S // ts
    tab = _full_spec((S5_LEVELS, S5_STATE))
    st = pl.BlockSpec((1, 1, S5_STATE), lambda b, t: (b, 0, 0))
    return pl.pallas_call(
        functools.partial(_s5_prompt_kernel, ts=ts),
        out_shape=(jax.ShapeDtypeStruct((B * S, 256), F32), jax.ShapeDtypeStruct((B, 1, S5_STATE), F32),
                   jax.ShapeDtypeStruct((B, 1, S5_STATE), F32)),
        grid=(B, nt),
        in_specs=[pl.BlockSpec((ts, 256), lambda b, t: (b * nt + t, COL_U // 256)),
                  _full_spec((256, 2 * S5_STATE)), _full_spec((2 * S5_STATE, 256)), tab, tab,
                  _full_spec((1, 256)), _full_spec((256, 256)), _full_spec((1, 256))],
        out_specs=(pl.BlockSpec((ts, 256), lambda b, t: (b * nt + t, 0)), st, st),
        scratch_shapes=[pltpu.VMEM((ts, 2 * S5_STATE), F32), pltpu.VMEM((ts, 2 * S5_STATE), BF16),
                        pltpu.VMEM((S5_PAD + S5_SCAN, S5_STATE), F32), pltpu.VMEM((S5_PAD + S5_SCAN, S5_STATE), F32),
                        pltpu.VMEM((1, S5_STATE), F32), pltpu.VMEM((1, S5_STATE), F32)],
        compiler_params=_cparams("parallel", "arbitrary"),
        name="s5_prompt",
    )(z, sp["wb"].astype(BF16), sp["cm"].astype(BF16), sp["pow_re"], sp["pow_im"], d, wg, bg)


def _gla_log_a(small, wa_ref, ba_ref):
    x = _dot(small, wa_ref[...], precision=HI) + ba_ref[...]
    return (jnp.minimum(x, 0.0) - jnp.log(1.0 + jnp.exp(-jnp.abs(x)))) * (1.0 / GLA_TAU)


def _gla_prompt_kernel(qk_ref, v_ref, cg_ref, sm_ref, wa_ref, ba_ref, gain_ref, segx_ref, seg_ref, bmask_ref,
                       o_ref, st_ref, b_scr, o_scr, st_scr, *, ts, nb):
    t = pl.program_id(0)
    n = GLA_SUB
    seqs = range(nb)

    @pl.when(t == 0)
    def _init():
        st_scr[...] = jnp.zeros(st_scr.shape, F32)

    ri = lax.broadcasted_iota(jnp.int32, (ts, ts), 0)
    ci = lax.broadcasted_iota(jnp.int32, (ts, ts), 1)
    tri = jnp.where((ri >= ci) & (ri // n == ci // n), 1.0, 0.0).astype(BF16)
    for i in seqs:
        log_a = _gla_log_a(sm_ref[i], wa_ref, ba_ref)
        b_scr[i] = _dot01_left(tri, log_a, 3)
    rows = lax.broadcasted_iota(jnp.int32, (n, 128), 0)
    segx = segx_ref[...]
    bmask = bmask_ref[...]

    def body(c, carry):
        r0 = pl.multiple_of(c * n, n)
        qk = [qk_ref[i, pl.ds(r0, n), :] for i in seqs]
        q = [x[:, :128] * (C_DK ** -0.5) for x in qk]
        k = [x[:, 128:] for x in qk]
        v = [v_ref[i, pl.ds(r0, n), :] for i in seqs]
        b = [b_scr[i, pl.ds(r0, n), :] for i in seqs]
        pall = []
        for i in seqs:
            parts = []
            for s in range(n):
                e = jnp.exp(jnp.where(rows >= s, b[i] - b[i][s:s + 1, :], NEG_INF))
                parts.append(q[i] * k[i][s:s + 1, :] * e)
            pall.append(jnp.concatenate(parts, axis=0).astype(BF16))
        aexp = [_dot(pall[i], segx) for i in seqs]
        st = [st_scr[i] for i in seqs]
        inter = [_dot_nt((q[i] * jnp.exp(b[i])).astype(BF16), st[i].astype(BF16)) for i in seqs]
        b_last = [b[i][n - 1:n, :] for i in seqs]
        upd = [_dot_tn(v[i].astype(BF16), (k[i] * jnp.exp(b_last[i] - b[i])).astype(BF16)) for i in seqs]
        for i in seqs:
            acc = inter[i]
            for s in range(n):
                acc = acc + aexp[i][s * n:(s + 1) * n, :] * v[i][s:s + 1, :]
            o_scr[i, pl.ds(r0, n), :] = acc
            st_scr[i] = st[i] * jnp.exp(b_last[i]) + upd[i] * bmask
        return carry

    lax.fori_loop(0, ts // n, body, 0)
    for i in seqs:
        o_ref[i] = _head_rms(o_scr[i], seg_ref, gain_ref[...]) * _silu(cg_ref[i])
    st_ref[...] = st_scr[...]


def _gla_consts():
    hk = np.arange(128) // C_DK
    hv = np.arange(256) // C_DV
    segx = jnp.asarray((hk[:, None] == hv[None, :]).astype(np.float32)).astype(BF16)
    bmask = jnp.asarray((hv[:, None] == hk[None, :]).astype(np.float32))
    return segx, bmask


def _gla_prompt(z, wa, ba, gain, B, S, ts):
    nt = S // ts
    segx, bmask = _gla_consts()
    blk = lambda w, col: pl.BlockSpec((B, ts, w), lambda t: (0, t, col // w))
    return pl.pallas_call(
        functools.partial(_gla_prompt_kernel, ts=ts, nb=B),
        out_shape=(jax.ShapeDtypeStruct((B, S, 256), F32), jax.ShapeDtypeStruct((B, 256, 128), F32)),
        grid=(nt,),
        in_specs=[blk(256, COL_CQK), blk(256, COL_CV), blk(256, COL_CG), blk(SMALL_W, COL_SMALL),
                  _full_spec((SMALL_W, 128)), _full_spec((1, 128)), _full_spec((1, 256)),
                  _full_spec((128, 256)), _full_spec((256, 256)), _full_spec((256, 128))],
        out_specs=(pl.BlockSpec((B, ts, 256), lambda t: (0, t, 0)), _full_spec((B, 256, 128))),
        scratch_shapes=[pltpu.VMEM((B, ts, 128), F32), pltpu.VMEM((B, ts, 256), F32), pltpu.VMEM((B, 256, 128), F32)],
        compiler_params=_cparams("arbitrary"),
        name="gla_prompt",
    )(z, z, z, z, wa, ba, gain, segx, _seg_matrix(256, 64), bmask)


def _dn_gates(small, acoef_ref, dtb_ref):
    beta = _sigmoid(small)
    g = acoef_ref[...] * _softplus(small + dtb_ref[...])
    return beta, g


def _dn_qkv(y, seg_ref):
    y = _silu(y)
    q, k, v = y[:, :256], y[:, 256:512], y[:, 512:768]
    nq = _dot01(q * q, seg_ref[...], 2)
    nk = _dot01(k * k, seg_ref[...], 2)
    q = q * lax.rsqrt(nq + EPS) * (D_DK ** -0.5)
    k = k * lax.rsqrt(nk + EPS)
    return q, k, v


def _dn_prompt_kernel(x_ref, dz_ref, sm_ref, cw_ref, acoef_ref, dtb_ref, gain_ref, seg_ref, esel_ref,
                      o_ref, st_ref, cs_ref, xp_scr, q_scr, k_scr, v_scr, gb_scr, ge_scr, be_scr,
                      u_scr, w_scr, att_scr, o_scr, st_scr, *, ts, nb):
    t = pl.program_id(0)
    C = DN_CHUNK
    PADR = 8
    W = D_HEADS * D_DK
    seqs = range(nb)

    @pl.when(t == 0)
    def _init():
        st_scr[...] = jnp.zeros(st_scr.shape, F32)
        for i in seqs:
            xp_scr[i, 0:PADR, :] = jnp.zeros((PADR, DN_QKV_W), F32)

    ri = lax.broadcasted_iota(jnp.int32, (ts, ts), 0)
    ci = lax.broadcasted_iota(jnp.int32, (ts, ts), 1)
    tri = jnp.where((ri >= ci) & (ri // C == ci // C), 1.0, 0.0).astype(BF16)
    lane = lax.broadcasted_iota(jnp.int32, (ts, 128), 1)
    for i in seqs:
        x = x_ref[i]
        xp_scr[i, PADR:PADR + ts, :] = x
        y = cw_ref[3:4, :] * x
        for j in range(CONV_W - 1):
            y = y + cw_ref[j:j + 1, :] * xp_scr[i, PADR - 3 + j:PADR - 3 + j + ts, :]
        xp_scr[i, PADR - 3:PADR, :] = x[ts - 3:ts, :]
        cs_ref[i] = x[ts - 3:ts, :]
        q, k, v = _dn_qkv(y, seg_ref)
        q_scr[i] = q
        k_scr[i] = k
        v_scr[i] = v
        beta, g = _dn_gates(sm_ref[i], acoef_ref, dtb_ref)
        gcum = _dot01_left(tri, g, 3)
        gb = jnp.where(lane >= SM_A, gcum, beta)
        gb_scr[i] = gb
        ex = _dot01(gb, esel_ref[...], 3)
        ge_scr[i] = ex[:, :W]
        be_scr[i] = ex[:, W:]

    rr = lax.broadcasted_iota(jnp.int32, (W, W), 0)
    cc = lax.broadcasted_iota(jnp.int32, (W, W), 1)
    same_head = (rr // C) == (cc // C)
    lower_incl = same_head & (rr >= cc)
    lower_strict = same_head & (rr > cc)
    eye = jnp.where(rr == cc, 1.0, 0.0)
    blocks = [(rr // w) == (cc // w) for w in (8, 16, 32, C)]

    def stack_heads(a):
        return jnp.where(same_head, jnp.concatenate([a] * D_HEADS, axis=0), 0.0)

    def collapse(a):
        return a[0:C] + a[C:2 * C] + a[2 * C:3 * C] + a[3 * C:4 * C]

    def solve_body(c, carry):
        r0 = pl.multiple_of(c * C, C)
        qc = [q_scr[i, pl.ds(r0, C), :] for i in seqs]
        kc = [k_scr[i, pl.ds(r0, C), :] for i in seqs]
        vc = [v_scr[i, pl.ds(r0, C), :] for i in seqs]
        gbc = [gb_scr[i, pl.ds(r0, C), :] for i in seqs]
        ge = [ge_scr[i, pl.ds(r0, C), :] for i in seqs]
        be = [be_scr[i, pl.ds(r0, C), :] for i in seqs]
        decay, kst = [], []
        for j in seqs:
            gbt = gbc[j].T
            gcol = jnp.concatenate([gbc[j][:, SM_A + h:SM_A + h + 1] for h in range(D_HEADS)], axis=0)
            grow = jnp.concatenate([gbt[SM_A + h:SM_A + h + 1, :] for h in range(D_HEADS)], axis=1)
            decay.append(jnp.exp(jnp.where(lower_incl, gcol - grow, NEG_INF)))
            kst.append(stack_heads(kc[j]).astype(BF16))
        kq = [_dot_nt(jnp.concatenate([stack_heads(kc[j] * be[j]), stack_heads(qc[j])], axis=0).astype(BF16), kst[j])
              for j in seqs]
        m = [jnp.where(lower_strict, -(kq[j][:W] * decay[j]), 0.0) for j in seqs]
        m1 = [jnp.where(blocks[0], m[j], 0.0) for j in seqs]
        m1b = [x.astype(BF16) for x in m1]
        m2 = [_dot(m1b[j], m1b[j]) for j in seqs]
        s1 = [eye + m1[j] for j in seqs]
        r2 = [_dot(m2[j].astype(BF16), jnp.concatenate([m2[j], s1[j]], axis=1).astype(BF16)) for j in seqs]
        s3 = [s1[j] + r2[j][:, W:] for j in seqs]
        tm = [s3[j] + _dot(r2[j][:, :W].astype(BF16), s3[j].astype(BF16)) for j in seqs]
        for inner, outer in zip(blocks[:-1], blocks[1:]):
            tmb = [x.astype(BF16) for x in tm]
            off = [jnp.where(outer & jnp.logical_not(inner), m[j], 0.0).astype(BF16) for j in seqs]
            t1 = [_dot(tmb[j], off[j]).astype(BF16) for j in seqs]
            tm = [tm[j] + _dot(t1[j], tmb[j]) for j in seqs]
        rhs = [jnp.concatenate([stack_heads(vc[j] * be[j]), stack_heads(kc[j] * (be[j] * jnp.exp(ge[j])))], axis=1)
               for j in seqs]
        uw = [_dot(tm[j].astype(BF16), rhs[j].astype(BF16)) for j in seqs]
        for j in seqs:
            u_scr[j, c] = collapse(uw[j][:, :W])
            w_scr[j, c] = collapse(uw[j][:, W:])
            att_scr[j, c] = (kq[j][W:] * decay[j]).astype(BF16)
        return carry

    lax.fori_loop(0, ts // C, solve_body, 0)

    def state_body(c, carry):
        r0 = pl.multiple_of(c * C, C)
        qc = [q_scr[i, pl.ds(r0, C), :] for i in seqs]
        kc = [k_scr[i, pl.ds(r0, C), :] for i in seqs]
        ge = [ge_scr[i, pl.ds(r0, C), :] for i in seqs]
        glast = [x[C - 1:C, :] for x in ge]
        s = [st_scr[i] for i in seqs]
        wq = [jnp.concatenate([w_scr[i, c], qc[i] * jnp.exp(ge[i])], axis=0) for i in seqs]
        sw = [_dot(wq[i].astype(BF16), s[i].astype(BF16)) for i in seqs]
        v_new = [u_scr[i, c] - sw[i][:C] for i in seqs]
        ov = [_dot(att_scr[i, c], stack_heads(v_new[i]).astype(BF16)) for i in seqs]
        upd = [_dot_tn((kc[i] * jnp.exp(glast[i] - ge[i])).astype(BF16), v_new[i].astype(BF16)) for i in seqs]
        for i in seqs:
            o_scr[i, pl.ds(r0, C), :] = sw[i][C:] + collapse(ov[i])
            st_scr[i] = s[i] * jnp.exp(glast[i]) + jnp.where(same_head, upd[i], 0.0)
        return carry

    lax.fori_loop(0, ts // C, state_body, 0)
    for i in seqs:
        o_ref[i] = _head_rms(o_scr[i], seg_ref, gain_ref[...]) * _silu(dz_ref[i])
    st_ref[...] = st_scr[...]


def _dn_esel():
    e = np.zeros((128, 512), np.float32)
    for h in range(D_HEADS):
        e[SM_A + h, h * 64:(h + 1) * 64] = 1.0
        e[SM_B + h, 256 + h * 64:256 + (h + 1) * 64] = 1.0
    return jnp.asarray(e).astype(BF16)


def _dn_prompt(z, cw, acoef, dtb, gain, B, S, ts):
    nt = S // ts
    nc = ts // DN_CHUNK
    W = D_HEADS * D_DK
    blk = lambda w, col: pl.BlockSpec((B, ts, w), lambda t: (0, t, col // w))
    return pl.pallas_call(
        functools.partial(_dn_prompt_kernel, ts=ts, nb=B),
        out_shape=(jax.ShapeDtypeStruct((B, S, 256), F32), jax.ShapeDtypeStruct((B, W, W), F32),
                   jax.ShapeDtypeStruct((B, CONV_W - 1, DN_QKV_W), F32)),
        grid=(nt,),
        in_specs=[blk(DN_QKV_W, COL_DQKV), blk(256, COL_DZ), blk(SMALL_W, COL_SMALL),
                  _full_spec((CONV_W, DN_QKV_W)), _full_spec((1, 128)), _full_spec((1, 128)), _full_spec((1, 256)),
                  _full_spec((256, 256)), _full_spec((128, 512))],
        out_specs=(pl.BlockSpec((B, ts, 256), lambda t: (0, t, 0)), _full_spec((B, W, W)),
                   _full_spec((B, CONV_W - 1, DN_QKV_W))),
        scratch_shapes=[pltpu.VMEM((B, 8 + ts, DN_QKV_W), F32), pltpu.VMEM((B, ts, 256), F32),
                        pltpu.VMEM((B, ts, 256), F32), pltpu.VMEM((B, ts, 256), F32), pltpu.VMEM((B, ts, 128), F32),
                        pltpu.VMEM((B, ts, W), F32), pltpu.VMEM((B, ts, W), F32),
                        pltpu.VMEM((B, nc, DN_CHUNK, W), F32), pltpu.VMEM((B, nc, DN_CHUNK, W), F32),
                        pltpu.VMEM((B, nc, W, W), BF16), pltpu.VMEM((B, ts, 256), F32), pltpu.VMEM((B, W, W), F32)],
        compiler_params=_cparams("arbitrary"),
        name="dn_prompt",
    )(z, z, z, cw, acoef, dtb, gain, _seg_matrix(256, 64), _dn_esel())


def _dec_prep_kernel(z_ref, h0r_ref, h0i_ref, conv_ref, wb_ref, cm_ref, abr_ref, abi_ref, d_ref, wg_ref, bg_ref,
                     wa_ref, ba_ref, cw_ref, acoef_ref, dtb_ref, seg_ref,
                     ob_ref, hr_ref, hi_ref, gq_ref, gk_ref, ga_ref, dq_ref, dk_ref, dv_ref, dgb_ref, cs_ref):
    z = z_ref[...]
    u = z[:, COL_U:COL_U + 256]
    bu = _dot(u, wb_ref[...], precision=HI)
    bur, bui = bu[:, :S5_STATE], bu[:, S5_STATE:]
    h0r, h0i = h0r_ref[...], h0i_ref[...]
    abr, abi = abr_ref[...], abi_ref[...]
    hr = abr * h0r - abi * h0i + bur
    hi = abr * h0i + abi * h0r + bui
    hr_ref[...] = hr
    hi_ref[...] = hi
    y = _dot(jnp.concatenate([hr, hi], axis=1).astype(BF16), cm_ref[...])
    ob_ref[...] = _s5_out(y, u, d_ref, wg_ref, bg_ref)
    small = z[:, COL_SMALL:COL_SMALL + SMALL_W]
    gq_ref[...] = z[:, COL_CQK:COL_CQK + 128] * (C_DK ** -0.5)
    gk_ref[...] = z[:, COL_CQK + 128:COL_CQK + 256]
    ga_ref[...] = _gla_log_a(small, wa_ref, ba_ref)
    x = z[:, COL_DQKV:COL_DQKV + DN_QKV_W]
    y = cw_ref[3:4, :] * x
    for i in range(CONV_W - 1):
        y = y + cw_ref[i:i + 1, :] * conv_ref[i]
    cs_ref[0] = conv_ref[1]
    cs_ref[1] = conv_ref[2]
    cs_ref[2] = x
    q, k, v = _dn_qkv(y, seg_ref)
    dq_ref[...] = q
    dk_ref[...] = k
    dv_ref[...] = v
    beta, g = _dn_gates(small, acoef_ref, dtb_ref)
    lane = lax.broadcasted_iota(jnp.int32, small.shape, 1)
    dgb_ref[...] = jnp.where(lane >= SM_A, jnp.exp(g), beta)


def _dec_prep(z, h0r, h0i, conv3, sp, d, wg, bg, wa, ba, cw, acoef, dtb):
    DB = z.shape[0]
    sds = lambda *s: jax.ShapeDtypeStruct(s, F32)
    out_shape = (sds(DB, 256), sds(DB, S5_STATE), sds(DB, S5_STATE), sds(DB, 128), sds(DB, 128), sds(DB, 128),
                 sds(DB, 256), sds(DB, 256), sds(DB, 256), sds(DB, 128), sds(CONV_W - 1, DB, DN_QKV_W))
    return pl.pallas_call(
        _dec_prep_kernel,
        out_shape=out_shape,
        compiler_params=pltpu.CompilerParams(vmem_limit_bytes=VMEM_LIMIT_BYTES),
        name="dec_prep",
    )(z, h0r, h0i, conv3, sp["wb"], sp["cm"].astype(BF16), sp["ab_re"], sp["ab_im"], d, wg, bg, wa, ba, cw, acoef, dtb,
      _seg_matrix(256, 64))


def _gla_step_kernel(s_ref, q_ref, k_ref, a_ref, v_ref, cg_ref, gain_ref, sn_ref, o_ref):
    v = v_ref[...]
    sn = s_ref[...] * jnp.exp(a_ref[...]) + k_ref[...] * v[:, None, :]
    sn_ref[...] = sn
    o = jnp.sum(q_ref[...] * sn, axis=1)
    o_ref[...] = _rms(o, gain_ref[...]) * _silu(cg_ref[...])


def _gla_step(s, qcol, kcol, acol, v, cg, gain, tb):
    n = s.shape[0]
    big = pl.BlockSpec((tb, C_DK, C_DV), lambda i: (i, 0, 0))
    row = pl.BlockSpec((tb, C_DV), lambda i: (i, 0))
    return pl.pallas_call(
        _gla_step_kernel,
        out_shape=(jax.ShapeDtypeStruct(s.shape, F32), jax.ShapeDtypeStruct((n, C_DV), F32)),
        grid=(n // tb,),
        in_specs=[big, big, big, big, row, row, _full_spec((1, C_DV))],
        out_specs=(big, row),
        compiler_params=_cparams("parallel"),
        name="gla_step",
    )(s, qcol, kcol, acol, v, cg, gain)


def _dn_step_kernel(s_ref, q_ref, k_ref, v_ref, eg_ref, beta_ref, dz_ref, gain_ref, sn_ref, o_ref):
    s = s_ref[...]
    kcol = k_ref[...]
    eg = eg_ref[...]
    ks = jnp.sum(kcol * s, axis=1)
    v_new = beta_ref[...] * (v_ref[...] - eg * ks)
    sn = s * eg[:, None, :] + kcol * v_new[:, None, :]
    sn_ref[...] = sn
    o = jnp.sum(q_ref[...] * sn, axis=1)
    o_ref[...] = _rms(o, gain_ref[...]) * _silu(dz_ref[...])


def _dn_step(s, qcol, kcol, v, eg, beta, dz, gain, tb):
    n = s.shape[0]
    big = pl.BlockSpec((tb, D_DK, D_DV), lambda i: (i, 0, 0))
    row = pl.BlockSpec((tb, D_DV), lambda i: (i, 0))
    return pl.pallas_call(
        _dn_step_kernel,
        out_shape=(jax.ShapeDtypeStruct(s.shape, F32), jax.ShapeDtypeStruct((n, D_DV), F32)),
        grid=(n // tb,),
        in_specs=[big, big, big, row, row, row, row, _full_spec((1, D_DV))],
        out_specs=(big, row),
        compiler_params=_cparams("parallel"),
        name="dn_step",
    )(s, qcol, kcol, v, eg, beta, dz, gain)


def _prep_w_in(w):
    o = np.cumsum([0, 256, 256, 256, 256, 128, 128, 256, 256, 16, 768, 4, 4, 256])
    q, k, v, u, cq, ck, cv, cg, lr, dqkv, db, da, dz = [w[:, o[i]:o[i + 1]] for i in range(13)]
    pad = jnp.zeros((w.shape[0], SMALL_W - 24), w.dtype)
    return jnp.concatenate([q, k, v, u, cq, ck, cv, dqkv, cg, dz, lr, db, da, pad], axis=1).astype(BF16)


def _lane_row(vals, offset):
    return jnp.zeros((1, 128), F32).at[0, offset:offset + vals.shape[0]].set(vals)


def kernel(x_prompt, x_sample, cache_k, cache_v, state_ssm_re, state_ssm_im, state_gla, state_delta, state_conv, page_table, p_prompt, p_sample, g_mix, w_in, lam_q1, lam_k1, lam_q2, lam_k2, attn_norm, s5_lam_re, s5_lam_im, s5_log_dt, s5_b_re, s5_b_im, s5_c_re, s5_c_im, s5_d, s5_w_glu, s5_b_glu, gla_w_a2, gla_b_a, gla_norm, dn_conv_w, dn_a_log, dn_dt_bias, dn_norm, w_branch_gate, b_branch_gate, w_branch, w_out, g_ffn, w_ff1, w_ff2, g_ple, w_ple_gate, w_ple_proj, g_final):
    B, S, _ = x_prompt.shape
    DB = x_sample.shape[0]
    T = B * S
    n_pool, page = cache_k.shape[1], cache_k.shape[2]
    hp = x_prompt.reshape(T, D_MODEL)
    hs = x_sample.reshape(DB, D_MODEL)
    row = lambda a: a.reshape(1, -1)
    ck_t = cache_k.transpose(0, 1, 3, 4, 2).reshape(DEPTH, n_pool, A_HEADS * 2 * A_DH, page)
    cv_t = cache_v.transpose(0, 1, 3, 4, 2).reshape(DEPTH, n_pool, A_HEADS * A_DV, page)
    st_p, st_s = [], []
    for l in range(DEPTH):
        lam_init = 0.8 - 0.6 * math.exp(-0.3 * l)
        final = l == DEPTH - 1
        w_in_l = _prep_w_in(w_in[l])
        wg, wb, wo = w_branch_gate[l].astype(BF16), w_branch[l].astype(BF16), w_out[l].astype(BF16)
        w1, w2 = w_ff1[l].astype(BF16), w_ff2[l].astype(BF16)
        wpg, wpp = w_ple_gate[l].astype(BF16), w_ple_proj[l].astype(BF16)
        lam_params = (row(lam_q1[l]), row(lam_k1[l]), row(lam_q2[l]), row(lam_k2[l]))
        sp = _s5_params(s5_lam_re[l], s5_lam_im[l], s5_log_dt[l], s5_b_re[l], s5_b_im[l], s5_c_re[l], s5_c_im[l])
        s5d, s5wg, s5bg = row(s5_d[l]), s5_w_glu[l].astype(BF16), row(s5_b_glu[l])
        wa = jnp.zeros((SMALL_W, 128), F32).at[SM_LR:SM_LR + C_RANK].set(gla_w_a2[l])
        ba = row(gla_b_a[l])
        gla_gain4 = row(jnp.tile(gla_norm[l], C_HEADS))
        dn_gain4 = row(jnp.tile(dn_norm[l], D_HEADS))
        acoef = _lane_row(-jnp.exp(dn_a_log[l]), SM_A)
        dtb = _lane_row(dn_dt_bias[l], SM_A)
        cw = dn_conv_w[l]

        zp, kt_p, vt_p = _inproj(hp, row(g_mix[l]), w_in_l, TM_PROJ, seq_len=S)
        oa = _attn_prompt(zp, lam_params, row(attn_norm[l]), B, S, TQ_ATTN, lam_init)
        ob, p_hre, p_him = _s5_prompt(zp, sp, s5d, s5wg, s5bg, B, S, TS_S5)
        oc, p_gla_t = _gla_prompt(zp.reshape(B, S, Z_W), wa, ba, gla_gain4, B, S, TS_SEQS)
        oc = oc.reshape(T, 256)
        od, p_dn_bd, p_conv = _dn_prompt(zp.reshape(B, S, Z_W), cw, acoef, dtb, dn_gain4, B, S, TS_SEQS)
        od = od.reshape(T, 256)
        p_dn = jnp.stack([p_dn_bd[:, h * D_DK:(h + 1) * D_DK, h * D_DV:(h + 1) * D_DV] for h in range(D_HEADS)], axis=1)
        hp = _merge(hp, oa, ob, oc, od, row(g_mix[l]), wg, b_branch_gate[l], wb, wo, TM_PROJ)
        hp = _ffn(hp, p_prompt[l].reshape(T, PLE_DIM), row(g_ffn[l]), w1, w2, row(g_ple[l]), wpg, wpp, row(g_final),
                  TM_FFN, final)
        p_gla = jnp.stack([p_gla_t[:, h * C_DV:(h + 1) * C_DV, h * C_DK:(h + 1) * C_DK] for h in range(C_HEADS)],
                          axis=1).swapaxes(-1, -2)
        st_p.append((kt_p.reshape(B, A_HEADS, 2 * A_DH, S).transpose(0, 3, 1, 2),
                     vt_p.reshape(B, A_HEADS, A_DV, S).transpose(0, 3, 1, 2),
                     p_hre.reshape(B, S5_GROUPS, S5_P), p_him.reshape(B, S5_GROUPS, S5_P), p_gla, p_dn, p_conv))

        zs, = _inproj(hs, row(g_mix[l]), w_in_l, DB)
        oa_s = _attn_decode(page_table, ck_t, cv_t, l, zs.reshape(DB, 1, Z_W), lam_params, row(attn_norm[l]),
                            lam_init).reshape(DB, 256)
        (ob_s, s_hre, s_him, gq, gk, ga, dq, dk, dv, dgb, s_conv3) = _dec_prep(
            zs, state_ssm_re[l].reshape(DB, S5_STATE), state_ssm_im[l].reshape(DB, S5_STATE),
            state_conv[l].swapaxes(0, 1), sp, s5d, s5wg, s5bg, wa, ba, cw, acoef, dtb)
        col = lambda a, h, dk, dv: jnp.broadcast_to(a.reshape(DB * h, dk, 1), (DB * h, dk, dv))
        s_gla, oc_s = _gla_step(state_gla[l].reshape(DB * C_HEADS, C_DK, C_DV), col(gq, C_HEADS, C_DK, C_DV),
                                col(gk, C_HEADS, C_DK, C_DV), col(ga, C_HEADS, C_DK, C_DV),
                                zs[:, COL_CV:COL_CV + 256].reshape(DB * C_HEADS, C_DV),
                                zs[:, COL_CG:COL_CG + 256].reshape(DB * C_HEADS, C_DV), row(gla_norm[l]), TB_GLA_STEP)
        lanes = lambda a, off: jnp.broadcast_to(a[:, off:off + D_HEADS].reshape(DB * D_HEADS, 1), (DB * D_HEADS, D_DV))
        s_dn, od_s = _dn_step(state_delta[l].reshape(DB * D_HEADS, D_DK, D_DV), col(dq, D_HEADS, D_DK, D_DV),
                              col(dk, D_HEADS, D_DK, D_DV), dv.reshape(DB * D_HEADS, D_DV), lanes(dgb, SM_A),
                              lanes(dgb, SM_B), zs[:, COL_DZ:COL_DZ + 256].reshape(DB * D_HEADS, D_DV),
                              row(dn_norm[l]), TB_DN_STEP)
        hs = _merge(hs, oa_s, ob_s, oc_s.reshape(DB, 256), od_s.reshape(DB, 256), row(g_mix[l]), wg, b_branch_gate[l],
                    wb, wo, DB)
        hs = _ffn(hs, p_sample[l].reshape(DB, PLE_DIM), row(g_ffn[l]), w1, w2, row(g_ple[l]), wpg, wpp, row(g_final),
                  DB, final)
        st_s.append((zs[:, COL_K:COL_K + 256].reshape(DB, 1, A_HEADS, 2 * A_DH),
                     zs[:, COL_V:COL_V + 256].reshape(DB, 1, A_HEADS, A_DV),
                     s_hre.reshape(DB, S5_GROUPS, S5_P), s_him.reshape(DB, S5_GROUPS, S5_P),
                     s_gla.reshape(DB, C_HEADS, C_DK, C_DV), s_dn.reshape(DB, D_HEADS, D_DK, D_DV),
                     s_conv3.swapaxes(0, 1)))

    stk = lambda states, i: jnp.stack([s[i] for s in states])
    return (hp.reshape(B, S, D_MODEL), hs.reshape(DB, 1, D_MODEL),
            stk(st_p, 0), stk(st_p, 1), stk(st_p, 2), stk(st_p, 3), stk(st_p, 4), stk(st_p, 5), stk(st_p, 6),
            stk(st_s, 0), stk(st_s, 1), stk(st_s, 2), stk(st_s, 3), stk(st_s, 4), stk(st_s, 5), stk(st_s, 6))
```

```python
import functools
import math

import numpy as np
import jax
import jax.numpy as jnp
from jax import lax
from jax.experimental import pallas as pl
from jax.experimental.pallas import tpu as pltpu

F32 = jnp.float32
BF16 = jnp.bfloat16
HI = lax.Precision.HIGHEST

D_MODEL = 1024
DEPTH = 2
A_HEADS, A_DH, A_DV = 4, 32, 64
S5_GROUPS, S5_GROUP, S5_P = 16, 16, 64
S5_STATE = S5_GROUPS * S5_P
C_HEADS, C_DK, C_DV, C_RANK = 4, 32, 64, 16
GLA_TAU = 16.0
D_HEADS, D_DK, D_DV = 4, 64, 64
CONV_W = 4
DN_QKV_W = 768
D_FF = 4096
PLE_DIM = 256
EPS = 1e-6
BRANCH_W = 256

Z_W = 2944
COL_Q, COL_K, COL_V, COL_U = 0, 256, 512, 768
COL_CQK, COL_CV, COL_DQKV, COL_CG, COL_DZ, COL_SMALL = 1024, 1280, 1536, 2304, 2560, 2816
SMALL_W = 128
SM_LR, SM_B, SM_A = 0, 16, 20

S5_SCAN = 64
S5_LEVELS = 6
S5_PAD = S5_SCAN // 2
GLA_SUB = 16
DN_CHUNK = 64
VMEM_LIMIT_BYTES = 56 * 1024 * 1024

TM_PROJ = 512
TM_FFN = 512
TQ_ATTN = 512
TS_S5 = 512
TS_SEQS = 256
TB_GLA_STEP = 128
TB_DN_STEP = 64
NEG_INF = float("-inf")


def _cparams(*sem):
    return pltpu.CompilerParams(dimension_semantics=sem, vmem_limit_bytes=VMEM_LIMIT_BYTES)


def _sigmoid(x):
    return 1.0 / (1.0 + jnp.exp(-x))


def _silu(x):
    return x * _sigmoid(x)


def _softplus(x):
    return jnp.maximum(x, 0.0) + jnp.log(1.0 + jnp.exp(-jnp.abs(x)))


def _gelu_tanh(x):
    return 0.5 * x * (1.0 + jnp.tanh(math.sqrt(2.0 / math.pi) * (x + 0.044715 * (x * x * x))))


def _rms(x, g):
    return x * lax.rsqrt(jnp.mean(x * x, axis=-1, keepdims=True) + EPS) * g


def _dot(a, b, **kw):
    return jnp.dot(a, b, preferred_element_type=F32, **kw)


def _dot_nt(a, b, **kw):
    return lax.dot_general(a, b, (((1,), (1,)), ((), ())), preferred_element_type=F32, **kw)


def _dot_tn(a, b, **kw):
    return lax.dot_general(a, b, (((0,), (0,)), ((), ())), preferred_element_type=F32, **kw)


def _full_spec(shape):
    nd = len(shape)
    return pl.BlockSpec(shape, lambda *_: (0,) * nd)


def _seg_matrix(width, seg):
    i = np.arange(width) // seg
    return jnp.asarray((i[:, None] == i[None, :]).astype(np.float32)).astype(BF16)


def _bf16_parts(x, parts):
    out = []
    for i in range(parts):
        p = x.astype(BF16)
        out.append(p)
        if i + 1 < parts:
            x = x - p.astype(F32)
    return out


def _dot01(x, e, parts):
    acc = None
    for p in _bf16_parts(x, parts):
        t = _dot(p, e)
        acc = t if acc is None else acc + t
    return acc


def _dot01_left(e, x, parts):
    acc = None
    for p in _bf16_parts(x, parts):
        t = _dot(e, p)
        acc = t if acc is None else acc + t
    return acc


def _head_rms(o, seg_ref, gain):
    ms = _dot01(o * o, seg_ref[...], 2) * (1.0 / 64.0)
    return o * lax.rsqrt(ms + EPS) * gain


def _inproj_kernel(x_ref, g_ref, w_ref, z_ref, *kv_refs):
    hn = _rms(x_ref[...], g_ref[...]).astype(BF16)
    z = _dot(hn, w_ref[...])
    z_ref[...] = z
    if kv_refs:
        kt_ref, vt_ref = kv_refs
        kt_ref[0] = z[:, COL_K:COL_K + 256].T
        vt_ref[0] = z[:, COL_V:COL_V + 256].T


def _inproj(x, g, w, tm, seq_len=None):
    T = x.shape[0]
    out_shape = [jax.ShapeDtypeStruct((T, Z_W), F32)]
    out_specs = [pl.BlockSpec((tm, Z_W), lambda i: (i, 0))]
    if seq_len is not None:
        nt = seq_len // tm
        out_shape += [jax.ShapeDtypeStruct((T // seq_len, 256, seq_len), F32)] * 2
        out_specs += [pl.BlockSpec((1, 256, tm), lambda i: (i // nt, 0, i % nt))] * 2
    return pl.pallas_call(
        _inproj_kernel,
        out_shape=tuple(out_shape),
        grid=(T // tm,),
        in_specs=[pl.BlockSpec((tm, D_MODEL), lambda i: (i, 0)), _full_spec((1, D_MODEL)), _full_spec((D_MODEL, Z_W))],
        out_specs=tuple(out_specs),
        compiler_params=_cparams("parallel"),
        name="inproj",
    )(x, g, w)


def _merge_kernel(h_ref, oa_ref, ob_ref, oc_ref, od_ref, g_ref, wg_ref, bg_ref, wb_ref, wo_ref, out_ref):
    h = h_ref[...]
    hn = _rms(h, g_ref[...]).astype(BF16)
    merged = None
    for n, br in enumerate((oa_ref, ob_ref, oc_ref, od_ref)):
        gate = _sigmoid(_dot(hn, wg_ref[n]) + bg_ref[n:n + 1, :])
        term = gate * _dot(br[...].astype(BF16), wb_ref[n])
        merged = term if merged is None else merged + term
    out_ref[...] = h + _dot(merged.astype(BF16), wo_ref[...])


def _merge(h, oa, ob, oc, od, g, wg, bg, wb, wo, tm):
    T = h.shape[0]
    row = lambda w: pl.BlockSpec((tm, w), lambda i: (i, 0))
    return pl.pallas_call(
        _merge_kernel,
        out_shape=jax.ShapeDtypeStruct((T, D_MODEL), F32),
        grid=(T // tm,),
        in_specs=[row(D_MODEL), row(BRANCH_W), row(BRANCH_W), row(BRANCH_W), row(BRANCH_W),
                  _full_spec((1, D_MODEL)), _full_spec((4, D_MODEL, D_MODEL)), _full_spec((4, D_MODEL)),
                  _full_spec((4, BRANCH_W, D_MODEL)), _full_spec((D_MODEL, D_MODEL))],
        out_specs=row(D_MODEL),
        compiler_params=_cparams("parallel"),
        name="merge",
    )(h, oa, ob, oc, od, g, wg, bg, wb, wo)


def _ffn_kernel(h_ref, p_ref, gf_ref, w1_ref, w2_ref, gp_ref, wpg_ref, wpp_ref, gfin_ref, out_ref, *, final):
    h = h_ref[...]
    hf = _rms(h, gf_ref[...]).astype(BF16)
    acc = h
    step = 1024
    for j in range(D_FF // step):
        a = jnp.maximum(_dot(hf, w1_ref[:, j * step:(j + 1) * step]), 0.0)
        acc = acc + _dot((a * a).astype(BF16), w2_ref[j * step:(j + 1) * step, :])
    pg = _sigmoid(_dot(_rms(acc, gp_ref[...]).astype(BF16), wpg_ref[...]))
    out = acc + pg * _dot(p_ref[...].astype(BF16), wpp_ref[...])
    if final:
        out = _rms(out, gfin_ref[...])
    out_ref[...] = out


def _ffn(h, p, gf, w1, w2, gp, wpg, wpp, gfin, tm, final):
    T = h.shape[0]
    row = lambda w: pl.BlockSpec((tm, w), lambda i: (i, 0))
    once = lambda shape: pl.BlockSpec(shape, lambda i: (0, 0), pipeline_mode=pl.Buffered(1))
    return pl.pallas_call(
        functools.partial(_ffn_kernel, final=final),
        out_shape=jax.ShapeDtypeStruct((T, D_MODEL), F32),
        grid=(T // tm,),
        in_specs=[row(D_MODEL), row(PLE_DIM), _full_spec((1, D_MODEL)), once((D_MODEL, D_FF)),
                  once((D_FF, D_MODEL)), _full_spec((1, D_MODEL)), once((D_MODEL, D_MODEL)),
                  once((PLE_DIM, D_MODEL)), _full_spec((1, D_MODEL))],
        out_specs=row(D_MODEL),
        compiler_params=_cparams("parallel"),
        name="ffn_ple",
    )(h, p, gf, w1, w2, gp, wpg, wpp, gfin)


_SLOPES = [float(s) for s in 2.0 ** (-8.0 * np.arange(1, A_HEADS + 1) / A_HEADS)]
LOG2E = 1.0 / math.log(2.0)


def _lam_value(lq1, lk1, lq2, lk2, lam_init):
    return (jnp.exp(jnp.sum(lq1[...] * lk1[...], axis=-1, keepdims=True))
            - jnp.exp(jnp.sum(lq2[...] * lk2[...], axis=-1, keepdims=True)) + lam_init)


def _attn_prompt_kernel(qi_ref, ki_ref, q_ref, k_ref, v_ref, lq1, lk1, lq2, lk2, gain_ref, o_ref,
                        qm_scr, m_scr, acc_scr, *, tq, lam_init):
    pair = pl.program_id(1)
    qi = qi_ref[pair]
    ki = ki_ref[pair]

    @pl.when(ki == 0)
    def _init():
        q = q_ref[...] * (A_DH ** -0.5 * LOG2E)
        lane = lax.broadcasted_iota(jnp.int32, q.shape, 1)
        for i in range(2 * A_HEADS):
            lo = (i // 2) * 2 * A_DH + (i % 2) * A_DH
            qm_scr[i] = jnp.where((lane >= lo) & (lane < lo + A_DH), q, 0.0).astype(BF16)
        m_scr[...] = jnp.full(m_scr.shape, NEG_INF, F32)
        acc_scr[...] = jnp.zeros(acc_scr.shape, F32)

    def _step(diag):
        k = k_ref[...].astype(BF16)
        vt = v_ref[...].T
        kpos = (lax.broadcasted_iota(jnp.int32, (tq, 1), 0) + (ki - qi) * tq).astype(F32)
        ones_row = jnp.where(lax.broadcasted_iota(jnp.int32, (A_DV, tq), 0) == 0, 1.0, 0.0)
        if diag:
            causal = (lax.broadcasted_iota(jnp.int32, (tq, tq), 0) <= lax.broadcasted_iota(jnp.int32, (tq, tq), 1))
        qq = jnp.concatenate([qm_scr[i] for i in range(2 * A_HEADS)], axis=0)
        s_all = _dot_nt(k, qq)
        for h in range(A_HEADS):
            vext = jnp.concatenate([vt[h * A_DV:(h + 1) * A_DV, :], ones_row], axis=0).astype(BF16)
            bias = kpos * (_SLOPES[h] * LOG2E)
            if diag:
                bias = jnp.where(causal, bias, NEG_INF)
                bias = jnp.concatenate([bias, bias], axis=1)
            i = 2 * h
            s = s_all[:, i * tq:(i + 2) * tq] + bias
            m_prev = jnp.concatenate([m_scr[i], m_scr[i + 1]], axis=1)
            m_new = jnp.maximum(m_prev, jnp.max(s, axis=0, keepdims=True))
            alpha = jnp.exp2(m_prev - m_new)
            p = jnp.exp2(s - m_new)
            pv = _dot(vext, p.astype(BF16))
            acc_scr[i] = alpha[:, :tq] * acc_scr[i] + pv[:, :tq]
            acc_scr[i + 1] = alpha[:, tq:] * acc_scr[i + 1] + pv[:, tq:]
            m_scr[i] = m_new[:, :tq]
            m_scr[i + 1] = m_new[:, tq:]

    @pl.when(ki < qi)
    def _off_diagonal():
        _step(False)

    @pl.when(ki == qi)
    def _diagonal():
        _step(True)
        lam = _lam_value(lq1, lk1, lq2, lk2, lam_init)
        outs = []
        for h in range(A_HEADS):
            a1, a2 = acc_scr[2 * h], acc_scr[2 * h + 1]
            o = a1[:A_DV, :] / a1[A_DV:A_DV + 1, :] - lam * (a2[:A_DV, :] / a2[A_DV:A_DV + 1, :])
            outs.append(o * lax.rsqrt(jnp.mean(o * o, axis=0, keepdims=True) + EPS))
        gain = jnp.concatenate([gain_ref[...]] * A_HEADS, axis=1)
        o_ref[...] = jnp.concatenate(outs, axis=0).T * gain * (1.0 - lam_init)


def _attn_prompt(z, lam_params, gain, B, S, tq, lam_init):
    nq = S // tq
    pairs = [(qi, ki) for qi in range(nq) for ki in range(qi + 1)]
    qi_arr = jnp.asarray([p[0] for p in pairs], jnp.int32)
    ki_arr = jnp.asarray([p[1] for p in pairs], jnp.int32)
    vec = pl.BlockSpec((1, A_DH), lambda b, p, qa, ka: (0, 0))
    grid_spec = pltpu.PrefetchScalarGridSpec(
        num_scalar_prefetch=2,
        grid=(B, len(pairs)),
        in_specs=[pl.BlockSpec((tq, 256), lambda b, p, qa, ka: (b * nq + qa[p], COL_Q // 256)),
                  pl.BlockSpec((tq, 256), lambda b, p, qa, ka: (b * nq + ka[p], COL_K // 256)),
                  pl.BlockSpec((tq, 256), lambda b, p, qa, ka: (b * nq + ka[p], COL_V // 256)),
                  vec, vec, vec, vec, pl.BlockSpec((1, A_DV), lambda b, p, qa, ka: (0, 0))],
        out_specs=pl.BlockSpec((tq, 256), lambda b, p, qa, ka: (b * nq + qa[p], 0)),
        scratch_shapes=[pltpu.VMEM((2 * A_HEADS, tq, 256), BF16), pltpu.VMEM((2 * A_HEADS, 1, tq), F32),
                        pltpu.VMEM((2 * A_HEADS, 128, tq), F32)],
    )
    return pl.pallas_call(
        functools.partial(_attn_prompt_kernel, tq=tq, lam_init=lam_init),
        out_shape=jax.ShapeDtypeStruct((B * S, A_HEADS * A_DV), F32),
        grid_spec=grid_spec,
        compiler_params=_cparams("parallel", "arbitrary"),
        name="attn_prompt",
    )(qi_arr, ki_arr, z, z, z, *lam_params, gain)


def _attn_decode_kernel(pt_ref, ck_ref, cv_ref, z_ref, lq1, lk1, lq2, lk2, gain_ref, o_ref, kbuf, vbuf, sem,
                        *, layer, n_pages, page, lam_init):
    b = pl.program_id(0)
    slot = b % 2

    def page_copies(seq, into):
        cps = []
        for j in range(n_pages):
            pid = pt_ref[seq, j]
            cps.append(pltpu.make_async_copy(ck_ref.at[layer, pid], kbuf.at[into, j], sem.at[into, 0]))
            cps.append(pltpu.make_async_copy(cv_ref.at[layer, pid], vbuf.at[into, j], sem.at[into, 1]))
        return cps

    @pl.when(b == 0)
    def _first():
        for cp in page_copies(0, 0):
            cp.start()

    @pl.when(b + 1 < pl.num_programs(0))
    def _next():
        for cp in page_copies(b + 1, 1 - slot):
            cp.start()

    for cp in page_copies(b, slot):
        cp.wait()
    past = n_pages * page
    zrow = z_ref[0]
    q = zrow[:, COL_Q:COL_Q + 256] * (A_DH ** -0.5)
    k_own = zrow[:, COL_K:COL_K + 256]
    v_own = zrow[:, COL_V:COL_V + 256]
    lane = lax.broadcasted_iota(jnp.int32, (2 * A_HEADS, 256), 1)
    r = lax.broadcasted_iota(jnp.int32, (2 * A_HEADS, 256), 0)
    lo = (r % A_HEADS) * 2 * A_DH + (r // A_HEADS) * A_DH
    qm = jnp.where((lane >= lo) & (lane < lo + A_DH), jnp.broadcast_to(q, (2 * A_HEADS, 256)), 0.0)
    qmb = qm.astype(BF16)
    kcat = jnp.concatenate([kbuf[slot, j].astype(BF16) for j in range(n_pages)], axis=1)
    s = _dot(qmb, kcat)
    rr = lax.broadcasted_iota(jnp.int32, (2 * A_HEADS, 1), 0) % A_HEADS
    slope = jnp.zeros((2 * A_HEADS, 1), F32)
    for h in range(A_HEADS):
        slope = jnp.where(rr == h, _SLOPES[h], slope)
    kpos = lax.broadcasted_iota(jnp.int32, (2 * A_HEADS, past), 1).astype(F32)
    s = s - slope * (float(past) - kpos)
    s_own = jnp.sum(qm * k_own, axis=-1, keepdims=True)
    m = jnp.maximum(jnp.max(s, axis=-1, keepdims=True), s_own)
    p = jnp.exp(s - m)
    p_own = jnp.exp(s_own - m)
    denom = jnp.sum(p, axis=-1, keepdims=True) + p_own
    pb = p.astype(BF16)
    vcat = jnp.concatenate([vbuf[slot, j].astype(BF16) for j in range(n_pages)], axis=1)
    o = (p_own * v_own + _dot_nt(pb, vcat)) / denom
    lam = _lam_value(lq1, lk1, lq2, lk2, lam_init)
    d = o[:A_HEADS] - lam * o[A_HEADS:]
    hl = lax.broadcasted_iota(jnp.int32, (A_HEADS, 256), 1) // A_DV
    hr = lax.broadcasted_iota(jnp.int32, (A_HEADS, 256), 0)
    own = hl == hr
    d = jnp.where(own, d, 0.0)
    ms = jnp.sum(d * d, axis=-1, keepdims=True) * (1.0 / A_DV)
    d = d * lax.rsqrt(ms + EPS)
    gain = jnp.concatenate([gain_ref[...]] * A_HEADS, axis=1)
    o_ref[0] = jnp.sum(d, axis=0, keepdims=True) * gain * (1.0 - lam_init)


def _attn_decode(page_table, ck, cv, layer, z3, lam_params, gain, lam_init):
    DB, n_pages = page_table.shape
    page = ck.shape[-1]
    hbm = pl.BlockSpec(memory_space=pl.ANY)
    vec = pl.BlockSpec((1, A_DH), lambda b, pt: (0, 0))
    grid_spec = pltpu.PrefetchScalarGridSpec(
        num_scalar_prefetch=1,
        grid=(DB,),
        in_specs=[hbm, hbm, pl.BlockSpec((1, 1, Z_W), lambda b, pt: (b, 0, 0)), vec, vec, vec, vec,
                  pl.BlockSpec((1, A_DV), lambda b, pt: (0, 0))],
        out_specs=pl.BlockSpec((1, 1, 256), lambda b, pt: (b, 0, 0)),
        scratch_shapes=[pltpu.VMEM((2, n_pages, 256, page), F32), pltpu.VMEM((2, n_pages, 256, page), F32),
                        pltpu.SemaphoreType.DMA((2, 2))],
    )
    return pl.pallas_call(
        functools.partial(_attn_decode_kernel, layer=layer, n_pages=n_pages, page=page, lam_init=lam_init),
        out_shape=jax.ShapeDtypeStruct((DB, 1, 256), F32),
        grid_spec=grid_spec,
        compiler_params=_cparams("arbitrary"),
        name="attn_decode",
    )(page_table, ck, cv, z3, *lam_params, gain)


def _s5_params(lam_re, lam_im, log_dt, b_re, b_im, c_re, c_im):
    dt = jnp.exp(log_dt)[:, None]
    lr, li = lam_re, lam_im
    a, th = lr * dt, li * dt
    mag = jnp.exp(a)
    ab_re, ab_im = mag * jnp.cos(th), mag * jnp.sin(th)
    den = lr * lr + li * li
    nr, ni = ab_re - 1.0, ab_im
    f_re = (nr * lr + ni * li) / den
    f_im = (ni * lr - nr * li) / den
    bb_re = f_re[..., None] * b_re - f_im[..., None] * b_im
    bb_im = f_re[..., None] * b_im + f_im[..., None] * b_re
    eye = jnp.eye(S5_GROUPS, dtype=F32)
    expand_b = lambda m: jnp.einsum("gpn,gh->gnhp", m, eye).reshape(S5_GROUPS * S5_GROUP, S5_STATE)
    wb = jnp.concatenate([expand_b(bb_re), expand_b(bb_im)], axis=1)
    expand_c = lambda m: jnp.einsum("gnp,gh->gphn", m, eye).reshape(S5_STATE, S5_GROUPS * S5_GROUP)
    cm = jnp.concatenate([expand_c(c_re), -expand_c(c_im)], axis=0)
    pr, pi = ab_re.reshape(1, S5_STATE), ab_im.reshape(1, S5_STATE)
    res, ims = [pr], [pi]
    for _ in range(S5_LEVELS - 1):
        pr, pi = pr * pr - pi * pi, 2.0 * pr * pi
        res.append(pr)
        ims.append(pi)
    return dict(wb=wb, cm=cm, ab_re=res[0], ab_im=ims[0], pow_re=jnp.concatenate(res, axis=0),
                pow_im=jnp.concatenate(ims, axis=0))


def _s5_out(y, u, d_ref, wg_ref, bg_ref):
    y = _gelu_tanh(y + d_ref[...] * u)
    return y * _sigmoid(_dot(y.astype(BF16), wg_ref[...]) + bg_ref[...])


def _s5_prompt_kernel(u_ref, wb_ref, cm_ref, pr_ref, pi_ref, d_ref, wg_ref, bg_ref, o_ref, hre_ref, him_ref,
                      bu_scr, hb_scr, sr_scr, si_scr, hr_scr, hi_scr, *, ts):
    t = pl.program_id(1)
    n = S5_SCAN
    P = S5_STATE
    pad = S5_PAD

    @pl.when(t == 0)
    def _init():
        hr_scr[...] = jnp.zeros(hr_scr.shape, F32)
        hi_scr[...] = jnp.zeros(hi_scr.shape, F32)
        sr_scr[0:pad, :] = jnp.zeros((pad, P), F32)
        si_scr[0:pad, :] = jnp.zeros((pad, P), F32)

    u = u_ref[...]
    bu_scr[...] = _dot(u.astype(BF16), wb_ref[...])

    def chunk(c, carry):
        r0 = pl.multiple_of(c * n, n)
        sr_scr[pad:pad + n, :] = bu_scr[pl.ds(r0, n), 0:P]
        si_scr[pad:pad + n, :] = bu_scr[pl.ds(r0, n), P:2 * P]
        ar, ai = pr_ref[0:1, :], pi_ref[0:1, :]
        h0r, h0i = hr_scr[...], hi_scr[...]
        sr_scr[pad:pad + 1, :] = sr_scr[pad:pad + 1, :] + (ar * h0r - ai * h0i)
        si_scr[pad:pad + 1, :] = si_scr[pad:pad + 1, :] + (ar * h0i + ai * h0r)
        for lv in range(S5_LEVELS):
            d = 1 << lv
            ar, ai = pr_ref[lv:lv + 1, :], pi_ref[lv:lv + 1, :]
            cr, ci = sr_scr[pad:pad + n, :], si_scr[pad:pad + n, :]
            qr, qi = sr_scr[pad - d:pad - d + n, :], si_scr[pad - d:pad - d + n, :]
            nr, ni = cr + (ar * qr - ai * qi), ci + (ar * qi + ai * qr)
            if lv + 1 < S5_LEVELS:
                sr_scr[pad:pad + n, :] = nr
                si_scr[pad:pad + n, :] = ni
        hr_scr[...] = nr[n - 1:n, :]
        hi_scr[...] = ni[n - 1:n, :]
        hb_scr[pl.ds(r0, n), 0:P] = nr.astype(BF16)
        hb_scr[pl.ds(r0, n), P:2 * P] = ni.astype(BF16)
        return carry

    lax.fori_loop(0, ts // n, chunk, 0)
    y = _dot(hb_scr[...], cm_ref[...])
    o_ref[...] = _s5_out(y, u, d_ref, wg_ref, bg_ref)
    hre_ref[0] = hr_scr[...]
    him_ref[0] = hi_scr[...]


def _s5_prompt(z, sp, d, wg, bg, B, S, ts):
    nt = S // ts
    tab = _full_spec((S5_LEVELS, S5_STATE))
    st = pl.BlockSpec((1, 1, S5_STATE), lambda b, t: (b, 0, 0))
    return pl.pallas_call(
        functools.partial(_s5_prompt_kernel, ts=ts),
        out_shape=(jax.ShapeDtypeStruct((B * S, 256), F32), jax.ShapeDtypeStruct((B, 1, S5_STATE), F32),
                   jax.ShapeDtypeStruct((B, 1, S5_STATE), F32)),
        grid=(B, nt),
        in_specs=[pl.BlockSpec((ts, 256), lambda b, t: (b * nt + t, COL_U // 256)),
                  _full_spec((256, 2 * S5_STATE)), _full_spec((2 * S5_STATE, 256)), tab, tab,
                  _full_spec((1, 256)), _full_spec((256, 256)), _full_spec((1, 256))],
        out_specs=(pl.BlockSpec((ts, 256), lambda b, t: (b * nt + t, 0)), st, st),
        scratch_shapes=[pltpu.VMEM((ts, 2 * S5_STATE), F32), pltpu.VMEM((ts, 2 * S5_STATE), BF16),
                        pltpu.VMEM((S5_PAD + S5_SCAN, S5_STATE), F32), pltpu.VMEM((S5_PAD + S5_SCAN, S5_STATE), F32),
                        pltpu.VMEM((1, S5_STATE), F32), pltpu.VMEM((1, S5_STATE), F32)],
        compiler_params=_cparams("parallel", "arbitrary"),
        name="s5_prompt",
    )(z, sp["wb"].astype(BF16), sp["cm"].astype(BF16), sp["pow_re"], sp["pow_im"], d, wg, bg)


def _gla_log_a(small, wa_ref, ba_ref):
    x = _dot(small, wa_ref[...], precision=HI) + ba_ref[...]
    return (jnp.minimum(x, 0.0) - jnp.log(1.0 + jnp.exp(-jnp.abs(x)))) * (1.0 / GLA_TAU)


def _gla_prompt_kernel(qk_ref, v_ref, cg_ref, sm_ref, wa_ref, ba_ref, gain_ref, segx_ref, seg_ref, bmask_ref,
                       o_ref, st_ref, b_scr, o_scr, st_scr, *, ts, nb):
    t = pl.program_id(0)
    n = GLA_SUB
    seqs = range(nb)

    @pl.when(t == 0)
    def _init():
        st_scr[...] = jnp.zeros(st_scr.shape, F32)

    ri = lax.broadcasted_iota(jnp.int32, (ts, ts), 0)
    ci = lax.broadcasted_iota(jnp.int32, (ts, ts), 1)
    tri = jnp.where((ri >= ci) & (ri // n == ci // n), 1.0, 0.0).astype(BF16)
    for i in seqs:
        log_a = _gla_log_a(sm_ref[i], wa_ref, ba_ref)
        b_scr[i] = _dot01_left(tri, log_a, 3)
    rows = lax.broadcasted_iota(jnp.int32, (n, 128), 0)
    segx = segx_ref[...]
    bmask = bmask_ref[...]

    def body(c, carry):
        r0 = pl.multiple_of(c * n, n)
        qk = [qk_ref[i, pl.ds(r0, n), :] for i in seqs]
        q = [x[:, :128] * (C_DK ** -0.5) for x in qk]
        k = [x[:, 128:] for x in qk]
        v = [v_ref[i, pl.ds(r0, n), :] for i in seqs]
        b = [b_scr[i, pl.ds(r0, n), :] for i in seqs]
        pall = []
        for i in seqs:
            parts = []
            for s in range(n):
                e = jnp.exp(jnp.where(rows >= s, b[i] - b[i][s:s + 1, :], NEG_INF))
                parts.append(q[i] * k[i][s:s + 1, :] * e)
            pall.append(jnp.concatenate(parts, axis=0).astype(BF16))
        aexp = [_dot(pall[i], segx) for i in seqs]
        st = [st_scr[i] for i in seqs]
        inter = [_dot_nt((q[i] * jnp.exp(b[i])).astype(BF16), st[i].astype(BF16)) for i in seqs]
        b_last = [b[i][n - 1:n, :] for i in seqs]
        upd = [_dot_tn(v[i].astype(BF16), (k[i] * jnp.exp(b_last[i] - b[i])).astype(BF16)) for i in seqs]
        for i in seqs:
            acc = inter[i]
            for s in range(n):
                acc = acc + aexp[i][s * n:(s + 1) * n, :] * v[i][s:s + 1, :]
            o_scr[i, pl.ds(r0, n), :] = acc
            st_scr[i] = st[i] * jnp.exp(b_last[i]) + upd[i] * bmask
        return carry

    lax.fori_loop(0, ts // n, body, 0)
    for i in seqs:
        o_ref[i] = _head_rms(o_scr[i], seg_ref, gain_ref[...]) * _silu(cg_ref[i])
    st_ref[...] = st_scr[...]


def _gla_consts():
    hk = np.arange(128) // C_DK
    hv = np.arange(256) // C_DV
    segx = jnp.asarray((hk[:, None] == hv[None, :]).astype(np.float32)).astype(BF16)
    bmask = jnp.asarray((hv[:, None] == hk[None, :]).astype(np.float32))
    return segx, bmask


def _gla_prompt(z, wa, ba, gain, B, S, ts):
    nt = S // ts
    segx, bmask = _gla_consts()
    blk = lambda w, col: pl.BlockSpec((B, ts, w), lambda t: (0, t, col // w))
    return pl.pallas_call(
        functools.partial(_gla_prompt_kernel, ts=ts, nb=B),
        out_shape=(jax.ShapeDtypeStruct((B, S, 256), F32), jax.ShapeDtypeStruct((B, 256, 128), F32)),
        grid=(nt,),
        in_specs=[blk(256, COL_CQK), blk(256, COL_CV), blk(256, COL_CG), blk(SMALL_W, COL_SMALL),
                  _full_spec((SMALL_W, 128)), _full_spec((1, 128)), _full_spec((1, 256)),
                  _full_spec((128, 256)), _full_spec((256, 256)), _full_spec((256, 128))],
        out_specs=(pl.BlockSpec((B, ts, 256), lambda t: (0, t, 0)), _full_spec((B, 256, 128))),
        scratch_shapes=[pltpu.VMEM((B, ts, 128), F32), pltpu.VMEM((B, ts, 256), F32), pltpu.VMEM((B, 256, 128), F32)],
        compiler_params=_cparams("arbitrary"),
        name="gla_prompt",
    )(z, z, z, z, wa, ba, gain, segx, _seg_matrix(256, 64), bmask)


def _dn_gates(small, acoef_ref, dtb_ref):
    beta = _sigmoid(small)
    g = acoef_ref[...] * _softplus(small + dtb_ref[...])
    return beta, g


def _dn_qkv(y, seg_ref):
    y = _silu(y)
    q, k, v = y[:, :256], y[:, 256:512], y[:, 512:768]
    nq = _dot01(q * q, seg_ref[...], 2)
    nk = _dot01(k * k, seg_ref[...], 2)
    q = q * lax.rsqrt(nq + EPS) * (D_DK ** -0.5)
    k = k * lax.rsqrt(nk + EPS)
    return q, k, v


def _dn_prompt_kernel(x_ref, dz_ref, sm_ref, cw_ref, acoef_ref, dtb_ref, gain_ref, seg_ref, esel_ref,
                      o_ref, st_ref, cs_ref, xp_scr, q_scr, k_scr, v_scr, gb_scr, ge_scr, be_scr,
                      u_scr, w_scr, att_scr, o_scr, st_scr, *, ts, nb):
    t = pl.program_id(0)
    C = DN_CHUNK
    PADR = 8
    W = D_HEADS * D_DK
    seqs = range(nb)

    @pl.when(t == 0)
    def _init():
        st_scr[...] = jnp.zeros(st_scr.shape, F32)
        for i in seqs:
            xp_scr[i, 0:PADR, :] = jnp.zeros((PADR, DN_QKV_W), F32)

    ri = lax.broadcasted_iota(jnp.int32, (ts, ts), 0)
    ci = lax.broadcasted_iota(jnp.int32, (ts, ts), 1)
    tri = jnp.where((ri >= ci) & (ri // C == ci // C), 1.0, 0.0).astype(BF16)
    lane = lax.broadcasted_iota(jnp.int32, (ts, 128), 1)
    for i in seqs:
        x = x_ref[i]
        xp_scr[i, PADR:PADR + ts, :] = x
        y = cw_ref[3:4, :] * x
        for j in range(CONV_W - 1):
            y = y + cw_ref[j:j + 1, :] * xp_scr[i, PADR - 3 + j:PADR - 3 + j + ts, :]
        xp_scr[i, PADR - 3:PADR, :] = x[ts - 3:ts, :]
        cs_ref[i] = x[ts - 3:ts, :]
        q, k, v = _dn_qkv(y, seg_ref)
        q_scr[i] = q
        k_scr[i] = k
        v_scr[i] = v
        beta, g = _dn_gates(sm_ref[i], acoef_ref, dtb_ref)
        gcum = _dot01_left(tri, g, 3)
        gb = jnp.where(lane >= SM_A, gcum, beta)
        gb_scr[i] = gb
        ex = _dot01(gb, esel_ref[...], 3)
        ge_scr[i] = ex[:, :W]
        be_scr[i] = ex[:, W:]

    rr = lax.broadcasted_iota(jnp.int32, (W, W), 0)
    cc = lax.broadcasted_iota(jnp.int32, (W, W), 1)
    same_head = (rr // C) == (cc // C)
    lower_incl = same_head & (rr >= cc)
    lower_strict = same_head & (rr > cc)
    eye = jnp.where(rr == cc, 1.0, 0.0)
    blocks = [(rr // w) == (cc // w) for w in (8, 16, 32, C)]

    def stack_heads(a):
        return jnp.where(same_head, jnp.concatenate([a] * D_HEADS, axis=0), 0.0)

    def collapse(a):
        return a[0:C] + a[C:2 * C] + a[2 * C:3 * C] + a[3 * C:4 * C]

    def solve_body(c, carry):
        r0 = pl.multiple_of(c * C, C)
        qc = [q_scr[i, pl.ds(r0, C), :] for i in seqs]
        kc = [k_scr[i, pl.ds(r0, C), :] for i in seqs]
        vc = [v_scr[i, pl.ds(r0, C), :] for i in seqs]
        gbc = [gb_scr[i, pl.ds(r0, C), :] for i in seqs]
        ge = [ge_scr[i, pl.ds(r0, C), :] for i in seqs]
        be = [be_scr[i, pl.ds(r0, C), :] for i in seqs]
        decay, kst = [], []
        for j in seqs:
            gbt = gbc[j].T
            gcol = jnp.concatenate([gbc[j][:, SM_A + h:SM_A + h + 1] for h in range(D_HEADS)], axis=0)
            grow = jnp.concatenate([gbt[SM_A + h:SM_A + h + 1, :] for h in range(D_HEADS)], axis=1)
            decay.append(jnp.exp(jnp.where(lower_incl, gcol - grow, NEG_INF)))
            kst.append(stack_heads(kc[j]).astype(BF16))
        kq = [_dot_nt(jnp.concatenate([stack_heads(kc[j] * be[j]), stack_heads(qc[j])], axis=0).astype(BF16), kst[j])
              for j in seqs]
        m = [jnp.where(lower_strict, -(kq[j][:W] * decay[j]), 0.0) for j in seqs]
        m1 = [jnp.where(blocks[0], m[j], 0.0) for j in seqs]
        m1b = [x.astype(BF16) for x in m1]
        m2 = [_dot(m1b[j], m1b[j]) for j in seqs]
        s1 = [eye + m1[j] for j in seqs]
        r2 = [_dot(m2[j].astype(BF16), jnp.concatenate([m2[j], s1[j]], axis=1).astype(BF16)) for j in seqs]
        s3 = [s1[j] + r2[j][:, W:] for j in seqs]
        tm = [s3[j] + _dot(r2[j][:, :W].astype(BF16), s3[j].astype(BF16)) for j in seqs]
        for inner, outer in zip(blocks[:-1], blocks[1:]):
            tmb = [x.astype(BF16) for x in tm]
            off = [jnp.where(outer & jnp.logical_not(inner), m[j], 0.0).astype(BF16) for j in seqs]
            t1 = [_dot(tmb[j], off[j]).astype(BF16) for j in seqs]
            tm = [tm[j] + _dot(t1[j], tmb[j]) for j in seqs]
        rhs = [jnp.concatenate([stack_heads(vc[j] * be[j]), stack_heads(kc[j] * (be[j] * jnp.exp(ge[j])))], axis=1)
               for j in seqs]
        uw = [_dot(tm[j].astype(BF16), rhs[j].astype(BF16)) for j in seqs]
        for j in seqs:
            u_scr[j, c] = collapse(uw[j][:, :W])
            w_scr[j, c] = collapse(uw[j][:, W:])
            att_scr[j, c] = (kq[j][W:] * decay[j]).astype(BF16)
        return carry

    lax.fori_loop(0, ts // C, solve_body, 0)

    def state_body(c, carry):
        r0 = pl.multiple_of(c * C, C)
        qc = [q_scr[i, pl.ds(r0, C), :] for i in seqs]
        kc = [k_scr[i, pl.ds(r0, C), :] for i in seqs]
        ge = [ge_scr[i, pl.ds(r0, C), :] for i in seqs]
        glast = [x[C - 1:C, :] for x in ge]
        s = [st_scr[i] for i in seqs]
        wq = [jnp.concatenate([w_scr[i, c], qc[i] * jnp.exp(ge[i])], axis=0) for i in seqs]
        sw = [_dot(wq[i].astype(BF16), s[i].astype(BF16)) for i in seqs]
        v_new = [u_scr[i, c] - sw[i][:C] for i in seqs]
        ov = [_dot(att_scr[i, c], stack_heads(v_new[i]).astype(BF16)) for i in seqs]
        upd = [_dot_tn((kc[i] * jnp.exp(glast[i] - ge[i])).astype(BF16), v_new[i].astype(BF16)) for i in seqs]
        for i in seqs:
            o_scr[i, pl.ds(r0, C), :] = sw[i][C:] + collapse(ov[i])
            st_scr[i] = s[i] * jnp.exp(glast[i]) + jnp.where(same_head, upd[i], 0.0)
        return carry

    lax.fori_loop(0, ts // C, state_body, 0)
    for i in seqs:
        o_ref[i] = _head_rms(o_scr[i], seg_ref, gain_ref[...]) * _silu(dz_ref[i])
    st_ref[...] = st_scr[...]


def _dn_esel():
    e = np.zeros((128, 512), np.float32)
    for h in range(D_HEADS):
        e[SM_A + h, h * 64:(h + 1) * 64] = 1.0
        e[SM_B + h, 256 + h * 64:256 + (h + 1) * 64] = 1.0
    return jnp.asarray(e).astype(BF16)


def _dn_prompt(z, cw, acoef, dtb, gain, B, S, ts):
    nt = S // ts
    nc = ts // DN_CHUNK
    W = D_HEADS * D_DK
    blk = lambda w, col: pl.BlockSpec((B, ts, w), lambda t: (0, t, col // w))
    return pl.pallas_call(
        functools.partial(_dn_prompt_kernel, ts=ts, nb=B),
        out_shape=(jax.ShapeDtypeStruct((B, S, 256), F32), jax.ShapeDtypeStruct((B, W, W), F32),
                   jax.ShapeDtypeStruct((B, CONV_W - 1, DN_QKV_W), F32)),
        grid=(nt,),
        in_specs=[blk(DN_QKV_W, COL_DQKV), blk(256, COL_DZ), blk(SMALL_W, COL_SMALL),
                  _full_spec((CONV_W, DN_QKV_W)), _full_spec((1, 128)), _full_spec((1, 128)), _full_spec((1, 256)),
                  _full_spec((256, 256)), _full_spec((128, 512))],
        out_specs=(pl.BlockSpec((B, ts, 256), lambda t: (0, t, 0)), _full_spec((B, W, W)),
                   _full_spec((B, CONV_W - 1, DN_QKV_W))),
        scratch_shapes=[pltpu.VMEM((B, 8 + ts, DN_QKV_W), F32), pltpu.VMEM((B, ts, 256), F32),
                        pltpu.VMEM((B, ts, 256), F32), pltpu.VMEM((B, ts, 256), F32), pltpu.VMEM((B, ts, 128), F32),
                        pltpu.VMEM((B, ts, W), F32), pltpu.VMEM((B, ts, W), F32),
                        pltpu.VMEM((B, nc, DN_CHUNK, W), F32), pltpu.VMEM((B, nc, DN_CHUNK, W), F32),
                        pltpu.VMEM((B, nc, W, W), BF16), pltpu.VMEM((B, ts, 256), F32), pltpu.VMEM((B, W, W), F32)],
        compiler_params=_cparams("arbitrary"),
        name="dn_prompt",
    )(z, z, z, cw, acoef, dtb, gain, _seg_matrix(256, 64), _dn_esel())


def _dec_prep_kernel(z_ref, h0r_ref, h0i_ref, conv_ref, wb_ref, cm_ref, abr_ref, abi_ref, d_ref, wg_ref, bg_ref,
                     wa_ref, ba_ref, cw_ref, acoef_ref, dtb_ref, seg_ref,
                     ob_ref, hr_ref, hi_ref, gq_ref, gk_ref, ga_ref, dq_ref, dk_ref, dv_ref, dgb_ref, cs_ref):
    z = z_ref[...]
    u = z[:, COL_U:COL_U + 256]
    bu = _dot(u, wb_ref[...], precision=HI)
    bur, bui = bu[:, :S5_STATE], bu[:, S5_STATE:]
    h0r, h0i = h0r_ref[...], h0i_ref[...]
    abr, abi = abr_ref[...], abi_ref[...]
    hr = abr * h0r - abi * h0i + bur
    hi = abr * h0i + abi * h0r + bui
    hr_ref[...] = hr
    hi_ref[...] = hi
    y = _dot(jnp.concatenate([hr, hi], axis=1).astype(BF16), cm_ref[...])
    ob_ref[...] = _s5_out(y, u, d_ref, wg_ref, bg_ref)
    small = z[:, COL_SMALL:COL_SMALL + SMALL_W]
    gq_ref[...] = z[:, COL_CQK:COL_CQK + 128] * (C_DK ** -0.5)
    gk_ref[...] = z[:, COL_CQK + 128:COL_CQK + 256]
    ga_ref[...] = _gla_log_a(small, wa_ref, ba_ref)
    x = z[:, COL_DQKV:COL_DQKV + DN_QKV_W]
    y = cw_ref[3:4, :] * x
    for i in range(CONV_W - 1):
        y = y + cw_ref[i:i + 1, :] * conv_ref[i]
    cs_ref[0] = conv_ref[1]
    cs_ref[1] = conv_ref[2]
    cs_ref[2] = x
    q, k, v = _dn_qkv(y, seg_ref)
    dq_ref[...] = q
    dk_ref[...] = k
    dv_ref[...] = v
    beta, g = _dn_gates(small, acoef_ref, dtb_ref)
    lane = lax.broadcasted_iota(jnp.int32, small.shape, 1)
    dgb_ref[...] = jnp.where(lane >= SM_A, jnp.exp(g), beta)


def _dec_prep(z, h0r, h0i, conv3, sp, d, wg, bg, wa, ba, cw, acoef, dtb):
    DB = z.shape[0]
    sds = lambda *s: jax.ShapeDtypeStruct(s, F32)
    out_shape = (sds(DB, 256), sds(DB, S5_STATE), sds(DB, S5_STATE), sds(DB, 128), sds(DB, 128), sds(DB, 128),
                 sds(DB, 256), sds(DB, 256), sds(DB, 256), sds(DB, 128), sds(CONV_W - 1, DB, DN_QKV_W))
    return pl.pallas_call(
        _dec_prep_kernel,
        out_shape=out_shape,
        compiler_params=pltpu.CompilerParams(vmem_limit_bytes=VMEM_LIMIT_BYTES),
        name="dec_prep",
    )(z, h0r, h0i, conv3, sp["wb"], sp["cm"].astype(BF16), sp["ab_re"], sp["ab_im"], d, wg, bg, wa, ba, cw, acoef, dtb,
      _seg_matrix(256, 64))


def _gla_step_kernel(s_ref, q_ref, k_ref, a_ref, v_ref, cg_ref, gain_ref, sn_ref, o_ref):
    v = v_ref[...]
    sn = s_ref[...] * jnp.exp(a_ref[...]) + k_ref[...] * v[:, None, :]
    sn_ref[...] = sn
    o = jnp.sum(q_ref[...] * sn, axis=1)
    o_ref[...] = _rms(o, gain_ref[...]) * _silu(cg_ref[...])


def _gla_step(s, qcol, kcol, acol, v, cg, gain, tb):
    n = s.shape[0]
    big = pl.BlockSpec((tb, C_DK, C_DV), lambda i: (i, 0, 0))
    row = pl.BlockSpec((tb, C_DV), lambda i: (i, 0))
    return pl.pallas_call(
        _gla_step_kernel,
        out_shape=(jax.ShapeDtypeStruct(s.shape, F32), jax.ShapeDtypeStruct((n, C_DV), F32)),
        grid=(n // tb,),
        in_specs=[big, big, big, big, row, row, _full_spec((1, C_DV))],
        out_specs=(big, row),
        compiler_params=_cparams("parallel"),
        name="gla_step",
    )(s, qcol, kcol, acol, v, cg, gain)


def _dn_step_kernel(s_ref, q_ref, k_ref, v_ref, eg_ref, beta_ref, dz_ref, gain_ref, sn_ref, o_ref):
    s = s_ref[...]
    kcol = k_ref[...]
    eg = eg_ref[...]
    ks = jnp.sum(kcol * s, axis=1)
    v_new = beta_ref[...] * (v_ref[...] - eg * ks)
    sn = s * eg[:, None, :] + kcol * v_new[:, None, :]
    sn_ref[...] = sn
    o = jnp.sum(q_ref[...] * sn, axis=1)
    o_ref[...] = _rms(o, gain_ref[...]) * _silu(dz_ref[...])


def _dn_step(s, qcol, kcol, v, eg, beta, dz, gain, tb):
    n = s.shape[0]
    big = pl.BlockSpec((tb, D_DK, D_DV), lambda i: (i, 0, 0))
    row = pl.BlockSpec((tb, D_DV), lambda i: (i, 0))
    return pl.pallas_call(
        _dn_step_kernel,
        out_shape=(jax.ShapeDtypeStruct(s.shape, F32), jax.ShapeDtypeStruct((n, D_DV), F32)),
        grid=(n // tb,),
        in_specs=[big, big, big, row, row, row, row, _full_spec((1, D_DV))],
        out_specs=(big, row),
        compiler_params=_cparams("parallel"),
        name="dn_step",
    )(s, qcol, kcol, v, eg, beta, dz, gain)


def _prep_w_in(w):
    o = np.cumsum([0, 256, 256, 256, 256, 128, 128, 256, 256, 16, 768, 4, 4, 256])
    q, k, v, u, cq, ck, cv, cg, lr, dqkv, db, da, dz = [w[:, o[i]:o[i + 1]] for i in range(13)]
    pad = jnp.zeros((w.shape[0], SMALL_W - 24), w.dtype)
    return jnp.concatenate([q, k, v, u, cq, ck, cv, dqkv, cg, dz, lr, db, da, pad], axis=1).astype(BF16)


def _lane_row(vals, offset):
    return jnp.zeros((1, 128), F32).at[0, offset:offset + vals.shape[0]].set(vals)


def kernel(x_prompt, x_sample, cache_k, cache_v, state_ssm_re, state_ssm_im, state_gla, state_delta, state_conv, page_table, p_prompt, p_sample, g_mix, w_in, lam_q1, lam_k1, lam_q2, lam_k2, attn_norm, s5_lam_re, s5_lam_im, s5_log_dt, s5_b_re, s5_b_im, s5_c_re, s5_c_im, s5_d, s5_w_glu, s5_b_glu, gla_w_a2, gla_b_a, gla_norm, dn_conv_w, dn_a_log, dn_dt_bias, dn_norm, w_branch_gate, b_branch_gate, w_branch, w_out, g_ffn, w_ff1, w_ff2, g_ple, w_ple_gate, w_ple_proj, g_final):
    B, S, _ = x_prompt.shape
    DB = x_sample.shape[0]
    T = B * S
    n_pool, page = cache_k.shape[1], cache_k.shape[2]
    hp = x_prompt.reshape(T, D_MODEL)
    hs = x_sample.reshape(DB, D_MODEL)
    row = lambda a: a.reshape(1, -1)
    ck_t = cache_k.transpose(0, 1, 3, 4, 2).reshape(DEPTH, n_pool, A_HEADS * 2 * A_DH, page)
    cv_t = cache_v.transpose(0, 1, 3, 4, 2).reshape(DEPTH, n_pool, A_HEADS * A_DV, page)
    st_p, st_s = [], []
    for l in range(DEPTH):
        lam_init = 0.8 - 0.6 * math.exp(-0.3 * l)
        final = l == DEPTH - 1
        w_in_l = _prep_w_in(w_in[l])
        wg, wb, wo = w_branch_gate[l].astype(BF16), w_branch[l].astype(BF16), w_out[l].astype(BF16)
        w1, w2 = w_ff1[l].astype(BF16), w_ff2[l].astype(BF16)
        wpg, wpp = w_ple_gate[l].astype(BF16), w_ple_proj[l].astype(BF16)
        lam_params = (row(lam_q1[l]), row(lam_k1[l]), row(lam_q2[l]), row(lam_k2[l]))
        sp = _s5_params(s5_lam_re[l], s5_lam_im[l], s5_log_dt[l], s5_b_re[l], s5_b_im[l], s5_c_re[l], s5_c_im[l])
        s5d, s5wg, s5bg = row(s5_d[l]), s5_w_glu[l].astype(BF16), row(s5_b_glu[l])
        wa = jnp.zeros((SMALL_W, 128), F32).at[SM_LR:SM_LR + C_RANK].set(gla_w_a2[l])
        ba = row(gla_b_a[l])
        gla_gain4 = row(jnp.tile(gla_norm[l], C_HEADS))
        dn_gain4 = row(jnp.tile(dn_norm[l], D_HEADS))
        acoef = _lane_row(-jnp.exp(dn_a_log[l]), SM_A)
        dtb = _lane_row(dn_dt_bias[l], SM_A)
        cw = dn_conv_w[l]

        zp, kt_p, vt_p = _inproj(hp, row(g_mix[l]), w_in_l, TM_PROJ, seq_len=S)
        oa = _attn_prompt(zp, lam_params, row(attn_norm[l]), B, S, TQ_ATTN, lam_init)
        ob, p_hre, p_him = _s5_prompt(zp, sp, s5d, s5wg, s5bg, B, S, TS_S5)
        oc, p_gla_t = _gla_prompt(zp.reshape(B, S, Z_W), wa, ba, gla_gain4, B, S, TS_SEQS)
        oc = oc.reshape(T, 256)
        od, p_dn_bd, p_conv = _dn_prompt(zp.reshape(B, S, Z_W), cw, acoef, dtb, dn_gain4, B, S, TS_SEQS)
        od = od.reshape(T, 256)
        p_dn = jnp.stack([p_dn_bd[:, h * D_DK:(h + 1) * D_DK, h * D_DV:(h + 1) * D_DV] for h in range(D_HEADS)], axis=1)
        hp = _merge(hp, oa, ob, oc, od, row(g_mix[l]), wg, b_branch_gate[l], wb, wo, TM_PROJ)
        hp = _ffn(hp, p_prompt[l].reshape(T, PLE_DIM), row(g_ffn[l]), w1, w2, row(g_ple[l]), wpg, wpp, row(g_final),
                  TM_FFN, final)
        p_gla = jnp.stack([p_gla_t[:, h * C_DV:(h + 1) * C_DV, h * C_DK:(h + 1) * C_DK] for h in range(C_HEADS)],
                          axis=1).swapaxes(-1, -2)
        st_p.append((kt_p.reshape(B, A_HEADS, 2 * A_DH, S).transpose(0, 3, 1, 2),
                     vt_p.reshape(B, A_HEADS, A_DV, S).transpose(0, 3, 1, 2),
                     p_hre.reshape(B, S5_GROUPS, S5_P), p_him.reshape(B, S5_GROUPS, S5_P), p_gla, p_dn, p_conv))

        zs, = _inproj(hs, row(g_mix[l]), w_in_l, DB)
        oa_s = _attn_decode(page_table, ck_t, cv_t, l, zs.reshape(DB, 1, Z_W), lam_params, row(attn_norm[l]),
                            lam_init).reshape(DB, 256)
        (ob_s, s_hre, s_him, gq, gk, ga, dq, dk, dv, dgb, s_conv3) = _dec_prep(
            zs, state_ssm_re[l].reshape(DB, S5_STATE), state_ssm_im[l].reshape(DB, S5_STATE),
            state_conv[l].swapaxes(0, 1), sp, s5d, s5wg, s5bg, wa, ba, cw, acoef, dtb)
        col = lambda a, h, dk, dv: jnp.broadcast_to(a.reshape(DB * h, dk, 1), (DB * h, dk, dv))
        s_gla, oc_s = _gla_step(state_gla[l].reshape(DB * C_HEADS, C_DK, C_DV), col(gq, C_HEADS, C_DK, C_DV),
                                col(gk, C_HEADS, C_DK, C_DV), col(ga, C_HEADS, C_DK, C_DV),
                                zs[:, COL_CV:COL_CV + 256].reshape(DB * C_HEADS, C_DV),
                                zs[:, COL_CG:COL_CG + 256].reshape(DB * C_HEADS, C_DV), row(gla_norm[l]), TB_GLA_STEP)
        lanes = lambda a, off: jnp.broadcast_to(a[:, off:off + D_HEADS].reshape(DB * D_HEADS, 1), (DB * D_HEADS, D_DV))
        s_dn, od_s = _dn_step(state_delta[l].reshape(DB * D_HEADS, D_DK, D_DV), col(dq, D_HEADS, D_DK, D_DV),
                              col(dk, D_HEADS, D_DK, D_DV), dv.reshape(DB * D_HEADS, D_DV), lanes(dgb, SM_A),
                              lanes(dgb, SM_B), zs[:, COL_DZ:COL_DZ + 256].reshape(DB * D_HEADS, D_DV),
                              row(dn_norm[l]), TB_DN_STEP)
        hs = _merge(hs, oa_s, ob_s, oc_s.reshape(DB, 256), od_s.reshape(DB, 256), row(g_mix[l]), wg, b_branch_gate[l],
                    wb, wo, DB)
        hs = _ffn(hs, p_sample[l].reshape(DB, PLE_DIM), row(g_ffn[l]), w1, w2, row(g_ple[l]), wpg, wpp, row(g_final),
                  DB, final)
        st_s.append((zs[:, COL_K:COL_K + 256].reshape(DB, 1, A_HEADS, 2 * A_DH),
                     zs[:, COL_V:COL_V + 256].reshape(DB, 1, A_HEADS, A_DV),
                     s_hre.reshape(DB, S5_GROUPS, S5_P), s_him.reshape(DB, S5_GROUPS, S5_P),
                     s_gla.reshape(DB, C_HEADS, C_DK, C_DV), s_dn.reshape(DB, D_HEADS, D_DK, D_DV),
                     s_conv3.swapaxes(0, 1)))

    stk = lambda states, i: jnp.stack([s[i] for s in states])
    return (hp.reshape(B, S, D_MODEL), hs.reshape(DB, 1, D_MODEL),
            stk(st_p, 0), stk(st_p, 1), stk(st_p, 2), stk(st_p, 3), stk(st_p, 4), stk(st_p, 5), stk(st_p, 6),
            stk(st_s, 0), stk(st_s, 1), stk(st_s, 2), stk(st_s, 3), stk(st_s, 4), stk(st_s, 5), stk(st_s, 6))
```
